```python
import math
import jax
import jax.numpy as jnp
from jax import lax
import numpy as np

D_MODEL = 1024
BATCH = 4
SEQ = 8192
DEPTH = 2

GRID_W = 64
CTX_LEN = 256
Q_BLOCK = 128
ROPE_THETA = 10000.0
NORM_EPS = 1e-6

SSD_INNER = D_MODEL // 2
SSD_HEAD_DIM = 64
SSD_HEADS = SSD_INNER // SSD_HEAD_DIM
SSD_GROUPS = 2
SSD_STATE = 64
SSD_CHUNK = 128
SSD_CONV_W = 5
SSD_CONV_CH = SSD_INNER + 2 * SSD_GROUPS * SSD_STATE

DIFF_HEADS = 4
DIFF_HD = 64
DIFF_QK = DIFF_HEADS * 2 * DIFF_HD
DIFF_WIDTH = DIFF_HEADS * 2 * DIFF_HD

GQA_HEADS = 8
GQA_KV_HEADS = 2
GQA_HD = 64
GQA_WIDTH = GQA_HEADS * GQA_HD
GQA_KV_WIDTH = GQA_KV_HEADS * GQA_HD

S5_GROUP_CH = 16
S5_STATE = 64
S5_WIDTH = 3 * D_MODEL // 8
S5_GROUPS = S5_WIDTH // S5_GROUP_CH

FFN_HIDDEN = -(-8 * D_MODEL // (3 * 256)) * 256

N_BRANCHES = 4
IN_SPLITS = (SSD_INNER + SSD_CONV_CH + 2 * SSD_HEADS,
             2 * DIFF_QK + DIFF_WIDTH,
             GQA_WIDTH + 2 * GQA_KV_WIDTH,
             S5_WIDTH)
IN_COLS = sum(IN_SPLITS)

kernel_name = 'hybrid_flow_backbone_block'


def split_cols(x, sizes):
    return jnp.split(x, np.cumsum(sizes)[:-1].tolist(), axis=-1)


def rms_norm(x, g):
    xf = x.astype(jnp.float32)
    y = xf * lax.rsqrt(jnp.mean(xf * xf, axis=-1, keepdims=True) + NORM_EPS)
    return (y * g.astype(jnp.float32)).astype(x.dtype)


def modulate(x, shift, scale):
    return x * (1.0 + scale) + shift


def swiglu(x, w_gate_up, w_down):
    gate, up = jnp.split(x @ w_gate_up, 2, axis=-1)
    return (jax.nn.silu(gate) * up) @ w_down


def _flip(t, direction):
    return jnp.flip(t, axis=1) if direction == 1 else t


def axial_rope_tables(n_tokens, head_dim):
    n_rows = n_tokens // GRID_W
    rows = jnp.repeat(jnp.arange(n_rows, dtype=jnp.float32), GRID_W)
    cols = jnp.tile(jnp.arange(GRID_W, dtype=jnp.float32), n_rows)
    quarter = head_dim // 4
    inv_freq = ROPE_THETA ** (-jnp.arange(quarter, dtype=jnp.float32) / quarter)
    ang_r = rows[:, None] * inv_freq
    ang_c = cols[:, None] * inv_freq
    ang = jnp.concatenate([ang_r, ang_r, ang_c, ang_c], axis=-1)
    return jnp.cos(ang), jnp.sin(ang)


def apply_axial_rope(x, cos, sin):
    r1, r2, c1, c2 = jnp.split(x, 4, axis=-1)
    rotated = jnp.concatenate([-r2, r1, -c2, c1], axis=-1)
    out = x * cos[None, :, None, :].astype(x.dtype) + rotated * sin[None, :, None, :].astype(x.dtype)
    return out.astype(x.dtype)


def sweep_query_blocks(block_fn, q):
    bsz, n = q.shape[:2]
    nb = n // Q_BLOCK
    qb = jnp.moveaxis(q.reshape((bsz, nb, Q_BLOCK) + q.shape[2:]), 1, 0)
    out = lax.map(block_fn, qb)
    return jnp.moveaxis(out, 0, 1).reshape((bsz, n) + out.shape[3:])


def depthwise_conv_centred(x, w, b):
    k = w.shape[0]
    pad = (k - 1) // 2
    y = lax.conv_general_dilated(x, w[:, None, :].astype(x.dtype), window_strides=(1,),
                                 padding=[(pad, pad)], dimension_numbers=('NWC', 'WIO', 'NWC'),
                                 feature_group_count=x.shape[-1])
    return y + b.astype(x.dtype)


def ssd_chunked_scan(xs, dt, a_neg, bs, cs, h0):
    bsz, n, nh, hp = xs.shape
    nc = n // SSD_CHUNK
    rep = nh // SSD_GROUPS
    bh = jnp.repeat(bs, rep, axis=2).reshape(bsz, nc, SSD_CHUNK, nh, SSD_STATE)
    ch = jnp.repeat(cs, rep, axis=2).reshape(bsz, nc, SSD_CHUNK, nh, SSD_STATE)
    xq = xs.reshape(bsz, nc, SSD_CHUNK, nh, hp)
    dtq = dt.reshape(bsz, nc, SSD_CHUNK, nh)
    a = jnp.moveaxis((dtq.astype(jnp.float32) * a_neg), 3, 1)
    a_cum = jnp.cumsum(a, axis=-1)
    lower = jnp.tril(jnp.ones((SSD_CHUNK, SSD_CHUNK), dtype=bool))
    seg = a_cum[..., :, None] - a_cum[..., None, :]
    decay = jnp.exp(jnp.where(lower, seg, -jnp.inf))
    xdt = xq * dtq[..., None]
    scores = jnp.einsum('bcqhn,bckhn->bhcqk', ch, bh) * decay
    y_diag = jnp.einsum('bhcqk,bckhp->bcqhp', scores, xdt)
    decay_to_end = jnp.exp(a_cum[..., -1:] - a_cum)
    states = jnp.einsum('bhck,bckhn,bckhp->cbhpn', decay_to_end, bh, xdt)
    chunk_decay = jnp.moveaxis(jnp.exp(a_cum[..., -1]), 2, 0)

    def carry_step(h, inp):
        s, d = inp
        return h * d[..., None, None] + s, h

    h_final, h_start = lax.scan(carry_step, h0, (states, chunk_decay))
    y_off = jnp.einsum('bcqhn,cbhpn,bhcq->bcqhp', ch, h_start, jnp.exp(a_cum))
    return (y_diag + y_off).reshape(bsz, n, nh, hp), h_final


def ssd_mixer(p_ctx, p_lat, conv_w, conv_b, a_log, dt_bias, d_skip, norm_g, ctx_out):
    def prep(p):
        bsz, n = p.shape[:2]
        z, xbc, dt_raw = split_cols(p, (SSD_INNER, SSD_CONV_CH, 2 * SSD_HEADS))
        xbc = jax.nn.silu(depthwise_conv_centred(xbc, conv_w, conv_b))
        xs, bs, cs = split_cols(xbc, (SSD_INNER, SSD_GROUPS * SSD_STATE, SSD_GROUPS * SSD_STATE))
        return (z, xs.reshape(bsz, n, SSD_HEADS, SSD_HEAD_DIM),
                bs.reshape(bsz, n, SSD_GROUPS, SSD_STATE),
                cs.reshape(bsz, n, SSD_GROUPS, SSD_STATE),
                dt_raw.reshape(bsz, n, 2, SSD_HEADS))

    zc, xc, bc, cc, dtc = prep(p_ctx)
    zl, xl, bl, cl, dtl = prep(p_lat)
    y_lat = d_skip[:, None] * xl
    y_ctx = d_skip[:, None] * xc
    for direction in range(2):
        a_neg = -jnp.exp(a_log[direction].astype(jnp.float32))
        dt_c = jax.nn.softplus((dtc[:, :, direction] + dt_bias[direction]).astype(jnp.float32))
        dt_l = jax.nn.softplus((dtl[:, :, direction] + dt_bias[direction]).astype(jnp.float32))
        h0 = jnp.zeros((xc.shape[0], SSD_HEADS, SSD_HEAD_DIM, SSD_STATE), jnp.float32)
        yc, hc = ssd_chunked_scan(_flip(xc, direction), _flip(dt_c, direction), a_neg,
                                  _flip(bc, direction), _flip(cc, direction), h0)
        yl, _ = ssd_chunked_scan(_flip(xl, direction), _flip(dt_l, direction), a_neg,
                                 _flip(bl, direction), _flip(cl, direction), hc)
        y_lat = y_lat + _flip(yl, direction)
        if ctx_out:
            y_ctx = y_ctx + _flip(yc, direction)

    def finish(y, z):
        bsz, n = y.shape[:2]
        y = y.reshape(bsz, n, SSD_INNER)
        return rms_norm(y * jax.nn.silu(z), norm_g).astype(z.dtype)

    return finish(y_lat, zl), (finish(y_ctx, zc) if ctx_out else None)


def diff_attention_mixer(p_ctx, p_lat, cos, sin, qn_g, kn_g, lam_q1, lam_k1, lam_q2, lam_k2,
                         subln_g, lam_init, ctx_out):
    def prep(p, rope):
        bsz, n = p.shape[:2]
        q, k, v = split_cols(p, (DIFF_QK, DIFF_QK, DIFF_WIDTH))
        q = rms_norm(q.reshape(bsz, n, 2 * DIFF_HEADS, DIFF_HD), qn_g)
        k = rms_norm(k.reshape(bsz, n, 2 * DIFF_HEADS, DIFF_HD), kn_g)
        if rope:
            q = apply_axial_rope(q, cos, sin)
            k = apply_axial_rope(k, cos, sin)
        return (q.reshape(bsz, n, DIFF_HEADS, 2, DIFF_HD), k.reshape(bsz, n, DIFF_HEADS, 2, DIFF_HD),
                v.reshape(bsz, n, DIFF_HEADS, 2 * DIFF_HD))

    f32 = jnp.float32
    lam = (jnp.exp(jnp.sum(lam_q1.astype(f32) * lam_k1.astype(f32)))
           - jnp.exp(jnp.sum(lam_q2.astype(f32) * lam_k2.astype(f32))) + lam_init)
    qc, kc, vc = prep(p_ctx, False)
    ql, kl, vl = prep(p_lat, True)
    k_all = jnp.concatenate([kc, kl], axis=1)
    v_all = jnp.concatenate([vc, vl], axis=1)
    scale = DIFF_HD ** -0.5

    def block(qb, k, v):
        s = jnp.einsum('bqhmd,bkhmd->bhmqk', qb, k).astype(f32) * scale
        p = jax.nn.softmax(s, axis=-1)
        a = p[:, :, 0] - lam * p[:, :, 1]
        return jnp.einsum('bhqk,bkhe->bqhe', a.astype(v.dtype), v)

    def finish(o):
        bsz, n = o.shape[:2]
        o = rms_norm(o, subln_g) * (1.0 - lam_init)
        return o.reshape(bsz, n, DIFF_WIDTH)

    out_lat = finish(sweep_query_blocks(lambda qb: block(qb, k_all, v_all), ql))
    out_ctx = finish(block(qc, kc, vc)) if ctx_out else None
    return out_lat, out_ctx


def gqa_mixer(p_ctx, p_lat, cos, sin, qn_g, kn_g, ctx_out):
    def prep(p, rope):
        bsz, n = p.shape[:2]
        q, k, v = split_cols(p, (GQA_WIDTH, GQA_KV_WIDTH, GQA_KV_WIDTH))
        q = rms_norm(q.reshape(bsz, n, GQA_HEADS, GQA_HD), qn_g)
        k = rms_norm(k.reshape(bsz, n, GQA_KV_HEADS, GQA_HD), kn_g)
        v = v.reshape(bsz, n, GQA_KV_HEADS, GQA_HD)
        if rope:
            q = apply_axial_rope(q, cos, sin)
            k = apply_axial_rope(k, cos, sin)
        q = q.reshape(bsz, n, GQA_KV_HEADS, GQA_HEADS // GQA_KV_HEADS, GQA_HD)
        return q, k, v

    qc, kc, vc = prep(p_ctx, False)
    ql, kl, vl = prep(p_lat, True)
    k_all = jnp.concatenate([kc, kl], axis=1)
    v_all = jnp.concatenate([vc, vl], axis=1)
    scale = GQA_HD ** -0.5

    def block(qb, k, v):
        s = jnp.einsum('bqngd,bsnd->bngqs', qb, k).astype(jnp.float32) * scale
        p = jax.nn.softmax(s, axis=-1).astype(v.dtype)
        return jnp.einsum('bngqs,bsnd->bqngd', p, v)

    def finish(o):
        bsz, n = o.shape[:2]
        return o.reshape(bsz, n, GQA_WIDTH)

    out_lat = finish(sweep_query_blocks(lambda qb: block(qb, k_all, v_all), ql))
    out_ctx = finish(block(qc, kc, vc)) if ctx_out else None
    return out_lat, out_ctx


def s5_discretise(lam_re, lam_im, log_dt, b_re, b_im):
    f32 = jnp.float32
    lr, li = lam_re.astype(f32), lam_im.astype(f32)
    step = jnp.exp(log_dt.astype(f32))[:, None]
    mag = jnp.exp(lr * step)
    ar, ai = mag * jnp.cos(li * step), mag * jnp.sin(li * step)
    den = lr * lr + li * li
    fr = ((ar - 1.0) * lr + ai * li) / den
    fi = (ai * lr - (ar - 1.0) * li) / den
    br, bi = b_re.astype(f32), b_im.astype(f32)
    bbr = fr[..., None] * br - fi[..., None] * bi
    bbi = fr[..., None] * bi + fi[..., None] * br
    return ar, ai, bbr, bbi


def complex_affine_combine(e1, e2):
    a1r, a1i, b1r, b1i = e1
    a2r, a2i, b2r, b2i = e2
    ar = a1r * a2r - a1i * a2i
    ai = a1r * a2i + a1i * a2r
    br = a2r * b1r - a2i * b1i + b2r
    bi = a2r * b1i + a2i * b1r + b2i
    return ar, ai, br, bi


def s5_states(u, ar, ai, bbr, bbi, h0r, h0i):
    bsz, n = u.shape[:2]
    ug = jnp.moveaxis(u.reshape(bsz, n, S5_GROUPS, S5_GROUP_CH), 1, 0).astype(jnp.float32)
    bur = jnp.einsum('gpc,lbgc->lbgp', bbr, ug)
    bui = jnp.einsum('gpc,lbgc->lbgp', bbi, ug)
    bur = bur.at[0].add(ar * h0r - ai * h0i)
    bui = bui.at[0].add(ar * h0i + ai * h0r)
    a_r = jnp.broadcast_to(ar, (n, 1) + ar.shape)
    a_i = jnp.broadcast_to(ai, (n, 1) + ai.shape)
    _, _, hr, hi = lax.associative_scan(complex_affine_combine, (a_r, a_i, bur, bui), axis=0)
    return hr, hi


def s5_readout(hr, hi, c_re, c_im):
    n, bsz = hr.shape[:2]
    y = (jnp.einsum('gcp,lbgp->blgc', c_re.astype(jnp.float32), hr)
         - jnp.einsum('gcp,lbgp->blgc', c_im.astype(jnp.float32), hi))
    return y.reshape(bsz, n, S5_WIDTH)


def s5_mixer(u_ctx, u_lat, lam_re, lam_im, log_dt, b_re, b_im, c_re, c_im, d_skip,
             glu_w, glu_b, ctx_out):
    y_lat = d_skip * u_lat
    y_ctx = d_skip * u_ctx
    for direction in range(2):
        ar, ai, bbr, bbi = s5_discretise(lam_re[direction], lam_im[direction], log_dt[direction], b_re, b_im)
        zero = jnp.zeros((u_ctx.shape[0], S5_GROUPS, S5_STATE), jnp.float32)
        cr, ci = s5_states(_flip(u_ctx, direction), ar, ai, bbr, bbi, zero, zero)
        lr_, li_ = s5_states(_flip(u_lat, direction), ar, ai, bbr, bbi, cr[-1], ci[-1])
        y_lat = y_lat + _flip(s5_readout(lr_, li_, c_re, c_im), direction)
        if ctx_out:
            y_ctx = y_ctx + _flip(s5_readout(cr, ci, c_re, c_im), direction)

    def glu(y, dtype):
        val, gate = jnp.split(jax.nn.gelu(y) @ glu_w + glu_b, 2, axis=-1)
        return (val * jax.nn.sigmoid(gate)).astype(dtype)

    return glu(y_lat, u_lat.dtype), (glu(y_ctx, u_ctx.dtype) if ctx_out else None)


def merge_branches(xn, ys, w_gate_l, w_brs, w_out_l):
    merged = None
    for i in range(N_BRANCHES):
        term = jax.nn.sigmoid(xn @ w_gate_l[i]) * (ys[i] @ w_brs[i])
        merged = term if merged is None else merged + term
    return merged @ w_out_l


def setup_inputs(seed: int = 0) -> dict:
    key = jax.random.key(seed)
    ks = iter(jax.random.split(key, 64))
    f32 = jnp.float32
    D = D_MODEL
    L = DEPTH

    def nrm(shape, scale):
        return jax.random.normal(next(ks), shape, f32) * scale

    def gain(shape):
        return 1.0 + nrm(shape, 0.05)

    ssd_a_log = jnp.log(jax.random.uniform(next(ks), (L, 2, SSD_HEADS), f32, 1.0, 16.0))
    dt0 = jnp.exp(jax.random.uniform(next(ks), (L, 2, SSD_HEADS), f32, math.log(1e-3), math.log(1e-1)))
    ssd_dt_bias = dt0 + jnp.log(-jnp.expm1(-dt0))
    n_idx = jnp.arange(S5_STATE, dtype=f32)
    s5_lam_re = -0.5 + nrm((L, 2, S5_GROUPS, S5_STATE), 0.01)
    s5_lam_im = math.pi * n_idx + nrm((L, 2, S5_GROUPS, S5_STATE), 0.01)
    s5_log_dt = jax.random.uniform(next(ks), (L, 2, S5_GROUPS), f32, math.log(1e-3), math.log(1e-1))
    b_scale = (2.0 * S5_GROUP_CH) ** -0.5
    c_scale = (2.0 * S5_STATE) ** -0.5
    return {
        'x': nrm((BATCH, SEQ, D), 1.0),
        'c': nrm((BATCH, D), 1.0),
        'ctx': nrm((BATCH, CTX_LEN, D), 1.0),
        'c_ctx': nrm((D,), 1.0),
        'w_mod': nrm((L, D, 6 * D), 0.5 * D ** -0.5),
        'b_mod': nrm((L, 6 * D), 0.01),
        'norm1_g': gain((L, D)),
        'norm2_g': gain((L, D)),
        'w_in': nrm((L, D, IN_COLS), D ** -0.5),
        'ssd_conv_w': nrm((L, SSD_CONV_W, SSD_CONV_CH), SSD_CONV_W ** -0.5),
        'ssd_conv_b': nrm((L, SSD_CONV_CH), 0.01),
        'ssd_a_log': ssd_a_log,
        'ssd_dt_bias': ssd_dt_bias,
        'ssd_d': gain((L, SSD_HEADS)),
        'ssd_norm_g': gain((L, SSD_INNER)),
        'diff_qn_g': gain((L, DIFF_HD)),
        'diff_kn_g': gain((L, DIFF_HD)),
        'diff_lam_q1': nrm((L, DIFF_HD), 0.1),
        'diff_lam_k1': nrm((L, DIFF_HD), 0.1),
        'diff_lam_q2': nrm((L, DIFF_HD), 0.1),
        'diff_lam_k2': nrm((L, DIFF_HD), 0.1),
        'diff_subln_g': gain((L, 2 * DIFF_HD)),
        'gqa_qn_g': gain((L, GQA_HD)),
        'gqa_kn_g': gain((L, GQA_HD)),
        's5_lam_re': s5_lam_re,
        's5_lam_im': s5_lam_im,
        's5_log_dt': s5_log_dt,
        's5_b_re': nrm((L, S5_GROUPS, S5_STATE, S5_GROUP_CH), b_scale),
        's5_b_im': nrm((L, S5_GROUPS, S5_STATE, S5_GROUP_CH), b_scale),
        's5_c_re': nrm((L, S5_GROUPS, S5_GROUP_CH, S5_STATE), c_scale),
        's5_c_im': nrm((L, S5_GROUPS, S5_GROUP_CH, S5_STATE), c_scale),
        's5_d': nrm((L, S5_WIDTH), 1.0),
        's5_glu_w': nrm((L, S5_WIDTH, 2 * S5_WIDTH), S5_WIDTH ** -0.5),
        's5_glu_b': nrm((L, 2 * S5_WIDTH), 0.01),
        'w_gate': nrm((L, N_BRANCHES, D, D), D ** -0.5),
        'w_br_ssd': nrm((L, SSD_INNER, D), SSD_INNER ** -0.5),
        'w_br_diff': nrm((L, DIFF_WIDTH, D), DIFF_WIDTH ** -0.5),
        'w_br_gqa': nrm((L, GQA_WIDTH, D), GQA_WIDTH ** -0.5),
        'w_br_s5': nrm((L, S5_WIDTH, D), S5_WIDTH ** -0.5),
        'w_out': nrm((L, D, D), D ** -0.5),
        'ffn_w_gate_up': nrm((L, D, 2 * FFN_HIDDEN), D ** -0.5),
        'ffn_w_down': nrm((L, FFN_HIDDEN, D), FFN_HIDDEN ** -0.5),
    }


def reference(x, c, ctx, c_ctx, w_mod, b_mod, norm1_g, norm2_g, w_in,
              ssd_conv_w, ssd_conv_b, ssd_a_log, ssd_dt_bias, ssd_d, ssd_norm_g,
              diff_qn_g, diff_kn_g, diff_lam_q1, diff_lam_k1, diff_lam_q2, diff_lam_k2, diff_subln_g,
              gqa_qn_g, gqa_kn_g,
              s5_lam_re, s5_lam_im, s5_log_dt, s5_b_re, s5_b_im, s5_c_re, s5_c_im, s5_d,
              s5_glu_w, s5_glu_b,
              w_gate, w_br_ssd, w_br_diff, w_br_gqa, w_br_s5, w_out,
              ffn_w_gate_up, ffn_w_down):
    n_tok = x.shape[1]
    cos, sin = axial_rope_tables(n_tok, GQA_HD)
    h_lat, h_ctx = x, ctx
    for layer in range(DEPTH):
        ctx_out = layer < DEPTH - 1
        mod_lat = jax.nn.silu(c) @ w_mod[layer] + b_mod[layer]
        mod_ctx = jax.nn.silu(c_ctx) @ w_mod[layer] + b_mod[layer]
        sh1, sc1, g1, sh2, sc2, g2 = jnp.split(mod_lat[:, None, :], 6, axis=-1)
        csh1, csc1, cg1, csh2, csc2, cg2 = jnp.split(mod_ctx, 6, axis=-1)

        xn_lat = modulate(rms_norm(h_lat, norm1_g[layer]), sh1, sc1)
        xn_ctx = modulate(rms_norm(h_ctx, norm1_g[layer]), csh1, csc1)
        pa_l, pb_l, pc_l, pd_l = split_cols(xn_lat @ w_in[layer], IN_SPLITS)
        pa_c, pb_c, pc_c, pd_c = split_cols(xn_ctx @ w_in[layer], IN_SPLITS)

        ya_l, ya_c = ssd_mixer(pa_c, pa_l, ssd_conv_w[layer], ssd_conv_b[layer], ssd_a_log[layer],
                               ssd_dt_bias[layer], ssd_d[layer], ssd_norm_g[layer], ctx_out)
        lam_init = 0.8 - 0.6 * math.exp(-0.3 * layer)
        yb_l, yb_c = diff_attention_mixer(pb_c, pb_l, cos, sin, diff_qn_g[layer], diff_kn_g[layer],
                                          diff_lam_q1[layer], diff_lam_k1[layer], diff_lam_q2[layer],
                                          diff_lam_k2[layer], diff_subln_g[layer], lam_init, ctx_out)
        yc_l, yc_c = gqa_mixer(pc_c, pc_l, cos, sin, gqa_qn_g[layer], gqa_kn_g[layer], ctx_out)
        yd_l, yd_c = s5_mixer(pd_c, pd_l, s5_lam_re[layer], s5_lam_im[layer], s5_log_dt[layer],
                              s5_b_re[layer], s5_b_im[layer], s5_c_re[layer], s5_c_im[layer],
                              s5_d[layer], s5_glu_w[layer], s5_glu_b[layer], ctx_out)
        w_brs = (w_br_ssd[layer], w_br_diff[layer], w_br_gqa[layer], w_br_s5[layer])
        h_lat = h_lat + g1 * merge_branches(xn_lat, (ya_l, yb_l, yc_l, yd_l), w_gate[layer], w_brs, w_out[layer])

        xf_lat = modulate(rms_norm(h_lat, norm2_g[layer]), sh2, sc2)
        h_lat = h_lat + g2 * swiglu(xf_lat, ffn_w_gate_up[layer], ffn_w_down[layer])

        if ctx_out:
            h_ctx = h_ctx + cg1 * merge_branches(xn_ctx, (ya_c, yb_c, yc_c, yd_c), w_gate[layer], w_brs, w_out[layer])
            xf_ctx = modulate(rms_norm(h_ctx, norm2_g[layer]), csh2, csc2)
            h_ctx = h_ctx + cg2 * swiglu(xf_ctx, ffn_w_gate_up[layer], ffn_w_down[layer])
    return h_lat
```

```python
import functools
import math

import jax
import jax.numpy as jnp
from jax import lax
from jax.experimental import pallas as pl
from jax.experimental.pallas import tpu as pltpu

F32 = jnp.float32
BF16 = jnp.bfloat16
HIGHEST = lax.Precision.HIGHEST

D_MODEL = 1024
GRID_W = 64
ROPE_THETA = 10000.0
NORM_EPS = 1e-6

SSD_INNER = 512
SSD_HEADS = 8
SSD_HEAD_DIM = 64
SSD_STATE = 64
SSD_CHUNK = 128
SSD_CONV_W = 5
SSD_CONV_CH = 768
SSD_DT_PAD = 128

DIFF_HEADS = 4
HEAD_DIM = 64
GQA_HEADS = 8
GQA_KV_HEADS = 2

S5_GROUP_CH = 16
S5_STATE = 64
S5_WIDTH = 384
S5_GROUPS = 24
S5_CHUNK = 64
S5_ROWS = 8

FFN_HIDDEN = 2816

ROW_TILE = 256
LANES = 128
VMEM_LIMIT = 56 * 1024 * 1024

_C_Z, _C_XBC, _C_DT = 0, 512, 1280
_C_QD, _C_KD, _C_VD = 1408, 1920, 2432
_C_QG, _C_KG, _C_VG = 2944, 3456, 3584
_C_U, _C_END = 3712, 4096


def _params(n_grid):
    return pltpu.CompilerParams(dimension_semantics=("arbitrary",) * n_grid,
                                vmem_limit_bytes=VMEM_LIMIT)


def _bdot(a, b):
    return jnp.dot(a.astype(BF16), b.astype(BF16), preferred_element_type=F32)


def _hdot(a, b):
    return jnp.dot(a, b, precision=HIGHEST, preferred_element_type=F32)


def _rms(x, g):
    return x * lax.rsqrt(jnp.mean(x * x, axis=-1, keepdims=True) + NORM_EPS) * g


def _sigmoid(x):
    return 1.0 / (1.0 + jnp.exp(-x))


def _silu(x):
    return x * _sigmoid(x)


def _mod_body(c_ref, w_ref, b_ref, o_ref):
    o_ref[...] = _hdot(_silu(c_ref[...]), w_ref[...]) + b_ref[...]


def _mod_call(cc, w_mod, b_mod):
    depth = w_mod.shape[0]
    d = D_MODEL
    return pl.pallas_call(
        _mod_body,
        grid=(depth, 6),
        in_specs=[pl.BlockSpec((8, d), lambda l, j: (0, 0)),
                  pl.BlockSpec((None, d, d), lambda l, j: (l, 0, j)),
                  pl.BlockSpec((None, 1, d), lambda l, j: (l, 0, j))],
        out_specs=pl.BlockSpec((None, 8, d), lambda l, j: (l, 0, j)),
        out_shape=jax.ShapeDtypeStruct((depth, 8, 6 * d), F32),
        compiler_params=_params(2),
        name="adaln_mod",
    )(cc, w_mod, b_mod.reshape(depth, 1, 6 * d))


def _head_rms(t, gain, ones):
    t2 = t * t
    hi = t2.astype(BF16)
    lo = (t2 - hi.astype(F32)).astype(BF16)
    ss = (jnp.dot(hi, ones, preferred_element_type=F32)
          + jnp.dot(lo, ones, preferred_element_type=F32))
    return t * lax.rsqrt(ss * (1.0 / HEAD_DIM) + NORM_EPS) * gain


def _rope(t, cos, sin_a, sin_b):
    w = t.shape[-1]
    return t * cos + pltpu.roll(t, w - 16, 1) * sin_a + pltpu.roll(t, 16, 1) * sin_b


def _in_proj_body(h_ref, mod_ref, g_ref, w_ref, ones_ref, gains_ref, rope_ref,
                  xn_ref, z_ref, xbc_ref, dt_ref, qd_ref, kd_ref, vd_ref,
                  qg_ref, kg_ref, vg_ref, u_ref):
    x = h_ref[...]
    mod = mod_ref[...]
    xn = _rms(x, g_ref[...]) * (1.0 + mod[1:2]) + mod[0:1]
    xb = xn.astype(BF16)
    xn_ref[...] = xb

    def proj(lo, hi):
        return jnp.dot(xb, w_ref[:, lo:hi], preferred_element_type=F32)

    z_ref[...] = proj(_C_Z, _C_XBC)
    xbc_ref[...] = proj(_C_XBC, _C_DT)
    dt_ref[...] = proj(_C_DT, _C_QD)
    u_ref[...] = proj(_C_U, _C_END)

    rope = rope_ref[...]
    cos1, sa1, sb1 = rope[:, 0:128], rope[:, 128:256], rope[:, 256:384]
    cos4 = jnp.concatenate([cos1] * 4, axis=1)
    sa4 = jnp.concatenate([sa1] * 4, axis=1)
    sb4 = jnp.concatenate([sb1] * 4, axis=1)
    ones = ones_ref[...]
    gains = gains_ref[...]
    lane = lax.broadcasted_iota(jnp.int32, (1, LANES), 1)
    lo_half = lane < HEAD_DIM
    scale = HEAD_DIM ** -0.5

    qd = _rope(_head_rms(proj(_C_QD, _C_KD), gains[0:1], ones), cos4, sa4, sb4) * scale
    kd = _rope(_head_rms(proj(_C_KD, _C_VD), gains[1:2], ones), cos4, sa4, sb4)
    kd_ref[...] = kd.astype(BF16)
    vd_ref[...] = proj(_C_VD, _C_QG).astype(BF16)
    for h in range(DIFF_HEADS):
        blk = qd[:, h * LANES:(h + 1) * LANES]
        qd_ref[h, 0] = jnp.where(lo_half, blk, 0.0).astype(BF16)
        qd_ref[h, 1] = jnp.where(lo_half, 0.0, blk).astype(BF16)

    qg = _rope(_head_rms(proj(_C_QG, _C_KG), gains[2:3], ones), cos4, sa4, sb4) * scale
    kg = _rope(_head_rms(proj(_C_KG, _C_VG), gains[3:4, :LANES], ones[:LANES, :LANES]),
               cos1, sa1, sb1)
    kg_ref[...] = kg.astype(BF16)
    vg_ref[...] = proj(_C_VG, _C_U).astype(BF16)
    qg_up = pltpu.roll(qg, 4 * LANES - HEAD_DIM, 1)
    qg_dn = pltpu.roll(qg, HEAD_DIM, 1)
    per_kv = GQA_HEADS // GQA_KV_HEADS
    for n in range(GQA_KV_HEADS):
        for i in range(per_kv):
            j = n * per_kv + i
            blk_idx = j // 2
            if (j % 2) == n:
                src = qg
            else:
                src = qg_up if n == 0 else qg_dn
            blk = src[:, blk_idx * LANES:(blk_idx + 1) * LANES]
            keep = lo_half if n == 0 else jnp.logical_not(lo_half)
            qg_ref[n, i] = jnp.where(keep, blk, 0.0).astype(BF16)


def _in_proj_call(h, modv, g1, w_cat, ones, gains, rope_tab, n_ctx_tiles):
    b, t, d = h.shape
    nt = t // ROW_TILE
    tm = ROW_TILE

    def row(width, dtype):
        return (pl.BlockSpec((None, tm, width), lambda bi, ti: (bi, ti, 0)),
                jax.ShapeDtypeStruct((b, t, width), dtype))

    outs = [row(d, BF16), row(512, F32), row(768, F32), row(SSD_DT_PAD, F32)]
    qd = (pl.BlockSpec((None, DIFF_HEADS, 2, tm, LANES), lambda bi, ti: (bi, 0, 0, ti, 0)),
          jax.ShapeDtypeStruct((b, DIFF_HEADS, 2, t, LANES), BF16))
    qg = (pl.BlockSpec((None, GQA_KV_HEADS, 4, tm, LANES), lambda bi, ti: (bi, 0, 0, ti, 0)),
          jax.ShapeDtypeStruct((b, GQA_KV_HEADS, 4, t, LANES), BF16))
    outs += [qd, row(512, BF16), row(512, BF16), qg, row(LANES, BF16), row(LANES, BF16),
             row(S5_WIDTH, F32)]
    return pl.pallas_call(
        _in_proj_body,
        grid=(b, nt),
        in_specs=[
            pl.BlockSpec((None, tm, d), lambda bi, ti: (bi, ti, 0)),
            pl.BlockSpec((None, None, 6, d),
                         lambda bi, ti: (bi, jnp.where(ti >= n_ctx_tiles, 1, 0), 0, 0)),
            pl.BlockSpec((1, d), lambda bi, ti: (0, 0)),
            pl.BlockSpec((d, _C_END), lambda bi, ti: (0, 0)),
            pl.BlockSpec((512, 512), lambda bi, ti: (0, 0)),
            pl.BlockSpec((8, 512), lambda bi, ti: (0, 0)),
            pl.BlockSpec((tm, 384), lambda bi, ti: (ti, 0)),
        ],
        out_specs=[o[0] for o in outs],
        out_shape=[o[1] for o in outs],
        compiler_params=_params(2),
        name="in_proj",
    )(h, modv, g1, w_cat, ones, gains, rope_tab)


def _conv_body(cur_ref, prev_ref, next_ref, w_ref, b_ref, o_ref, buf_ref, *, n_ctx_tiles, n_tiles):
    t = pl.program_id(1)
    tm = ROW_TILE
    has_prev = jnp.logical_and(t != 0, t != n_ctx_tiles)
    has_next = jnp.logical_and(t != n_ctx_tiles - 1, t != n_tiles - 1)
    buf_ref[0:8, :] = jnp.where(has_prev, prev_ref[...], 0.0)
    buf_ref[8:8 + tm, :] = cur_ref[...]
    buf_ref[8 + tm:16 + tm, :] = jnp.where(has_next, next_ref[...], 0.0)
    w = w_ref[...]
    bias = b_ref[...]
    pad = (SSD_CONV_W - 1) // 2
    rows = 64
    for r in range(tm // rows):
        acc = bias + w[0:1] * buf_ref[8 - pad + r * rows:8 - pad + (r + 1) * rows, :]
        for k in range(1, SSD_CONV_W):
            lo = 8 - pad + k + r * rows
            acc = acc + w[k:k + 1] * buf_ref[lo:lo + rows, :]
        o_ref[r * rows:(r + 1) * rows, :] = _silu(acc)


def _conv_call(xbc, conv_w, conv_b, n_ctx_tiles):
    b, t, ch = xbc.shape
    tm = ROW_TILE
    nt = t // tm
    per = tm // 8
    last8 = t // 8 - 1
    return pl.pallas_call(
        functools.partial(_conv_body, n_ctx_tiles=n_ctx_tiles, n_tiles=nt),
        grid=(b, nt),
        in_specs=[
            pl.BlockSpec((None, tm, ch), lambda bi, ti: (bi, ti, 0)),
            pl.BlockSpec((None, 8, ch), lambda bi, ti: (bi, jnp.maximum(ti * per - 1, 0), 0)),
            pl.BlockSpec((None, 8, ch), lambda bi, ti: (bi, jnp.minimum((ti + 1) * per, last8), 0)),
            pl.BlockSpec((8, ch), lambda bi, ti: (0, 0)),
            pl.BlockSpec((1, ch), lambda bi, ti: (0, 0)),
        ],
        out_specs=pl.BlockSpec((None, tm, ch), lambda bi, ti: (bi, ti, 0)),
        out_shape=jax.ShapeDtypeStruct((b, t, ch), F32),
        scratch_shapes=[pltpu.VMEM((tm + 16, ch), F32)],
        compiler_params=_params(2),
        name="ssd_conv",
    )(xbc, xbc, xbc, conv_w, conv_b)


def _ssd_chunk_index(d, s, n_ctx_chunks, n_chunks):
    bwd = jnp.where(s < n_ctx_chunks, n_ctx_chunks - 1 - s, n_chunks - 1 - (s - n_ctx_chunks))
    return jnp.where(d == 0, s, bwd)


def _ssd_body(xbc_ref, dt_ref, z_ref, aneg_ref, dtb_ref, dsk_ref, ng_ref, o_ref,
              hst_ref, ysc_ref, *, n_ctx_chunks, n_chunks):
    d = pl.program_id(1)
    s = pl.program_id(2)
    c = _ssd_chunk_index(d, s, n_ctx_chunks, n_chunks)
    q = SSD_CHUNK

    @pl.when(s == 0)
    def _():
        hst_ref[...] = jnp.zeros_like(hst_ref)

    def run(direction):
        xbc = xbc_ref[...]
        x = xbc[:, 0:512]
        bm = xbc[:, 512:640]
        cm = xbc[:, 640:768]
        pre = dt_ref[...] + dtb_ref[...]
        dtv = jnp.maximum(pre, 0.0) + jnp.log(1.0 + jnp.exp(-jnp.abs(pre)))
        a = dtv * aneg_ref[...]
        row = lax.broadcasted_iota(jnp.int32, (q, q), 0)
        col = lax.broadcasted_iota(jnp.int32, (q, q), 1)
        tri = (row >= col) if direction == 0 else (row <= col)
        acs = _hdot(tri.astype(F32), a)
        acs_t = acs.T
        last = acs[q - 1:q, :] if direction == 0 else acs[0:1, :]
        dte = jnp.exp(last - acs)
        expa = jnp.exp(acs)
        cdec = jnp.broadcast_to(jnp.exp(last), (8, LANES))
        erow = lax.broadcasted_iota(jnp.int32, (LANES, SSD_INNER), 0)
        ecol = lax.broadcasted_iota(jnp.int32, (LANES, SSD_INNER), 1)
        expand = (erow == ecol // SSD_HEAD_DIM + SSD_HEADS * direction).astype(F32)
        spread = _hdot(jnp.concatenate([dtv, dtv * dte, expa, cdec], axis=0), expand)
        xdt = x * spread[0:q]
        xs = x * spread[q:2 * q]
        expa_x = spread[2 * q:3 * q]
        cdec_x = spread[3 * q:3 * q + 1]
        bt = bm.T.astype(BF16)
        bb = bm.astype(BF16)
        lane = lax.broadcasted_iota(jnp.int32, (1, LANES), 1)
        ys = []
        for g in range(2):
            cg = jnp.where(lane // SSD_STATE == g, cm, 0.0).astype(BF16)
            gmat = lax.dot_general(cg, bb, (((1,), (1,)), ((), ())), preferred_element_type=F32)
            for hp in range(2):
                p = g * 2 + hp
                sl = slice(p * LANES, (p + 1) * LANES)
                ypair = None
                for e in range(2):
                    j = SSD_HEADS * direction + 2 * p + e
                    seg = acs[:, j:j + 1] - acs_t[j:j + 1, :]
                    dec = jnp.exp(jnp.where(tri, seg, -jnp.inf))
                    xh = jnp.where(lane // SSD_HEAD_DIM == e, xdt[:, sl], 0.0)
                    term = _bdot(gmat * dec, xh)
                    ypair = term if ypair is None else ypair + term
                hprev = hst_ref[p]
                ypair = ypair + _bdot(cg, hprev) * expa_x[:, sl]
                hst_ref[p] = hprev * cdec_x[:, sl] + jnp.dot(
                    bt, xs[:, sl].astype(BF16), preferred_element_type=F32)
                ys.append(ypair)
        return x, ys

    @pl.when(d == 0)
    def _():
        _, ys = run(0)
        for p in range(4):
            ysc_ref[c, :, p * LANES:(p + 1) * LANES] = ys[p]

    @pl.when(d == 1)
    def _():
        x, ys = run(1)
        y = jnp.concatenate(ys, axis=1) + ysc_ref[c] + dsk_ref[...] * x
        y = y * _silu(z_ref[...])
        o_ref[...] = _rms(y, ng_ref[...]).astype(o_ref.dtype)


def _ssd_call(xbc_act, dt, z, aneg, dtb, dsk, ng, n_ctx_chunks):
    b, t, _ = xbc_act.shape
    q = SSD_CHUNK
    nc = t // q
    cidx = functools.partial(_ssd_chunk_index, n_ctx_chunks=n_ctx_chunks, n_chunks=nc)

    def late(bi, di, si):
        return (bi, jnp.where(di == 0, n_ctx_chunks - 1, cidx(di, si)), 0)

    return pl.pallas_call(
        functools.partial(_ssd_body, n_ctx_chunks=n_ctx_chunks, n_chunks=nc),
        grid=(b, 2, nc),
        in_specs=[
            pl.BlockSpec((None, q, SSD_CONV_CH), lambda bi, di, si: (bi, cidx(di, si), 0)),
            pl.BlockSpec((None, q, SSD_DT_PAD), lambda bi, di, si: (bi, cidx(di, si), 0)),
            pl.BlockSpec((None, q, SSD_INNER), late),
            pl.BlockSpec((1, LANES), lambda bi, di, si: (0, 0)),
            pl.BlockSpec((1, LANES), lambda bi, di, si: (0, 0)),
            pl.BlockSpec((1, SSD_INNER), lambda bi, di, si: (0, 0)),
            pl.BlockSpec((1, SSD_INNER), lambda bi, di, si: (0, 0)),
        ],
        out_specs=pl.BlockSpec((None, q, SSD_INNER), late),
        out_shape=jax.ShapeDtypeStruct((b, t, SSD_INNER), BF16),
        scratch_shapes=[pltpu.VMEM((4, LANES, LANES), F32),
                        pltpu.VMEM((nc, q, SSD_INNER), F32)],
        compiler_params=_params(3),
        name="ssd_scan",
    )(xbc_act, dt, z, aneg, dtb, dsk, ng)


def _flash(q, k_ref, v_ref, m_ref, l_ref, acc_ref, *, tk, n_kv, n_ctx_kv, ctx_len, is_ctx_tile):
    r = q.shape[0]
    m_ref[...] = jnp.full(m_ref.shape, -jnp.inf, F32)
    l_ref[...] = jnp.zeros(l_ref.shape, F32)
    acc_ref[...] = jnp.zeros(acc_ref.shape, F32)
    reps = tk // LANES

    def step(kc, masked):
        start = kc * tk if isinstance(kc, int) else pl.multiple_of(kc * tk, tk)
        k = k_ref[pl.ds(start, tk), :]
        v = v_ref[pl.ds(start, tk), :]
        s = lax.dot_general(q, k, (((1,), (1,)), ((), ())), preferred_element_type=F32)
        if masked:
            col = lax.broadcasted_iota(jnp.int32, (1, tk), 1) + kc * tk
            s = jnp.where(col < ctx_len, s, -jnp.inf)
        m_old = m_ref[...]
        m_new = jnp.maximum(m_old, jnp.max(s, axis=-1, keepdims=True))
        alpha = jnp.exp(m_old - m_new)
        p = jnp.exp(s - jnp.concatenate([m_new] * reps, axis=1))
        psum = p[:, 0:LANES]
        for i in range(1, reps):
            psum = psum + p[:, i * LANES:(i + 1) * LANES]
        l_ref[...] = alpha * l_ref[...] + psum
        acc_ref[...] = alpha * acc_ref[...] + jnp.dot(p.astype(BF16), v, preferred_element_type=F32)
        m_ref[...] = m_new

    @pl.when(is_ctx_tile)
    def _():
        for kc in range(n_ctx_kv):
            step(kc, True)

    @pl.when(jnp.logical_not(is_ctx_tile))
    def _():
        def body(kc, carry):
            step(kc, False)
            return carry
        lax.fori_loop(0, n_kv, body, 0)

    inv_l = 1.0 / jnp.sum(l_ref[...], axis=-1, keepdims=True)
    return acc_ref[...] * inv_l


def _diff_attn_body(q_ref, k_ref, v_ref, lam_ref, g_ref, o_ref, m_ref, l_ref, acc_ref,
                    *, tq, lam_init, **kw):
    qi = pl.program_id(2)
    q = q_ref[...].reshape(2 * tq, LANES)
    o = _flash(q, k_ref, v_ref, m_ref, l_ref, acc_ref, is_ctx_tile=qi < kw.pop("n_ctx_q"), **kw)
    lv = lam_ref[...]
    lam = (jnp.exp(jnp.sum(lv[0:1] * lv[1:2], axis=-1, keepdims=True))
           - jnp.exp(jnp.sum(lv[2:3] * lv[3:4], axis=-1, keepdims=True)) + lam_init)
    out = o[0:tq] - lam * o[tq:2 * tq]
    o_ref[...] = (_rms(out, g_ref[...]) * (1.0 - lam_init)).astype(o_ref.dtype)


def _gqa_attn_body(q_ref, k_ref, v_ref, o_ref, m_ref, l_ref, acc_ref, *, tq, **kw):
    n = pl.program_id(1)
    qi = pl.program_id(2)
    q = q_ref[...].reshape(4 * tq, LANES)
    o = _flash(q, k_ref, v_ref, m_ref, l_ref, acc_ref, is_ctx_tile=qi < kw.pop("n_ctx_q"), **kw)
    lane = lax.broadcasted_iota(jnp.int32, (1, LANES), 1)
    lo_half = lane < HEAD_DIM
    for pair in range(2):
        a = o[(2 * pair) * tq:(2 * pair + 1) * tq]
        b = o[(2 * pair + 1) * tq:(2 * pair + 2) * tq]
        from_lo = jnp.where(lo_half, a, pltpu.roll(b, HEAD_DIM, 1))
        from_hi = jnp.where(lo_half, pltpu.roll(a, HEAD_DIM, 1), b)
        o_ref[:, pair * LANES:(pair + 1) * LANES] = jnp.where(n == 0, from_lo, from_hi).astype(o_ref.dtype)


def _attn_tiles(t, ctx_len):
    tq = ROW_TILE
    tk = 768 if t % 768 == 0 else 256
    assert ctx_len % tq == 0 and t % tq == 0 and t % tk == 0
    return dict(tq=tq, tk=tk, n_kv=t // tk, n_ctx_kv=-(-ctx_len // tk), ctx_len=ctx_len,
                n_ctx_q=ctx_len // tq)


def _diff_attn_call(qd, kd, vd, lamv, subln_g, lam_init, ctx_len):
    b, _, _, t, _ = qd.shape
    cfg = _attn_tiles(t, ctx_len)
    tq = cfg["tq"]
    r = 2 * tq
    return pl.pallas_call(
        functools.partial(_diff_attn_body, lam_init=lam_init, **cfg),
        grid=(b, DIFF_HEADS, t // tq),
        in_specs=[
            pl.BlockSpec((None, None, 2, tq, LANES), lambda bi, hi, qi: (bi, hi, 0, qi, 0)),
            pl.BlockSpec((None, t, LANES), lambda bi, hi, qi: (bi, 0, hi)),
            pl.BlockSpec((None, t, LANES), lambda bi, hi, qi: (bi, 0, hi)),
            pl.BlockSpec((4, LANES), lambda bi, hi, qi: (0, 0)),
            pl.BlockSpec((1, LANES), lambda bi, hi, qi: (0, 0)),
        ],
        out_specs=pl.BlockSpec((None, tq, LANES), lambda bi, hi, qi: (bi, qi, hi)),
        out_shape=jax.ShapeDtypeStruct((b, t, DIFF_HEADS * LANES), BF16),
        scratch_shapes=[pltpu.VMEM((r, LANES), F32)] * 3,
        compiler_params=_params(3),
        name="diff_attention",
    )(qd, kd, vd, lamv, subln_g)


def _gqa_attn_call(qg, kg, vg, ctx_len):
    b, _, _, t, _ = qg.shape
    cfg = _attn_tiles(t, ctx_len)
    tq = cfg["tq"]
    r = 4 * tq
    return pl.pallas_call(
        functools.partial(_gqa_attn_body, **cfg),
        grid=(b, GQA_KV_HEADS, t // tq),
        in_specs=[
            pl.BlockSpec((None, None, 4, tq, LANES), lambda bi, ni, qi: (bi, ni, 0, qi, 0)),
            pl.BlockSpec((None, t, LANES), lambda bi, ni, qi: (bi, 0, 0)),
            pl.BlockSpec((None, t, LANES), lambda bi, ni, qi: (bi, 0, 0)),
        ],
        out_specs=pl.BlockSpec((None, tq, 2 * LANES), lambda bi, ni, qi: (bi, qi, ni)),
        out_shape=jax.ShapeDtypeStruct((b, t, GQA_HEADS * HEAD_DIM), BF16),
        scratch_shapes=[pltpu.VMEM((r, LANES), F32)] * 3,
        compiler_params=_params(3),
        name="gqa_attention",
    )(qg, kg, vg)


def _s5_weights(lam_re, lam_im, log_dt, b_re, b_im, c_re, c_im):
    tc = S5_CHUNK
    ks = jnp.arange(tc + 1, dtype=F32)
    w_intra = 0.0
    w_state, w_out, avec = [], [], []
    s_idx = jnp.arange(tc)[:, None]
    t_idx = jnp.arange(tc)[None, :]
    br, bi = b_re.astype(F32), b_im.astype(F32)
    cr, ci = c_re.astype(F32), c_im.astype(F32)
    for direction in range(2):
        lr, li = lam_re[direction].astype(F32), lam_im[direction].astype(F32)
        step = jnp.exp(log_dt[direction].astype(F32))[:, None]
        mag = jnp.exp(lr * step)
        ar, ai = mag * jnp.cos(li * step), mag * jnp.sin(li * step)
        den = lr * lr + li * li
        fr = ((ar - 1.0) * lr + ai * li) / den
        fi = (ai * lr - (ar - 1.0) * li) / den
        bbr = fr[..., None] * br - fi[..., None] * bi
        bbi = fr[..., None] * bi + fi[..., None] * br
        pmag = jnp.exp(ks[:, None, None] * (lr * step)[None])
        pang = ks[:, None, None] * (li * step)[None]
        er, ei = pmag * jnp.cos(pang), pmag * jnp.sin(pang)
        cer = cr[None] * er[:, :, None, :] - ci[None] * ei[:, :, None, :]
        cei = cr[None] * ei[:, :, None, :] + ci[None] * er[:, :, None, :]
        kern = (jnp.einsum('kgcp,gpd->kgcd', cer, bbr, precision=HIGHEST)
                - jnp.einsum('kgcp,gpd->kgcd', cei, bbi, precision=HIGHEST))
        lag = (t_idx - s_idx) if direction == 0 else (s_idx - t_idx)
        blk = kern[jnp.clip(lag, 0, tc - 1)]
        blk = jnp.where((lag >= 0)[:, :, None, None, None], blk, 0.0)
        w_intra = w_intra + jnp.transpose(blk, (2, 0, 4, 1, 3)).reshape(
            S5_GROUPS, tc * S5_GROUP_CH, tc * S5_GROUP_CH)
        spow = (tc - 1 - jnp.arange(tc)) if direction == 0 else jnp.arange(tc)
        esr, esi = er[spow], ei[spow]
        wsr = esr[:, :, :, None] * bbr[None] - esi[:, :, :, None] * bbi[None]
        wsi = esr[:, :, :, None] * bbi[None] + esi[:, :, :, None] * bbr[None]
        flat = lambda m: jnp.transpose(m, (1, 0, 3, 2)).reshape(S5_GROUPS, tc * S5_GROUP_CH, S5_STATE)
        w_state.append((flat(wsr), flat(wsi)))
        opow = (jnp.arange(tc) + 1) if direction == 0 else (tc - jnp.arange(tc))
        oer, oei = cer[opow], cei[opow]
        oflat = lambda m: jnp.transpose(m, (1, 3, 0, 2)).reshape(S5_GROUPS, S5_STATE, tc * S5_GROUP_CH)
        w_out.append((oflat(oer), -oflat(oei)))
        atr, ati = er[tc], ei[tc]
        avec += [jnp.concatenate([atr, atr], -1), jnp.concatenate([-ati, ati], -1),
                 jnp.concatenate([ati, -ati], -1)]
    (fsr, fsi), (bsr, bsi) = w_state
    w_state = jnp.concatenate([fsr, fsi, bsr, bsi, fsi, fsr, bsi, bsr], axis=-1)
    (fo_r, fo_i), (bo_r, bo_i) = w_out
    w_out = jnp.concatenate([fo_r, fo_i, bo_r, bo_i], axis=1)
    zero = jnp.zeros_like(avec[0])
    avec = jnp.stack(avec + [zero, zero], axis=1)
    return w_intra.astype(BF16), w_state.astype(BF16), w_out.astype(BF16), avec


def _s5_body(u_ref, wi_ref, ws_ref, wo_ref, av_ref, y_ref, s_ref, h_ref, *, n_ctx_chunks, n_chunks):
    u = u_ref[...]
    s_ref[...] = jnp.dot(u, ws_ref[...], preferred_element_type=F32)
    av = av_ref[...]
    rows = S5_ROWS
    nj = n_chunks

    def bcast(i):
        return jnp.broadcast_to(av[i:i + 1], (rows, LANES))

    a1f, a2f, a2sf, a1b, a2b, a2sb = [bcast(i) for i in range(6)]

    def body(i, carry):
        hf, hfs, hb, hbs = carry
        jf = pl.multiple_of(i * rows, rows)
        jb_chunk = jnp.where(i < n_ctx_chunks, n_ctx_chunks - 1 - i, nj - 1 - (i - n_ctx_chunks))
        jb = pl.multiple_of(jb_chunk * rows, rows)
        h_ref[pl.ds(jf, rows), 0:LANES] = hf
        h_ref[pl.ds(jb, rows), LANES:2 * LANES] = hb
        sf = s_ref[pl.ds(jf, rows), 0:LANES]
        sfs = s_ref[pl.ds(jf, rows), 2 * LANES:3 * LANES]
        sb = s_ref[pl.ds(jb, rows), LANES:2 * LANES]
        sbs = s_ref[pl.ds(jb, rows), 3 * LANES:4 * LANES]
        return (a1f * hf + a2f * hfs + sf, a1f * hfs + a2sf * hf + sfs,
                a1b * hb + a2b * hbs + sb, a1b * hbs + a2sb * hb + sbs)

    zero = jnp.zeros((rows, LANES), F32)
    lax.fori_loop(0, nj, body, (zero, zero, zero, zero))
    y_ref[...] = (jnp.dot(u, wi_ref[...], preferred_element_type=F32)
                  + jnp.dot(h_ref[...].astype(BF16), wo_ref[...], preferred_element_type=F32))


def _s5_call(ug, w_intra, w_state, w_out, avec, n_ctx_chunks):
    g, r, w = ug.shape
    nj = r // S5_ROWS
    return pl.pallas_call(
        functools.partial(_s5_body, n_ctx_chunks=n_ctx_chunks, n_chunks=nj),
        grid=(g,),
        in_specs=[
            pl.BlockSpec((None, r, w), lambda gi: (gi, 0, 0)),
            pl.BlockSpec((None, w, w), lambda gi: (gi, 0, 0)),
            pl.BlockSpec((None, w, 4 * LANES), lambda gi: (gi, 0, 0)),
            pl.BlockSpec((None, 2 * LANES, w), lambda gi: (gi, 0, 0)),
            pl.BlockSpec((None, 8, LANES), lambda gi: (gi, 0, 0)),
        ],
        out_specs=pl.BlockSpec((None, r, w), lambda gi: (gi, 0, 0)),
        out_shape=jax.ShapeDtypeStruct((g, r, w), F32),
        scratch_shapes=[pltpu.VMEM((r, 4 * LANES), F32), pltpu.VMEM((r, 2 * LANES), F32)],
        compiler_params=_params(1),
        name="s5_scan",
    )(ug, w_intra, w_state, w_out, avec)


def _s5_to_groups(u, b):
    t = u.shape[1]
    nj = t // S5_CHUNK
    x = u.astype(BF16).reshape(b, nj, S5_CHUNK, S5_GROUPS, S5_GROUP_CH)
    x = jnp.transpose(x, (3, 1, 0, 2, 4)).reshape(S5_GROUPS, nj, b, S5_CHUNK * S5_GROUP_CH)
    x = jnp.pad(x, ((0, 0), (0, 0), (0, S5_ROWS - b), (0, 0)))
    return x.reshape(S5_GROUPS, nj * S5_ROWS, S5_CHUNK * S5_GROUP_CH)


def _s5_from_groups(y, b):
    nj = y.shape[1] // S5_ROWS
    x = y.reshape(S5_GROUPS, nj, S5_ROWS, S5_CHUNK, S5_GROUP_CH)[:, :, :b]
    x = jnp.transpose(x, (2, 1, 3, 0, 4))
    return x.reshape(b, nj * S5_CHUNK, S5_WIDTH)


def _merge_body(h_ref, mod_ref, xn_ref, ya_ref, yb_ref, yc_ref, u_ref, y5_ref,
                wg_ref, wa_ref, wb_ref, wc_ref, wd_ref, glw_ref, glb_ref, s5d_ref, wo_ref, o_ref):
    xn = xn_ref[...]
    y5 = y5_ref[...] + s5d_ref[...] * u_ref[...]
    gelu = 0.5 * y5 * (1.0 + jnp.tanh(0.7978845608028654 * (y5 + 0.044715 * y5 * y5 * y5)))
    glu = _bdot(gelu, glw_ref[...]) + glb_ref[...]
    yd = glu[:, 0:S5_WIDTH] * _sigmoid(glu[:, S5_WIDTH:2 * S5_WIDTH])
    branches = ((ya_ref[...], wa_ref), (yb_ref[...], wb_ref), (yc_ref[...], wc_ref),
                (yd.astype(BF16), wd_ref))
    merged = None
    for i, (y, w_ref) in enumerate(branches):
        gate = _sigmoid(jnp.dot(xn, wg_ref[i], preferred_element_type=F32))
        term = gate * jnp.dot(y, w_ref[...], preferred_element_type=F32)
        merged = term if merged is None else merged + term
    out = _bdot(merged, wo_ref[...])
    o_ref[...] = h_ref[...] + mod_ref[2:3, :] * out


def _merge_call(h, modv, xn, ya, yb, yc, u, y5, wts, n_ctx_tiles):
    b, t, d = h.shape
    tm = ROW_TILE

    def row(width):
        return pl.BlockSpec((None, tm, width), lambda bi, ti: (bi, ti, 0))

    def const(shape):
        return pl.BlockSpec(shape, lambda bi, ti: (0,) * len(shape), pipeline_mode=pl.Buffered(1))

    w_gate, w_a, w_b, w_c, w_d, glu_w, glu_b, s5_d, w_out = wts
    return pl.pallas_call(
        _merge_body,
        grid=(b, t // tm),
        in_specs=[row(d),
                  pl.BlockSpec((None, None, 6, d),
                               lambda bi, ti: (bi, jnp.where(ti >= n_ctx_tiles, 1, 0), 0, 0)),
                  row(d), row(512), row(512), row(512), row(S5_WIDTH), row(S5_WIDTH),
                  const((4, d, d)), const((512, d)), const((512, d)), const((512, d)),
                  const((S5_WIDTH, d)), const((S5_WIDTH, 2 * S5_WIDTH)), const((1, 2 * S5_WIDTH)),
                  const((1, S5_WIDTH)), const((d, d))],
        out_specs=row(d),
        out_shape=jax.ShapeDtypeStruct((b, t, d), F32),
        compiler_params=_params(2),
        name="branch_merge",
    )(h, modv, xn, ya, yb, yc, u, y5, w_gate, w_a, w_b, w_c, w_d, glu_w, glu_b, s5_d, w_out)


def _ffn_body(h_ref, mod_ref, g_ref, wgu_ref, wd_ref, o_ref):
    h = h_ref[...]
    mod = mod_ref[...]
    xf = (_rms(h, g_ref[...]) * (1.0 + mod[4:5]) + mod[3:4]).astype(BF16)
    gate = jnp.dot(xf, wgu_ref[:, 0:FFN_HIDDEN], preferred_element_type=F32)
    up = jnp.dot(xf, wgu_ref[:, FFN_HIDDEN:2 * FFN_HIDDEN], preferred_element_type=F32)
    act = (_silu(gate) * up).astype(BF16)
    o_ref[...] = h + mod[5:6] * jnp.dot(act, wd_ref[...], preferred_element_type=F32)


def _ffn_call(h, modv, g2, w_gu, w_down, n_ctx_tiles, out_rows=None):
    b, t, d = h.shape
    tm = ROW_TILE

    def const(shape):
        return pl.BlockSpec(shape, lambda bi, ti: (0,) * len(shape), pipeline_mode=pl.Buffered(1))

    return pl.pallas_call(
        _ffn_body,
        grid=(b, t // tm),
        in_specs=[pl.BlockSpec((None, tm, d), lambda bi, ti: (bi, ti, 0)),
                  pl.BlockSpec((None, None, 6, d),
                               lambda bi, ti: (bi, jnp.where(ti >= n_ctx_tiles, 1, 0), 0, 0)),
                  const((1, d)), const((d, 2 * FFN_HIDDEN)), const((FFN_HIDDEN, d))],
        out_specs=pl.BlockSpec((None, tm, d), lambda bi, ti: (bi, ti, 0)),
        out_shape=jax.ShapeDtypeStruct((b, t, d), F32),
        compiler_params=_params(2),
        name="swiglu_ffn",
    )(h, modv, g2, w_gu, w_down)


def _rope_tables(ctx_len, seq_len):
    n_rows = seq_len // GRID_W
    rows = jnp.repeat(jnp.arange(n_rows, dtype=F32), GRID_W)
    cols = jnp.tile(jnp.arange(GRID_W, dtype=F32), n_rows)
    quarter = HEAD_DIM // 4
    inv_freq = ROPE_THETA ** (-jnp.arange(quarter, dtype=F32) / quarter)
    ang_r = rows[:, None] * inv_freq
    ang_c = cols[:, None] * inv_freq
    ang = jnp.concatenate([ang_r, ang_r, ang_c, ang_c], axis=-1)
    cos = jnp.concatenate([jnp.ones((ctx_len, HEAD_DIM), F32), jnp.cos(ang)], axis=0)
    sin = jnp.concatenate([jnp.zeros((ctx_len, HEAD_DIM), F32), jnp.sin(ang)], axis=0)
    first = (jnp.arange(HEAD_DIM) % 32) < 16
    sin_a = jnp.where(first, -sin, 0.0)
    sin_b = jnp.where(first, 0.0, sin)
    two = lambda m: jnp.concatenate([m, m], axis=1)
    return jnp.concatenate([two(cos), two(sin_a), two(sin_b)], axis=1)


def _w_in_layout(w_in):
    d = w_in.shape[0]
    a0 = 0
    z = w_in[:, a0:a0 + 512]
    xbc = w_in[:, a0 + 512:a0 + 1280]
    dt = jnp.pad(w_in[:, a0 + 1280:a0 + 1296], ((0, 0), (0, SSD_DT_PAD - 16)))
    rest = w_in[:, 1296:]
    out = jnp.concatenate([z, xbc, dt, rest], axis=1).astype(BF16)
    assert out.shape == (d, _C_END)
    return out


def kernel(x, c, ctx, c_ctx, w_mod, b_mod, norm1_g, norm2_g, w_in, ssd_conv_w, ssd_conv_b, ssd_a_log, ssd_dt_bias, ssd_d, ssd_norm_g, diff_qn_g, diff_kn_g, diff_lam_q1, diff_lam_k1, diff_lam_q2, diff_lam_k2, diff_subln_g, gqa_qn_g, gqa_kn_g, s5_lam_re, s5_lam_im, s5_log_dt, s5_b_re, s5_b_im, s5_c_re, s5_c_im, s5_d, s5_glu_w, s5_glu_b, w_gate, w_br_ssd, w_br_diff, w_br_gqa, w_br_s5, w_out, ffn_w_gate_up, ffn_w_down):
    b, seq_len, d = x.shape
    ctx_len = ctx.shape[1]
    depth = w_mod.shape[0]
    assert b <= S5_ROWS and b + 1 <= 8
    assert ctx_len % ROW_TILE == 0 and seq_len % ROW_TILE == 0
    n_ctx_tiles = ctx_len // ROW_TILE

    h = jnp.concatenate([ctx, x], axis=1)
    cc = jnp.concatenate([c, c_ctx[None], jnp.zeros((8 - b - 1, d), F32)], axis=0)
    mods = _mod_call(cc, w_mod, b_mod)

    rope_tab = _rope_tables(ctx_len, seq_len)
    blk = jnp.arange(512) // HEAD_DIM
    ones = (blk[:, None] == blk[None, :]).astype(BF16)
    tile8 = lambda g: jnp.tile(g.astype(F32), 512 // HEAD_DIM)

    for layer in range(depth):
        m = mods[layer]
        lat = m[:b].reshape(b, 6, d)
        cmod = jnp.broadcast_to(m[b].reshape(1, 6, d), (b, 6, d))
        modv = jnp.stack([cmod, lat], axis=1)

        gains = jnp.stack([tile8(diff_qn_g[layer]), tile8(diff_kn_g[layer]),
                           tile8(gqa_qn_g[layer]), tile8(gqa_kn_g[layer])]
                          + [jnp.zeros((512,), F32)] * 4, axis=0)
        (xn, z, xbc, dt, qd, kd, vd, qg, kg, vg, u) = _in_proj_call(
            h, modv, norm1_g[layer][None], _w_in_layout(w_in[layer]), ones, gains, rope_tab,
            n_ctx_tiles)

        conv_w = jnp.pad(ssd_conv_w[layer].astype(F32), ((0, 8 - SSD_CONV_W), (0, 0)))
        xbc_act = _conv_call(xbc, conv_w, ssd_conv_b[layer][None].astype(F32), n_ctx_tiles)
        pad16 = lambda v: jnp.pad(v.reshape(1, 16).astype(F32), ((0, 0), (0, LANES - 16)))
        aneg = pad16(-jnp.exp(ssd_a_log[layer].astype(F32)))
        dtb = pad16(ssd_dt_bias[layer])
        dsk = jnp.repeat(ssd_d[layer].astype(F32), SSD_HEAD_DIM)[None]
        ya = _ssd_call(xbc_act, dt, z, aneg, dtb, dsk, ssd_norm_g[layer][None].astype(F32),
                       ctx_len // SSD_CHUNK)

        lam_init = 0.8 - 0.6 * math.exp(-0.3 * layer)
        lamv = jnp.pad(jnp.stack([diff_lam_q1[layer], diff_lam_k1[layer],
                                  diff_lam_q2[layer], diff_lam_k2[layer]]).astype(F32),
                       ((0, 0), (0, LANES - HEAD_DIM)))
        yb = _diff_attn_call(qd, kd, vd, lamv, diff_subln_g[layer][None].astype(F32), lam_init, ctx_len)

        yc = _gqa_attn_call(qg, kg, vg, ctx_len)

        s5w = _s5_weights(s5_lam_re[layer], s5_lam_im[layer], s5_log_dt[layer],
                          s5_b_re[layer], s5_b_im[layer], s5_c_re[layer], s5_c_im[layer])
        y5 = _s5_from_groups(_s5_call(_s5_to_groups(u, b), *s5w, ctx_len // S5_CHUNK), b)

        wts = (w_gate[layer].astype(BF16), w_br_ssd[layer].astype(BF16), w_br_diff[layer].astype(BF16),
               w_br_gqa[layer].astype(BF16), w_br_s5[layer].astype(BF16), s5_glu_w[layer].astype(BF16),
               s5_glu_b[layer][None].astype(F32), s5_d[layer][None].astype(F32), w_out[layer].astype(BF16))
        h = _merge_call(h, modv, xn, ya, yb, yc, u, y5, wts, n_ctx_tiles)
        h = _ffn_call(h, modv, norm2_g[layer][None].astype(F32), ffn_w_gate_up[layer].astype(BF16),
                      ffn_w_down[layer].astype(BF16), n_ctx_tiles)
    return h[:, ctx_len:, :]
```

```python
import functools
import math

import jax
import jax.numpy as jnp
from jax import lax
from jax.experimental import pallas as pl
from jax.experimental.pallas import tpu as pltpu

F32 = jnp.float32
BF16 = jnp.bfloat16
HIGHEST = lax.Precision.HIGHEST

D_MODEL = 1024
GRID_W = 64
ROPE_THETA = 10000.0
NORM_EPS = 1e-6

SSD_INNER = 512
SSD_HEADS = 8
SSD_HEAD_DIM = 64
SSD_STATE = 64
SSD_CHUNK = 128
SSD_CONV_W = 5
SSD_CONV_CH = 768
SSD_DT_PAD = 128

DIFF_HEADS = 4
HEAD_DIM = 64
GQA_HEADS = 8
GQA_KV_HEADS = 2

S5_GROUP_CH = 16
S5_STATE = 64
S5_WIDTH = 384
S5_GROUPS = 24
S5_CHUNK = 64
S5_ROWS = 8

FFN_HIDDEN = 2816

ROW_TILE = 256
LANES = 128
VMEM_LIMIT = 56 * 1024 * 1024

_C_Z, _C_XBC, _C_DT = 0, 512, 1280
_C_QD, _C_KD, _C_VD = 1408, 1920, 2432
_C_QG, _C_KG, _C_VG = 2944, 3456, 3584
_C_U, _C_END = 3712, 4096


def _params(n_grid):
    return pltpu.CompilerParams(dimension_semantics=("arbitrary",) * n_grid,
                                vmem_limit_bytes=VMEM_LIMIT)


def _bdot(a, b):
    return jnp.dot(a.astype(BF16), b.astype(BF16), preferred_element_type=F32)


def _hdot(a, b):
    return jnp.dot(a, b, precision=HIGHEST, preferred_element_type=F32)


def _rms(x, g):
    return x * lax.rsqrt(jnp.mean(x * x, axis=-1, keepdims=True) + NORM_EPS) * g


def _sigmoid(x):
    return 1.0 / (1.0 + jnp.exp(-x))


def _silu(x):
    return x * _sigmoid(x)


def _mod_body(c_ref, w_ref, b_ref, o_ref):
    o_ref[...] = _hdot(_silu(c_ref[...]), w_ref[...]) + b_ref[...]


def _mod_call(cc, w_mod, b_mod):
    depth = w_mod.shape[0]
    d = D_MODEL
    return pl.pallas_call(
        _mod_body,
        grid=(depth, 6),
        in_specs=[pl.BlockSpec((8, d), lambda l, j: (0, 0)),
                  pl.BlockSpec((None, d, d), lambda l, j: (l, 0, j)),
                  pl.BlockSpec((None, 1, d), lambda l, j: (l, 0, j))],
        out_specs=pl.BlockSpec((None, 8, d), lambda l, j: (l, 0, j)),
        out_shape=jax.ShapeDtypeStruct((depth, 8, 6 * d), F32),
        compiler_params=_params(2),
        name="adaln_mod",
    )(cc, w_mod, b_mod.reshape(depth, 1, 6 * d))


def _head_rms(t, gain, ones):
    t2 = t * t
    hi = t2.astype(BF16)
    lo = (t2 - hi.astype(F32)).astype(BF16)
    ss = (jnp.dot(hi, ones, preferred_element_type=F32)
          + jnp.dot(lo, ones, preferred_element_type=F32))
    return t * lax.rsqrt(ss * (1.0 / HEAD_DIM) + NORM_EPS) * gain


def _rope(t, cos, sin_a, sin_b):
    w = t.shape[-1]
    return t * cos + pltpu.roll(t, w - 16, 1) * sin_a + pltpu.roll(t, 16, 1) * sin_b


def _in_proj_body(h_ref, mod_ref, g_ref, w_ref, ones_ref, gains_ref, rope_ref,
                  xn_ref, z_ref, xbc_ref, dt_ref, qd_ref, kd_ref, vd_ref,
                  qg_ref, kg_ref, vg_ref, u_ref):
    x = h_ref[...]
    mod = mod_ref[...]
    xn = _rms(x, g_ref[...]) * (1.0 + mod[1:2]) + mod[0:1]
    xb = xn.astype(BF16)
    xn_ref[...] = xb

    def proj(lo, hi):
        return jnp.dot(xb, w_ref[:, lo:hi], preferred_element_type=F32)

    z_ref[...] = proj(_C_Z, _C_XBC)
    xbc_ref[...] = proj(_C_XBC, _C_DT)
    dt_ref[...] = proj(_C_DT, _C_QD)
    u_ref[...] = proj(_C_U, _C_END)

    rope = rope_ref[...]
    cos1, sa1, sb1 = rope[:, 0:128], rope[:, 128:256], rope[:, 256:384]
    cos4 = jnp.concatenate([cos1] * 4, axis=1)
    sa4 = jnp.concatenate([sa1] * 4, axis=1)
    sb4 = jnp.concatenate([sb1] * 4, axis=1)
    ones = ones_ref[...]
    gains = gains_ref[...]
    lane = lax.broadcasted_iota(jnp.int32, (1, LANES), 1)
    lo_half = lane < HEAD_DIM
    scale = HEAD_DIM ** -0.5 * math.log2(math.e)

    qd = _rope(_head_rms(proj(_C_QD, _C_KD), gains[0:1], ones), cos4, sa4, sb4) * scale
    kd = _rope(_head_rms(proj(_C_KD, _C_VD), gains[1:2], ones), cos4, sa4, sb4)
    kd_ref[...] = kd.astype(BF16)
    vd_ref[...] = proj(_C_VD, _C_QG).astype(BF16)
    for h in range(DIFF_HEADS):
        blk = qd[:, h * LANES:(h + 1) * LANES]
        qd_ref[h, 0] = jnp.where(lo_half, blk, 0.0).astype(BF16)
        qd_ref[h, 1] = jnp.where(lo_half, 0.0, blk).astype(BF16)

    qg = _rope(_head_rms(proj(_C_QG, _C_KG), gains[2:3], ones), cos4, sa4, sb4) * scale
    kg = _rope(_head_rms(proj(_C_KG, _C_VG), gains[3:4, :LANES], ones[:LANES, :LANES]),
               cos1, sa1, sb1)
    kg_ref[...] = kg.astype(BF16)
    vg_ref[...] = proj(_C_VG, _C_U).astype(BF16)
    qg_up = pltpu.roll(qg, 4 * LANES - HEAD_DIM, 1)
    qg_dn = pltpu.roll(qg, HEAD_DIM, 1)
    per_kv = GQA_HEADS // GQA_KV_HEADS
    for n in range(GQA_KV_HEADS):
        for i in range(per_kv):
            j = n * per_kv + i
            blk_idx = j // 2
            if (j % 2) == n:
                src = qg
            else:
                src = qg_up if n == 0 else qg_dn
            blk = src[:, blk_idx * LANES:(blk_idx + 1) * LANES]
            keep = lo_half if n == 0 else jnp.logical_not(lo_half)
            qg_ref[n, i] = jnp.where(keep, blk, 0.0).astype(BF16)


def _in_proj_call(h, modv, g1, w_cat, ones, gains, rope_tab, n_ctx_tiles):
    b, t, d = h.shape
    nt = t // ROW_TILE
    tm = ROW_TILE

    def row(width, dtype):
        return (pl.BlockSpec((None, tm, width), lambda bi, ti: (bi, ti, 0)),
                jax.ShapeDtypeStruct((b, t, width), dtype))

    outs = [row(d, BF16), row(512, F32), row(768, F32), row(SSD_DT_PAD, F32)]
    qd = (pl.BlockSpec((None, DIFF_HEADS, 2, tm, LANES), lambda bi, ti: (bi, 0, 0, ti, 0)),
          jax.ShapeDtypeStruct((b, DIFF_HEADS, 2, t, LANES), BF16))
    qg = (pl.BlockSpec((None, GQA_KV_HEADS, 4, tm, LANES), lambda bi, ti: (bi, 0, 0, ti, 0)),
          jax.ShapeDtypeStruct((b, GQA_KV_HEADS, 4, t, LANES), BF16))
    outs += [qd, row(512, BF16), row(512, BF16), qg, row(LANES, BF16), row(LANES, BF16),
             row(S5_WIDTH, F32)]
    return pl.pallas_call(
        _in_proj_body,
        grid=(b, nt),
        in_specs=[
            pl.BlockSpec((None, tm, d), lambda bi, ti: (bi, ti, 0)),
            pl.BlockSpec((None, None, 6, d),
                         lambda bi, ti: (bi, jnp.where(ti >= n_ctx_tiles, 1, 0), 0, 0)),
            pl.BlockSpec((1, d), lambda bi, ti: (0, 0)),
            pl.BlockSpec((d, _C_END), lambda bi, ti: (0, 0)),
            pl.BlockSpec((512, 512), lambda bi, ti: (0, 0)),
            pl.BlockSpec((8, 512), lambda bi, ti: (0, 0)),
            pl.BlockSpec((tm, 384), lambda bi, ti: (ti, 0)),
        ],
        out_specs=[o[0] for o in outs],
        out_shape=[o[1] for o in outs],
        compiler_params=_params(2),
        name="in_proj",
    )(h, modv, g1, w_cat, ones, gains, rope_tab)


def _conv_body(cur_ref, prev_ref, next_ref, w_ref, b_ref, o_ref, buf_ref, *, n_ctx_tiles, n_tiles):
    t = pl.program_id(1)
    tm = ROW_TILE
    has_prev = jnp.logical_and(t != 0, t != n_ctx_tiles)
    has_next = jnp.logical_and(t != n_ctx_tiles - 1, t != n_tiles - 1)
    buf_ref[0:8, :] = jnp.where(has_prev, prev_ref[...], 0.0)
    buf_ref[8:8 + tm, :] = cur_ref[...]
    buf_ref[8 + tm:16 + tm, :] = jnp.where(has_next, next_ref[...], 0.0)
    w = w_ref[...]
    bias = b_ref[...]
    pad = (SSD_CONV_W - 1) // 2
    rows = 64
    for r in range(tm // rows):
        acc = bias + w[0:1] * buf_ref[8 - pad + r * rows:8 - pad + (r + 1) * rows, :]
        for k in range(1, SSD_CONV_W):
            lo = 8 - pad + k + r * rows
            acc = acc + w[k:k + 1] * buf_ref[lo:lo + rows, :]
        o_ref[r * rows:(r + 1) * rows, :] = _silu(acc)


def _conv_call(xbc, conv_w, conv_b, n_ctx_tiles):
    b, t, ch = xbc.shape
    tm = ROW_TILE
    nt = t // tm
    per = tm // 8
    last8 = t // 8 - 1
    return pl.pallas_call(
        functools.partial(_conv_body, n_ctx_tiles=n_ctx_tiles, n_tiles=nt),
        grid=(b, nt),
        in_specs=[
            pl.BlockSpec((None, tm, ch), lambda bi, ti: (bi, ti, 0)),
            pl.BlockSpec((None, 8, ch), lambda bi, ti: (bi, jnp.maximum(ti * per - 1, 0), 0)),
            pl.BlockSpec((None, 8, ch), lambda bi, ti: (bi, jnp.minimum((ti + 1) * per, last8), 0)),
            pl.BlockSpec((8, ch), lambda bi, ti: (0, 0)),
            pl.BlockSpec((1, ch), lambda bi, ti: (0, 0)),
        ],
        out_specs=pl.BlockSpec((None, tm, ch), lambda bi, ti: (bi, ti, 0)),
        out_shape=jax.ShapeDtypeStruct((b, t, ch), F32),
        scratch_shapes=[pltpu.VMEM((tm + 16, ch), F32)],
        compiler_params=_params(2),
        name="ssd_conv",
    )(xbc, xbc, xbc, conv_w, conv_b)


def _ssd_chunk_index(d, s, n_ctx_chunks, n_chunks):
    bwd = jnp.where(s < n_ctx_chunks, n_ctx_chunks - 1 - s, n_chunks - 1 - (s - n_ctx_chunks))
    return jnp.where(d == 0, s, bwd)


def _ssd_body(xbc_ref, dt_ref, z_ref, aneg_ref, dtb_ref, dsk_ref, ng_ref, o_ref,
              hst_ref, ysc_ref, *, n_ctx_chunks, n_chunks):
    d = pl.program_id(1)
    s = pl.program_id(2)
    c = _ssd_chunk_index(d, s, n_ctx_chunks, n_chunks)
    q = SSD_CHUNK

    @pl.when(s == 0)
    def _():
        hst_ref[...] = jnp.zeros_like(hst_ref)

    def run(direction):
        xbc = xbc_ref[...]
        x = xbc[:, 0:512]
        bm = xbc[:, 512:640]
        cm = xbc[:, 640:768]
        pre = dt_ref[...] + dtb_ref[...]
        dtv = jnp.maximum(pre, 0.0) + jnp.log(1.0 + jnp.exp(-jnp.abs(pre)))
        a = dtv * aneg_ref[...]
        row = lax.broadcasted_iota(jnp.int32, (q, q), 0)
        col = lax.broadcasted_iota(jnp.int32, (q, q), 1)
        tri = (row >= col) if direction == 0 else (row <= col)
        acs = _hdot(tri.astype(F32), a)
        acs_t = acs.T
        last = acs[q - 1:q, :] if direction == 0 else acs[0:1, :]
        dte = jnp.exp(last - acs)
        expa = jnp.exp(acs)
        cdec = jnp.broadcast_to(jnp.exp(last), (8, LANES))
        erow = lax.broadcasted_iota(jnp.int32, (LANES, SSD_INNER), 0)
        ecol = lax.broadcasted_iota(jnp.int32, (LANES, SSD_INNER), 1)
        expand = (erow == ecol // SSD_HEAD_DIM + SSD_HEADS * direction).astype(F32)
        spread = _hdot(jnp.concatenate([dtv, dtv * dte, expa, cdec], axis=0), expand)
        xdt = x * spread[0:q]
        xs = x * spread[q:2 * q]
        expa_x = spread[2 * q:3 * q]
        cdec_x = spread[3 * q:3 * q + 1]
        bt = bm.T.astype(BF16)
        bb = bm.astype(BF16)
        lane = lax.broadcasted_iota(jnp.int32, (1, LANES), 1)
        ys = []
        for g in range(2):
            cg = jnp.where(lane // SSD_STATE == g, cm, 0.0).astype(BF16)
            gmat = lax.dot_general(cg, bb, (((1,), (1,)), ((), ())), preferred_element_type=F32)
            for hp in range(2):
                p = g * 2 + hp
                sl = slice(p * LANES, (p + 1) * LANES)
                ypair = None
                for e in range(2):
                    j = SSD_HEADS * direction + 2 * p + e
                    seg = acs[:, j:j + 1] - acs_t[j:j + 1, :]
                    dec = jnp.exp(jnp.where(tri, seg, -jnp.inf))
                    xh = jnp.where(lane // SSD_HEAD_DIM == e, xdt[:, sl], 0.0)
                    term = _bdot(gmat * dec, xh)
                    ypair = term if ypair is None else ypair + term
                hprev = hst_ref[p]
                ypair = ypair + _bdot(cg, hprev) * expa_x[:, sl]
                hst_ref[p] = hprev * cdec_x[:, sl] + jnp.dot(
                    bt, xs[:, sl].astype(BF16), preferred_element_type=F32)
                ys.append(ypair)
        return x, ys

    @pl.when(d == 0)
    def _():
        _, ys = run(0)
        for p in range(4):
            ysc_ref[c, :, p * LANES:(p + 1) * LANES] = ys[p]

    @pl.when(d == 1)
    def _():
        x, ys = run(1)
        y = jnp.concatenate(ys, axis=1) + ysc_ref[c] + dsk_ref[...] * x
        y = y * _silu(z_ref[...])
        o_ref[...] = _rms(y, ng_ref[...]).astype(o_ref.dtype)


def _ssd_call(xbc_act, dt, z, aneg, dtb, dsk, ng, n_ctx_chunks):
    b, t, _ = xbc_act.shape
    q = SSD_CHUNK
    nc = t // q
    cidx = functools.partial(_ssd_chunk_index, n_ctx_chunks=n_ctx_chunks, n_chunks=nc)

    def late(bi, di, si):
        return (bi, jnp.where(di == 0, n_ctx_chunks - 1, cidx(di, si)), 0)

    return pl.pallas_call(
        functools.partial(_ssd_body, n_ctx_chunks=n_ctx_chunks, n_chunks=nc),
        grid=(b, 2, nc),
        in_specs=[
            pl.BlockSpec((None, q, SSD_CONV_CH), lambda bi, di, si: (bi, cidx(di, si), 0)),
            pl.BlockSpec((None, q, SSD_DT_PAD), lambda bi, di, si: (bi, cidx(di, si), 0)),
            pl.BlockSpec((None, q, SSD_INNER), late),
            pl.BlockSpec((1, LANES), lambda bi, di, si: (0, 0)),
            pl.BlockSpec((1, LANES), lambda bi, di, si: (0, 0)),
            pl.BlockSpec((1, SSD_INNER), lambda bi, di, si: (0, 0)),
            pl.BlockSpec((1, SSD_INNER), lambda bi, di, si: (0, 0)),
        ],
        out_specs=pl.BlockSpec((None, q, SSD_INNER), late),
        out_shape=jax.ShapeDtypeStruct((b, t, SSD_INNER), BF16),
        scratch_shapes=[pltpu.VMEM((4, LANES, LANES), F32),
                        pltpu.VMEM((nc, q, SSD_INNER), F32)],
        compiler_params=_params(3),
        name="ssd_scan",
    )(xbc_act, dt, z, aneg, dtb, dsk, ng)


def _flash(q, k_ref, v_ref, m_ref, l_ref, acc_ref, *, tk, n_kv, n_ctx_kv, ctx_len, is_ctx_tile):
    r = q.shape[0]
    m_ref[...] = jnp.full(m_ref.shape, -jnp.inf, F32)
    l_ref[...] = jnp.zeros(l_ref.shape, F32)
    acc_ref[...] = jnp.zeros(acc_ref.shape, F32)
    reps = tk // LANES

    def step(kc, masked):
        start = kc * tk if isinstance(kc, int) else pl.multiple_of(kc * tk, tk)
        k = k_ref[pl.ds(start, tk), :]
        v = v_ref[pl.ds(start, tk), :]
        s = lax.dot_general(q, k, (((1,), (1,)), ((), ())), preferred_element_type=F32)
        if masked:
            col = lax.broadcasted_iota(jnp.int32, (1, tk), 1) + kc * tk
            s = jnp.where(col < ctx_len, s, -jnp.inf)
        m_old = m_ref[...]
        m_new = jnp.maximum(m_old, jnp.max(s, axis=-1, keepdims=True))
        alpha = jnp.exp2(m_old - m_new)
        p = jnp.exp2(s - jnp.concatenate([m_new] * reps, axis=1))
        psum = p[:, 0:LANES]
        for i in range(1, reps):
            psum = psum + p[:, i * LANES:(i + 1) * LANES]
        l_ref[...] = alpha * l_ref[...] + psum
        acc_ref[...] = alpha * acc_ref[...] + jnp.dot(p.astype(BF16), v, preferred_element_type=F32)
        m_ref[...] = m_new

    @pl.when(is_ctx_tile)
    def _():
        for kc in range(n_ctx_kv):
            step(kc, True)

    @pl.when(jnp.logical_not(is_ctx_tile))
    def _():
        def body(kc, carry):
            step(kc, False)
            return carry
        lax.fori_loop(0, n_kv, body, 0, unroll=True)

    inv_l = 1.0 / jnp.sum(l_ref[...], axis=-1, keepdims=True)
    return acc_ref[...] * inv_l


def _diff_attn_body(q_ref, k_ref, v_ref, lam_ref, g_ref, o_ref, m_ref, l_ref, acc_ref,
                    *, tq, lam_init, **kw):
    qi = pl.program_id(2)
    q = q_ref[...].reshape(2 * tq, LANES)
    o = _flash(q, k_ref, v_ref, m_ref, l_ref, acc_ref, is_ctx_tile=qi < kw.pop("n_ctx_q"), **kw)
    lv = lam_ref[...]
    lam = (jnp.exp(jnp.sum(lv[0:1] * lv[1:2], axis=-1, keepdims=True))
           - jnp.exp(jnp.sum(lv[2:3] * lv[3:4], axis=-1, keepdims=True)) + lam_init)
    out = o[0:tq] - lam * o[tq:2 * tq]
    o_ref[...] = (_rms(out, g_ref[...]) * (1.0 - lam_init)).astype(o_ref.dtype)


def _gqa_attn_body(q_ref, k_ref, v_ref, o_ref, m_ref, l_ref, acc_ref, *, tq, **kw):
    n = pl.program_id(1)
    qi = pl.program_id(2)
    q = q_ref[...].reshape(4 * tq, LANES)
    o = _flash(q, k_ref, v_ref, m_ref, l_ref, acc_ref, is_ctx_tile=qi < kw.pop("n_ctx_q"), **kw)
    lane = lax.broadcasted_iota(jnp.int32, (1, LANES), 1)
    lo_half = lane < HEAD_DIM
    for pair in range(2):
        a = o[(2 * pair) * tq:(2 * pair + 1) * tq]
        b = o[(2 * pair + 1) * tq:(2 * pair + 2) * tq]
        from_lo = jnp.where(lo_half, a, pltpu.roll(b, HEAD_DIM, 1))
        from_hi = jnp.where(lo_half, pltpu.roll(a, HEAD_DIM, 1), b)
        o_ref[:, pair * LANES:(pair + 1) * LANES] = jnp.where(n == 0, from_lo, from_hi).astype(o_ref.dtype)


def _attn_tiles(t, ctx_len):
    tq = ROW_TILE
    tk = next(c for c in (2816, 768, 256) if t % c == 0)
    assert ctx_len % tq == 0 and t % tq == 0 and t % tk == 0
    return dict(tq=tq, tk=tk, n_kv=t // tk, n_ctx_kv=-(-ctx_len // tk), ctx_len=ctx_len,
                n_ctx_q=ctx_len // tq)


def _diff_attn_call(qd, kd, vd, lamv, subln_g, lam_init, ctx_len):
    b, _, _, t, _ = qd.shape
    cfg = _attn_tiles(t, ctx_len)
    tq = cfg["tq"]
    r = 2 * tq
    return pl.pallas_call(
        functools.partial(_diff_attn_body, lam_init=lam_init, **cfg),
        grid=(b, DIFF_HEADS, t // tq),
        in_specs=[
            pl.BlockSpec((None, None, 2, tq, LANES), lambda bi, hi, qi: (bi, hi, 0, qi, 0)),
            pl.BlockSpec((None, t, LANES), lambda bi, hi, qi: (bi, 0, hi)),
            pl.BlockSpec((None, t, LANES), lambda bi, hi, qi: (bi, 0, hi)),
            pl.BlockSpec((4, LANES), lambda bi, hi, qi: (0, 0)),
            pl.BlockSpec((1, LANES), lambda bi, hi, qi: (0, 0)),
        ],
        out_specs=pl.BlockSpec((None, tq, LANES), lambda bi, hi, qi: (bi, qi, hi)),
        out_shape=jax.ShapeDtypeStruct((b, t, DIFF_HEADS * LANES), BF16),
        scratch_shapes=[pltpu.VMEM((r, LANES), F32)] * 3,
        compiler_params=_params(3),
        name="diff_attention",
    )(qd, kd, vd, lamv, subln_g)


def _gqa_attn_call(qg, kg, vg, ctx_len):
    b, _, _, t, _ = qg.shape
    cfg = _attn_tiles(t, ctx_len)
    tq = cfg["tq"]
    r = 4 * tq
    return pl.pallas_call(
        functools.partial(_gqa_attn_body, **cfg),
        grid=(b, GQA_KV_HEADS, t // tq),
        in_specs=[
            pl.BlockSpec((None, None, 4, tq, LANES), lambda bi, ni, qi: (bi, ni, 0, qi, 0)),
            pl.BlockSpec((None, t, LANES), lambda bi, ni, qi: (bi, 0, 0)),
            pl.BlockSpec((None, t, LANES), lambda bi, ni, qi: (bi, 0, 0)),
        ],
        out_specs=pl.BlockSpec((None, tq, 2 * LANES), lambda bi, ni, qi: (bi, qi, ni)),
        out_shape=jax.ShapeDtypeStruct((b, t, GQA_HEADS * HEAD_DIM), BF16),
        scratch_shapes=[pltpu.VMEM((r, LANES), F32)] * 3,
        compiler_params=_params(3),
        name="gqa_attention",
    )(qg, kg, vg)


def _s5_weights(lam_re, lam_im, log_dt, b_re, b_im, c_re, c_im):
    tc = S5_CHUNK
    ks = jnp.arange(tc + 1, dtype=F32)
    w_intra = 0.0
    w_state, w_out, avec = [], [], []
    s_idx = jnp.arange(tc)[:, None]
    t_idx = jnp.arange(tc)[None, :]
    br, bi = b_re.astype(F32), b_im.astype(F32)
    cr, ci = c_re.astype(F32), c_im.astype(F32)
    for direction in range(2):
        lr, li = lam_re[direction].astype(F32), lam_im[direction].astype(F32)
        step = jnp.exp(log_dt[direction].astype(F32))[:, None]
        mag = jnp.exp(lr * step)
        ar, ai = mag * jnp.cos(li * step), mag * jnp.sin(li * step)
        den = lr * lr + li * li
        fr = ((ar - 1.0) * lr + ai * li) / den
        fi = (ai * lr - (ar - 1.0) * li) / den
        bbr = fr[..., None] * br - fi[..., None] * bi
        bbi = fr[..., None] * bi + fi[..., None] * br
        pmag = jnp.exp(ks[:, None, None] * (lr * step)[None])
        pang = ks[:, None, None] * (li * step)[None]
        er, ei = pmag * jnp.cos(pang), pmag * jnp.sin(pang)
        cer = cr[None] * er[:, :, None, :] - ci[None] * ei[:, :, None, :]
        cei = cr[None] * ei[:, :, None, :] + ci[None] * er[:, :, None, :]
        kern = (jnp.einsum('kgcp,gpd->kgcd', cer, bbr, precision=HIGHEST)
                - jnp.einsum('kgcp,gpd->kgcd', cei, bbi, precision=HIGHEST))
        lag = (t_idx - s_idx) if direction == 0 else (s_idx - t_idx)
        blk = kern[jnp.clip(lag, 0, tc - 1)]
        blk = jnp.where((lag >= 0)[:, :, None, None, None], blk, 0.0)
        w_intra = w_intra + jnp.transpose(blk, (2, 0, 4, 1, 3)).reshape(
            S5_GROUPS, tc * S5_GROUP_CH, tc * S5_GROUP_CH)
        spow = (tc - 1 - jnp.arange(tc)) if direction == 0 else jnp.arange(tc)
        esr, esi = er[spow], ei[spow]
        wsr = esr[:, :, :, None] * bbr[None] - esi[:, :, :, None] * bbi[None]
        wsi = esr[:, :, :, None] * bbi[None] + esi[:, :, :, None] * bbr[None]
        flat = lambda m: jnp.transpose(m, (1, 0, 3, 2)).reshape(S5_GROUPS, tc * S5_GROUP_CH, S5_STATE)
        w_state.append((flat(wsr), flat(wsi)))
        opow = (jnp.arange(tc) + 1) if direction == 0 else (tc - jnp.arange(tc))
        oer, oei = cer[opow], cei[opow]
        oflat = lambda m: jnp.transpose(m, (1, 3, 0, 2)).reshape(S5_GROUPS, S5_STATE, tc * S5_GROUP_CH)
        w_out.append((oflat(oer), -oflat(oei)))
        atr, ati = er[tc], ei[tc]
        avec += [jnp.concatenate([atr, atr], -1), jnp.concatenate([-ati, ati], -1),
                 jnp.concatenate([ati, -ati], -1)]
    (fsr, fsi), (bsr, bsi) = w_state
    w_state = jnp.concatenate([fsr, fsi, bsr, bsi, fsi, fsr, bsi, bsr], axis=-1)
    (fo_r, fo_i), (bo_r, bo_i) = w_out
    w_out = jnp.concatenate([fo_r, fo_i, bo_r, bo_i], axis=1)
    zero = jnp.zeros_like(avec[0])
    avec = jnp.stack(avec + [zero, zero], axis=1)
    return w_intra.astype(BF16), w_state.astype(BF16), w_out.astype(BF16), avec


def _s5_body(u_ref, wi_ref, ws_ref, wo_ref, av_ref, y_ref, s_ref, h_ref, *, n_ctx_chunks, n_chunks):
    u = u_ref[...]
    s_ref[...] = jnp.dot(u, ws_ref[...], preferred_element_type=F32)
    av = av_ref[...]
    rows = S5_ROWS
    nj = n_chunks

    def bcast(i):
        return jnp.broadcast_to(av[i:i + 1], (rows, LANES))

    a1f, a2f, a2sf, a1b, a2b, a2sb = [bcast(i) for i in range(6)]

    def body(i, carry):
        hf, hfs, hb, hbs = carry
        jf = pl.multiple_of(i * rows, rows)
        jb_chunk = jnp.where(i < n_ctx_chunks, n_ctx_chunks - 1 - i, nj - 1 - (i - n_ctx_chunks))
        jb = pl.multiple_of(jb_chunk * rows, rows)
        h_ref[pl.ds(jf, rows), 0:LANES] = hf
        h_ref[pl.ds(jb, rows), LANES:2 * LANES] = hb
        sf = s_ref[pl.ds(jf, rows), 0:LANES]
        sfs = s_ref[pl.ds(jf, rows), 2 * LANES:3 * LANES]
        sb = s_ref[pl.ds(jb, rows), LANES:2 * LANES]
        sbs = s_ref[pl.ds(jb, rows), 3 * LANES:4 * LANES]
        return (a1f * hf + a2f * hfs + sf, a1f * hfs + a2sf * hf + sfs,
                a1b * hb + a2b * hbs + sb, a1b * hbs + a2sb * hb + sbs)

    zero = jnp.zeros((rows, LANES), F32)
    lax.fori_loop(0, nj, body, (zero, zero, zero, zero))
    y_ref[...] = (jnp.dot(u, wi_ref[...], preferred_element_type=F32)
                  + jnp.dot(h_ref[...].astype(BF16), wo_ref[...], preferred_element_type=F32))


def _s5_call(ug, w_intra, w_state, w_out, avec, n_ctx_chunks):
    g, r, w = ug.shape
    nj = r // S5_ROWS
    return pl.pallas_call(
        functools.partial(_s5_body, n_ctx_chunks=n_ctx_chunks, n_chunks=nj),
        grid=(g,),
        in_specs=[
            pl.BlockSpec((None, r, w), lambda gi: (gi, 0, 0)),
            pl.BlockSpec((None, w, w), lambda gi: (gi, 0, 0)),
            pl.BlockSpec((None, w, 4 * LANES), lambda gi: (gi, 0, 0)),
            pl.BlockSpec((None, 2 * LANES, w), lambda gi: (gi, 0, 0)),
            pl.BlockSpec((None, 8, LANES), lambda gi: (gi, 0, 0)),
        ],
        out_specs=pl.BlockSpec((None, r, w), lambda gi: (gi, 0, 0)),
        out_shape=jax.ShapeDtypeStruct((g, r, w), F32),
        scratch_shapes=[pltpu.VMEM((r, 4 * LANES), F32), pltpu.VMEM((r, 2 * LANES), F32)],
        compiler_params=_params(1),
        name="s5_scan",
    )(ug, w_intra, w_state, w_out, avec)


def _s5_to_groups(u, b):
    t = u.shape[1]
    nj = t // S5_CHUNK
    x = u.astype(BF16).reshape(b, nj, S5_CHUNK, S5_GROUPS, S5_GROUP_CH)
    x = jnp.transpose(x, (3, 1, 0, 2, 4)).reshape(S5_GROUPS, nj, b, S5_CHUNK * S5_GROUP_CH)
    x = jnp.pad(x, ((0, 0), (0, 0), (0, S5_ROWS - b), (0, 0)))
    return x.reshape(S5_GROUPS, nj * S5_ROWS, S5_CHUNK * S5_GROUP_CH)


def _s5_from_groups(y, b):
    nj = y.shape[1] // S5_ROWS
    x = y.reshape(S5_GROUPS, nj, S5_ROWS, S5_CHUNK, S5_GROUP_CH)[:, :, :b]
    x = jnp.transpose(x, (2, 1, 3, 0, 4))
    return x.reshape(b, nj * S5_CHUNK, S5_WIDTH)


def _merge_body(h_ref, mod_ref, xn_ref, ya_ref, yb_ref, yc_ref, u_ref, y5_ref,
                wg_ref, wa_ref, wb_ref, wc_ref, wd_ref, glw_ref, glb_ref, s5d_ref, wo_ref, o_ref):
    xn = xn_ref[...]
    y5 = y5_ref[...] + s5d_ref[...] * u_ref[...]
    gelu = 0.5 * y5 * (1.0 + jnp.tanh(0.7978845608028654 * (y5 + 0.044715 * y5 * y5 * y5)))
    glu = _bdot(gelu, glw_ref[...]) + glb_ref[...]
    yd = glu[:, 0:S5_WIDTH] * _sigmoid(glu[:, S5_WIDTH:2 * S5_WIDTH])
    branches = ((ya_ref[...], wa_ref), (yb_ref[...], wb_ref), (yc_ref[...], wc_ref),
                (yd.astype(BF16), wd_ref))
    merged = None
    for i, (y, w_ref) in enumerate(branches):
        gate = _sigmoid(jnp.dot(xn, wg_ref[i], preferred_element_type=F32))
        term = gate * jnp.dot(y, w_ref[...], preferred_element_type=F32)
        merged = term if merged is None else merged + term
    out = _bdot(merged, wo_ref[...])
    o_ref[...] = h_ref[...] + mod_ref[2:3, :] * out


def _merge_call(h, modv, xn, ya, yb, yc, u, y5, wts, n_ctx_tiles):
    b, t, d = h.shape
    tm = ROW_TILE

    def row(width):
        return pl.BlockSpec((None, tm, width), lambda bi, ti: (bi, ti, 0))

    def const(shape):
        return pl.BlockSpec(shape, lambda bi, ti: (0,) * len(shape), pipeline_mode=pl.Buffered(1))

    w_gate, w_a, w_b, w_c, w_d, glu_w, glu_b, s5_d, w_out = wts
    return pl.pallas_call(
        _merge_body,
        grid=(b, t // tm),
        in_specs=[row(d),
                  pl.BlockSpec((None, None, 6, d),
                               lambda bi, ti: (bi, jnp.where(ti >= n_ctx_tiles, 1, 0), 0, 0)),
                  row(d), row(512), row(512), row(512), row(S5_WIDTH), row(S5_WIDTH),
                  const((4, d, d)), const((512, d)), const((512, d)), const((512, d)),
                  const((S5_WIDTH, d)), const((S5_WIDTH, 2 * S5_WIDTH)), const((1, 2 * S5_WIDTH)),
                  const((1, S5_WIDTH)), const((d, d))],
        out_specs=row(d),
        out_shape=jax.ShapeDtypeStruct((b, t, d), F32),
        compiler_params=_params(2),
        name="branch_merge",
    )(h, modv, xn, ya, yb, yc, u, y5, w_gate, w_a, w_b, w_c, w_d, glu_w, glu_b, s5_d, w_out)


def _ffn_body(h_ref, mod_ref, g_ref, wgu_ref, wd_ref, o_ref):
    h = h_ref[...]
    mod = mod_ref[...]
    xf = (_rms(h, g_ref[...]) * (1.0 + mod[4:5]) + mod[3:4]).astype(BF16)
    gate = jnp.dot(xf, wgu_ref[:, 0:FFN_HIDDEN], preferred_element_type=F32)
    up = jnp.dot(xf, wgu_ref[:, FFN_HIDDEN:2 * FFN_HIDDEN], preferred_element_type=F32)
    act = (_silu(gate) * up).astype(BF16)
    o_ref[...] = h + mod[5:6] * jnp.dot(act, wd_ref[...], preferred_element_type=F32)


def _ffn_call(h, modv, g2, w_gu, w_down, n_ctx_tiles, latent_only):
    b, t, d = h.shape
    tm = ROW_TILE
    skip = n_ctx_tiles if latent_only else 0

    def const(shape):
        return pl.BlockSpec(shape, lambda bi, ti: (0,) * len(shape), pipeline_mode=pl.Buffered(1))

    return pl.pallas_call(
        _ffn_body,
        grid=(b, t // tm - skip),
        in_specs=[pl.BlockSpec((None, tm, d), lambda bi, ti: (bi, ti + skip, 0)),
                  pl.BlockSpec((None, None, 6, d),
                               lambda bi, ti: (bi, jnp.where(ti + skip >= n_ctx_tiles, 1, 0), 0, 0)),
                  const((1, d)), const((d, 2 * FFN_HIDDEN)), const((FFN_HIDDEN, d))],
        out_specs=pl.BlockSpec((None, tm, d), lambda bi, ti: (bi, ti, 0)),
        out_shape=jax.ShapeDtypeStruct((b, t - skip * tm, d), F32),
        compiler_params=_params(2),
        name="swiglu_ffn",
    )(h, modv, g2, w_gu, w_down)


def _rope_tables(ctx_len, seq_len):
    n_rows = seq_len // GRID_W
    rows = jnp.repeat(jnp.arange(n_rows, dtype=F32), GRID_W)
    cols = jnp.tile(jnp.arange(GRID_W, dtype=F32), n_rows)
    quarter = HEAD_DIM // 4
    inv_freq = ROPE_THETA ** (-jnp.arange(quarter, dtype=F32) / quarter)
    ang_r = rows[:, None] * inv_freq
    ang_c = cols[:, None] * inv_freq
    ang = jnp.concatenate([ang_r, ang_r, ang_c, ang_c], axis=-1)
    cos = jnp.concatenate([jnp.ones((ctx_len, HEAD_DIM), F32), jnp.cos(ang)], axis=0)
    sin = jnp.concatenate([jnp.zeros((ctx_len, HEAD_DIM), F32), jnp.sin(ang)], axis=0)
    first = (jnp.arange(HEAD_DIM) % 32) < 16
    sin_a = jnp.where(first, -sin, 0.0)
    sin_b = jnp.where(first, 0.0, sin)
    two = lambda m: jnp.concatenate([m, m], axis=1)
    return jnp.concatenate([two(cos), two(sin_a), two(sin_b)], axis=1)


def _w_in_layout(w_in):
    d = w_in.shape[0]
    a0 = 0
    z = w_in[:, a0:a0 + 512]
    xbc = w_in[:, a0 + 512:a0 + 1280]
    dt = jnp.pad(w_in[:, a0 + 1280:a0 + 1296], ((0, 0), (0, SSD_DT_PAD - 16)))
    rest = w_in[:, 1296:]
    out = jnp.concatenate([z, xbc, dt, rest], axis=1).astype(BF16)
    assert out.shape == (d, _C_END)
    return out


def kernel(x, c, ctx, c_ctx, w_mod, b_mod, norm1_g, norm2_g, w_in, ssd_conv_w, ssd_conv_b, ssd_a_log, ssd_dt_bias, ssd_d, ssd_norm_g, diff_qn_g, diff_kn_g, diff_lam_q1, diff_lam_k1, diff_lam_q2, diff_lam_k2, diff_subln_g, gqa_qn_g, gqa_kn_g, s5_lam_re, s5_lam_im, s5_log_dt, s5_b_re, s5_b_im, s5_c_re, s5_c_im, s5_d, s5_glu_w, s5_glu_b, w_gate, w_br_ssd, w_br_diff, w_br_gqa, w_br_s5, w_out, ffn_w_gate_up, ffn_w_down):
    b, seq_len, d = x.shape
    ctx_len = ctx.shape[1]
    depth = w_mod.shape[0]
    assert b <= S5_ROWS and b + 1 <= 8
    assert ctx_len % ROW_TILE == 0 and seq_len % ROW_TILE == 0
    n_ctx_tiles = ctx_len // ROW_TILE

    h = jnp.concatenate([ctx, x], axis=1)
    cc = jnp.concatenate([c, c_ctx[None], jnp.zeros((8 - b - 1, d), F32)], axis=0)
    mods = _mod_call(cc, w_mod, b_mod)

    rope_tab = _rope_tables(ctx_len, seq_len)
    blk = jnp.arange(512) // HEAD_DIM
    ones = (blk[:, None] == blk[None, :]).astype(BF16)
    tile8 = lambda g: jnp.tile(g.astype(F32), 512 // HEAD_DIM)

    for layer in range(depth):
        m = mods[layer]
        lat = m[:b].reshape(b, 6, d)
        cmod = jnp.broadcast_to(m[b].reshape(1, 6, d), (b, 6, d))
        modv = jnp.stack([cmod, lat], axis=1)

        gains = jnp.stack([tile8(diff_qn_g[layer]), tile8(diff_kn_g[layer]),
                           tile8(gqa_qn_g[layer]), tile8(gqa_kn_g[layer])]
                          + [jnp.zeros((512,), F32)] * 4, axis=0)
        (xn, z, xbc, dt, qd, kd, vd, qg, kg, vg, u) = _in_proj_call(
            h, modv, norm1_g[layer][None], _w_in_layout(w_in[layer]), ones, gains, rope_tab,
            n_ctx_tiles)

        conv_w = jnp.pad(ssd_conv_w[layer].astype(F32), ((0, 8 - SSD_CONV_W), (0, 0)))
        xbc_act = _conv_call(xbc, conv_w, ssd_conv_b[layer][None].astype(F32), n_ctx_tiles)
        pad16 = lambda v: jnp.pad(v.reshape(1, 16).astype(F32), ((0, 0), (0, LANES - 16)))
        aneg = pad16(-jnp.exp(ssd_a_log[layer].astype(F32)))
        dtb = pad16(ssd_dt_bias[layer])
        dsk = jnp.repeat(ssd_d[layer].astype(F32), SSD_HEAD_DIM)[None]
        ya = _ssd_call(xbc_act, dt, z, aneg, dtb, dsk, ssd_norm_g[layer][None].astype(F32),
                       ctx_len // SSD_CHUNK)

        lam_init = 0.8 - 0.6 * math.exp(-0.3 * layer)
        lamv = jnp.pad(jnp.stack([diff_lam_q1[layer], diff_lam_k1[layer],
                                  diff_lam_q2[layer], diff_lam_k2[layer]]).astype(F32),
                       ((0, 0), (0, LANES - HEAD_DIM)))
        yb = _diff_attn_call(qd, kd, vd, lamv, diff_subln_g[layer][None].astype(F32), lam_init, ctx_len)

        yc = _gqa_attn_call(qg, kg, vg, ctx_len)

        s5w = _s5_weights(s5_lam_re[layer], s5_lam_im[layer], s5_log_dt[layer],
                          s5_b_re[layer], s5_b_im[layer], s5_c_re[layer], s5_c_im[layer])
        y5 = _s5_from_groups(_s5_call(_s5_to_groups(u, b), *s5w, ctx_len // S5_CHUNK), b)

        wts = (w_gate[layer].astype(BF16), w_br_ssd[layer].astype(BF16), w_br_diff[layer].astype(BF16),
               w_br_gqa[layer].astype(BF16), w_br_s5[layer].astype(BF16), s5_glu_w[layer].astype(BF16),
               s5_glu_b[layer][None].astype(F32), s5_d[layer][None].astype(F32), w_out[layer].astype(BF16))
        h = _merge_call(h, modv, xn, ya, yb, yc, u, y5, wts, n_ctx_tiles)
        h = _ffn_call(h, modv, norm2_g[layer][None].astype(F32), ffn_w_gate_up[layer].astype(BF16),
                      ffn_w_down[layer].astype(BF16), n_ctx_tiles, latent_only=layer == depth - 1)
    return h
```

```python
import functools
import math

import jax
import jax.numpy as jnp
from jax import lax
from jax.experimental import pallas as pl
from jax.experimental.pallas import tpu as pltpu

F32 = jnp.float32
BF16 = jnp.bfloat16
HIGHEST = lax.Precision.HIGHEST

D_MODEL = 1024
GRID_W = 64
ROPE_THETA = 10000.0
NORM_EPS = 1e-6

SSD_INNER = 512
SSD_HEADS = 8
SSD_HEAD_DIM = 64
SSD_STATE = 64
SSD_CHUNK = 128
SSD_CONV_W = 5
SSD_CONV_CH = 768
SSD_DT_PAD = 128

DIFF_HEADS = 4
HEAD_DIM = 64
GQA_HEADS = 8
GQA_KV_HEADS = 2

S5_GROUP_CH = 16
S5_STATE = 64
S5_WIDTH = 384
S5_GROUPS = 24
S5_CHUNK = 64
S5_ROWS = 8
S5_FLAT = S5_CHUNK * S5_GROUP_CH

FFN_HIDDEN = 2816

ROW_TILE = 256
LANES = 128
VMEM_LIMIT = 56 * 1024 * 1024

_C_Z, _C_XBC, _C_DT = 0, 512, 1280
_C_QD, _C_KD, _C_VD = 1408, 1920, 2432
_C_QG, _C_KG, _C_VG = 2944, 3456, 3584
_C_U, _C_END = 3712, 4096


def _params(n_grid):
    return pltpu.CompilerParams(dimension_semantics=("arbitrary",) * n_grid,
                                vmem_limit_bytes=VMEM_LIMIT)


def _bdot(a, b):
    return jnp.dot(a.astype(BF16), b.astype(BF16), preferred_element_type=F32)


def _hdot(a, b):
    return jnp.dot(a, b, precision=HIGHEST, preferred_element_type=F32)


def _rms(x, g):
    return x * lax.rsqrt(jnp.mean(x * x, axis=-1, keepdims=True) + NORM_EPS) * g


def _sigmoid(x):
    return 1.0 / (1.0 + jnp.exp(-x))


def _silu(x):
    return x * _sigmoid(x)


def _mod_body(c_ref, w_ref, b_ref, o_ref):
    o_ref[...] = _hdot(_silu(c_ref[...]), w_ref[...]) + b_ref[...]


def _mod_call(cc, w_mod, b_mod):
    depth = w_mod.shape[0]
    d = D_MODEL
    return pl.pallas_call(
        _mod_body,
        grid=(depth, 6),
        in_specs=[pl.BlockSpec((8, d), lambda l, j: (0, 0)),
                  pl.BlockSpec((None, d, d), lambda l, j: (l, 0, j)),
                  pl.BlockSpec((None, 1, d), lambda l, j: (l, 0, j))],
        out_specs=pl.BlockSpec((None, 8, d), lambda l, j: (l, 0, j)),
        out_shape=jax.ShapeDtypeStruct((depth, 8, 6 * d), F32),
        compiler_params=_params(2),
        name="adaln_mod",
    )(cc, w_mod, b_mod.reshape(depth, 1, 6 * d))


def _head_rms(t, gain, ones):
    t2 = t * t
    hi = t2.astype(BF16)
    lo = (t2 - hi.astype(F32)).astype(BF16)
    ss = (jnp.dot(hi, ones, preferred_element_type=F32)
          + jnp.dot(lo, ones, preferred_element_type=F32))
    return t * lax.rsqrt(ss * (1.0 / HEAD_DIM) + NORM_EPS) * gain


def _rope(t, cos, sin_a, sin_b):
    w = t.shape[-1]
    return t * cos + pltpu.roll(t, w - 16, 1) * sin_a + pltpu.roll(t, 16, 1) * sin_b


def _in_proj_body(h_ref, mod_ref, g_ref, w_ref, ones_ref, gains_ref, rope_ref,
                  xn_ref, z_ref, xbc_ref, dt_ref, qd_ref, kd_ref, vd_ref,
                  qg_ref, kg_ref, vg_ref, u_ref):
    x = h_ref[...]
    mod = mod_ref[...]
    xn = _rms(x, g_ref[...]) * (1.0 + mod[1:2]) + mod[0:1]
    xb = xn.astype(BF16)
    xn_ref[...] = xb

    def proj(lo, hi):
        return jnp.dot(xb, w_ref[:, lo:hi], preferred_element_type=F32)

    z_ref[...] = proj(_C_Z, _C_XBC)
    xbc_ref[...] = proj(_C_XBC, _C_DT)
    dt_ref[...] = proj(_C_DT, _C_QD)
    u_ref[...] = proj(_C_U, _C_END)

    rope = rope_ref[...]
    cos1, sa1, sb1 = rope[:, 0:128], rope[:, 128:256], rope[:, 256:384]
    cos4 = jnp.concatenate([cos1] * 4, axis=1)
    sa4 = jnp.concatenate([sa1] * 4, axis=1)
    sb4 = jnp.concatenate([sb1] * 4, axis=1)
    ones = ones_ref[...]
    gains = gains_ref[...]
    lane = lax.broadcasted_iota(jnp.int32, (1, LANES), 1)
    lo_half = lane < HEAD_DIM
    scale = HEAD_DIM ** -0.5 * math.log2(math.e)

    qd = _rope(_head_rms(proj(_C_QD, _C_KD), gains[0:1], ones), cos4, sa4, sb4) * scale
    kd = _rope(_head_rms(proj(_C_KD, _C_VD), gains[1:2], ones), cos4, sa4, sb4)
    kd_ref[...] = kd.astype(BF16)
    vd_ref[...] = proj(_C_VD, _C_QG).astype(BF16)
    for h in range(DIFF_HEADS):
        blk = qd[:, h * LANES:(h + 1) * LANES]
        qd_ref[h, 0] = jnp.where(lo_half, blk, 0.0).astype(BF16)
        qd_ref[h, 1] = jnp.where(lo_half, 0.0, blk).astype(BF16)

    qg = _rope(_head_rms(proj(_C_QG, _C_KG), gains[2:3], ones), cos4, sa4, sb4) * scale
    kg = _rope(_head_rms(proj(_C_KG, _C_VG), gains[3:4, :LANES], ones[:LANES, :LANES]),
               cos1, sa1, sb1)
    kg_ref[...] = kg.astype(BF16)
    vg_ref[...] = proj(_C_VG, _C_U).astype(BF16)
    qg_up = pltpu.roll(qg, 4 * LANES - HEAD_DIM, 1)
    qg_dn = pltpu.roll(qg, HEAD_DIM, 1)
    per_kv = GQA_HEADS // GQA_KV_HEADS
    for n in range(GQA_KV_HEADS):
        for i in range(per_kv):
            j = n * per_kv + i
            blk_idx = j // 2
            if (j % 2) == n:
                src = qg
            else:
                src = qg_up if n == 0 else qg_dn
            blk = src[:, blk_idx * LANES:(blk_idx + 1) * LANES]
            keep = lo_half if n == 0 else jnp.logical_not(lo_half)
            qg_ref[n, i] = jnp.where(keep, blk, 0.0).astype(BF16)


def _in_proj_call(h, modv, g1, w_cat, ones, gains, rope_tab, n_ctx_tiles):
    b, t, d = h.shape
    nt = t // ROW_TILE
    tm = ROW_TILE

    def row(width, dtype):
        return (pl.BlockSpec((None, tm, width), lambda bi, ti: (bi, ti, 0)),
                jax.ShapeDtypeStruct((b, t, width), dtype))

    outs = [row(d, BF16), row(512, F32), row(768, F32), row(SSD_DT_PAD, F32)]
    qd = (pl.BlockSpec((None, DIFF_HEADS, 2, tm, LANES), lambda bi, ti: (bi, 0, 0, ti, 0)),
          jax.ShapeDtypeStruct((b, DIFF_HEADS, 2, t, LANES), BF16))
    qg = (pl.BlockSpec((None, GQA_KV_HEADS, 4, tm, LANES), lambda bi, ti: (bi, 0, 0, ti, 0)),
          jax.ShapeDtypeStruct((b, GQA_KV_HEADS, 4, t, LANES), BF16))
    outs += [qd, row(512, BF16), row(512, BF16), qg, row(LANES, BF16), row(LANES, BF16),
             row(S5_WIDTH, F32)]
    return pl.pallas_call(
        _in_proj_body,
        grid=(b, nt),
        in_specs=[
            pl.BlockSpec((None, tm, d), lambda bi, ti: (bi, ti, 0)),
            pl.BlockSpec((None, None, 6, d),
                         lambda bi, ti: (bi, jnp.where(ti >= n_ctx_tiles, 1, 0), 0, 0)),
            pl.BlockSpec((1, d), lambda bi, ti: (0, 0)),
            pl.BlockSpec((d, _C_END), lambda bi, ti: (0, 0)),
            pl.BlockSpec((512, 512), lambda bi, ti: (0, 0)),
            pl.BlockSpec((8, 512), lambda bi, ti: (0, 0)),
            pl.BlockSpec((tm, 384), lambda bi, ti: (ti, 0)),
        ],
        out_specs=[o[0] for o in outs],
        out_shape=[o[1] for o in outs],
        compiler_params=_params(2),
        name="in_proj",
    )(h, modv, g1, w_cat, ones, gains, rope_tab)


def _conv_body(cur_ref, prev_ref, next_ref, w_ref, b_ref, o_ref, buf_ref, *, n_ctx_tiles, n_tiles):
    t = pl.program_id(1)
    tm = ROW_TILE
    has_prev = jnp.logical_and(t != 0, t != n_ctx_tiles)
    has_next = jnp.logical_and(t != n_ctx_tiles - 1, t != n_tiles - 1)
    buf_ref[0:8, :] = jnp.where(has_prev, prev_ref[...], 0.0)
    buf_ref[8:8 + tm, :] = cur_ref[...]
    buf_ref[8 + tm:16 + tm, :] = jnp.where(has_next, next_ref[...], 0.0)
    w = w_ref[...]
    bias = b_ref[...]
    pad = (SSD_CONV_W - 1) // 2
    rows = 64
    for r in range(tm // rows):
        acc = bias + w[0:1] * buf_ref[8 - pad + r * rows:8 - pad + (r + 1) * rows, :]
        for k in range(1, SSD_CONV_W):
            lo = 8 - pad + k + r * rows
            acc = acc + w[k:k + 1] * buf_ref[lo:lo + rows, :]
        o_ref[r * rows:(r + 1) * rows, :] = _silu(acc)


def _conv_call(xbc, conv_w, conv_b, n_ctx_tiles):
    b, t, ch = xbc.shape
    tm = ROW_TILE
    nt = t // tm
    per = tm // 8
    last8 = t // 8 - 1
    return pl.pallas_call(
        functools.partial(_conv_body, n_ctx_tiles=n_ctx_tiles, n_tiles=nt),
        grid=(b, nt),
        in_specs=[
            pl.BlockSpec((None, tm, ch), lambda bi, ti: (bi, ti, 0)),
            pl.BlockSpec((None, 8, ch), lambda bi, ti: (bi, jnp.maximum(ti * per - 1, 0), 0)),
            pl.BlockSpec((None, 8, ch), lambda bi, ti: (bi, jnp.minimum((ti + 1) * per, last8), 0)),
            pl.BlockSpec((8, ch), lambda bi, ti: (0, 0)),
            pl.BlockSpec((1, ch), lambda bi, ti: (0, 0)),
        ],
        out_specs=pl.BlockSpec((None, tm, ch), lambda bi, ti: (bi, ti, 0)),
        out_shape=jax.ShapeDtypeStruct((b, t, ch), F32),
        scratch_shapes=[pltpu.VMEM((tm + 16, ch), F32)],
        compiler_params=_params(2),
        name="ssd_conv",
    )(xbc, xbc, xbc, conv_w, conv_b)


def _ssd_chunk_index(d, s, n_ctx_chunks, n_chunks):
    bwd = jnp.where(s < n_ctx_chunks, n_ctx_chunks - 1 - s, n_chunks - 1 - (s - n_ctx_chunks))
    return jnp.where(d == 0, s, bwd)


def _ssd_body(xbc_ref, dt_ref, z_ref, aneg_ref, dtb_ref, dsk_ref, ng_ref, o_ref,
              hst_ref, ysc_ref, *, n_ctx_chunks, n_chunks):
    d = pl.program_id(1)
    s = pl.program_id(2)
    c = _ssd_chunk_index(d, s, n_ctx_chunks, n_chunks)
    q = SSD_CHUNK

    @pl.when(s == 0)
    def _():
        hst_ref[...] = jnp.zeros_like(hst_ref)

    def run(direction):
        xbc = xbc_ref[...]
        x = xbc[:, 0:512]
        bm = xbc[:, 512:640]
        cm = xbc[:, 640:768]
        pre = dt_ref[...] + dtb_ref[...]
        dtv = jnp.maximum(pre, 0.0) + jnp.log(1.0 + jnp.exp(-jnp.abs(pre)))
        a = dtv * aneg_ref[...]
        row = lax.broadcasted_iota(jnp.int32, (q, q), 0)
        col = lax.broadcasted_iota(jnp.int32, (q, q), 1)
        tri = (row >= col) if direction == 0 else (row <= col)
        acs = _hdot(tri.astype(F32), a)
        acs_t = acs.T
        last = acs[q - 1:q, :] if direction == 0 else acs[0:1, :]
        dte = jnp.exp(last - acs)
        expa = jnp.exp(acs)
        cdec = jnp.broadcast_to(jnp.exp(last), (8, LANES))
        erow = lax.broadcasted_iota(jnp.int32, (LANES, SSD_INNER), 0)
        ecol = lax.broadcasted_iota(jnp.int32, (LANES, SSD_INNER), 1)
        expand = (erow == ecol // SSD_HEAD_DIM + SSD_HEADS * direction).astype(F32)
        spread = _hdot(jnp.concatenate([dtv, dtv * dte, expa, cdec], axis=0), expand)
        xdt = x * spread[0:q]
        xs = x * spread[q:2 * q]
        expa_x = spread[2 * q:3 * q]
        cdec_x = spread[3 * q:3 * q + 1]
        bt = bm.T.astype(BF16)
        bb = bm.astype(BF16)
        lane = lax.broadcasted_iota(jnp.int32, (1, LANES), 1)
        ys = []
        for g in range(2):
            cg = jnp.where(lane // SSD_STATE == g, cm, 0.0).astype(BF16)
            gmat = lax.dot_general(cg, bb, (((1,), (1,)), ((), ())), preferred_element_type=F32)
            for hp in range(2):
                p = g * 2 + hp
                sl = slice(p * LANES, (p + 1) * LANES)
                ypair = None
                for e in range(2):
                    j = SSD_HEADS * direction + 2 * p + e
                    seg = acs[:, j:j + 1] - acs_t[j:j + 1, :]
                    dec = jnp.exp(jnp.where(tri, seg, -jnp.inf))
                    xh = jnp.where(lane // SSD_HEAD_DIM == e, xdt[:, sl], 0.0)
                    term = _bdot(gmat * dec, xh)
                    ypair = term if ypair is None else ypair + term
                hprev = hst_ref[p]
                ypair = ypair + _bdot(cg, hprev) * expa_x[:, sl]
                hst_ref[p] = hprev * cdec_x[:, sl] + jnp.dot(
                    bt, xs[:, sl].astype(BF16), preferred_element_type=F32)
                ys.append(ypair)
        return x, ys

    @pl.when(d == 0)
    def _():
        _, ys = run(0)
        for p in range(4):
            ysc_ref[c, :, p * LANES:(p + 1) * LANES] = ys[p]

    @pl.when(d == 1)
    def _():
        x, ys = run(1)
        y = jnp.concatenate(ys, axis=1) + ysc_ref[c] + dsk_ref[...] * x
        y = y * _silu(z_ref[...])
        o_ref[...] = _rms(y, ng_ref[...]).astype(o_ref.dtype)


def _ssd_call(xbc_act, dt, z, aneg, dtb, dsk, ng, n_ctx_chunks):
    b, t, _ = xbc_act.shape
    q = SSD_CHUNK
    nc = t // q
    cidx = functools.partial(_ssd_chunk_index, n_ctx_chunks=n_ctx_chunks, n_chunks=nc)

    def late(bi, di, si):
        return (bi, jnp.where(di == 0, n_ctx_chunks - 1, cidx(di, si)), 0)

    return pl.pallas_call(
        functools.partial(_ssd_body, n_ctx_chunks=n_ctx_chunks, n_chunks=nc),
        grid=(b, 2, nc),
        in_specs=[
            pl.BlockSpec((None, q, SSD_CONV_CH), lambda bi, di, si: (bi, cidx(di, si), 0)),
            pl.BlockSpec((None, q, SSD_DT_PAD), lambda bi, di, si: (bi, cidx(di, si), 0)),
            pl.BlockSpec((None, q, SSD_INNER), late),
            pl.BlockSpec((1, LANES), lambda bi, di, si: (0, 0)),
            pl.BlockSpec((1, LANES), lambda bi, di, si: (0, 0)),
            pl.BlockSpec((1, SSD_INNER), lambda bi, di, si: (0, 0)),
            pl.BlockSpec((1, SSD_INNER), lambda bi, di, si: (0, 0)),
        ],
        out_specs=pl.BlockSpec((None, q, SSD_INNER), late),
        out_shape=jax.ShapeDtypeStruct((b, t, SSD_INNER), BF16),
        scratch_shapes=[pltpu.VMEM((4, LANES, LANES), F32),
                        pltpu.VMEM((nc, q, SSD_INNER), F32)],
        compiler_params=_params(3),
        name="ssd_scan",
    )(xbc_act, dt, z, aneg, dtb, dsk, ng)


def _flash(q, k_ref, v_ref, m_ref, l_ref, acc_ref, *, tk, n_kv, n_ctx_kv, ctx_len, is_ctx_tile):
    r = q.shape[0]
    m_ref[...] = jnp.full(m_ref.shape, -jnp.inf, F32)
    l_ref[...] = jnp.zeros(l_ref.shape, F32)
    acc_ref[...] = jnp.zeros(acc_ref.shape, F32)
    reps = tk // LANES

    def step(kc, masked):
        start = kc * tk if isinstance(kc, int) else pl.multiple_of(kc * tk, tk)
        k = k_ref[pl.ds(start, tk), :]
        v = v_ref[pl.ds(start, tk), :]
        s = lax.dot_general(q, k, (((1,), (1,)), ((), ())), preferred_element_type=F32)
        if masked:
            col = lax.broadcasted_iota(jnp.int32, (1, tk), 1) + kc * tk
            s = jnp.where(col < ctx_len, s, -jnp.inf)
        m_old = m_ref[...]
        m_new = jnp.maximum(m_old, jnp.max(s, axis=-1, keepdims=True))
        alpha = jnp.exp2(m_old - m_new)
        p = jnp.exp2(s - jnp.concatenate([m_new] * reps, axis=1))
        psum = p[:, 0:LANES]
        for i in range(1, reps):
            psum = psum + p[:, i * LANES:(i + 1) * LANES]
        l_ref[...] = alpha * l_ref[...] + psum
        acc_ref[...] = alpha * acc_ref[...] + jnp.dot(p.astype(BF16), v, preferred_element_type=F32)
        m_ref[...] = m_new

    @pl.when(is_ctx_tile)
    def _():
        for kc in range(n_ctx_kv):
            step(kc, True)

    @pl.when(jnp.logical_not(is_ctx_tile))
    def _():
        def body(kc, carry):
            step(kc, False)
            return carry
        lax.fori_loop(0, n_kv, body, 0, unroll=True)

    inv_l = 1.0 / jnp.sum(l_ref[...], axis=-1, keepdims=True)
    return acc_ref[...] * inv_l


def _diff_attn_body(q_ref, k_ref, v_ref, lam_ref, g_ref, o_ref, m_ref, l_ref, acc_ref,
                    *, tq, lam_init, **kw):
    qi = pl.program_id(2)
    q = q_ref[...].reshape(2 * tq, LANES)
    o = _flash(q, k_ref, v_ref, m_ref, l_ref, acc_ref, is_ctx_tile=qi < kw.pop("n_ctx_q"), **kw)
    lv = lam_ref[...]
    lam = (jnp.exp(jnp.sum(lv[0:1] * lv[1:2], axis=-1, keepdims=True))
           - jnp.exp(jnp.sum(lv[2:3] * lv[3:4], axis=-1, keepdims=True)) + lam_init)
    out = o[0:tq] - lam * o[tq:2 * tq]
    o_ref[...] = (_rms(out, g_ref[...]) * (1.0 - lam_init)).astype(o_ref.dtype)


def _gqa_attn_body(q_ref, k_ref, v_ref, o_ref, m_ref, l_ref, acc_ref, *, tq, **kw):
    n = pl.program_id(1)
    qi = pl.program_id(2)
    q = q_ref[...].reshape(4 * tq, LANES)
    o = _flash(q, k_ref, v_ref, m_ref, l_ref, acc_ref, is_ctx_tile=qi < kw.pop("n_ctx_q"), **kw)
    lane = lax.broadcasted_iota(jnp.int32, (1, LANES), 1)
    lo_half = lane < HEAD_DIM
    for pair in range(2):
        a = o[(2 * pair) * tq:(2 * pair + 1) * tq]
        b = o[(2 * pair + 1) * tq:(2 * pair + 2) * tq]
        from_lo = jnp.where(lo_half, a, pltpu.roll(b, HEAD_DIM, 1))
        from_hi = jnp.where(lo_half, pltpu.roll(a, HEAD_DIM, 1), b)
        o_ref[:, pair * LANES:(pair + 1) * LANES] = jnp.where(n == 0, from_lo, from_hi).astype(o_ref.dtype)


def _attn_tiles(t, ctx_len):
    tq = ROW_TILE
    tk = next(c for c in (2816, 768, 256) if t % c == 0)
    assert ctx_len % tq == 0 and t % tq == 0 and t % tk == 0
    return dict(tq=tq, tk=tk, n_kv=t // tk, n_ctx_kv=-(-ctx_len // tk), ctx_len=ctx_len,
                n_ctx_q=ctx_len // tq)


def _diff_attn_call(qd, kd, vd, lamv, subln_g, lam_init, ctx_len):
    b, _, _, t, _ = qd.shape
    cfg = _attn_tiles(t, ctx_len)
    tq = cfg["tq"]
    r = 2 * tq
    return pl.pallas_call(
        functools.partial(_diff_attn_body, lam_init=lam_init, **cfg),
        grid=(b, DIFF_HEADS, t // tq),
        in_specs=[
            pl.BlockSpec((None, None, 2, tq, LANES), lambda bi, hi, qi: (bi, hi, 0, qi, 0)),
            pl.BlockSpec((None, t, LANES), lambda bi, hi, qi: (bi, 0, hi)),
            pl.BlockSpec((None, t, LANES), lambda bi, hi, qi: (bi, 0, hi)),
            pl.BlockSpec((4, LANES), lambda bi, hi, qi: (0, 0)),
            pl.BlockSpec((1, LANES), lambda bi, hi, qi: (0, 0)),
        ],
        out_specs=pl.BlockSpec((None, tq, LANES), lambda bi, hi, qi: (bi, qi, hi)),
        out_shape=jax.ShapeDtypeStruct((b, t, DIFF_HEADS * LANES), BF16),
        scratch_shapes=[pltpu.VMEM((r, LANES), F32)] * 3,
        compiler_params=_params(3),
        name="diff_attention",
    )(qd, kd, vd, lamv, subln_g)


def _gqa_attn_call(qg, kg, vg, ctx_len):
    b, _, _, t, _ = qg.shape
    cfg = _attn_tiles(t, ctx_len)
    tq = cfg["tq"]
    r = 4 * tq
    return pl.pallas_call(
        functools.partial(_gqa_attn_body, **cfg),
        grid=(b, GQA_KV_HEADS, t // tq),
        in_specs=[
            pl.BlockSpec((None, None, 4, tq, LANES), lambda bi, ni, qi: (bi, ni, 0, qi, 0)),
            pl.BlockSpec((None, t, LANES), lambda bi, ni, qi: (bi, 0, 0)),
            pl.BlockSpec((None, t, LANES), lambda bi, ni, qi: (bi, 0, 0)),
        ],
        out_specs=pl.BlockSpec((None, tq, 2 * LANES), lambda bi, ni, qi: (bi, qi, ni)),
        out_shape=jax.ShapeDtypeStruct((b, t, GQA_HEADS * HEAD_DIM), BF16),
        scratch_shapes=[pltpu.VMEM((r, LANES), F32)] * 3,
        compiler_params=_params(3),
        name="gqa_attention",
    )(qg, kg, vg)


def _cmul(ar, ai, br, bi):
    return ar * br - ai * bi, ar * bi + ai * br


def _stack_rows(er, ei, vr, vi):
    re = [er * vr[c:c + 1] - ei * vi[c:c + 1] for c in range(S5_GROUP_CH)]
    im = [er * vi[c:c + 1] + ei * vr[c:c + 1] for c in range(S5_GROUP_CH)]
    return jnp.concatenate(re, axis=0), jnp.concatenate(im, axis=0)


def _split_dot_t(a, b):
    dn = (((1,), (1,)), ((), ()))
    ah = a.astype(BF16)
    al = (a - ah.astype(F32)).astype(BF16)
    bh = b.astype(BF16)
    bl = (b - bh.astype(F32)).astype(BF16)
    return (lax.dot_general(ah, bh, dn, preferred_element_type=F32)
            + lax.dot_general(ah, bl, dn, preferred_element_type=F32)
            + lax.dot_general(al, bh, dn, preferred_element_type=F32))


def _s5_weights_body(lam_ref, bt_ref, c_ref, wi_ref, ws_ref, wo_ref, av_ref):
    tc = S5_CHUNK
    lam = lam_ref[...]
    br_t, bi_t = bt_ref[0:16, :], bt_ref[16:32, :]
    cr, ci = c_ref[0:16, :], c_ref[16:32, :]
    kk = lax.broadcasted_iota(jnp.int32, (tc, S5_STATE), 0).astype(F32)
    k8 = lax.broadcasted_iota(jnp.int32, (8, S5_STATE), 0)
    k8 = jnp.where(k8 == 0, 1.0, jnp.where(k8 == 1, tc - 1.0, float(tc)))
    row = lax.broadcasted_iota(jnp.int32, (S5_FLAT, S5_FLAT), 0) % tc
    col = lax.broadcasted_iota(jnp.int32, (S5_FLAT, S5_FLAT), 1) % tc
    w_intra = None
    state_cols, out_cols, a_rows = [], [], []
    for direction in range(2):
        lr = lam[2 * direction:2 * direction + 1]
        li = lam[2 * direction + 1:2 * direction + 2]
        step = jnp.exp(lam[4 + direction:5 + direction])
        mag = jnp.exp(lr * step)
        ar, ai = mag * jnp.cos(li * step), mag * jnp.sin(li * step)
        den = lr * lr + li * li
        fr = ((ar - 1.0) * lr + ai * li) / den
        fi = (ai * lr - (ar - 1.0) * li) / den
        bbr = fr * br_t - fi * bi_t
        bbi = fr * bi_t + fi * br_t
        cs, sn = jnp.cos(kk * (li * step)), jnp.sin(kk * (li * step))
        grow, decay = jnp.exp(-kk * (lr * step)), jnp.exp(kk * (lr * step))
        pr, pi = decay * cs, decay * sn
        nr, ni = grow * cs, -grow * sn
        m8 = jnp.exp(k8 * (lr * step))
        c8r, c8i = m8 * jnp.cos(k8 * (li * step)), m8 * jnp.sin(k8 * (li * step))
        a_one = (c8r[0:1], c8i[0:1])
        a_last = (c8r[1:2], c8i[1:2])
        a_tc = (c8r[2:3], c8i[2:3])
        if direction == 0:
            x_e, y_e = (nr, ni), (pr, pi)
            s_e = _cmul(nr, ni, *a_last)
            o_e = _cmul(pr, pi, *a_one)
            keep = col >= row
        else:
            x_e, y_e = (pr, pi), (nr, ni)
            s_e = (pr, pi)
            o_e = _cmul(nr, ni, *a_tc)
            keep = row >= col
        xr, xi = _stack_rows(*x_e, bbr, bbi)
        yr, yi = _stack_rows(*y_e, cr, ci)
        full = _split_dot_t(jnp.concatenate([xr, -xi], axis=1), jnp.concatenate([yr, yi], axis=1))
        part = jnp.where(keep, full, 0.0)
        w_intra = part if w_intra is None else w_intra + part
        state_cols.append(_stack_rows(*s_e, bbr, bbi))
        o_r, o_i = _stack_rows(*o_e, cr, ci)
        out_cols += [o_r, -o_i]
        a_rows += [jnp.concatenate([a_tc[0], a_tc[0]], axis=1),
                   jnp.concatenate([-a_tc[1], a_tc[1]], axis=1),
                   jnp.concatenate([a_tc[1], -a_tc[1]], axis=1)]
    wi_ref[...] = w_intra.astype(BF16)
    (fr_, fi_), (br_, bi_) = state_cols
    ws_ref[...] = jnp.concatenate([fr_, fi_, br_, bi_, fi_, fr_, bi_, br_], axis=1).astype(BF16)
    wo_ref[...] = jnp.concatenate(out_cols, axis=1).astype(BF16)
    zero = jnp.zeros((1, LANES), F32)
    av_ref[...] = jnp.concatenate(a_rows + [zero, zero], axis=0)


def _s5_weights_call(lam_re, lam_im, log_dt, b_re, b_im, c_re, c_im):
    g, p = S5_GROUPS, S5_STATE
    bc = lambda v: jnp.broadcast_to(v.astype(F32)[:, None], (g, p))
    zero = jnp.zeros((g, p), F32)
    lam = jnp.stack([lam_re[0], lam_im[0], lam_re[1], lam_im[1], bc(log_dt[0]), bc(log_dt[1]),
                     zero, zero], axis=1).astype(F32)
    bt = jnp.concatenate([jnp.swapaxes(b_re, 1, 2), jnp.swapaxes(b_im, 1, 2)], axis=1).astype(F32)
    cc = jnp.concatenate([c_re, c_im], axis=1).astype(F32)
    n = S5_FLAT
    return pl.pallas_call(
        _s5_weights_body,
        grid=(g,),
        in_specs=[pl.BlockSpec((None, 8, p), lambda gi: (gi, 0, 0)),
                  pl.BlockSpec((None, 32, p), lambda gi: (gi, 0, 0)),
                  pl.BlockSpec((None, 32, p), lambda gi: (gi, 0, 0))],
        out_specs=[pl.BlockSpec((None, n, n), lambda gi: (gi, 0, 0)),
                   pl.BlockSpec((None, n, 4 * LANES), lambda gi: (gi, 0, 0)),
                   pl.BlockSpec((None, n, 2 * LANES), lambda gi: (gi, 0, 0)),
                   pl.BlockSpec((None, 8, LANES), lambda gi: (gi, 0, 0))],
        out_shape=[jax.ShapeDtypeStruct((g, n, n), BF16),
                   jax.ShapeDtypeStruct((g, n, 4 * LANES), BF16),
                   jax.ShapeDtypeStruct((g, n, 2 * LANES), BF16),
                   jax.ShapeDtypeStruct((g, 8, LANES), F32)],
        compiler_params=_params(1),
        name="s5_weights",
    )(lam, bt, cc)


def _s5_body(u_ref, wi_ref, ws_ref, wo_ref, av_ref, y_ref, s_ref, h_ref, *, n_ctx_chunks, n_chunks):
    u = u_ref[...]
    s_ref[...] = jnp.dot(u, ws_ref[...], preferred_element_type=F32)
    av = av_ref[...]
    rows = S5_ROWS
    nj = n_chunks

    def bcast(i):
        return jnp.broadcast_to(av[i:i + 1], (rows, LANES))

    a1f, a2f, a2sf, a1b, a2b, a2sb = [bcast(i) for i in range(6)]

    def body(i, carry):
        hf, hfs, hb, hbs = carry
        jf = pl.multiple_of(i * rows, rows)
        jb_chunk = jnp.where(i < n_ctx_chunks, n_ctx_chunks - 1 - i, nj - 1 - (i - n_ctx_chunks))
        jb = pl.multiple_of(jb_chunk * rows, rows)
        h_ref[pl.ds(jf, rows), 0:LANES] = hf
        h_ref[pl.ds(jb, rows), LANES:2 * LANES] = hb
        sf = s_ref[pl.ds(jf, rows), 0:LANES]
        sfs = s_ref[pl.ds(jf, rows), 2 * LANES:3 * LANES]
        sb = s_ref[pl.ds(jb, rows), LANES:2 * LANES]
        sbs = s_ref[pl.ds(jb, rows), 3 * LANES:4 * LANES]
        return (a1f * hf + a2f * hfs + sf, a1f * hfs + a2sf * hf + sfs,
                a1b * hb + a2b * hbs + sb, a1b * hbs + a2sb * hb + sbs)

    zero = jnp.zeros((rows, LANES), F32)
    lax.fori_loop(0, nj, body, (zero, zero, zero, zero))
    y_ref[...] = (jnp.dot(u, wi_ref[...], preferred_element_type=F32)
                  + lax.dot_general(h_ref[...].astype(BF16), wo_ref[...], (((1,), (1,)), ((), ())),
                                    preferred_element_type=F32))


def _s5_call(ug, w_intra, w_state, w_out, avec, n_ctx_chunks):
    g, r, w = ug.shape
    nj = r // S5_ROWS
    return pl.pallas_call(
        functools.partial(_s5_body, n_ctx_chunks=n_ctx_chunks, n_chunks=nj),
        grid=(g,),
        in_specs=[
            pl.BlockSpec((None, r, w), lambda gi: (gi, 0, 0)),
            pl.BlockSpec((None, w, w), lambda gi: (gi, 0, 0)),
            pl.BlockSpec((None, w, 4 * LANES), lambda gi: (gi, 0, 0)),
            pl.BlockSpec((None, w, 2 * LANES), lambda gi: (gi, 0, 0)),
            pl.BlockSpec((None, 8, LANES), lambda gi: (gi, 0, 0)),
        ],
        out_specs=pl.BlockSpec((None, r, w), lambda gi: (gi, 0, 0)),
        out_shape=jax.ShapeDtypeStruct((g, r, w), F32),
        scratch_shapes=[pltpu.VMEM((r, 4 * LANES), F32), pltpu.VMEM((r, 2 * LANES), F32)],
        compiler_params=_params(1),
        name="s5_scan",
    )(ug, w_intra, w_state, w_out, avec)


def _s5_to_groups(u):
    b, t, _ = u.shape
    nj = t // S5_CHUNK
    x = jnp.swapaxes(u.astype(BF16).reshape(b, nj, S5_CHUNK, S5_WIDTH), 2, 3)
    x = lax.optimization_barrier(x.reshape(b, nj, S5_GROUPS, S5_FLAT))
    x = jnp.pad(jnp.transpose(x, (2, 1, 0, 3)), ((0, 0), (0, 0), (0, S5_ROWS - b), (0, 0)))
    return x.reshape(S5_GROUPS, nj * S5_ROWS, S5_FLAT)


def _s5_from_groups(y, b):
    g, r, n = y.shape
    nj = r // S5_ROWS
    x = jnp.transpose(y.reshape(g, nj, S5_ROWS, n)[:, :, :b], (2, 1, 0, 3))
    x = lax.optimization_barrier(x).reshape(b, nj, S5_WIDTH, S5_CHUNK)
    return jnp.swapaxes(x, 2, 3).reshape(b, nj * S5_CHUNK, S5_WIDTH)


def _merge_body(h_ref, mod_ref, xn_ref, ya_ref, yb_ref, yc_ref, u_ref, y5_ref,
                wg_ref, wa_ref, wb_ref, wc_ref, wd_ref, glw_ref, glb_ref, s5d_ref, wo_ref, o_ref):
    xn = xn_ref[...]
    y5 = y5_ref[...] + s5d_ref[...] * u_ref[...]
    gelu = 0.5 * y5 * (1.0 + jnp.tanh(0.7978845608028654 * (y5 + 0.044715 * y5 * y5 * y5)))
    glu = _bdot(gelu, glw_ref[...]) + glb_ref[...]
    yd = glu[:, 0:S5_WIDTH] * _sigmoid(glu[:, S5_WIDTH:2 * S5_WIDTH])
    branches = ((ya_ref[...], wa_ref), (yb_ref[...], wb_ref), (yc_ref[...], wc_ref),
                (yd.astype(BF16), wd_ref))
    merged = None
    for i, (y, w_ref) in enumerate(branches):
        gate = _sigmoid(jnp.dot(xn, wg_ref[i], preferred_element_type=F32))
        term = gate * jnp.dot(y, w_ref[...], preferred_element_type=F32)
        merged = term if merged is None else merged + term
    out = _bdot(merged, wo_ref[...])
    o_ref[...] = h_ref[...] + mod_ref[2:3, :] * out


def _merge_call(h, modv, xn, ya, yb, yc, u, y5, wts, n_ctx_tiles):
    b, t, d = h.shape
    tm = ROW_TILE

    def row(width):
        return pl.BlockSpec((None, tm, width), lambda bi, ti: (bi, ti, 0))

    def const(shape):
        return pl.BlockSpec(shape, lambda bi, ti: (0,) * len(shape), pipeline_mode=pl.Buffered(1))

    w_gate, w_a, w_b, w_c, w_d, glu_w, glu_b, s5_d, w_out = wts
    return pl.pallas_call(
        _merge_body,
        grid=(b, t // tm),
        in_specs=[row(d),
                  pl.BlockSpec((None, None, 6, d),
                               lambda bi, ti: (bi, jnp.where(ti >= n_ctx_tiles, 1, 0), 0, 0)),
                  row(d), row(512), row(512), row(512), row(S5_WIDTH), row(S5_WIDTH),
                  const((4, d, d)), const((512, d)), const((512, d)), const((512, d)),
                  const((S5_WIDTH, d)), const((S5_WIDTH, 2 * S5_WIDTH)), const((1, 2 * S5_WIDTH)),
                  const((1, S5_WIDTH)), const((d, d))],
        out_specs=row(d),
        out_shape=jax.ShapeDtypeStruct((b, t, d), F32),
        compiler_params=_params(2),
        name="branch_merge",
    )(h, modv, xn, ya, yb, yc, u, y5, w_gate, w_a, w_b, w_c, w_d, glu_w, glu_b, s5_d, w_out)


def _ffn_body(h_ref, mod_ref, g_ref, wgu_ref, wd_ref, o_ref):
    h = h_ref[...]
    mod = mod_ref[...]
    xf = (_rms(h, g_ref[...]) * (1.0 + mod[4:5]) + mod[3:4]).astype(BF16)
    gate = jnp.dot(xf, wgu_ref[:, 0:FFN_HIDDEN], preferred_element_type=F32)
    up = jnp.dot(xf, wgu_ref[:, FFN_HIDDEN:2 * FFN_HIDDEN], preferred_element_type=F32)
    act = (_silu(gate) * up).astype(BF16)
    o_ref[...] = h + mod[5:6] * jnp.dot(act, wd_ref[...], preferred_element_type=F32)


def _ffn_call(h, modv, g2, w_gu, w_down, n_ctx_tiles, latent_only):
    b, t, d = h.shape
    tm = ROW_TILE
    skip = n_ctx_tiles if latent_only else 0

    def const(shape):
        return pl.BlockSpec(shape, lambda bi, ti: (0,) * len(shape), pipeline_mode=pl.Buffered(1))

    return pl.pallas_call(
        _ffn_body,
        grid=(b, t // tm - skip),
        in_specs=[pl.BlockSpec((None, tm, d), lambda bi, ti: (bi, ti + skip, 0)),
                  pl.BlockSpec((None, None, 6, d),
                               lambda bi, ti: (bi, jnp.where(ti + skip >= n_ctx_tiles, 1, 0), 0, 0)),
                  const((1, d)), const((d, 2 * FFN_HIDDEN)), const((FFN_HIDDEN, d))],
        out_specs=pl.BlockSpec((None, tm, d), lambda bi, ti: (bi, ti, 0)),
        out_shape=jax.ShapeDtypeStruct((b, t - skip * tm, d), F32),
        compiler_params=_params(2),
        name="swiglu_ffn",
    )(h, modv, g2, w_gu, w_down)


def _rope_tables(ctx_len, seq_len):
    n_rows = seq_len // GRID_W
    rows = jnp.repeat(jnp.arange(n_rows, dtype=F32), GRID_W)
    cols = jnp.tile(jnp.arange(GRID_W, dtype=F32), n_rows)
    quarter = HEAD_DIM // 4
    inv_freq = ROPE_THETA ** (-jnp.arange(quarter, dtype=F32) / quarter)
    ang_r = rows[:, None] * inv_freq
    ang_c = cols[:, None] * inv_freq
    ang = jnp.concatenate([ang_r, ang_r, ang_c, ang_c], axis=-1)
    cos = jnp.concatenate([jnp.ones((ctx_len, HEAD_DIM), F32), jnp.cos(ang)], axis=0)
    sin = jnp.concatenate([jnp.zeros((ctx_len, HEAD_DIM), F32), jnp.sin(ang)], axis=0)
    first = (jnp.arange(HEAD_DIM) % 32) < 16
    sin_a = jnp.where(first, -sin, 0.0)
    sin_b = jnp.where(first, 0.0, sin)
    two = lambda m: jnp.concatenate([m, m], axis=1)
    return jnp.concatenate([two(cos), two(sin_a), two(sin_b)], axis=1)


def _w_in_layout(w_in):
    d = w_in.shape[0]
    a0 = 0
    z = w_in[:, a0:a0 + 512]
    xbc = w_in[:, a0 + 512:a0 + 1280]
    dt = jnp.pad(w_in[:, a0 + 1280:a0 + 1296], ((0, 0), (0, SSD_DT_PAD - 16)))
    rest = w_in[:, 1296:]
    out = jnp.concatenate([z, xbc, dt, rest], axis=1).astype(BF16)
    assert out.shape == (d, _C_END)
    return out


def kernel(x, c, ctx, c_ctx, w_mod, b_mod, norm1_g, norm2_g, w_in, ssd_conv_w, ssd_conv_b, ssd_a_log, ssd_dt_bias, ssd_d, ssd_norm_g, diff_qn_g, diff_kn_g, diff_lam_q1, diff_lam_k1, diff_lam_q2, diff_lam_k2, diff_subln_g, gqa_qn_g, gqa_kn_g, s5_lam_re, s5_lam_im, s5_log_dt, s5_b_re, s5_b_im, s5_c_re, s5_c_im, s5_d, s5_glu_w, s5_glu_b, w_gate, w_br_ssd, w_br_diff, w_br_gqa, w_br_s5, w_out, ffn_w_gate_up, ffn_w_down):
    b, seq_len, d = x.shape
    ctx_len = ctx.shape[1]
    depth = w_mod.shape[0]
    assert b <= S5_ROWS and b + 1 <= 8
    assert ctx_len % ROW_TILE == 0 and seq_len % ROW_TILE == 0
    n_ctx_tiles = ctx_len // ROW_TILE

    h = jnp.concatenate([ctx, x], axis=1)
    cc = jnp.concatenate([c, c_ctx[None], jnp.zeros((8 - b - 1, d), F32)], axis=0)
    mods = _mod_call(cc, w_mod, b_mod)

    rope_tab = _rope_tables(ctx_len, seq_len)
    blk = jnp.arange(512) // HEAD_DIM
    ones = (blk[:, None] == blk[None, :]).astype(BF16)
    tile8 = lambda g: jnp.tile(g.astype(F32), 512 // HEAD_DIM)

    for layer in range(depth):
        m = mods[layer]
        lat = m[:b].reshape(b, 6, d)
        cmod = jnp.broadcast_to(m[b].reshape(1, 6, d), (b, 6, d))
        modv = jnp.stack([cmod, lat], axis=1)

        gains = jnp.stack([tile8(diff_qn_g[layer]), tile8(diff_kn_g[layer]),
                           tile8(gqa_qn_g[layer]), tile8(gqa_kn_g[layer])]
                          + [jnp.zeros((512,), F32)] * 4, axis=0)
        (xn, z, xbc, dt, qd, kd, vd, qg, kg, vg, u) = _in_proj_call(
            h, modv, norm1_g[layer][None], _w_in_layout(w_in[layer]), ones, gains, rope_tab,
            n_ctx_tiles)

        conv_w = jnp.pad(ssd_conv_w[layer].astype(F32), ((0, 8 - SSD_CONV_W), (0, 0)))
        xbc_act = _conv_call(xbc, conv_w, ssd_conv_b[layer][None].astype(F32), n_ctx_tiles)
        pad16 = lambda v: jnp.pad(v.reshape(1, 16).astype(F32), ((0, 0), (0, LANES - 16)))
        aneg = pad16(-jnp.exp(ssd_a_log[layer].astype(F32)))
        dtb = pad16(ssd_dt_bias[layer])
        dsk = jnp.repeat(ssd_d[layer].astype(F32), SSD_HEAD_DIM)[None]
        ya = _ssd_call(xbc_act, dt, z, aneg, dtb, dsk, ssd_norm_g[layer][None].astype(F32),
                       ctx_len // SSD_CHUNK)

        lam_init = 0.8 - 0.6 * math.exp(-0.3 * layer)
        lamv = jnp.pad(jnp.stack([diff_lam_q1[layer], diff_lam_k1[layer],
                                  diff_lam_q2[layer], diff_lam_k2[layer]]).astype(F32),
                       ((0, 0), (0, LANES - HEAD_DIM)))
        yb = _diff_attn_call(qd, kd, vd, lamv, diff_subln_g[layer][None].astype(F32), lam_init, ctx_len)

        yc = _gqa_attn_call(qg, kg, vg, ctx_len)

        s5w = _s5_weights_call(s5_lam_re[layer], s5_lam_im[layer], s5_log_dt[layer],
                               s5_b_re[layer], s5_b_im[layer], s5_c_re[layer], s5_c_im[layer])
        y5 = _s5_from_groups(_s5_call(_s5_to_groups(u), *s5w, ctx_len // S5_CHUNK), b)

        wts = (w_gate[layer].astype(BF16), w_br_ssd[layer].astype(BF16), w_br_diff[layer].astype(BF16),
               w_br_gqa[layer].astype(BF16), w_br_s5[layer].astype(BF16), s5_glu_w[layer].astype(BF16),
               s5_glu_b[layer][None].astype(F32), s5_d[layer][None].astype(F32), w_out[layer].astype(BF16))
        h = _merge_call(h, modv, xn, ya, yb, yc, u, y5, wts, n_ctx_tiles)
        h = _ffn_call(h, modv, norm2_g[layer][None].astype(F32), ffn_w_gate_up[layer].astype(BF16),
                      ffn_w_down[layer].astype(BF16), n_ctx_tiles, latent_only=layer == depth - 1)
    return h
```

```python
import functools
import math

import jax
import jax.numpy as jnp
from jax import lax
from jax.experimental import pallas as pl
from jax.experimental.pallas import tpu as pltpu

F32 = jnp.float32
BF16 = jnp.bfloat16
HIGHEST = lax.Precision.HIGHEST

D_MODEL = 1024
GRID_W = 64
ROPE_THETA = 10000.0
NORM_EPS = 1e-6

SSD_INNER = 512
SSD_HEADS = 8
SSD_HEAD_DIM = 64
SSD_STATE = 64
SSD_CHUNK = 128
SSD_CONV_W = 5
SSD_CONV_CH = 768
SSD_DT_PAD = 128

DIFF_HEADS = 4
HEAD_DIM = 64
GQA_HEADS = 8
GQA_KV_HEADS = 2

S5_GROUP_CH = 16
S5_STATE = 64
S5_WIDTH = 384
S5_GROUPS = 24
S5_CHUNK = 64
S5_ROWS = 8
S5_FLAT = S5_CHUNK * S5_GROUP_CH

FFN_HIDDEN = 2816

ROW_TILE = 256
LANES = 128
VMEM_LIMIT = 56 * 1024 * 1024

_C_Z, _C_XBC, _C_DT = 0, 512, 1280
_C_QD, _C_KD, _C_VD = 1408, 1920, 2432
_C_QG, _C_KG, _C_VG = 2944, 3456, 3584
_C_U, _C_END = 3712, 4096


def _params(n_grid):
    return pltpu.CompilerParams(dimension_semantics=("arbitrary",) * n_grid,
                                vmem_limit_bytes=VMEM_LIMIT)


def _bdot(a, b):
    return jnp.dot(a.astype(BF16), b.astype(BF16), preferred_element_type=F32)


def _hdot(a, b):
    return jnp.dot(a, b, precision=HIGHEST, preferred_element_type=F32)


def _rms(x, g):
    return x * lax.rsqrt(jnp.mean(x * x, axis=-1, keepdims=True) + NORM_EPS) * g


def _sigmoid(x):
    return 1.0 / (1.0 + jnp.exp(-x))


def _silu(x):
    return x * _sigmoid(x)


def _mod_body(c_ref, w_ref, b_ref, o_ref):
    o_ref[...] = _hdot(_silu(c_ref[...]), w_ref[...]) + b_ref[...]


def _mod_call(cc, w_mod, b_mod):
    depth = w_mod.shape[0]
    d = D_MODEL
    return pl.pallas_call(
        _mod_body,
        grid=(depth, 6),
        in_specs=[pl.BlockSpec((8, d), lambda l, j: (0, 0)),
                  pl.BlockSpec((None, d, d), lambda l, j: (l, 0, j)),
                  pl.BlockSpec((None, 1, d), lambda l, j: (l, 0, j))],
        out_specs=pl.BlockSpec((None, 8, d), lambda l, j: (l, 0, j)),
        out_shape=jax.ShapeDtypeStruct((depth, 8, 6 * d), F32),
        compiler_params=_params(2),
        name="adaln_mod",
    )(cc, w_mod, b_mod.reshape(depth, 1, 6 * d))


def _head_rms(t, gain, ones):
    ss = jnp.dot((t * t).astype(BF16), ones, preferred_element_type=F32)
    return t * lax.rsqrt(ss * (1.0 / HEAD_DIM) + NORM_EPS) * gain


def _rope(t, cos, sin_a, sin_b):
    w = t.shape[-1]
    return t * cos + pltpu.roll(t, w - 16, 1) * sin_a + pltpu.roll(t, 16, 1) * sin_b


def _in_proj_body(h_ref, mod_ref, g_ref, w_ref, ones_ref, gains_ref, rope_ref,
                  xn_ref, z_ref, xbc_ref, dt_ref, qd_ref, kd_ref, vd_ref,
                  qg_ref, kg_ref, vg_ref, u_ref):
    x = h_ref[...]
    mod = mod_ref[...]
    xn = _rms(x, g_ref[...]) * (1.0 + mod[1:2]) + mod[0:1]
    xb = xn.astype(BF16)
    xn_ref[...] = xb

    def proj(lo, hi):
        return jnp.dot(xb, w_ref[:, lo:hi], preferred_element_type=F32)

    z_ref[...] = proj(_C_Z, _C_XBC)
    xbc_ref[...] = proj(_C_XBC, _C_DT)
    dt_ref[...] = proj(_C_DT, _C_QD)
    u_ref[...] = proj(_C_U, _C_END)

    rope = rope_ref[...]
    cos1, sa1, sb1 = rope[:, 0:128], rope[:, 128:256], rope[:, 256:384]
    cos4 = jnp.concatenate([cos1] * 4, axis=1)
    sa4 = jnp.concatenate([sa1] * 4, axis=1)
    sb4 = jnp.concatenate([sb1] * 4, axis=1)
    ones = ones_ref[...]
    gains = gains_ref[...]
    lane = lax.broadcasted_iota(jnp.int32, (1, LANES), 1)
    lo_half = lane < HEAD_DIM
    scale = HEAD_DIM ** -0.5 * math.log2(math.e)

    qd = _rope(_head_rms(proj(_C_QD, _C_KD), gains[0:1], ones), cos4, sa4, sb4) * scale
    kd = _rope(_head_rms(proj(_C_KD, _C_VD), gains[1:2], ones), cos4, sa4, sb4)
    kd_ref[...] = kd.astype(BF16)
    vd_ref[...] = proj(_C_VD, _C_QG).astype(BF16)
    for h in range(DIFF_HEADS):
        blk = qd[:, h * LANES:(h + 1) * LANES]
        qd_ref[h, 0] = jnp.where(lo_half, blk, 0.0).astype(BF16)
        qd_ref[h, 1] = jnp.where(lo_half, 0.0, blk).astype(BF16)

    qg = _rope(_head_rms(proj(_C_QG, _C_KG), gains[2:3], ones), cos4, sa4, sb4) * scale
    kg = _rope(_head_rms(proj(_C_KG, _C_VG), gains[3:4, :LANES], ones[:LANES, :LANES]),
               cos1, sa1, sb1)
    kg_ref[...] = kg.astype(BF16)
    vg_ref[...] = proj(_C_VG, _C_U).astype(BF16)
    qg_up = pltpu.roll(qg, 4 * LANES - HEAD_DIM, 1)
    qg_dn = pltpu.roll(qg, HEAD_DIM, 1)
    per_kv = GQA_HEADS // GQA_KV_HEADS
    for n in range(GQA_KV_HEADS):
        for i in range(per_kv):
            j = n * per_kv + i
            blk_idx = j // 2
            if (j % 2) == n:
                src = qg
            else:
                src = qg_up if n == 0 else qg_dn
            blk = src[:, blk_idx * LANES:(blk_idx + 1) * LANES]
            keep = lo_half if n == 0 else jnp.logical_not(lo_half)
            qg_ref[n, i] = jnp.where(keep, blk, 0.0).astype(BF16)


def _in_proj_call(h, modv, g1, w_cat, ones, gains, rope_tab, n_ctx_tiles):
    b, t, d = h.shape
    nt = t // ROW_TILE
    tm = ROW_TILE

    def row(width, dtype):
        return (pl.BlockSpec((None, tm, width), lambda bi, ti: (bi, ti, 0)),
                jax.ShapeDtypeStruct((b, t, width), dtype))

    outs = [row(d, BF16), row(512, F32), row(768, F32), row(SSD_DT_PAD, F32)]
    qd = (pl.BlockSpec((None, DIFF_HEADS, 2, tm, LANES), lambda bi, ti: (bi, 0, 0, ti, 0)),
          jax.ShapeDtypeStruct((b, DIFF_HEADS, 2, t, LANES), BF16))
    qg = (pl.BlockSpec((None, GQA_KV_HEADS, 4, tm, LANES), lambda bi, ti: (bi, 0, 0, ti, 0)),
          jax.ShapeDtypeStruct((b, GQA_KV_HEADS, 4, t, LANES), BF16))
    outs += [qd, row(512, BF16), row(512, BF16), qg, row(LANES, BF16), row(LANES, BF16),
             row(S5_WIDTH, F32)]
    return pl.pallas_call(
        _in_proj_body,
        grid=(b, nt),
        in_specs=[
            pl.BlockSpec((None, tm, d), lambda bi, ti: (bi, ti, 0)),
            pl.BlockSpec((None, None, 6, d),
                         lambda bi, ti: (bi, jnp.where(ti >= n_ctx_tiles, 1, 0), 0, 0)),
            pl.BlockSpec((1, d), lambda bi, ti: (0, 0)),
            pl.BlockSpec((d, _C_END), lambda bi, ti: (0, 0)),
            pl.BlockSpec((512, 512), lambda bi, ti: (0, 0)),
            pl.BlockSpec((8, 512), lambda bi, ti: (0, 0)),
            pl.BlockSpec((tm, 384), lambda bi, ti: (ti, 0)),
        ],
        out_specs=[o[0] for o in outs],
        out_shape=[o[1] for o in outs],
        compiler_params=_params(2),
        name="in_proj",
    )(h, modv, g1, w_cat, ones, gains, rope_tab)


def _conv_body(cur_ref, prev_ref, next_ref, w_ref, b_ref, o_ref, buf_ref, *, n_ctx_tiles, n_tiles):
    t = pl.program_id(1)
    tm = ROW_TILE
    has_prev = jnp.logical_and(t != 0, t != n_ctx_tiles)
    has_next = jnp.logical_and(t != n_ctx_tiles - 1, t != n_tiles - 1)
    buf_ref[0:8, :] = jnp.where(has_prev, prev_ref[...], 0.0)
    buf_ref[8:8 + tm, :] = cur_ref[...]
    buf_ref[8 + tm:16 + tm, :] = jnp.where(has_next, next_ref[...], 0.0)
    w = w_ref[...]
    bias = b_ref[...]
    pad = (SSD_CONV_W - 1) // 2
    rows = 64
    for r in range(tm // rows):
        acc = bias + w[0:1] * buf_ref[8 - pad + r * rows:8 - pad + (r + 1) * rows, :]
        for k in range(1, SSD_CONV_W):
            lo = 8 - pad + k + r * rows
            acc = acc + w[k:k + 1] * buf_ref[lo:lo + rows, :]
        o_ref[r * rows:(r + 1) * rows, :] = _silu(acc)


def _conv_call(xbc, conv_w, conv_b, n_ctx_tiles):
    b, t, ch = xbc.shape
    tm = ROW_TILE
    nt = t // tm
    per = tm // 8
    last8 = t // 8 - 1
    return pl.pallas_call(
        functools.partial(_conv_body, n_ctx_tiles=n_ctx_tiles, n_tiles=nt),
        grid=(b, nt),
        in_specs=[
            pl.BlockSpec((None, tm, ch), lambda bi, ti: (bi, ti, 0)),
            pl.BlockSpec((None, 8, ch), lambda bi, ti: (bi, jnp.maximum(ti * per - 1, 0), 0)),
            pl.BlockSpec((None, 8, ch), lambda bi, ti: (bi, jnp.minimum((ti + 1) * per, last8), 0)),
            pl.BlockSpec((8, ch), lambda bi, ti: (0, 0)),
            pl.BlockSpec((1, ch), lambda bi, ti: (0, 0)),
        ],
        out_specs=pl.BlockSpec((None, tm, ch), lambda bi, ti: (bi, ti, 0)),
        out_shape=jax.ShapeDtypeStruct((b, t, ch), F32),
        scratch_shapes=[pltpu.VMEM((tm + 16, ch), F32)],
        compiler_params=_params(2),
        name="ssd_conv",
    )(xbc, xbc, xbc, conv_w, conv_b)


def _ssd_chunk_index(d, s, n_ctx_chunks, n_chunks):
    bwd = jnp.where(s < n_ctx_chunks, n_ctx_chunks - 1 - s, n_chunks - 1 - (s - n_ctx_chunks))
    return jnp.where(d == 0, s, bwd)


def _ssd_body(xbc_ref, dt_ref, z_ref, aneg_ref, dtb_ref, dsk_ref, ng_ref, o_ref,
              hst_ref, ysc_ref, *, n_ctx_chunks, n_chunks):
    d = pl.program_id(1)
    s = pl.program_id(2)
    c = _ssd_chunk_index(d, s, n_ctx_chunks, n_chunks)
    q = SSD_CHUNK

    @pl.when(s == 0)
    def _():
        hst_ref[...] = jnp.zeros_like(hst_ref)

    def run(direction):
        xbc = xbc_ref[...]
        x = xbc[:, 0:512]
        bm = xbc[:, 512:640]
        cm = xbc[:, 640:768]
        pre = dt_ref[...] + dtb_ref[...]
        dtv = jnp.maximum(pre, 0.0) + jnp.log(1.0 + jnp.exp(-jnp.abs(pre)))
        a = dtv * aneg_ref[...]
        row = lax.broadcasted_iota(jnp.int32, (q, q), 0)
        col = lax.broadcasted_iota(jnp.int32, (q, q), 1)
        tri = (row >= col) if direction == 0 else (row <= col)
        a_hi = a.astype(BF16)
        r1 = a - a_hi.astype(F32)
        a_mid = r1.astype(BF16)
        a_lo = (r1 - a_mid.astype(F32)).astype(BF16)
        parts = jnp.dot(tri.astype(BF16), jnp.concatenate([a_hi, a_mid, a_lo], axis=1),
                        preferred_element_type=F32)
        acs = parts[:, 0:LANES] + parts[:, LANES:2 * LANES] + parts[:, 2 * LANES:3 * LANES]
        acs_t = acs.T
        last = acs[q - 1:q, :] if direction == 0 else acs[0:1, :]
        dte = jnp.exp(last - acs)
        expa = jnp.exp(acs)
        cdec = jnp.broadcast_to(jnp.exp(last), (8, LANES))
        erow = lax.broadcasted_iota(jnp.int32, (LANES, SSD_INNER), 0)
        ecol = lax.broadcasted_iota(jnp.int32, (LANES, SSD_INNER), 1)
        expand = (erow == ecol // SSD_HEAD_DIM + SSD_HEADS * direction).astype(BF16)
        spread = jnp.dot(jnp.concatenate([dtv, dtv * dte], axis=0).astype(BF16), expand,
                         preferred_element_type=F32)
        xdt = x * spread[0:q]
        xs = x * spread[q:2 * q]
        fine = jnp.concatenate([expa, cdec], axis=0)
        fine_hi = fine.astype(BF16)
        fine_lo = (fine - fine_hi.astype(F32)).astype(BF16)
        spread2 = (jnp.dot(fine_hi, expand, preferred_element_type=F32)
                   + jnp.dot(fine_lo, expand, preferred_element_type=F32))
        expa_x = spread2[0:q]
        cdec_x = spread2[q:q + 1]
        bt = bm.T.astype(BF16)
        bb = bm.astype(BF16)
        lane = lax.broadcasted_iota(jnp.int32, (1, LANES), 1)
        ys = []
        for g in range(2):
            cg = jnp.where(lane // SSD_STATE == g, cm, 0.0).astype(BF16)
            gmat = lax.dot_general(cg, bb, (((1,), (1,)), ((), ())), preferred_element_type=F32)
            for hp in range(2):
                p = g * 2 + hp
                sl = slice(p * LANES, (p + 1) * LANES)
                ypair = None
                for e in range(2):
                    j = SSD_HEADS * direction + 2 * p + e
                    seg = acs[:, j:j + 1] - acs_t[j:j + 1, :]
                    dec = jnp.exp(jnp.where(tri, seg, -jnp.inf))
                    xh = jnp.where(lane // SSD_HEAD_DIM == e, xdt[:, sl], 0.0)
                    term = _bdot(gmat * dec, xh)
                    ypair = term if ypair is None else ypair + term
                hprev = hst_ref[p]
                ypair = ypair + _bdot(cg, hprev) * expa_x[:, sl]
                hst_ref[p] = hprev * cdec_x[:, sl] + jnp.dot(
                    bt, xs[:, sl].astype(BF16), preferred_element_type=F32)
                ys.append(ypair)
        return x, ys

    @pl.when(d == 0)
    def _():
        _, ys = run(0)
        for p in range(4):
            ysc_ref[c, :, p * LANES:(p + 1) * LANES] = ys[p]

    @pl.when(d == 1)
    def _():
        x, ys = run(1)
        y = jnp.concatenate(ys, axis=1) + ysc_ref[c] + dsk_ref[...] * x
        y = y * _silu(z_ref[...])
        o_ref[...] = _rms(y, ng_ref[...]).astype(o_ref.dtype)


def _ssd_call(xbc_act, dt, z, aneg, dtb, dsk, ng, n_ctx_chunks):
    b, t, _ = xbc_act.shape
    q = SSD_CHUNK
    nc = t // q
    cidx = functools.partial(_ssd_chunk_index, n_ctx_chunks=n_ctx_chunks, n_chunks=nc)

    def late(bi, di, si):
        return (bi, jnp.where(di == 0, n_ctx_chunks - 1, cidx(di, si)), 0)

    return pl.pallas_call(
        functools.partial(_ssd_body, n_ctx_chunks=n_ctx_chunks, n_chunks=nc),
        grid=(b, 2, nc),
        in_specs=[
            pl.BlockSpec((None, q, SSD_CONV_CH), lambda bi, di, si: (bi, cidx(di, si), 0)),
            pl.BlockSpec((None, q, SSD_DT_PAD), lambda bi, di, si: (bi, cidx(di, si), 0)),
            pl.BlockSpec((None, q, SSD_INNER), late),
            pl.BlockSpec((1, LANES), lambda bi, di, si: (0, 0)),
            pl.BlockSpec((1, LANES), lambda bi, di, si: (0, 0)),
            pl.BlockSpec((1, SSD_INNER), lambda bi, di, si: (0, 0)),
            pl.BlockSpec((1, SSD_INNER), lambda bi, di, si: (0, 0)),
        ],
        out_specs=pl.BlockSpec((None, q, SSD_INNER), late),
        out_shape=jax.ShapeDtypeStruct((b, t, SSD_INNER), BF16),
        scratch_shapes=[pltpu.VMEM((4, LANES, LANES), F32),
                        pltpu.VMEM((nc, q, SSD_INNER), F32)],
        compiler_params=_params(3),
        name="ssd_scan",
    )(xbc_act, dt, z, aneg, dtb, dsk, ng)


def _flash(qs, k_ref, v_ref, m_ref, l_ref, acc_ref, *, tk, n_kv, ctx_len, is_ctx_tile):
    m_ref[...] = jnp.full(m_ref.shape, -jnp.inf, F32)
    l_ref[...] = jnp.zeros(l_ref.shape, F32)
    acc_ref[...] = jnp.zeros(acc_ref.shape, F32)

    def step(start, size, limit):
        for i, q in enumerate(qs):
            lanes = slice(i * LANES, (i + 1) * LANES)
            k = k_ref[start:start + size, lanes]
            v = v_ref[start:start + size, lanes]
            s = lax.dot_general(q, k, (((1,), (1,)), ((), ())), preferred_element_type=F32)
            if limit is not None:
                col = lax.broadcasted_iota(jnp.int32, (1, size), 1) + start
                s = jnp.where(col < limit, s, -jnp.inf)
            m_old = m_ref[i]
            m_new = jnp.maximum(m_old, jnp.max(s, axis=-1, keepdims=True))
            alpha = jnp.exp2(m_old - m_new)
            p = jnp.exp2(s - jnp.concatenate([m_new] * (size // LANES), axis=1))
            psum = p[:, 0:LANES]
            for j in range(1, size // LANES):
                psum = psum + p[:, j * LANES:(j + 1) * LANES]
            l_ref[i] = alpha * l_ref[i] + psum
            acc_ref[i] = alpha * acc_ref[i] + jnp.dot(p.astype(BF16), v, preferred_element_type=F32)
            m_ref[i] = m_new

    @pl.when(is_ctx_tile)
    def _():
        ctx_pad = -(-ctx_len // LANES) * LANES
        for start in range(0, ctx_pad, tk):
            step(start, min(tk, ctx_pad - start), ctx_len if ctx_pad != ctx_len else None)

    @pl.when(jnp.logical_not(is_ctx_tile))
    def _():
        for kc in range(n_kv):
            step(kc * tk, tk, None)

    return [acc_ref[i] * (1.0 / jnp.sum(l_ref[i], axis=-1, keepdims=True)) for i in range(len(qs))]


DIFF_HEADS_PER_STEP = 2


def _diff_attn_body(q_ref, k_ref, v_ref, lam_ref, g_ref, o_ref, m_ref, l_ref, acc_ref,
                    *, tq, lam_init, n_ctx_q, **kw):
    qi = pl.program_id(2)
    qs = [q_ref[i].reshape(2 * tq, LANES) for i in range(DIFF_HEADS_PER_STEP)]
    outs = _flash(qs, k_ref, v_ref, m_ref, l_ref, acc_ref, is_ctx_tile=qi < n_ctx_q, **kw)
    lv = lam_ref[...]
    lam = (jnp.exp(jnp.sum(lv[0:1] * lv[1:2], axis=-1, keepdims=True))
           - jnp.exp(jnp.sum(lv[2:3] * lv[3:4], axis=-1, keepdims=True)) + lam_init)
    for i, o in enumerate(outs):
        out = o[0:tq] - lam * o[tq:2 * tq]
        o_ref[:, i * LANES:(i + 1) * LANES] = (_rms(out, g_ref[...]) * (1.0 - lam_init)).astype(o_ref.dtype)


def _gqa_attn_body(q_ref, k_ref, v_ref, o_ref, m_ref, l_ref, acc_ref, *, tq, n_ctx_q, **kw):
    n = pl.program_id(1)
    qi = pl.program_id(2)
    q = q_ref[...].reshape(4 * tq, LANES)
    (o,) = _flash([q], k_ref, v_ref, m_ref, l_ref, acc_ref, is_ctx_tile=qi < n_ctx_q, **kw)
    lane = lax.broadcasted_iota(jnp.int32, (1, LANES), 1)
    lo_half = lane < HEAD_DIM
    for pair in range(2):
        a = o[(2 * pair) * tq:(2 * pair + 1) * tq]
        b = o[(2 * pair + 1) * tq:(2 * pair + 2) * tq]
        from_lo = jnp.where(lo_half, a, pltpu.roll(b, HEAD_DIM, 1))
        from_hi = jnp.where(lo_half, pltpu.roll(a, HEAD_DIM, 1), b)
        o_ref[:, pair * LANES:(pair + 1) * LANES] = jnp.where(n == 0, from_lo, from_hi).astype(o_ref.dtype)


def _attn_tiles(t, ctx_len):
    tq = ROW_TILE
    tk = next(c for c in (2816, 768, 256) if t % c == 0)
    assert ctx_len % tq == 0 and t % tq == 0 and t % tk == 0
    return dict(tq=tq, tk=tk, n_kv=t // tk, ctx_len=ctx_len, n_ctx_q=ctx_len // tq)


def _diff_attn_call(qd, kd, vd, lamv, subln_g, lam_init, ctx_len):
    b, _, _, t, _ = qd.shape
    cfg = _attn_tiles(t, ctx_len)
    tq = cfg["tq"]
    r = 2 * tq
    hs = DIFF_HEADS_PER_STEP
    return pl.pallas_call(
        functools.partial(_diff_attn_body, lam_init=lam_init, **cfg),
        grid=(b, DIFF_HEADS // hs, t // tq),
        in_specs=[
            pl.BlockSpec((None, hs, 2, tq, LANES), lambda bi, hi, qi: (bi, hi, 0, qi, 0)),
            pl.BlockSpec((None, t, hs * LANES), lambda bi, hi, qi: (bi, 0, hi)),
            pl.BlockSpec((None, t, hs * LANES), lambda bi, hi, qi: (bi, 0, hi)),
            pl.BlockSpec((4, LANES), lambda bi, hi, qi: (0, 0)),
            pl.BlockSpec((1, LANES), lambda bi, hi, qi: (0, 0)),
        ],
        out_specs=pl.BlockSpec((None, tq, hs * LANES), lambda bi, hi, qi: (bi, qi, hi)),
        out_shape=jax.ShapeDtypeStruct((b, t, DIFF_HEADS * LANES), BF16),
        scratch_shapes=[pltpu.VMEM((hs, r, LANES), F32)] * 3,
        compiler_params=_params(3),
        name="diff_attention",
    )(qd, kd, vd, lamv, subln_g)


def _gqa_attn_call(qg, kg, vg, ctx_len):
    b, _, _, t, _ = qg.shape
    cfg = _attn_tiles(t, ctx_len)
    tq = cfg["tq"]
    r = 4 * tq
    return pl.pallas_call(
        functools.partial(_gqa_attn_body, **cfg),
        grid=(b, GQA_KV_HEADS, t // tq),
        in_specs=[
            pl.BlockSpec((None, None, 4, tq, LANES), lambda bi, ni, qi: (bi, ni, 0, qi, 0)),
            pl.BlockSpec((None, t, LANES), lambda bi, ni, qi: (bi, 0, 0)),
            pl.BlockSpec((None, t, LANES), lambda bi, ni, qi: (bi, 0, 0)),
        ],
        out_specs=pl.BlockSpec((None, tq, 2 * LANES), lambda bi, ni, qi: (bi, qi, ni)),
        out_shape=jax.ShapeDtypeStruct((b, t, GQA_HEADS * HEAD_DIM), BF16),
        scratch_shapes=[pltpu.VMEM((1, r, LANES), F32)] * 3,
        compiler_params=_params(3),
        name="gqa_attention",
    )(qg, kg, vg)


def _cmul(ar, ai, br, bi):
    return ar * br - ai * bi, ar * bi + ai * br


def _stack_rows(er, ei, vr, vi):
    re = [er * vr[c:c + 1] - ei * vi[c:c + 1] for c in range(S5_GROUP_CH)]
    im = [er * vi[c:c + 1] + ei * vr[c:c + 1] for c in range(S5_GROUP_CH)]
    return jnp.concatenate(re, axis=0), jnp.concatenate(im, axis=0)


def _split_dot_t(a, b):
    dn = (((1,), (1,)), ((), ()))
    ah = a.astype(BF16)
    al = (a - ah.astype(F32)).astype(BF16)
    bh = b.astype(BF16)
    bl = (b - bh.astype(F32)).astype(BF16)
    return (lax.dot_general(ah, bh, dn, preferred_element_type=F32)
            + lax.dot_general(ah, bl, dn, preferred_element_type=F32)
            + lax.dot_general(al, bh, dn, preferred_element_type=F32))


def _s5_weights_body(lam_ref, bt_ref, c_ref, wi_ref, ws_ref, wo_ref, av_ref):
    tc = S5_CHUNK
    lam = lam_ref[...]
    br_t, bi_t = bt_ref[0:16, :], bt_ref[16:32, :]
    cr, ci = c_ref[0:16, :], c_ref[16:32, :]
    kk = lax.broadcasted_iota(jnp.int32, (tc, S5_STATE), 0).astype(F32)
    k8 = lax.broadcasted_iota(jnp.int32, (8, S5_STATE), 0)
    k8 = jnp.where(k8 == 0, 1.0, jnp.where(k8 == 1, tc - 1.0, float(tc)))
    row = lax.broadcasted_iota(jnp.int32, (S5_FLAT, S5_FLAT), 0) % tc
    col = lax.broadcasted_iota(jnp.int32, (S5_FLAT, S5_FLAT), 1) % tc
    w_intra = None
    state_cols, out_cols, a_rows = [], [], []
    for direction in range(2):
        lr = lam[2 * direction:2 * direction + 1]
        li = lam[2 * direction + 1:2 * direction + 2]
        step = jnp.exp(lam[4 + direction:5 + direction])
        mag = jnp.exp(lr * step)
        ar, ai = mag * jnp.cos(li * step), mag * jnp.sin(li * step)
        den = lr * lr + li * li
        fr = ((ar - 1.0) * lr + ai * li) / den
        fi = (ai * lr - (ar - 1.0) * li) / den
        bbr = fr * br_t - fi * bi_t
        bbi = fr * bi_t + fi * br_t
        cs, sn = jnp.cos(kk * (li * step)), jnp.sin(kk * (li * step))
        grow, decay = jnp.exp(-kk * (lr * step)), jnp.exp(kk * (lr * step))
        pr, pi = decay * cs, decay * sn
        nr, ni = grow * cs, -grow * sn
        m8 = jnp.exp(k8 * (lr * step))
        c8r, c8i = m8 * jnp.cos(k8 * (li * step)), m8 * jnp.sin(k8 * (li * step))
        a_one = (c8r[0:1], c8i[0:1])
        a_last = (c8r[1:2], c8i[1:2])
        a_tc = (c8r[2:3], c8i[2:3])
        if direction == 0:
            x_e, y_e = (nr, ni), (pr, pi)
            s_e = _cmul(nr, ni, *a_last)
            o_e = _cmul(pr, pi, *a_one)
            keep = col >= row
        else:
            x_e, y_e = (pr, pi), (nr, ni)
            s_e = (pr, pi)
            o_e = _cmul(nr, ni, *a_tc)
            keep = row >= col
        xr, xi = _stack_rows(*x_e, bbr, bbi)
        yr, yi = _stack_rows(*y_e, cr, ci)
        full = _split_dot_t(jnp.concatenate([xr, -xi], axis=1), jnp.concatenate([yr, yi], axis=1))
        part = jnp.where(keep, full, 0.0)
        w_intra = part if w_intra is None else w_intra + part
        state_cols.append(_stack_rows(*s_e, bbr, bbi))
        o_r, o_i = _stack_rows(*o_e, cr, ci)
        out_cols += [o_r, -o_i]
        a_rows += [jnp.concatenate([a_tc[0], a_tc[0]], axis=1),
                   jnp.concatenate([-a_tc[1], a_tc[1]], axis=1),
                   jnp.concatenate([a_tc[1], -a_tc[1]], axis=1)]
    wi_ref[...] = w_intra.astype(BF16)
    (fr_, fi_), (br_, bi_) = state_cols
    ws_ref[...] = jnp.concatenate([fr_, fi_, br_, bi_, fi_, fr_, bi_, br_], axis=1).astype(BF16)
    wo_ref[...] = jnp.concatenate(out_cols, axis=1).astype(BF16)
    zero = jnp.zeros((1, LANES), F32)
    av_ref[...] = jnp.concatenate(a_rows + [zero, zero], axis=0)


def _s5_weights_call(lam_re, lam_im, log_dt, b_re, b_im, c_re, c_im):
    g, p = S5_GROUPS, S5_STATE
    bc = lambda v: jnp.broadcast_to(v.astype(F32)[:, None], (g, p))
    zero = jnp.zeros((g, p), F32)
    lam = jnp.stack([lam_re[0], lam_im[0], lam_re[1], lam_im[1], bc(log_dt[0]), bc(log_dt[1]),
                     zero, zero], axis=1).astype(F32)
    bt = jnp.concatenate([jnp.swapaxes(b_re, 1, 2), jnp.swapaxes(b_im, 1, 2)], axis=1).astype(F32)
    cc = jnp.concatenate([c_re, c_im], axis=1).astype(F32)
    n = S5_FLAT
    return pl.pallas_call(
        _s5_weights_body,
        grid=(g,),
        in_specs=[pl.BlockSpec((None, 8, p), lambda gi: (gi, 0, 0)),
                  pl.BlockSpec((None, 32, p), lambda gi: (gi, 0, 0)),
                  pl.BlockSpec((None, 32, p), lambda gi: (gi, 0, 0))],
        out_specs=[pl.BlockSpec((None, n, n), lambda gi: (gi, 0, 0)),
                   pl.BlockSpec((None, n, 4 * LANES), lambda gi: (gi, 0, 0)),
                   pl.BlockSpec((None, n, 2 * LANES), lambda gi: (gi, 0, 0)),
                   pl.BlockSpec((None, 8, LANES), lambda gi: (gi, 0, 0))],
        out_shape=[jax.ShapeDtypeStruct((g, n, n), BF16),
                   jax.ShapeDtypeStruct((g, n, 4 * LANES), BF16),
                   jax.ShapeDtypeStruct((g, n, 2 * LANES), BF16),
                   jax.ShapeDtypeStruct((g, 8, LANES), F32)],
        compiler_params=_params(1),
        name="s5_weights",
    )(lam, bt, cc)


def _s5_body(u_ref, wi_ref, ws_ref, wo_ref, av_ref, y_ref, s_ref, h_ref, *, n_ctx_chunks, n_chunks):
    u = u_ref[...]
    s_ref[...] = jnp.dot(u, ws_ref[...], preferred_element_type=F32)
    av = av_ref[...]
    rows = S5_ROWS
    nj = n_chunks

    def bcast(i):
        return jnp.broadcast_to(av[i:i + 1], (rows, LANES))

    a1f, a2f, a2sf, a1b, a2b, a2sb = [bcast(i) for i in range(6)]

    def body(i, carry):
        hf, hfs, hb, hbs = carry
        jf = pl.multiple_of(i * rows, rows)
        jb_chunk = jnp.where(i < n_ctx_chunks, n_ctx_chunks - 1 - i, nj - 1 - (i - n_ctx_chunks))
        jb = pl.multiple_of(jb_chunk * rows, rows)
        h_ref[pl.ds(jf, rows), 0:LANES] = hf
        h_ref[pl.ds(jb, rows), LANES:2 * LANES] = hb
        sf = s_ref[pl.ds(jf, rows), 0:LANES]
        sfs = s_ref[pl.ds(jf, rows), 2 * LANES:3 * LANES]
        sb = s_ref[pl.ds(jb, rows), LANES:2 * LANES]
        sbs = s_ref[pl.ds(jb, rows), 3 * LANES:4 * LANES]
        return (a1f * hf + a2f * hfs + sf, a1f * hfs + a2sf * hf + sfs,
                a1b * hb + a2b * hbs + sb, a1b * hbs + a2sb * hb + sbs)

    zero = jnp.zeros((rows, LANES), F32)
    lax.fori_loop(0, nj, body, (zero, zero, zero, zero))
    y_ref[...] = (jnp.dot(u, wi_ref[...], preferred_element_type=F32)
                  + lax.dot_general(h_ref[...].astype(BF16), wo_ref[...], (((1,), (1,)), ((), ())),
                                    preferred_element_type=F32))


def _s5_call(ug, w_intra, w_state, w_out, avec, n_ctx_chunks):
    g, r, w = ug.shape
    nj = r // S5_ROWS
    return pl.pallas_call(
        functools.partial(_s5_body, n_ctx_chunks=n_ctx_chunks, n_chunks=nj),
        grid=(g,),
        in_specs=[
            pl.BlockSpec((None, r, w), lambda gi: (gi, 0, 0)),
            pl.BlockSpec((None, w, w), lambda gi: (gi, 0, 0)),
            pl.BlockSpec((None, w, 4 * LANES), lambda gi: (gi, 0, 0)),
            pl.BlockSpec((None, w, 2 * LANES), lambda gi: (gi, 0, 0)),
            pl.BlockSpec((None, 8, LANES), lambda gi: (gi, 0, 0)),
        ],
        out_specs=pl.BlockSpec((None, r, w), lambda gi: (gi, 0, 0)),
        out_shape=jax.ShapeDtypeStruct((g, r, w), F32),
        scratch_shapes=[pltpu.VMEM((r, 4 * LANES), F32), pltpu.VMEM((r, 2 * LANES), F32)],
        compiler_params=_params(1),
        name="s5_scan",
    )(ug, w_intra, w_state, w_out, avec)


def _s5_to_groups(u):
    b, t, _ = u.shape
    nj = t // S5_CHUNK
    x = jnp.swapaxes(u.astype(BF16).reshape(b, nj, S5_CHUNK, S5_WIDTH), 2, 3)
    x = lax.optimization_barrier(x.reshape(b, nj, S5_GROUPS, S5_FLAT))
    x = jnp.pad(jnp.transpose(x, (2, 1, 0, 3)), ((0, 0), (0, 0), (0, S5_ROWS - b), (0, 0)))
    return x.reshape(S5_GROUPS, nj * S5_ROWS, S5_FLAT)


def _s5_from_groups(y, b):
    g, r, n = y.shape
    nj = r // S5_ROWS
    x = jnp.transpose(y.reshape(g, nj, S5_ROWS, n)[:, :, :b], (2, 1, 0, 3))
    x = lax.optimization_barrier(x).reshape(b, nj, S5_WIDTH, S5_CHUNK)
    return jnp.swapaxes(x, 2, 3).reshape(b, nj * S5_CHUNK, S5_WIDTH)


def _merge_body(h_ref, mod_ref, xn_ref, ya_ref, yb_ref, yc_ref, u_ref, y5_ref,
                wg_ref, wa_ref, wb_ref, wc_ref, wd_ref, glw_ref, glb_ref, s5d_ref, wo_ref, o_ref):
    xn = xn_ref[...]
    y5 = y5_ref[...] + s5d_ref[...] * u_ref[...]
    gelu = 0.5 * y5 * (1.0 + jnp.tanh(0.7978845608028654 * (y5 + 0.044715 * y5 * y5 * y5)))
    glu = _bdot(gelu, glw_ref[...]) + glb_ref[...]
    yd = glu[:, 0:S5_WIDTH] * _sigmoid(glu[:, S5_WIDTH:2 * S5_WIDTH])
    branches = ((ya_ref[...], wa_ref), (yb_ref[...], wb_ref), (yc_ref[...], wc_ref),
                (yd.astype(BF16), wd_ref))
    merged = None
    for i, (y, w_ref) in enumerate(branches):
        gate = _sigmoid(jnp.dot(xn, wg_ref[i], preferred_element_type=F32))
        term = gate * jnp.dot(y, w_ref[...], preferred_element_type=F32)
        merged = term if merged is None else merged + term
    out = _bdot(merged, wo_ref[...])
    o_ref[...] = h_ref[...] + mod_ref[2:3, :] * out


def _merge_call(h, modv, xn, ya, yb, yc, u, y5, wts, n_ctx_tiles):
    b, t, d = h.shape
    tm = ROW_TILE

    def row(width):
        return pl.BlockSpec((None, tm, width), lambda bi, ti: (bi, ti, 0))

    def const(shape):
        return pl.BlockSpec(shape, lambda bi, ti: (0,) * len(shape), pipeline_mode=pl.Buffered(1))

    w_gate, w_a, w_b, w_c, w_d, glu_w, glu_b, s5_d, w_out = wts
    return pl.pallas_call(
        _merge_body,
        grid=(b, t // tm),
        in_specs=[row(d),
                  pl.BlockSpec((None, None, 6, d),
                               lambda bi, ti: (bi, jnp.where(ti >= n_ctx_tiles, 1, 0), 0, 0)),
                  row(d), row(512), row(512), row(512), row(S5_WIDTH), row(S5_WIDTH),
                  const((4, d, d)), const((512, d)), const((512, d)), const((512, d)),
                  const((S5_WIDTH, d)), const((S5_WIDTH, 2 * S5_WIDTH)), const((1, 2 * S5_WIDTH)),
                  const((1, S5_WIDTH)), const((d, d))],
        out_specs=row(d),
        out_shape=jax.ShapeDtypeStruct((b, t, d), F32),
        compiler_params=_params(2),
        name="branch_merge",
    )(h, modv, xn, ya, yb, yc, u, y5, w_gate, w_a, w_b, w_c, w_d, glu_w, glu_b, s5_d, w_out)


def _ffn_body(h_ref, mod_ref, g_ref, wgu_ref, wd_ref, o_ref):
    h = h_ref[...]
    mod = mod_ref[...]
    xf = (_rms(h, g_ref[...]) * (1.0 + mod[4:5]) + mod[3:4]).astype(BF16)
    gate = jnp.dot(xf, wgu_ref[:, 0:FFN_HIDDEN], preferred_element_type=F32)
    up = jnp.dot(xf, wgu_ref[:, FFN_HIDDEN:2 * FFN_HIDDEN], preferred_element_type=F32)
    act = (_silu(gate) * up).astype(BF16)
    o_ref[...] = h + mod[5:6] * jnp.dot(act, wd_ref[...], preferred_element_type=F32)


def _ffn_call(h, modv, g2, w_gu, w_down, n_ctx_tiles, latent_only):
    b, t, d = h.shape
    tm = ROW_TILE
    skip = n_ctx_tiles if latent_only else 0

    def const(shape):
        return pl.BlockSpec(shape, lambda bi, ti: (0,) * len(shape), pipeline_mode=pl.Buffered(1))

    return pl.pallas_call(
        _ffn_body,
        grid=(b, t // tm - skip),
        in_specs=[pl.BlockSpec((None, tm, d), lambda bi, ti: (bi, ti + skip, 0)),
                  pl.BlockSpec((None, None, 6, d),
                               lambda bi, ti: (bi, jnp.where(ti + skip >= n_ctx_tiles, 1, 0), 0, 0)),
                  const((1, d)), const((d, 2 * FFN_HIDDEN)), const((FFN_HIDDEN, d))],
        out_specs=pl.BlockSpec((None, tm, d), lambda bi, ti: (bi, ti, 0)),
        out_shape=jax.ShapeDtypeStruct((b, t - skip * tm, d), F32),
        compiler_params=_params(2),
        name="swiglu_ffn",
    )(h, modv, g2, w_gu, w_down)


def _rope_tables(ctx_len, seq_len):
    n_rows = seq_len // GRID_W
    rows = jnp.repeat(jnp.arange(n_rows, dtype=F32), GRID_W)
    cols = jnp.tile(jnp.arange(GRID_W, dtype=F32), n_rows)
    quarter = HEAD_DIM // 4
    inv_freq = ROPE_THETA ** (-jnp.arange(quarter, dtype=F32) / quarter)
    ang_r = rows[:, None] * inv_freq
    ang_c = cols[:, None] * inv_freq
    ang = jnp.concatenate([ang_r, ang_r, ang_c, ang_c], axis=-1)
    cos = jnp.concatenate([jnp.ones((ctx_len, HEAD_DIM), F32), jnp.cos(ang)], axis=0)
    sin = jnp.concatenate([jnp.zeros((ctx_len, HEAD_DIM), F32), jnp.sin(ang)], axis=0)
    first = (jnp.arange(HEAD_DIM) % 32) < 16
    sin_a = jnp.where(first, -sin, 0.0)
    sin_b = jnp.where(first, 0.0, sin)
    two = lambda m: jnp.concatenate([m, m], axis=1)
    return jnp.concatenate([two(cos), two(sin_a), two(sin_b)], axis=1)


def _w_in_layout(w_in):
    d = w_in.shape[0]
    a0 = 0
    z = w_in[:, a0:a0 + 512]
    xbc = w_in[:, a0 + 512:a0 + 1280]
    dt = jnp.pad(w_in[:, a0 + 1280:a0 + 1296], ((0, 0), (0, SSD_DT_PAD - 16)))
    rest = w_in[:, 1296:]
    out = jnp.concatenate([z, xbc, dt, rest], axis=1).astype(BF16)
    assert out.shape == (d, _C_END)
    return out


def kernel(x, c, ctx, c_ctx, w_mod, b_mod, norm1_g, norm2_g, w_in, ssd_conv_w, ssd_conv_b, ssd_a_log, ssd_dt_bias, ssd_d, ssd_norm_g, diff_qn_g, diff_kn_g, diff_lam_q1, diff_lam_k1, diff_lam_q2, diff_lam_k2, diff_subln_g, gqa_qn_g, gqa_kn_g, s5_lam_re, s5_lam_im, s5_log_dt, s5_b_re, s5_b_im, s5_c_re, s5_c_im, s5_d, s5_glu_w, s5_glu_b, w_gate, w_br_ssd, w_br_diff, w_br_gqa, w_br_s5, w_out, ffn_w_gate_up, ffn_w_down):
    b, seq_len, d = x.shape
    ctx_len = ctx.shape[1]
    depth = w_mod.shape[0]
    assert b <= S5_ROWS and b + 1 <= 8
    assert ctx_len % ROW_TILE == 0 and seq_len % ROW_TILE == 0
    n_ctx_tiles = ctx_len // ROW_TILE

    h = jnp.concatenate([ctx, x], axis=1)
    cc = jnp.concatenate([c, c_ctx[None], jnp.zeros((8 - b - 1, d), F32)], axis=0)
    mods = _mod_call(cc, w_mod, b_mod)

    rope_tab = _rope_tables(ctx_len, seq_len)
    blk = jnp.arange(512) // HEAD_DIM
    ones = (blk[:, None] == blk[None, :]).astype(BF16)
    tile8 = lambda g: jnp.tile(g.astype(F32), 512 // HEAD_DIM)

    for layer in range(depth):
        m = mods[layer]
        lat = m[:b].reshape(b, 6, d)
        cmod = jnp.broadcast_to(m[b].reshape(1, 6, d), (b, 6, d))
        modv = jnp.stack([cmod, lat], axis=1)

        gains = jnp.stack([tile8(diff_qn_g[layer]), tile8(diff_kn_g[layer]),
                           tile8(gqa_qn_g[layer]), tile8(gqa_kn_g[layer])]
                          + [jnp.zeros((512,), F32)] * 4, axis=0)
        (xn, z, xbc, dt, qd, kd, vd, qg, kg, vg, u) = _in_proj_call(
            h, modv, norm1_g[layer][None], _w_in_layout(w_in[layer]), ones, gains, rope_tab,
            n_ctx_tiles)

        conv_w = jnp.pad(ssd_conv_w[layer].astype(F32), ((0, 8 - SSD_CONV_W), (0, 0)))
        xbc_act = _conv_call(xbc, conv_w, ssd_conv_b[layer][None].astype(F32), n_ctx_tiles)
        pad16 = lambda v: jnp.pad(v.reshape(1, 16).astype(F32), ((0, 0), (0, LANES - 16)))
        aneg = pad16(-jnp.exp(ssd_a_log[layer].astype(F32)))
        dtb = pad16(ssd_dt_bias[layer])
        dsk = jnp.repeat(ssd_d[layer].astype(F32), SSD_HEAD_DIM)[None]
        ya = _ssd_call(xbc_act, dt, z, aneg, dtb, dsk, ssd_norm_g[layer][None].astype(F32),
                       ctx_len // SSD_CHUNK)

        lam_init = 0.8 - 0.6 * math.exp(-0.3 * layer)
        lamv = jnp.pad(jnp.stack([diff_lam_q1[layer], diff_lam_k1[layer],
                                  diff_lam_q2[layer], diff_lam_k2[layer]]).astype(F32),
                       ((0, 0), (0, LANES - HEAD_DIM)))
        yb = _diff_attn_call(qd, kd, vd, lamv, diff_subln_g[layer][None].astype(F32), lam_init, ctx_len)

        yc = _gqa_attn_call(qg, kg, vg, ctx_len)

        s5w = _s5_weights_call(s5_lam_re[layer], s5_lam_im[layer], s5_log_dt[layer],
                               s5_b_re[layer], s5_b_im[layer], s5_c_re[layer], s5_c_im[layer])
        y5 = _s5_from_groups(_s5_call(_s5_to_groups(u), *s5w, ctx_len // S5_CHUNK), b)

        wts = (w_gate[layer].astype(BF16), w_br_ssd[layer].astype(BF16), w_br_diff[layer].astype(BF16),
               w_br_gqa[layer].astype(BF16), w_br_s5[layer].astype(BF16), s5_glu_w[layer].astype(BF16),
               s5_glu_b[layer][None].astype(F32), s5_d[layer][None].astype(F32), w_out[layer].astype(BF16))
        h = _merge_call(h, modv, xn, ya, yb, yc, u, y5, wts, n_ctx_tiles)
        h = _ffn_call(h, modv, norm2_g[layer][None].astype(F32), ffn_w_gate_up[layer].astype(BF16),
                      ffn_w_down[layer].astype(BF16), n_ctx_tiles, latent_only=layer == depth - 1)
    return h
```

```python
import functools
import math

import jax
import jax.numpy as jnp
from jax import lax
from jax.experimental import pallas as pl
from jax.experimental.pallas import tpu as pltpu

F32 = jnp.float32
BF16 = jnp.bfloat16
HIGHEST = lax.Precision.HIGHEST

D_MODEL = 1024
GRID_W = 64
ROPE_THETA = 10000.0
NORM_EPS = 1e-6

SSD_INNER = 512
SSD_HEADS = 8
SSD_HEAD_DIM = 64
SSD_STATE = 64
SSD_CHUNK = 128
SSD_CONV_W = 5
SSD_CONV_CH = 768
SSD_DT_PAD = 128
SSD_BATCH_PER_STEP = 2

DIFF_HEADS = 4
HEAD_DIM = 64
GQA_HEADS = 8
GQA_KV_HEADS = 2

S5_GROUP_CH = 16
S5_STATE = 64
S5_WIDTH = 384
S5_GROUPS = 24
S5_CHUNK = 64
S5_FLAT = S5_CHUNK * S5_GROUP_CH

FFN_HIDDEN = 2816

ROW_TILE = 256
LANES = 128
VMEM_LIMIT = 56 * 1024 * 1024

_C_Z, _C_XBC, _C_DT = 0, 512, 1280
_C_QD, _C_KD, _C_VD = 1408, 1920, 2432
_C_QG, _C_KG, _C_VG = 2944, 3456, 3584
_C_U, _C_END = 3712, 4096


def _params(n_grid):
    return pltpu.CompilerParams(dimension_semantics=("arbitrary",) * n_grid,
                                vmem_limit_bytes=VMEM_LIMIT)


def _bdot(a, b):
    return jnp.dot(a.astype(BF16), b.astype(BF16), preferred_element_type=F32)


def _hdot(a, b):
    return jnp.dot(a, b, precision=HIGHEST, preferred_element_type=F32)


def _rms(x, g):
    return x * lax.rsqrt(jnp.mean(x * x, axis=-1, keepdims=True) + NORM_EPS) * g


def _sigmoid(x):
    return 1.0 / (1.0 + jnp.exp(-x))


def _silu(x):
    return x * _sigmoid(x)


def _mod_body(c_ref, w_ref, b_ref, o_ref):
    o_ref[...] = _hdot(_silu(c_ref[...]), w_ref[...]) + b_ref[...]


def _mod_call(cc, w_mod, b_mod):
    depth = w_mod.shape[0]
    d = D_MODEL
    return pl.pallas_call(
        _mod_body,
        grid=(depth, 6),
        in_specs=[pl.BlockSpec((8, d), lambda l, j: (0, 0)),
                  pl.BlockSpec((None, d, d), lambda l, j: (l, 0, j)),
                  pl.BlockSpec((None, 1, d), lambda l, j: (l, 0, j))],
        out_specs=pl.BlockSpec((None, 8, d), lambda l, j: (l, 0, j)),
        out_shape=jax.ShapeDtypeStruct((depth, 8, 6 * d), F32),
        compiler_params=_params(2),
        name="adaln_mod",
    )(cc, w_mod, b_mod.reshape(depth, 1, 6 * d))


def _head_rms(t, gain, ones):
    ss = jnp.dot((t * t).astype(BF16), ones, preferred_element_type=F32)
    return t * lax.rsqrt(ss * (1.0 / HEAD_DIM) + NORM_EPS) * gain


def _rope(t, cos, sin_a, sin_b):
    w = t.shape[-1]
    return t * cos + pltpu.roll(t, w - 16, 1) * sin_a + pltpu.roll(t, 16, 1) * sin_b


def _in_proj_body(h_ref, mod_ref, g_ref, w_ref, ones_ref, gains_ref, rope_ref,
                  xn_ref, z_ref, xbc_ref, dt_ref, qd_ref, kd_ref, vd_ref,
                  qg_ref, kg_ref, vg_ref, u_ref):
    x = h_ref[...]
    mod = mod_ref[...]
    xn = _rms(x, g_ref[...]) * (1.0 + mod[1:2]) + mod[0:1]
    xb = xn.astype(BF16)
    xn_ref[...] = xb

    def proj(lo, hi):
        return jnp.dot(xb, w_ref[:, lo:hi], preferred_element_type=F32)

    z_ref[...] = proj(_C_Z, _C_XBC)
    xbc_ref[...] = proj(_C_XBC, _C_DT)
    dt_ref[...] = proj(_C_DT, _C_QD)
    u_ref[...] = proj(_C_U, _C_END)

    rope = rope_ref[...]
    cos1, sa1, sb1 = rope[:, 0:128], rope[:, 128:256], rope[:, 256:384]
    cos4 = jnp.concatenate([cos1] * 4, axis=1)
    sa4 = jnp.concatenate([sa1] * 4, axis=1)
    sb4 = jnp.concatenate([sb1] * 4, axis=1)
    ones = ones_ref[...]
    gains = gains_ref[...]
    lane = lax.broadcasted_iota(jnp.int32, (1, LANES), 1)
    lo_half = lane < HEAD_DIM
    scale = HEAD_DIM ** -0.5 * math.log2(math.e)

    qd = _rope(_head_rms(proj(_C_QD, _C_KD), gains[0:1], ones), cos4, sa4, sb4) * scale
    kd = _rope(_head_rms(proj(_C_KD, _C_VD), gains[1:2], ones), cos4, sa4, sb4)
    kd_ref[...] = kd.astype(BF16)
    vd_ref[...] = proj(_C_VD, _C_QG).astype(BF16)
    for h in range(DIFF_HEADS):
        blk = qd[:, h * LANES:(h + 1) * LANES]
        qd_ref[h, 0] = jnp.where(lo_half, blk, 0.0).astype(BF16)
        qd_ref[h, 1] = jnp.where(lo_half, 0.0, blk).astype(BF16)

    qg = _rope(_head_rms(proj(_C_QG, _C_KG), gains[2:3], ones), cos4, sa4, sb4) * scale
    kg = _rope(_head_rms(proj(_C_KG, _C_VG), gains[3:4, :LANES], ones[:LANES, :LANES]),
               cos1, sa1, sb1)
    kg_ref[...] = kg.astype(BF16)
    vg_ref[...] = proj(_C_VG, _C_U).astype(BF16)
    qg_up = pltpu.roll(qg, 4 * LANES - HEAD_DIM, 1)
    qg_dn = pltpu.roll(qg, HEAD_DIM, 1)
    per_kv = GQA_HEADS // GQA_KV_HEADS
    for n in range(GQA_KV_HEADS):
        for i in range(per_kv):
            j = n * per_kv + i
            blk_idx = j // 2
            if (j % 2) == n:
                src = qg
            else:
                src = qg_up if n == 0 else qg_dn
            blk = src[:, blk_idx * LANES:(blk_idx + 1) * LANES]
            keep = lo_half if n == 0 else jnp.logical_not(lo_half)
            qg_ref[n, i] = jnp.where(keep, blk, 0.0).astype(BF16)


def _in_proj_call(h, modv, g1, w_cat, ones, gains, rope_tab, n_ctx_tiles):
    b, t, d = h.shape
    nt = t // ROW_TILE
    tm = ROW_TILE

    def row(width, dtype):
        return (pl.BlockSpec((None, tm, width), lambda bi, ti: (bi, ti, 0)),
                jax.ShapeDtypeStruct((b, t, width), dtype))

    outs = [row(d, BF16), row(512, F32), row(768, F32), row(SSD_DT_PAD, F32)]
    qd = (pl.BlockSpec((None, DIFF_HEADS, 2, tm, LANES), lambda bi, ti: (bi, 0, 0, ti, 0)),
          jax.ShapeDtypeStruct((b, DIFF_HEADS, 2, t, LANES), BF16))
    qg = (pl.BlockSpec((None, GQA_KV_HEADS, 4, tm, LANES), lambda bi, ti: (bi, 0, 0, ti, 0)),
          jax.ShapeDtypeStruct((b, GQA_KV_HEADS, 4, t, LANES), BF16))
    outs += [qd, row(512, BF16), row(512, BF16), qg, row(LANES, BF16), row(LANES, BF16),
             row(S5_WIDTH, F32)]
    return pl.pallas_call(
        _in_proj_body,
        grid=(b, nt),
        in_specs=[
            pl.BlockSpec((None, tm, d), lambda bi, ti: (bi, ti, 0)),
            pl.BlockSpec((None, None, 6, d),
                         lambda bi, ti: (bi, jnp.where(ti >= n_ctx_tiles, 1, 0), 0, 0)),
            pl.BlockSpec((1, d), lambda bi, ti: (0, 0)),
            pl.BlockSpec((d, _C_END), lambda bi, ti: (0, 0)),
            pl.BlockSpec((512, 512), lambda bi, ti: (0, 0)),
            pl.BlockSpec((8, 512), lambda bi, ti: (0, 0)),
            pl.BlockSpec((tm, 384), lambda bi, ti: (ti, 0)),
        ],
        out_specs=[o[0] for o in outs],
        out_shape=[o[1] for o in outs],
        compiler_params=_params(2),
        name="in_proj",
    )(h, modv, g1, w_cat, ones, gains, rope_tab)


def _conv_body(cur_ref, prev_ref, next_ref, w_ref, b_ref, o_ref, buf_ref, *, n_ctx_tiles, n_tiles):
    t = pl.program_id(1)
    tm = ROW_TILE
    has_prev = jnp.logical_and(t != 0, t != n_ctx_tiles)
    has_next = jnp.logical_and(t != n_ctx_tiles - 1, t != n_tiles - 1)
    buf_ref[0:8, :] = jnp.where(has_prev, prev_ref[...], 0.0)
    buf_ref[8:8 + tm, :] = cur_ref[...]
    buf_ref[8 + tm:16 + tm, :] = jnp.where(has_next, next_ref[...], 0.0)
    w = w_ref[...]
    bias = b_ref[...]
    pad = (SSD_CONV_W - 1) // 2
    rows = 64
    for r in range(tm // rows):
        acc = bias + w[0:1] * buf_ref[8 - pad + r * rows:8 - pad + (r + 1) * rows, :]
        for k in range(1, SSD_CONV_W):
            lo = 8 - pad + k + r * rows
            acc = acc + w[k:k + 1] * buf_ref[lo:lo + rows, :]
        o_ref[r * rows:(r + 1) * rows, :] = _silu(acc)


def _conv_call(xbc, conv_w, conv_b, n_ctx_tiles):
    b, t, ch = xbc.shape
    tm = ROW_TILE
    nt = t // tm
    per = tm // 8
    last8 = t // 8 - 1
    return pl.pallas_call(
        functools.partial(_conv_body, n_ctx_tiles=n_ctx_tiles, n_tiles=nt),
        grid=(b, nt),
        in_specs=[
            pl.BlockSpec((None, tm, ch), lambda bi, ti: (bi, ti, 0)),
            pl.BlockSpec((None, 8, ch), lambda bi, ti: (bi, jnp.maximum(ti * per - 1, 0), 0)),
            pl.BlockSpec((None, 8, ch), lambda bi, ti: (bi, jnp.minimum((ti + 1) * per, last8), 0)),
            pl.BlockSpec((8, ch), lambda bi, ti: (0, 0)),
            pl.BlockSpec((1, ch), lambda bi, ti: (0, 0)),
        ],
        out_specs=pl.BlockSpec((None, tm, ch), lambda bi, ti: (bi, ti, 0)),
        out_shape=jax.ShapeDtypeStruct((b, t, ch), F32),
        scratch_shapes=[pltpu.VMEM((tm + 16, ch), F32)],
        compiler_params=_params(2),
        name="ssd_conv",
    )(xbc, xbc, xbc, conv_w, conv_b)


def _ssd_chunk_index(d, s, n_ctx_chunks, n_chunks):
    bwd = jnp.where(s < n_ctx_chunks, n_ctx_chunks - 1 - s, n_chunks - 1 - (s - n_ctx_chunks))
    return jnp.where(d == 0, s, bwd)


def _ssd_body(xbc_ref, dt_ref, z_ref, aneg_ref, dtb_ref, dsk_ref, ng_ref, o_ref,
              hst_ref, ysc_ref, *, n_ctx_chunks, n_chunks):
    d = pl.program_id(1)
    s = pl.program_id(2)
    c = _ssd_chunk_index(d, s, n_ctx_chunks, n_chunks)
    q = SSD_CHUNK
    nb = xbc_ref.shape[0]

    @pl.when(s == 0)
    def _():
        hst_ref[...] = jnp.zeros_like(hst_ref)

    def run(direction, bi):
        xbc = xbc_ref[bi]
        x = xbc[:, 0:512]
        bm = xbc[:, 512:640]
        cm = xbc[:, 640:768]
        pre = dt_ref[bi] + dtb_ref[...]
        dtv = jnp.maximum(pre, 0.0) + jnp.log(1.0 + jnp.exp(-jnp.abs(pre)))
        a = dtv * aneg_ref[...]
        row = lax.broadcasted_iota(jnp.int32, (q, q), 0)
        col = lax.broadcasted_iota(jnp.int32, (q, q), 1)
        tri = (row >= col) if direction == 0 else (row <= col)
        a_hi = a.astype(BF16)
        r1 = a - a_hi.astype(F32)
        a_mid = r1.astype(BF16)
        a_lo = (r1 - a_mid.astype(F32)).astype(BF16)
        parts = jnp.dot(tri.astype(BF16), jnp.concatenate([a_hi, a_mid, a_lo], axis=1),
                        preferred_element_type=F32)
        acs = parts[:, 0:LANES] + parts[:, LANES:2 * LANES] + parts[:, 2 * LANES:3 * LANES]
        acs_t = acs.T
        last = acs[q - 1:q, :] if direction == 0 else acs[0:1, :]
        dte = jnp.exp(last - acs)
        expa = jnp.exp(acs)
        cdec = jnp.broadcast_to(jnp.exp(last), (8, LANES))
        erow = lax.broadcasted_iota(jnp.int32, (LANES, SSD_INNER), 0)
        ecol = lax.broadcasted_iota(jnp.int32, (LANES, SSD_INNER), 1)
        expand = (erow == ecol // SSD_HEAD_DIM + SSD_HEADS * direction).astype(BF16)
        spread = jnp.dot(jnp.concatenate([dtv, dtv * dte], axis=0).astype(BF16), expand,
                         preferred_element_type=F32)
        xdt = x * spread[0:q]
        xs = x * spread[q:2 * q]
        fine = jnp.concatenate([expa, cdec], axis=0)
        fine_hi = fine.astype(BF16)
        fine_lo = (fine - fine_hi.astype(F32)).astype(BF16)
        spread2 = (jnp.dot(fine_hi, expand, preferred_element_type=F32)
                   + jnp.dot(fine_lo, expand, preferred_element_type=F32))
        expa_x = spread2[0:q]
        cdec_x = spread2[q:q + 1]
        bt = bm.T.astype(BF16)
        bb = bm.astype(BF16)
        lane = lax.broadcasted_iota(jnp.int32, (1, LANES), 1)
        ys = []
        for g in range(2):
            cg = jnp.where(lane // SSD_STATE == g, cm, 0.0).astype(BF16)
            gmat = lax.dot_general(cg, bb, (((1,), (1,)), ((), ())), preferred_element_type=F32)
            for hp in range(2):
                p = g * 2 + hp
                sl = slice(p * LANES, (p + 1) * LANES)
                ypair = None
                for e in range(2):
                    j = SSD_HEADS * direction + 2 * p + e
                    seg = acs[:, j:j + 1] - acs_t[j:j + 1, :]
                    dec = jnp.exp(jnp.where(tri, seg, -jnp.inf))
                    xh = jnp.where(lane // SSD_HEAD_DIM == e, xdt[:, sl], 0.0)
                    term = _bdot(gmat * dec, xh)
                    ypair = term if ypair is None else ypair + term
                hprev = hst_ref[bi, p]
                ypair = ypair + _bdot(cg, hprev) * expa_x[:, sl]
                hst_ref[bi, p] = hprev * cdec_x[:, sl] + jnp.dot(
                    bt, xs[:, sl].astype(BF16), preferred_element_type=F32)
                ys.append(ypair)
        return x, ys

    @pl.when(d == 0)
    def _():
        for bi in range(nb):
            _, ys = run(0, bi)
            for p in range(4):
                ysc_ref[bi, c, :, p * LANES:(p + 1) * LANES] = ys[p]

    @pl.when(d == 1)
    def _():
        for bi in range(nb):
            x, ys = run(1, bi)
            y = jnp.concatenate(ys, axis=1) + ysc_ref[bi, c] + dsk_ref[...] * x
            y = y * _silu(z_ref[bi])
            o_ref[bi] = _rms(y, ng_ref[...]).astype(o_ref.dtype)


def _ssd_call(xbc_act, dt, z, aneg, dtb, dsk, ng, n_ctx_chunks):
    b, t, _ = xbc_act.shape
    q = SSD_CHUNK
    nc = t // q
    nb = SSD_BATCH_PER_STEP if b % SSD_BATCH_PER_STEP == 0 else 1
    cidx = functools.partial(_ssd_chunk_index, n_ctx_chunks=n_ctx_chunks, n_chunks=nc)

    def late(bi, di, si):
        return (bi, jnp.where(di == 0, n_ctx_chunks - 1, cidx(di, si)), 0)

    return pl.pallas_call(
        functools.partial(_ssd_body, n_ctx_chunks=n_ctx_chunks, n_chunks=nc),
        grid=(b // nb, 2, nc),
        in_specs=[
            pl.BlockSpec((nb, q, SSD_CONV_CH), lambda bi, di, si: (bi, cidx(di, si), 0)),
            pl.BlockSpec((nb, q, SSD_DT_PAD), lambda bi, di, si: (bi, cidx(di, si), 0)),
            pl.BlockSpec((nb, q, SSD_INNER), late),
            pl.BlockSpec((1, LANES), lambda bi, di, si: (0, 0)),
            pl.BlockSpec((1, LANES), lambda bi, di, si: (0, 0)),
            pl.BlockSpec((1, SSD_INNER), lambda bi, di, si: (0, 0)),
            pl.BlockSpec((1, SSD_INNER), lambda bi, di, si: (0, 0)),
        ],
        out_specs=pl.BlockSpec((nb, q, SSD_INNER), late),
        out_shape=jax.ShapeDtypeStruct((b, t, SSD_INNER), BF16),
        scratch_shapes=[pltpu.VMEM((nb, 4, LANES, LANES), F32),
                        pltpu.VMEM((nb, nc, q, SSD_INNER), F32)],
        compiler_params=_params(3),
        name="ssd_scan",
    )(xbc_act, dt, z, aneg, dtb, dsk, ng)


def _flash(qs, k_ref, v_ref, m_ref, l_ref, acc_ref, *, tk, n_kv, ctx_len, is_ctx_tile):
    m_ref[...] = jnp.full(m_ref.shape, -jnp.inf, F32)
    l_ref[...] = jnp.zeros(l_ref.shape, F32)
    acc_ref[...] = jnp.zeros(acc_ref.shape, F32)

    def step(start, size, limit):
        for i, q in enumerate(qs):
            lanes = slice(i * LANES, (i + 1) * LANES)
            k = k_ref[start:start + size, lanes]
            v = v_ref[start:start + size, lanes]
            s = lax.dot_general(q, k, (((1,), (1,)), ((), ())), preferred_element_type=F32)
            if limit is not None:
                col = lax.broadcasted_iota(jnp.int32, (1, size), 1) + start
                s = jnp.where(col < limit, s, -jnp.inf)
            m_old = m_ref[i]
            m_new = jnp.maximum(m_old, jnp.max(s, axis=-1, keepdims=True))
            alpha = jnp.exp2(m_old - m_new)
            p = jnp.exp2(s - jnp.concatenate([m_new] * (size // LANES), axis=1))
            psum = p[:, 0:LANES]
            for j in range(1, size // LANES):
                psum = psum + p[:, j * LANES:(j + 1) * LANES]
            l_ref[i] = alpha * l_ref[i] + psum
            acc_ref[i] = alpha * acc_ref[i] + jnp.dot(p.astype(BF16), v, preferred_element_type=F32)
            m_ref[i] = m_new

    @pl.when(is_ctx_tile)
    def _():
        ctx_pad = -(-ctx_len // LANES) * LANES
        for start in range(0, ctx_pad, tk):
            step(start, min(tk, ctx_pad - start), ctx_len if ctx_pad != ctx_len else None)

    @pl.when(jnp.logical_not(is_ctx_tile))
    def _():
        for kc in range(n_kv):
            step(kc * tk, tk, None)

    return [acc_ref[i] * (1.0 / jnp.sum(l_ref[i], axis=-1, keepdims=True)) for i in range(len(qs))]


DIFF_HEADS_PER_STEP = 2


def _diff_attn_body(q_ref, k_ref, v_ref, lam_ref, g_ref, o_ref, m_ref, l_ref, acc_ref,
                    *, tq, lam_init, n_ctx_q, **kw):
    qi = pl.program_id(2)
    qs = [q_ref[i].reshape(2 * tq, LANES) for i in range(DIFF_HEADS_PER_STEP)]
    outs = _flash(qs, k_ref, v_ref, m_ref, l_ref, acc_ref, is_ctx_tile=qi < n_ctx_q, **kw)
    lv = lam_ref[...]
    lam = (jnp.exp(jnp.sum(lv[0:1] * lv[1:2], axis=-1, keepdims=True))
           - jnp.exp(jnp.sum(lv[2:3] * lv[3:4], axis=-1, keepdims=True)) + lam_init)
    for i, o in enumerate(outs):
        out = o[0:tq] - lam * o[tq:2 * tq]
        o_ref[:, i * LANES:(i + 1) * LANES] = (_rms(out, g_ref[...]) * (1.0 - lam_init)).astype(o_ref.dtype)


def _gqa_attn_body(q_ref, k_ref, v_ref, o_ref, m_ref, l_ref, acc_ref, *, tq, n_ctx_q, **kw):
    n = pl.program_id(1)
    qi = pl.program_id(2)
    q = q_ref[...].reshape(4 * tq, LANES)
    (o,) = _flash([q], k_ref, v_ref, m_ref, l_ref, acc_ref, is_ctx_tile=qi < n_ctx_q, **kw)
    lane = lax.broadcasted_iota(jnp.int32, (1, LANES), 1)
    lo_half = lane < HEAD_DIM
    for pair in range(2):
        a = o[(2 * pair) * tq:(2 * pair + 1) * tq]
        b = o[(2 * pair + 1) * tq:(2 * pair + 2) * tq]
        from_lo = jnp.where(lo_half, a, pltpu.roll(b, HEAD_DIM, 1))
        from_hi = jnp.where(lo_half, pltpu.roll(a, HEAD_DIM, 1), b)
        o_ref[:, pair * LANES:(pair + 1) * LANES] = jnp.where(n == 0, from_lo, from_hi).astype(o_ref.dtype)


def _attn_tiles(t, ctx_len):
    tq = ROW_TILE
    tk = next(c for c in (2816, 768, 256) if t % c == 0)
    assert ctx_len % tq == 0 and t % tq == 0 and t % tk == 0
    return dict(tq=tq, tk=tk, n_kv=t // tk, ctx_len=ctx_len, n_ctx_q=ctx_len // tq)


def _diff_attn_call(qd, kd, vd, lamv, subln_g, lam_init, ctx_len):
    b, _, _, t, _ = qd.shape
    cfg = _attn_tiles(t, ctx_len)
    tq = cfg["tq"]
    r = 2 * tq
    hs = DIFF_HEADS_PER_STEP
    return pl.pallas_call(
        functools.partial(_diff_attn_body, lam_init=lam_init, **cfg),
        grid=(b, DIFF_HEADS // hs, t // tq),
        in_specs=[
            pl.BlockSpec((None, hs, 2, tq, LANES), lambda bi, hi, qi: (bi, hi, 0, qi, 0)),
            pl.BlockSpec((None, t, hs * LANES), lambda bi, hi, qi: (bi, 0, hi)),
            pl.BlockSpec((None, t, hs * LANES), lambda bi, hi, qi: (bi, 0, hi)),
            pl.BlockSpec((4, LANES), lambda bi, hi, qi: (0, 0)),
            pl.BlockSpec((1, LANES), lambda bi, hi, qi: (0, 0)),
        ],
        out_specs=pl.BlockSpec((None, tq, hs * LANES), lambda bi, hi, qi: (bi, qi, hi)),
        out_shape=jax.ShapeDtypeStruct((b, t, DIFF_HEADS * LANES), BF16),
        scratch_shapes=[pltpu.VMEM((hs, r, LANES), F32)] * 3,
        compiler_params=_params(3),
        name="diff_attention",
    )(qd, kd, vd, lamv, subln_g)


def _gqa_attn_call(qg, kg, vg, ctx_len):
    b, _, _, t, _ = qg.shape
    cfg = _attn_tiles(t, ctx_len)
    tq = cfg["tq"]
    r = 4 * tq
    return pl.pallas_call(
        functools.partial(_gqa_attn_body, **cfg),
        grid=(b, GQA_KV_HEADS, t // tq),
        in_specs=[
            pl.BlockSpec((None, None, 4, tq, LANES), lambda bi, ni, qi: (bi, ni, 0, qi, 0)),
            pl.BlockSpec((None, t, LANES), lambda bi, ni, qi: (bi, 0, 0)),
            pl.BlockSpec((None, t, LANES), lambda bi, ni, qi: (bi, 0, 0)),
        ],
        out_specs=pl.BlockSpec((None, tq, 2 * LANES), lambda bi, ni, qi: (bi, qi, ni)),
        out_shape=jax.ShapeDtypeStruct((b, t, GQA_HEADS * HEAD_DIM), BF16),
        scratch_shapes=[pltpu.VMEM((1, r, LANES), F32)] * 3,
        compiler_params=_params(3),
        name="gqa_attention",
    )(qg, kg, vg)


def _cmul(ar, ai, br, bi):
    return ar * br - ai * bi, ar * bi + ai * br


def _stack_rows(er, ei, vr, vi):
    re = [er * vr[c:c + 1] - ei * vi[c:c + 1] for c in range(S5_GROUP_CH)]
    im = [er * vi[c:c + 1] + ei * vr[c:c + 1] for c in range(S5_GROUP_CH)]
    return jnp.concatenate(re, axis=0), jnp.concatenate(im, axis=0)


def _split_dot_t(a, b):
    dn = (((1,), (1,)), ((), ()))
    ah = a.astype(BF16)
    al = (a - ah.astype(F32)).astype(BF16)
    bh = b.astype(BF16)
    bl = (b - bh.astype(F32)).astype(BF16)
    return (lax.dot_general(ah, bh, dn, preferred_element_type=F32)
            + lax.dot_general(ah, bl, dn, preferred_element_type=F32)
            + lax.dot_general(al, bh, dn, preferred_element_type=F32))


def _s5_weights_body(lam_ref, bt_ref, c_ref, wi_ref, ws_ref, wo_ref, av_ref):
    tc = S5_CHUNK
    lam = lam_ref[...]
    br_t, bi_t = bt_ref[0:16, :], bt_ref[16:32, :]
    cr, ci = c_ref[0:16, :], c_ref[16:32, :]
    kk = lax.broadcasted_iota(jnp.int32, (tc, S5_STATE), 0).astype(F32)
    k8 = lax.broadcasted_iota(jnp.int32, (8, S5_STATE), 0)
    k8 = jnp.where(k8 == 0, 1.0, jnp.where(k8 == 1, tc - 1.0, float(tc)))
    row = lax.broadcasted_iota(jnp.int32, (S5_FLAT, S5_FLAT), 0) % tc
    col = lax.broadcasted_iota(jnp.int32, (S5_FLAT, S5_FLAT), 1) % tc
    w_intra = None
    state_cols, out_cols, a_rows = [], [], []
    for direction in range(2):
        lr = lam[2 * direction:2 * direction + 1]
        li = lam[2 * direction + 1:2 * direction + 2]
        step = jnp.exp(lam[4 + direction:5 + direction])
        mag = jnp.exp(lr * step)
        ar, ai = mag * jnp.cos(li * step), mag * jnp.sin(li * step)
        den = lr * lr + li * li
        fr = ((ar - 1.0) * lr + ai * li) / den
        fi = (ai * lr - (ar - 1.0) * li) / den
        bbr = fr * br_t - fi * bi_t
        bbi = fr * bi_t + fi * br_t
        cs, sn = jnp.cos(kk * (li * step)), jnp.sin(kk * (li * step))
        grow, decay = jnp.exp(-kk * (lr * step)), jnp.exp(kk * (lr * step))
        pr, pi = decay * cs, decay * sn
        nr, ni = grow * cs, -grow * sn
        m8 = jnp.exp(k8 * (lr * step))
        c8r, c8i = m8 * jnp.cos(k8 * (li * step)), m8 * jnp.sin(k8 * (li * step))
        a_one = (c8r[0:1], c8i[0:1])
        a_last = (c8r[1:2], c8i[1:2])
        a_tc = (c8r[2:3], c8i[2:3])
        if direction == 0:
            x_e, y_e = (nr, ni), (pr, pi)
            s_e = _cmul(nr, ni, *a_last)
            o_e = _cmul(pr, pi, *a_one)
            keep = col >= row
        else:
            x_e, y_e = (pr, pi), (nr, ni)
            s_e = (pr, pi)
            o_e = _cmul(nr, ni, *a_tc)
            keep = row >= col
        xr, xi = _stack_rows(*x_e, bbr, bbi)
        yr, yi = _stack_rows(*y_e, cr, ci)
        full = _split_dot_t(jnp.concatenate([xr, -xi], axis=1), jnp.concatenate([yr, yi], axis=1))
        part = jnp.where(keep, full, 0.0)
        w_intra = part if w_intra is None else w_intra + part
        state_cols.append(_stack_rows(*s_e, bbr, bbi))
        o_r, o_i = _stack_rows(*o_e, cr, ci)
        out_cols += [o_r, -o_i]
        a_rows += [jnp.concatenate([a_tc[0], a_tc[0]], axis=1),
                   jnp.concatenate([-a_tc[1], a_tc[1]], axis=1),
                   jnp.concatenate([a_tc[1], -a_tc[1]], axis=1)]
    wi_ref[...] = w_intra.astype(BF16)
    (fr_, fi_), (br_, bi_) = state_cols
    ws_ref[...] = jnp.concatenate([fr_, fi_, br_, bi_, fi_, fr_, bi_, br_], axis=1).astype(BF16)
    wo_ref[...] = jnp.concatenate(out_cols, axis=1).astype(BF16)
    zero = jnp.zeros((1, LANES), F32)
    av_ref[...] = jnp.concatenate(a_rows + [zero, zero], axis=0)


def _s5_weights_call(lam_re, lam_im, log_dt, b_re, b_im, c_re, c_im):
    g, p = S5_GROUPS, S5_STATE
    bc = lambda v: jnp.broadcast_to(v.astype(F32)[:, None], (g, p))
    zero = jnp.zeros((g, p), F32)
    lam = jnp.stack([lam_re[0], lam_im[0], lam_re[1], lam_im[1], bc(log_dt[0]), bc(log_dt[1]),
                     zero, zero], axis=1).astype(F32)
    bt = jnp.concatenate([jnp.swapaxes(b_re, 1, 2), jnp.swapaxes(b_im, 1, 2)], axis=1).astype(F32)
    cc = jnp.concatenate([c_re, c_im], axis=1).astype(F32)
    n = S5_FLAT
    return pl.pallas_call(
        _s5_weights_body,
        grid=(g,),
        in_specs=[pl.BlockSpec((None, 8, p), lambda gi: (gi, 0, 0)),
                  pl.BlockSpec((None, 32, p), lambda gi: (gi, 0, 0)),
                  pl.BlockSpec((None, 32, p), lambda gi: (gi, 0, 0))],
        out_specs=[pl.BlockSpec((None, n, n), lambda gi: (gi, 0, 0)),
                   pl.BlockSpec((None, n, 4 * LANES), lambda gi: (gi, 0, 0)),
                   pl.BlockSpec((None, n, 2 * LANES), lambda gi: (gi, 0, 0)),
                   pl.BlockSpec((None, 8, LANES), lambda gi: (gi, 0, 0))],
        out_shape=[jax.ShapeDtypeStruct((g, n, n), BF16),
                   jax.ShapeDtypeStruct((g, n, 4 * LANES), BF16),
                   jax.ShapeDtypeStruct((g, n, 2 * LANES), BF16),
                   jax.ShapeDtypeStruct((g, 8, LANES), F32)],
        compiler_params=_params(1),
        name="s5_weights",
    )(lam, bt, cc)


def _s5_body(u_ref, wi_ref, ws_ref, wo_ref, av_ref, y_ref, s_ref, h_ref, *, n_ctx_chunks, n_chunks):
    u = u_ref[...]
    s_ref[...] = jnp.dot(u, ws_ref[...], preferred_element_type=F32)
    av = av_ref[...]
    nj = n_chunks
    rows = u_ref.shape[0] // nj
    a1f, a2f, a2sf, a1b, a2b, a2sb = [jnp.broadcast_to(av[i:i + 1], (rows, LANES)) for i in range(6)]
    hf = hfs = hb = hbs = jnp.zeros((rows, LANES), F32)
    for i in range(nj):
        jf = i * rows
        jb = (n_ctx_chunks - 1 - i if i < n_ctx_chunks else nj - 1 - (i - n_ctx_chunks)) * rows
        h_ref[jf:jf + rows, 0:LANES] = hf
        h_ref[jb:jb + rows, LANES:2 * LANES] = hb
        sf = s_ref[jf:jf + rows, 0:LANES]
        sfs = s_ref[jf:jf + rows, 2 * LANES:3 * LANES]
        sb = s_ref[jb:jb + rows, LANES:2 * LANES]
        sbs = s_ref[jb:jb + rows, 3 * LANES:4 * LANES]
        hf, hfs = a1f * hf + a2f * hfs + sf, a1f * hfs + a2sf * hf + sfs
        hb, hbs = a1b * hb + a2b * hbs + sb, a1b * hbs + a2sb * hb + sbs
    y_ref[...] = (jnp.dot(u, wi_ref[...], preferred_element_type=F32)
                  + lax.dot_general(h_ref[...].astype(BF16), wo_ref[...], (((1,), (1,)), ((), ())),
                                    preferred_element_type=F32))


def _s5_call(ug, w_intra, w_state, w_out, avec, n_ctx_chunks, n_chunks):
    g, r, w = ug.shape
    return pl.pallas_call(
        functools.partial(_s5_body, n_ctx_chunks=n_ctx_chunks, n_chunks=n_chunks),
        grid=(g,),
        in_specs=[
            pl.BlockSpec((None, r, w), lambda gi: (gi, 0, 0)),
            pl.BlockSpec((None, w, w), lambda gi: (gi, 0, 0)),
            pl.BlockSpec((None, w, 4 * LANES), lambda gi: (gi, 0, 0)),
            pl.BlockSpec((None, w, 2 * LANES), lambda gi: (gi, 0, 0)),
            pl.BlockSpec((None, 8, LANES), lambda gi: (gi, 0, 0)),
        ],
        out_specs=pl.BlockSpec((None, r, w), lambda gi: (gi, 0, 0)),
        out_shape=jax.ShapeDtypeStruct((g, r, w), F32),
        scratch_shapes=[pltpu.VMEM((r, 4 * LANES), F32), pltpu.VMEM((r, 2 * LANES), F32)],
        compiler_params=_params(1),
        name="s5_scan",
    )(ug, w_intra, w_state, w_out, avec)


def _s5_to_groups(u):
    b, t, _ = u.shape
    nj = t // S5_CHUNK
    x = jnp.swapaxes(u.astype(BF16).reshape(b, nj, S5_CHUNK, S5_WIDTH), 2, 3)
    x = lax.optimization_barrier(x.reshape(b, nj, S5_GROUPS, S5_FLAT))
    return jnp.transpose(x, (2, 1, 0, 3)).reshape(S5_GROUPS, nj * b, S5_FLAT)


def _s5_from_groups(y, b):
    g, r, n = y.shape
    nj = r // b
    x = jnp.transpose(y.reshape(g, nj, b, n), (2, 1, 0, 3))
    x = lax.optimization_barrier(x).reshape(b, nj, S5_WIDTH, S5_CHUNK)
    return jnp.swapaxes(x, 2, 3).reshape(b, nj * S5_CHUNK, S5_WIDTH)


def _merge_body(h_ref, mod_ref, xn_ref, ya_ref, yb_ref, yc_ref, u_ref, y5_ref,
                wg_ref, wa_ref, wb_ref, wc_ref, wd_ref, glw_ref, glb_ref, s5d_ref, wo_ref, o_ref):
    xn = xn_ref[...]
    y5 = y5_ref[...] + s5d_ref[...] * u_ref[...]
    gelu = 0.5 * y5 * (1.0 + jnp.tanh(0.7978845608028654 * (y5 + 0.044715 * y5 * y5 * y5)))
    glu = _bdot(gelu, glw_ref[...]) + glb_ref[...]
    yd = glu[:, 0:S5_WIDTH] * _sigmoid(glu[:, S5_WIDTH:2 * S5_WIDTH])
    branches = ((ya_ref[...], wa_ref), (yb_ref[...], wb_ref), (yc_ref[...], wc_ref),
                (yd.astype(BF16), wd_ref))
    merged = None
    for i, (y, w_ref) in enumerate(branches):
        gate = _sigmoid(jnp.dot(xn, wg_ref[i], preferred_element_type=F32))
        term = gate * jnp.dot(y, w_ref[...], preferred_element_type=F32)
        merged = term if merged is None else merged + term
    out = _bdot(merged, wo_ref[...])
    o_ref[...] = h_ref[...] + mod_ref[2:3, :] * out


def _merge_call(h, modv, xn, ya, yb, yc, u, y5, wts, n_ctx_tiles):
    b, t, d = h.shape
    tm = ROW_TILE

    def row(width):
        return pl.BlockSpec((None, tm, width), lambda bi, ti: (bi, ti, 0))

    def const(shape):
        return pl.BlockSpec(shape, lambda bi, ti: (0,) * len(shape), pipeline_mode=pl.Buffered(1))

    w_gate, w_a, w_b, w_c, w_d, glu_w, glu_b, s5_d, w_out = wts
    return pl.pallas_call(
        _merge_body,
        grid=(b, t // tm),
        in_specs=[row(d),
                  pl.BlockSpec((None, None, 6, d),
                               lambda bi, ti: (bi, jnp.where(ti >= n_ctx_tiles, 1, 0), 0, 0)),
                  row(d), row(512), row(512), row(512), row(S5_WIDTH), row(S5_WIDTH),
                  const((4, d, d)), const((512, d)), const((512, d)), const((512, d)),
                  const((S5_WIDTH, d)), const((S5_WIDTH, 2 * S5_WIDTH)), const((1, 2 * S5_WIDTH)),
                  const((1, S5_WIDTH)), const((d, d))],
        out_specs=row(d),
        out_shape=jax.ShapeDtypeStruct((b, t, d), F32),
        compiler_params=_params(2),
        name="branch_merge",
    )(h, modv, xn, ya, yb, yc, u, y5, w_gate, w_a, w_b, w_c, w_d, glu_w, glu_b, s5_d, w_out)


def _ffn_body(h_ref, mod_ref, g_ref, wgu_ref, wd_ref, o_ref):
    h = h_ref[...]
    mod = mod_ref[...]
    xf = (_rms(h, g_ref[...]) * (1.0 + mod[4:5]) + mod[3:4]).astype(BF16)
    gate = jnp.dot(xf, wgu_ref[:, 0:FFN_HIDDEN], preferred_element_type=F32)
    up = jnp.dot(xf, wgu_ref[:, FFN_HIDDEN:2 * FFN_HIDDEN], preferred_element_type=F32)
    act = (_silu(gate) * up).astype(BF16)
    o_ref[...] = h + mod[5:6] * jnp.dot(act, wd_ref[...], preferred_element_type=F32)


def _ffn_call(h, modv, g2, w_gu, w_down, n_ctx_tiles, latent_only):
    b, t, d = h.shape
    tm = ROW_TILE
    skip = n_ctx_tiles if latent_only else 0

    def const(shape):
        return pl.BlockSpec(shape, lambda bi, ti: (0,) * len(shape), pipeline_mode=pl.Buffered(1))

    return pl.pallas_call(
        _ffn_body,
        grid=(b, t // tm - skip),
        in_specs=[pl.BlockSpec((None, tm, d), lambda bi, ti: (bi, ti + skip, 0)),
                  pl.BlockSpec((None, None, 6, d),
                               lambda bi, ti: (bi, jnp.where(ti + skip >= n_ctx_tiles, 1, 0), 0, 0)),
                  const((1, d)), const((d, 2 * FFN_HIDDEN)), const((FFN_HIDDEN, d))],
        out_specs=pl.BlockSpec((None, tm, d), lambda bi, ti: (bi, ti, 0)),
        out_shape=jax.ShapeDtypeStruct((b, t - skip * tm, d), F32),
        compiler_params=_params(2),
        name="swiglu_ffn",
    )(h, modv, g2, w_gu, w_down)


def _rope_tables(ctx_len, seq_len):
    n_rows = seq_len // GRID_W
    rows = jnp.repeat(jnp.arange(n_rows, dtype=F32), GRID_W)
    cols = jnp.tile(jnp.arange(GRID_W, dtype=F32), n_rows)
    quarter = HEAD_DIM // 4
    inv_freq = ROPE_THETA ** (-jnp.arange(quarter, dtype=F32) / quarter)
    ang_r = rows[:, None] * inv_freq
    ang_c = cols[:, None] * inv_freq
    ang = jnp.concatenate([ang_r, ang_r, ang_c, ang_c], axis=-1)
    cos = jnp.concatenate([jnp.ones((ctx_len, HEAD_DIM), F32), jnp.cos(ang)], axis=0)
    sin = jnp.concatenate([jnp.zeros((ctx_len, HEAD_DIM), F32), jnp.sin(ang)], axis=0)
    first = (jnp.arange(HEAD_DIM) % 32) < 16
    sin_a = jnp.where(first, -sin, 0.0)
    sin_b = jnp.where(first, 0.0, sin)
    two = lambda m: jnp.concatenate([m, m], axis=1)
    return jnp.concatenate([two(cos), two(sin_a), two(sin_b)], axis=1)


def _w_in_layout(w_in):
    d = w_in.shape[0]
    a0 = 0
    z = w_in[:, a0:a0 + 512]
    xbc = w_in[:, a0 + 512:a0 + 1280]
    dt = jnp.pad(w_in[:, a0 + 1280:a0 + 1296], ((0, 0), (0, SSD_DT_PAD - 16)))
    rest = w_in[:, 1296:]
    out = jnp.concatenate([z, xbc, dt, rest], axis=1).astype(BF16)
    assert out.shape == (d, _C_END)
    return out


def kernel(x, c, ctx, c_ctx, w_mod, b_mod, norm1_g, norm2_g, w_in, ssd_conv_w, ssd_conv_b, ssd_a_log, ssd_dt_bias, ssd_d, ssd_norm_g, diff_qn_g, diff_kn_g, diff_lam_q1, diff_lam_k1, diff_lam_q2, diff_lam_k2, diff_subln_g, gqa_qn_g, gqa_kn_g, s5_lam_re, s5_lam_im, s5_log_dt, s5_b_re, s5_b_im, s5_c_re, s5_c_im, s5_d, s5_glu_w, s5_glu_b, w_gate, w_br_ssd, w_br_diff, w_br_gqa, w_br_s5, w_out, ffn_w_gate_up, ffn_w_down):
    b, seq_len, d = x.shape
    ctx_len = ctx.shape[1]
    depth = w_mod.shape[0]
    assert b + 1 <= 8
    assert ctx_len % ROW_TILE == 0 and seq_len % ROW_TILE == 0
    n_ctx_tiles = ctx_len // ROW_TILE

    h = jnp.concatenate([ctx, x], axis=1)
    cc = jnp.concatenate([c, c_ctx[None], jnp.zeros((8 - b - 1, d), F32)], axis=0)
    mods = _mod_call(cc, w_mod, b_mod)

    rope_tab = _rope_tables(ctx_len, seq_len)
    blk = jnp.arange(512) // HEAD_DIM
    ones = (blk[:, None] == blk[None, :]).astype(BF16)
    tile8 = lambda g: jnp.tile(g.astype(F32), 512 // HEAD_DIM)

    for layer in range(depth):
        m = mods[layer]
        lat = m[:b].reshape(b, 6, d)
        cmod = jnp.broadcast_to(m[b].reshape(1, 6, d), (b, 6, d))
        modv = jnp.stack([cmod, lat], axis=1)

        gains = jnp.stack([tile8(diff_qn_g[layer]), tile8(diff_kn_g[layer]),
                           tile8(gqa_qn_g[layer]), tile8(gqa_kn_g[layer])]
                          + [jnp.zeros((512,), F32)] * 4, axis=0)
        (xn, z, xbc, dt, qd, kd, vd, qg, kg, vg, u) = _in_proj_call(
            h, modv, norm1_g[layer][None], _w_in_layout(w_in[layer]), ones, gains, rope_tab,
            n_ctx_tiles)

        conv_w = jnp.pad(ssd_conv_w[layer].astype(F32), ((0, 8 - SSD_CONV_W), (0, 0)))
        xbc_act = _conv_call(xbc, conv_w, ssd_conv_b[layer][None].astype(F32), n_ctx_tiles)
        pad16 = lambda v: jnp.pad(v.reshape(1, 16).astype(F32), ((0, 0), (0, LANES - 16)))
        aneg = pad16(-jnp.exp(ssd_a_log[layer].astype(F32)))
        dtb = pad16(ssd_dt_bias[layer])
        dsk = jnp.repeat(ssd_d[layer].astype(F32), SSD_HEAD_DIM)[None]
        ya = _ssd_call(xbc_act, dt, z, aneg, dtb, dsk, ssd_norm_g[layer][None].astype(F32),
                       ctx_len // SSD_CHUNK)

        lam_init = 0.8 - 0.6 * math.exp(-0.3 * layer)
        lamv = jnp.pad(jnp.stack([diff_lam_q1[layer], diff_lam_k1[layer],
                                  diff_lam_q2[layer], diff_lam_k2[layer]]).astype(F32),
                       ((0, 0), (0, LANES - HEAD_DIM)))
        yb = _diff_attn_call(qd, kd, vd, lamv, diff_subln_g[layer][None].astype(F32), lam_init, ctx_len)

        yc = _gqa_attn_call(qg, kg, vg, ctx_len)

        s5w = _s5_weights_call(s5_lam_re[layer], s5_lam_im[layer], s5_log_dt[layer],
                               s5_b_re[layer], s5_b_im[layer], s5_c_re[layer], s5_c_im[layer])
        y5 = _s5_from_groups(_s5_call(_s5_to_groups(u), *s5w, ctx_len // S5_CHUNK,
                                      (ctx_len + seq_len) // S5_CHUNK), b)

        wts = (w_gate[layer].astype(BF16), w_br_ssd[layer].astype(BF16), w_br_diff[layer].astype(BF16),
               w_br_gqa[layer].astype(BF16), w_br_s5[layer].astype(BF16), s5_glu_w[layer].astype(BF16),
               s5_glu_b[layer][None].astype(F32), s5_d[layer][None].astype(F32), w_out[layer].astype(BF16))
        h = _merge_call(h, modv, xn, ya, yb, yc, u, y5, wts, n_ctx_tiles)
        h = _ffn_call(h, modv, norm2_g[layer][None].astype(F32), ffn_w_gate_up[layer].astype(BF16),
                      ffn_w_down[layer].astype(BF16), n_ctx_tiles, latent_only=layer == depth - 1)
    return h
```

```python
import functools
import math

import jax
import jax.numpy as jnp
from jax import lax
from jax.experimental import pallas as pl
from jax.experimental.pallas import tpu as pltpu

F32 = jnp.float32
BF16 = jnp.bfloat16
HIGHEST = lax.Precision.HIGHEST

D_MODEL = 1024
GRID_W = 64
ROPE_THETA = 10000.0
NORM_EPS = 1e-6

SSD_INNER = 512
SSD_HEADS = 8
SSD_HEAD_DIM = 64
SSD_STATE = 64
SSD_CHUNK = 128
SSD_CONV_W = 5
SSD_CONV_CH = 768
SSD_DT_PAD = 128
SSD_BATCH_PER_STEP = 2

DIFF_HEADS = 4
HEAD_DIM = 64
GQA_HEADS = 8
GQA_KV_HEADS = 2

S5_GROUP_CH = 16
S5_STATE = 64
S5_WIDTH = 384
S5_GROUPS = 24
S5_CHUNK = 64
S5_FLAT = S5_CHUNK * S5_GROUP_CH

FFN_HIDDEN = 2816

ROW_TILE = 256
LANES = 128
VMEM_LIMIT = 56 * 1024 * 1024

_C_Z, _C_XBC, _C_DT = 0, 512, 1280
_C_QD, _C_KD, _C_VD = 1408, 1920, 2432
_C_QG, _C_KG, _C_VG = 2944, 3456, 3584
_C_U, _C_END = 3712, 4096


def _params(n_grid):
    return pltpu.CompilerParams(dimension_semantics=("arbitrary",) * n_grid,
                                vmem_limit_bytes=VMEM_LIMIT)


def _bdot(a, b):
    return jnp.dot(a.astype(BF16), b.astype(BF16), preferred_element_type=F32)


def _hdot(a, b):
    return jnp.dot(a, b, precision=HIGHEST, preferred_element_type=F32)


def _rms(x, g):
    return x * lax.rsqrt(jnp.mean(x * x, axis=-1, keepdims=True) + NORM_EPS) * g


def _sigmoid(x):
    return 1.0 / (1.0 + jnp.exp(-x))


def _silu(x):
    return x * _sigmoid(x)


def _mod_body(c_ref, w_ref, b_ref, o_ref):
    o_ref[...] = _hdot(_silu(c_ref[...]), w_ref[...]) + b_ref[...]


def _mod_call(cc, w_mod, b_mod):
    depth = w_mod.shape[0]
    d = D_MODEL
    return pl.pallas_call(
        _mod_body,
        grid=(depth, 6),
        in_specs=[pl.BlockSpec((8, d), lambda l, j: (0, 0)),
                  pl.BlockSpec((None, d, d), lambda l, j: (l, 0, j)),
                  pl.BlockSpec((None, 1, d), lambda l, j: (l, 0, j))],
        out_specs=pl.BlockSpec((None, 8, d), lambda l, j: (l, 0, j)),
        out_shape=jax.ShapeDtypeStruct((depth, 8, 6 * d), F32),
        compiler_params=_params(2),
        name="adaln_mod",
    )(cc, w_mod, b_mod.reshape(depth, 1, 6 * d))


def _head_rms(t, gain, ones):
    ss = jnp.dot((t * t).astype(BF16), ones, preferred_element_type=F32)
    return t * lax.rsqrt(ss * (1.0 / HEAD_DIM) + NORM_EPS) * gain


def _rope(t, cos, sin_a, sin_b):
    w = t.shape[-1]
    return t * cos + pltpu.roll(t, w - 16, 1) * sin_a + pltpu.roll(t, 16, 1) * sin_b


def _in_proj_body(h_ref, mod_ref, g_ref, w_ref, ones_ref, gains_ref, rope_ref,
                  xn_ref, z_ref, xbc_ref, dt_ref, qd_ref, kd_ref, vd_ref,
                  qg_ref, kg_ref, vg_ref, u_ref):
    x = h_ref[...]
    mod = mod_ref[...]
    xn = _rms(x, g_ref[...]) * (1.0 + mod[1:2]) + mod[0:1]
    xb = xn.astype(BF16)
    xn_ref[...] = xb

    def proj(lo, hi):
        return jnp.dot(xb, w_ref[:, lo:hi], preferred_element_type=F32)

    z_ref[...] = proj(_C_Z, _C_XBC)
    xbc_ref[...] = proj(_C_XBC, _C_DT)
    dt_ref[...] = proj(_C_DT, _C_QD)
    u_ref[...] = proj(_C_U, _C_END).astype(u_ref.dtype)

    rope = rope_ref[...]
    cos1, sa1, sb1 = rope[:, 0:128], rope[:, 128:256], rope[:, 256:384]
    cos4 = jnp.concatenate([cos1] * 4, axis=1)
    sa4 = jnp.concatenate([sa1] * 4, axis=1)
    sb4 = jnp.concatenate([sb1] * 4, axis=1)
    ones = ones_ref[...]
    gains = gains_ref[...]
    lane = lax.broadcasted_iota(jnp.int32, (1, LANES), 1)
    lo_half = lane < HEAD_DIM
    scale = HEAD_DIM ** -0.5 * math.log2(math.e)

    qd = _rope(_head_rms(proj(_C_QD, _C_KD), gains[0:1], ones), cos4, sa4, sb4) * scale
    kd = _rope(_head_rms(proj(_C_KD, _C_VD), gains[1:2], ones), cos4, sa4, sb4)
    kd_ref[...] = kd.astype(BF16)
    vd_ref[...] = proj(_C_VD, _C_QG).astype(BF16)
    for h in range(DIFF_HEADS):
        blk = qd[:, h * LANES:(h + 1) * LANES]
        qd_ref[h, 0] = jnp.where(lo_half, blk, 0.0).astype(BF16)
        qd_ref[h, 1] = jnp.where(lo_half, 0.0, blk).astype(BF16)

    qg = _rope(_head_rms(proj(_C_QG, _C_KG), gains[2:3], ones), cos4, sa4, sb4) * scale
    kg = _rope(_head_rms(proj(_C_KG, _C_VG), gains[3:4, :LANES], ones[:LANES, :LANES]),
               cos1, sa1, sb1)
    kg_ref[...] = kg.astype(BF16)
    vg_ref[...] = proj(_C_VG, _C_U).astype(BF16)
    qg_up = pltpu.roll(qg, 4 * LANES - HEAD_DIM, 1)
    qg_dn = pltpu.roll(qg, HEAD_DIM, 1)
    per_kv = GQA_HEADS // GQA_KV_HEADS
    for n in range(GQA_KV_HEADS):
        for i in range(per_kv):
            j = n * per_kv + i
            blk_idx = j // 2
            if (j % 2) == n:
                src = qg
            else:
                src = qg_up if n == 0 else qg_dn
            blk = src[:, blk_idx * LANES:(blk_idx + 1) * LANES]
            keep = lo_half if n == 0 else jnp.logical_not(lo_half)
            qg_ref[n, i] = jnp.where(keep, blk, 0.0).astype(BF16)


def _in_proj_call(h, modv, g1, w_cat, ones, gains, rope_tab, n_ctx_tiles):
    b, t, d = h.shape
    nt = t // ROW_TILE
    tm = ROW_TILE

    def row(width, dtype):
        return (pl.BlockSpec((None, tm, width), lambda bi, ti: (bi, ti, 0)),
                jax.ShapeDtypeStruct((b, t, width), dtype))

    outs = [row(d, BF16), row(512, F32), row(768, F32), row(SSD_DT_PAD, F32)]
    qd = (pl.BlockSpec((None, DIFF_HEADS, 2, tm, LANES), lambda bi, ti: (bi, 0, 0, ti, 0)),
          jax.ShapeDtypeStruct((b, DIFF_HEADS, 2, t, LANES), BF16))
    qg = (pl.BlockSpec((None, GQA_KV_HEADS, 4, tm, LANES), lambda bi, ti: (bi, 0, 0, ti, 0)),
          jax.ShapeDtypeStruct((b, GQA_KV_HEADS, 4, t, LANES), BF16))
    outs += [qd, row(512, BF16), row(512, BF16), qg, row(LANES, BF16), row(LANES, BF16),
             row(S5_WIDTH, BF16)]
    return pl.pallas_call(
        _in_proj_body,
        grid=(b, nt),
        in_specs=[
            pl.BlockSpec((None, tm, d), lambda bi, ti: (bi, ti, 0)),
            pl.BlockSpec((None, None, 6, d),
                         lambda bi, ti: (bi, jnp.where(ti >= n_ctx_tiles, 1, 0), 0, 0)),
            pl.BlockSpec((1, d), lambda bi, ti: (0, 0)),
            pl.BlockSpec((d, _C_END), lambda bi, ti: (0, 0)),
            pl.BlockSpec((512, 512), lambda bi, ti: (0, 0)),
            pl.BlockSpec((8, 512), lambda bi, ti: (0, 0)),
            pl.BlockSpec((tm, 384), lambda bi, ti: (ti, 0)),
        ],
        out_specs=[o[0] for o in outs],
        out_shape=[o[1] for o in outs],
        compiler_params=_params(2),
        name="in_proj",
    )(h, modv, g1, w_cat, ones, gains, rope_tab)


def _conv_body(cur_ref, prev_ref, next_ref, w_ref, b_ref, o_ref, buf_ref, *, n_ctx_tiles, n_tiles):
    t = pl.program_id(1)
    tm = ROW_TILE
    has_prev = jnp.logical_and(t != 0, t != n_ctx_tiles)
    has_next = jnp.logical_and(t != n_ctx_tiles - 1, t != n_tiles - 1)
    buf_ref[0:8, :] = jnp.where(has_prev, prev_ref[...], 0.0)
    buf_ref[8:8 + tm, :] = cur_ref[...]
    buf_ref[8 + tm:16 + tm, :] = jnp.where(has_next, next_ref[...], 0.0)
    w = w_ref[...]
    bias = b_ref[...]
    pad = (SSD_CONV_W - 1) // 2
    full = buf_ref[...]
    n = tm + 16
    acc = bias + w[pad:pad + 1] * full[8:8 + tm]
    for k in range(SSD_CONV_W):
        if k != pad:
            acc = acc + w[k:k + 1] * pltpu.roll(full, (pad - k) % n, 0)[8:8 + tm]
    o_ref[...] = _silu(acc)


def _conv_call(xbc, conv_w, conv_b, n_ctx_tiles):
    b, t, ch = xbc.shape
    tm = ROW_TILE
    nt = t // tm
    per = tm // 8
    last8 = t // 8 - 1
    return pl.pallas_call(
        functools.partial(_conv_body, n_ctx_tiles=n_ctx_tiles, n_tiles=nt),
        grid=(b, nt),
        in_specs=[
            pl.BlockSpec((None, tm, ch), lambda bi, ti: (bi, ti, 0)),
            pl.BlockSpec((None, 8, ch), lambda bi, ti: (bi, jnp.maximum(ti * per - 1, 0), 0)),
            pl.BlockSpec((None, 8, ch), lambda bi, ti: (bi, jnp.minimum((ti + 1) * per, last8), 0)),
            pl.BlockSpec((8, ch), lambda bi, ti: (0, 0)),
            pl.BlockSpec((1, ch), lambda bi, ti: (0, 0)),
        ],
        out_specs=pl.BlockSpec((None, tm, ch), lambda bi, ti: (bi, ti, 0)),
        out_shape=jax.ShapeDtypeStruct((b, t, ch), F32),
        scratch_shapes=[pltpu.VMEM((tm + 16, ch), F32)],
        compiler_params=_params(2),
        name="ssd_conv",
    )(xbc, xbc, xbc, conv_w, conv_b)


def _ssd_chunk_index(d, s, n_ctx_chunks, n_chunks):
    bwd = jnp.where(s < n_ctx_chunks, n_ctx_chunks - 1 - s, n_chunks - 1 - (s - n_ctx_chunks))
    return jnp.where(d == 0, s, bwd)


def _ssd_body(xbc_ref, dt_ref, z_ref, aneg_ref, dtb_ref, dsk_ref, ng_ref, o_ref,
              hst_ref, ysc_ref, *, n_ctx_chunks, n_chunks):
    d = pl.program_id(1)
    s = pl.program_id(2)
    c = _ssd_chunk_index(d, s, n_ctx_chunks, n_chunks)
    q = SSD_CHUNK
    nb = xbc_ref.shape[0]

    @pl.when(s == 0)
    def _():
        hst_ref[...] = jnp.zeros_like(hst_ref)

    def run(direction, bi):
        xbc = xbc_ref[bi]
        x = xbc[:, 0:512]
        bm = xbc[:, 512:640]
        cm = xbc[:, 640:768]
        pre = dt_ref[bi] + dtb_ref[...]
        dtv = jnp.maximum(pre, 0.0) + jnp.log(1.0 + jnp.exp(-jnp.abs(pre)))
        a = dtv * aneg_ref[...]
        row = lax.broadcasted_iota(jnp.int32, (q, q), 0)
        col = lax.broadcasted_iota(jnp.int32, (q, q), 1)
        tri = (row >= col) if direction == 0 else (row <= col)
        a_hi = a.astype(BF16)
        r1 = a - a_hi.astype(F32)
        a_mid = r1.astype(BF16)
        a_lo = (r1 - a_mid.astype(F32)).astype(BF16)
        parts = jnp.dot(tri.astype(BF16), jnp.concatenate([a_hi, a_mid, a_lo], axis=1),
                        preferred_element_type=F32)
        acs = parts[:, 0:LANES] + parts[:, LANES:2 * LANES] + parts[:, 2 * LANES:3 * LANES]
        acs_t = acs.T
        last = acs[q - 1:q, :] if direction == 0 else acs[0:1, :]
        dte = jnp.exp(last - acs)
        expa = jnp.exp(acs)
        cdec = jnp.broadcast_to(jnp.exp(last), (8, LANES))
        erow = lax.broadcasted_iota(jnp.int32, (LANES, SSD_INNER), 0)
        ecol = lax.broadcasted_iota(jnp.int32, (LANES, SSD_INNER), 1)
        expand = (erow == ecol // SSD_HEAD_DIM + SSD_HEADS * direction).astype(BF16)
        spread = jnp.dot(jnp.concatenate([dtv, dtv * dte], axis=0).astype(BF16), expand,
                         preferred_element_type=F32)
        xdt = x * spread[0:q]
        xs = x * spread[q:2 * q]
        fine = jnp.concatenate([expa, cdec], axis=0)
        fine_hi = fine.astype(BF16)
        fine_lo = (fine - fine_hi.astype(F32)).astype(BF16)
        spread2 = (jnp.dot(fine_hi, expand, preferred_element_type=F32)
                   + jnp.dot(fine_lo, expand, preferred_element_type=F32))
        expa_x = spread2[0:q]
        cdec_x = spread2[q:q + 1]
        bt = bm.T.astype(BF16)
        bb = bm.astype(BF16)
        lane = lax.broadcasted_iota(jnp.int32, (1, LANES), 1)
        ys = []
        for g in range(2):
            cg = jnp.where(lane // SSD_STATE == g, cm, 0.0).astype(BF16)
            gmat = lax.dot_general(cg, bb, (((1,), (1,)), ((), ())), preferred_element_type=F32)
            for hp in range(2):
                p = g * 2 + hp
                sl = slice(p * LANES, (p + 1) * LANES)
                ypair = None
                for e in range(2):
                    j = SSD_HEADS * direction + 2 * p + e
                    seg = acs[:, j:j + 1] - acs_t[j:j + 1, :]
                    dec = jnp.exp(jnp.where(tri, seg, -jnp.inf))
                    xh = jnp.where(lane // SSD_HEAD_DIM == e, xdt[:, sl], 0.0)
                    term = _bdot(gmat * dec, xh)
                    ypair = term if ypair is None else ypair + term
                hprev = hst_ref[bi, p]
                ypair = ypair + _bdot(cg, hprev) * expa_x[:, sl]
                hst_ref[bi, p] = hprev * cdec_x[:, sl] + jnp.dot(
                    bt, xs[:, sl].astype(BF16), preferred_element_type=F32)
                ys.append(ypair)
        return x, ys

    @pl.when(d == 0)
    def _():
        for bi in range(nb):
            _, ys = run(0, bi)
            for p in range(4):
                ysc_ref[bi, c, :, p * LANES:(p + 1) * LANES] = ys[p]

    @pl.when(d == 1)
    def _():
        for bi in range(nb):
            x, ys = run(1, bi)
            y = jnp.concatenate(ys, axis=1) + ysc_ref[bi, c] + dsk_ref[...] * x
            y = y * _silu(z_ref[bi])
            o_ref[bi] = _rms(y, ng_ref[...]).astype(o_ref.dtype)


def _ssd_call(xbc_act, dt, z, aneg, dtb, dsk, ng, n_ctx_chunks):
    b, t, _ = xbc_act.shape
    q = SSD_CHUNK
    nc = t // q
    nb = SSD_BATCH_PER_STEP if b % SSD_BATCH_PER_STEP == 0 else 1
    cidx = functools.partial(_ssd_chunk_index, n_ctx_chunks=n_ctx_chunks, n_chunks=nc)

    def late(bi, di, si):
        return (bi, jnp.where(di == 0, n_ctx_chunks - 1, cidx(di, si)), 0)

    return pl.pallas_call(
        functools.partial(_ssd_body, n_ctx_chunks=n_ctx_chunks, n_chunks=nc),
        grid=(b // nb, 2, nc),
        in_specs=[
            pl.BlockSpec((nb, q, SSD_CONV_CH), lambda bi, di, si: (bi, cidx(di, si), 0)),
            pl.BlockSpec((nb, q, SSD_DT_PAD), lambda bi, di, si: (bi, cidx(di, si), 0)),
            pl.BlockSpec((nb, q, SSD_INNER), late),
            pl.BlockSpec((1, LANES), lambda bi, di, si: (0, 0)),
            pl.BlockSpec((1, LANES), lambda bi, di, si: (0, 0)),
            pl.BlockSpec((1, SSD_INNER), lambda bi, di, si: (0, 0)),
            pl.BlockSpec((1, SSD_INNER), lambda bi, di, si: (0, 0)),
        ],
        out_specs=pl.BlockSpec((nb, q, SSD_INNER), late),
        out_shape=jax.ShapeDtypeStruct((b, t, SSD_INNER), BF16),
        scratch_shapes=[pltpu.VMEM((nb, 4, LANES, LANES), F32),
                        pltpu.VMEM((nb, nc, q, SSD_INNER), F32)],
        compiler_params=_params(3),
        name="ssd_scan",
    )(xbc_act, dt, z, aneg, dtb, dsk, ng)


def _flash(qs, k_ref, v_ref, m_ref, l_ref, acc_ref, *, tk, n_kv, ctx_len, is_ctx_tile):
    m_ref[...] = jnp.full(m_ref.shape, -jnp.inf, F32)
    l_ref[...] = jnp.zeros(l_ref.shape, F32)
    acc_ref[...] = jnp.zeros(acc_ref.shape, F32)

    def step(start, size, limit):
        for i, q in enumerate(qs):
            lanes = slice(i * LANES, (i + 1) * LANES)
            k = k_ref[start:start + size, lanes]
            v = v_ref[start:start + size, lanes]
            s = lax.dot_general(q, k, (((1,), (1,)), ((), ())), preferred_element_type=F32)
            if limit is not None:
                col = lax.broadcasted_iota(jnp.int32, (1, size), 1) + start
                s = jnp.where(col < limit, s, -jnp.inf)
            m_old = m_ref[i]
            m_new = jnp.maximum(m_old, jnp.max(s, axis=-1, keepdims=True))
            alpha = jnp.exp2(m_old - m_new)
            p = jnp.exp2(s - jnp.concatenate([m_new] * (size // LANES), axis=1))
            psum = p[:, 0:LANES]
            for j in range(1, size // LANES):
                psum = psum + p[:, j * LANES:(j + 1) * LANES]
            l_ref[i] = alpha * l_ref[i] + psum
            acc_ref[i] = alpha * acc_ref[i] + jnp.dot(p.astype(BF16), v, preferred_element_type=F32)
            m_ref[i] = m_new

    @pl.when(is_ctx_tile)
    def _():
        ctx_pad = -(-ctx_len // LANES) * LANES
        for start in range(0, ctx_pad, tk):
            step(start, min(tk, ctx_pad - start), ctx_len if ctx_pad != ctx_len else None)

    @pl.when(jnp.logical_not(is_ctx_tile))
    def _():
        for kc in range(n_kv):
            step(kc * tk, tk, None)

    return [acc_ref[i] * (1.0 / jnp.sum(l_ref[i], axis=-1, keepdims=True)) for i in range(len(qs))]


DIFF_HEADS_PER_STEP = 2


def _diff_attn_body(q_ref, k_ref, v_ref, lam_ref, g_ref, o_ref, m_ref, l_ref, acc_ref,
                    *, tq, lam_init, n_ctx_q, **kw):
    qi = pl.program_id(2)
    qs = [q_ref[i].reshape(2 * tq, LANES) for i in range(DIFF_HEADS_PER_STEP)]
    outs = _flash(qs, k_ref, v_ref, m_ref, l_ref, acc_ref, is_ctx_tile=qi < n_ctx_q, **kw)
    lv = lam_ref[...]
    lam = (jnp.exp(jnp.sum(lv[0:1] * lv[1:2], axis=-1, keepdims=True))
           - jnp.exp(jnp.sum(lv[2:3] * lv[3:4], axis=-1, keepdims=True)) + lam_init)
    for i, o in enumerate(outs):
        out = o[0:tq] - lam * o[tq:2 * tq]
        o_ref[:, i * LANES:(i + 1) * LANES] = (_rms(out, g_ref[...]) * (1.0 - lam_init)).astype(o_ref.dtype)


def _gqa_attn_body(q_ref, k_ref, v_ref, o_ref, m_ref, l_ref, acc_ref, *, tq, n_ctx_q, **kw):
    n = pl.program_id(1)
    qi = pl.program_id(2)
    q = q_ref[...].reshape(4 * tq, LANES)
    (o,) = _flash([q], k_ref, v_ref, m_ref, l_ref, acc_ref, is_ctx_tile=qi < n_ctx_q, **kw)
    lane = lax.broadcasted_iota(jnp.int32, (1, LANES), 1)
    lo_half = lane < HEAD_DIM
    for pair in range(2):
        a = o[(2 * pair) * tq:(2 * pair + 1) * tq]
        b = o[(2 * pair + 1) * tq:(2 * pair + 2) * tq]
        from_lo = jnp.where(lo_half, a, pltpu.roll(b, HEAD_DIM, 1))
        from_hi = jnp.where(lo_half, pltpu.roll(a, HEAD_DIM, 1), b)
        o_ref[:, pair * LANES:(pair + 1) * LANES] = jnp.where(n == 0, from_lo, from_hi).astype(o_ref.dtype)


def _attn_tiles(t, ctx_len):
    tq = ROW_TILE
    tk = next(c for c in (2816, 768, 256) if t % c == 0)
    assert ctx_len % tq == 0 and t % tq == 0 and t % tk == 0
    return dict(tq=tq, tk=tk, n_kv=t // tk, ctx_len=ctx_len, n_ctx_q=ctx_len // tq)


def _diff_attn_call(qd, kd, vd, lamv, subln_g, lam_init, ctx_len):
    b, _, _, t, _ = qd.shape
    cfg = _attn_tiles(t, ctx_len)
    tq = cfg["tq"]
    r = 2 * tq
    hs = DIFF_HEADS_PER_STEP
    return pl.pallas_call(
        functools.partial(_diff_attn_body, lam_init=lam_init, **cfg),
        grid=(b, DIFF_HEADS // hs, t // tq),
        in_specs=[
            pl.BlockSpec((None, hs, 2, tq, LANES), lambda bi, hi, qi: (bi, hi, 0, qi, 0)),
            pl.BlockSpec((None, t, hs * LANES), lambda bi, hi, qi: (bi, 0, hi)),
            pl.BlockSpec((None, t, hs * LANES), lambda bi, hi, qi: (bi, 0, hi)),
            pl.BlockSpec((4, LANES), lambda bi, hi, qi: (0, 0)),
            pl.BlockSpec((1, LANES), lambda bi, hi, qi: (0, 0)),
        ],
        out_specs=pl.BlockSpec((None, tq, hs * LANES), lambda bi, hi, qi: (bi, qi, hi)),
        out_shape=jax.ShapeDtypeStruct((b, t, DIFF_HEADS * LANES), BF16),
        scratch_shapes=[pltpu.VMEM((hs, r, LANES), F32)] * 3,
        compiler_params=_params(3),
        name="diff_attention",
    )(qd, kd, vd, lamv, subln_g)


def _gqa_attn_call(qg, kg, vg, ctx_len):
    b, _, _, t, _ = qg.shape
    cfg = _attn_tiles(t, ctx_len)
    tq = cfg["tq"]
    r = 4 * tq
    return pl.pallas_call(
        functools.partial(_gqa_attn_body, **cfg),
        grid=(b, GQA_KV_HEADS, t // tq),
        in_specs=[
            pl.BlockSpec((None, None, 4, tq, LANES), lambda bi, ni, qi: (bi, ni, 0, qi, 0)),
            pl.BlockSpec((None, t, LANES), lambda bi, ni, qi: (bi, 0, 0)),
            pl.BlockSpec((None, t, LANES), lambda bi, ni, qi: (bi, 0, 0)),
        ],
        out_specs=pl.BlockSpec((None, tq, 2 * LANES), lambda bi, ni, qi: (bi, qi, ni)),
        out_shape=jax.ShapeDtypeStruct((b, t, GQA_HEADS * HEAD_DIM), BF16),
        scratch_shapes=[pltpu.VMEM((1, r, LANES), F32)] * 3,
        compiler_params=_params(3),
        name="gqa_attention",
    )(qg, kg, vg)


def _cmul(ar, ai, br, bi):
    return ar * br - ai * bi, ar * bi + ai * br


def _stack_rows(er, ei, vr, vi):
    re = [er * vr[c:c + 1] - ei * vi[c:c + 1] for c in range(S5_GROUP_CH)]
    im = [er * vi[c:c + 1] + ei * vr[c:c + 1] for c in range(S5_GROUP_CH)]
    return jnp.concatenate(re, axis=0), jnp.concatenate(im, axis=0)


def _split_dot_t(a, b):
    dn = (((1,), (1,)), ((), ()))
    ah = a.astype(BF16)
    al = (a - ah.astype(F32)).astype(BF16)
    bh = b.astype(BF16)
    bl = (b - bh.astype(F32)).astype(BF16)
    return (lax.dot_general(ah, bh, dn, preferred_element_type=F32)
            + lax.dot_general(ah, bl, dn, preferred_element_type=F32)
            + lax.dot_general(al, bh, dn, preferred_element_type=F32))


def _s5_weights_body(lam_ref, bt_ref, c_ref, wi_ref, ws_ref, wo_ref, av_ref):
    tc = S5_CHUNK
    lam = lam_ref[...]
    br_t, bi_t = bt_ref[0:16, :], bt_ref[16:32, :]
    cr, ci = c_ref[0:16, :], c_ref[16:32, :]
    kk = lax.broadcasted_iota(jnp.int32, (tc, S5_STATE), 0).astype(F32)
    k8 = lax.broadcasted_iota(jnp.int32, (8, S5_STATE), 0)
    k8 = jnp.where(k8 == 0, 1.0, jnp.where(k8 == 1, tc - 1.0, float(tc)))
    row = lax.broadcasted_iota(jnp.int32, (S5_FLAT, S5_FLAT), 0) % tc
    col = lax.broadcasted_iota(jnp.int32, (S5_FLAT, S5_FLAT), 1) % tc
    w_intra = None
    state_cols, out_cols, a_rows = [], [], []
    for direction in range(2):
        lr = lam[2 * direction:2 * direction + 1]
        li = lam[2 * direction + 1:2 * direction + 2]
        step = jnp.exp(lam[4 + direction:5 + direction])
        mag = jnp.exp(lr * step)
        ar, ai = mag * jnp.cos(li * step), mag * jnp.sin(li * step)
        den = lr * lr + li * li
        fr = ((ar - 1.0) * lr + ai * li) / den
        fi = (ai * lr - (ar - 1.0) * li) / den
        bbr = fr * br_t - fi * bi_t
        bbi = fr * bi_t + fi * br_t
        cs, sn = jnp.cos(kk * (li * step)), jnp.sin(kk * (li * step))
        grow, decay = jnp.exp(-kk * (lr * step)), jnp.exp(kk * (lr * step))
        pr, pi = decay * cs, decay * sn
        nr, ni = grow * cs, -grow * sn
        m8 = jnp.exp(k8 * (lr * step))
        c8r, c8i = m8 * jnp.cos(k8 * (li * step)), m8 * jnp.sin(k8 * (li * step))
        a_one = (c8r[0:1], c8i[0:1])
        a_last = (c8r[1:2], c8i[1:2])
        a_tc = (c8r[2:3], c8i[2:3])
        if direction == 0:
            x_e, y_e = (nr, ni), (pr, pi)
            s_e = _cmul(nr, ni, *a_last)
            o_e = _cmul(pr, pi, *a_one)
            keep = col >= row
        else:
            x_e, y_e = (pr, pi), (nr, ni)
            s_e = (pr, pi)
            o_e = _cmul(nr, ni, *a_tc)
            keep = row >= col
        xr, xi = _stack_rows(*x_e, bbr, bbi)
        yr, yi = _stack_rows(*y_e, cr, ci)
        full = _split_dot_t(jnp.concatenate([xr, -xi], axis=1), jnp.concatenate([yr, yi], axis=1))
        part = jnp.where(keep, full, 0.0)
        w_intra = part if w_intra is None else w_intra + part
        state_cols.append(_stack_rows(*s_e, bbr, bbi))
        o_r, o_i = _stack_rows(*o_e, cr, ci)
        out_cols += [o_r, -o_i]
        a_rows += [jnp.concatenate([a_tc[0], a_tc[0]], axis=1),
                   jnp.concatenate([-a_tc[1], a_tc[1]], axis=1),
                   jnp.concatenate([a_tc[1], -a_tc[1]], axis=1)]
    wi_ref[...] = w_intra.astype(BF16)
    (fr_, fi_), (br_, bi_) = state_cols
    ws_ref[...] = jnp.concatenate([fr_, fi_, br_, bi_, fi_, fr_, bi_, br_], axis=1).astype(BF16)
    wo_ref[...] = jnp.concatenate(out_cols, axis=1).astype(BF16)
    zero = jnp.zeros((1, LANES), F32)
    av_ref[...] = jnp.concatenate(a_rows + [zero, zero], axis=0)


def _s5_weights_call(lam_re, lam_im, log_dt, b_re, b_im, c_re, c_im):
    g, p = S5_GROUPS, S5_STATE
    bc = lambda v: jnp.broadcast_to(v.astype(F32)[:, None], (g, p))
    zero = jnp.zeros((g, p), F32)
    lam = jnp.stack([lam_re[0], lam_im[0], lam_re[1], lam_im[1], bc(log_dt[0]), bc(log_dt[1]),
                     zero, zero], axis=1).astype(F32)
    bt = jnp.concatenate([jnp.swapaxes(b_re, 1, 2), jnp.swapaxes(b_im, 1, 2)], axis=1).astype(F32)
    cc = jnp.concatenate([c_re, c_im], axis=1).astype(F32)
    n = S5_FLAT
    return pl.pallas_call(
        _s5_weights_body,
        grid=(g,),
        in_specs=[pl.BlockSpec((None, 8, p), lambda gi: (gi, 0, 0)),
                  pl.BlockSpec((None, 32, p), lambda gi: (gi, 0, 0)),
                  pl.BlockSpec((None, 32, p), lambda gi: (gi, 0, 0))],
        out_specs=[pl.BlockSpec((None, n, n), lambda gi: (gi, 0, 0)),
                   pl.BlockSpec((None, n, 4 * LANES), lambda gi: (gi, 0, 0)),
                   pl.BlockSpec((None, n, 2 * LANES), lambda gi: (gi, 0, 0)),
                   pl.BlockSpec((None, 8, LANES), lambda gi: (gi, 0, 0))],
        out_shape=[jax.ShapeDtypeStruct((g, n, n), BF16),
                   jax.ShapeDtypeStruct((g, n, 4 * LANES), BF16),
                   jax.ShapeDtypeStruct((g, n, 2 * LANES), BF16),
                   jax.ShapeDtypeStruct((g, 8, LANES), F32)],
        compiler_params=_params(1),
        name="s5_weights",
    )(lam, bt, cc)


def _s5_body(u_ref, wi_ref, ws_ref, wo_ref, av_ref, y_ref, s_ref, h_ref, *, n_ctx_chunks, n_chunks):
    u = u_ref[...]
    s_ref[...] = jnp.dot(u, ws_ref[...], preferred_element_type=F32)
    av = av_ref[...]
    nj = n_chunks
    rows = u_ref.shape[0] // nj
    a1f, a2f, a2sf, a1b, a2b, a2sb = [jnp.broadcast_to(av[i:i + 1], (rows, LANES)) for i in range(6)]
    hf = hfs = hb = hbs = jnp.zeros((rows, LANES), F32)
    for i in range(nj):
        jf = i * rows
        jb = (n_ctx_chunks - 1 - i if i < n_ctx_chunks else nj - 1 - (i - n_ctx_chunks)) * rows
        h_ref[jf:jf + rows, 0:LANES] = hf
        h_ref[jb:jb + rows, LANES:2 * LANES] = hb
        sf = s_ref[jf:jf + rows, 0:LANES]
        sfs = s_ref[jf:jf + rows, 2 * LANES:3 * LANES]
        sb = s_ref[jb:jb + rows, LANES:2 * LANES]
        sbs = s_ref[jb:jb + rows, 3 * LANES:4 * LANES]
        hf, hfs = a1f * hf + a2f * hfs + sf, a1f * hfs + a2sf * hf + sfs
        hb, hbs = a1b * hb + a2b * hbs + sb, a1b * hbs + a2sb * hb + sbs
    y = (jnp.dot(u, wi_ref[...], preferred_element_type=F32)
         + lax.dot_general(h_ref[...].astype(BF16), wo_ref[...], (((1,), (1,)), ((), ())),
                           preferred_element_type=F32))
    y_ref[...] = y.astype(y_ref.dtype)


def _s5_call(ug, w_intra, w_state, w_out, avec, n_ctx_chunks, n_chunks):
    g, r, w = ug.shape
    return pl.pallas_call(
        functools.partial(_s5_body, n_ctx_chunks=n_ctx_chunks, n_chunks=n_chunks),
        grid=(g,),
        in_specs=[
            pl.BlockSpec((None, r, w), lambda gi: (gi, 0, 0)),
            pl.BlockSpec((None, w, w), lambda gi: (gi, 0, 0)),
            pl.BlockSpec((None, w, 4 * LANES), lambda gi: (gi, 0, 0)),
            pl.BlockSpec((None, w, 2 * LANES), lambda gi: (gi, 0, 0)),
            pl.BlockSpec((None, 8, LANES), lambda gi: (gi, 0, 0)),
        ],
        out_specs=pl.BlockSpec((None, r, w), lambda gi: (gi, 0, 0)),
        out_shape=jax.ShapeDtypeStruct((g, r, w), BF16),
        scratch_shapes=[pltpu.VMEM((r, 4 * LANES), F32), pltpu.VMEM((r, 2 * LANES), F32)],
        compiler_params=_params(1),
        name="s5_scan",
    )(ug, w_intra, w_state, w_out, avec)


def _s5_to_groups(u):
    b, t, _ = u.shape
    nj = t // S5_CHUNK
    x = jnp.swapaxes(u.astype(BF16).reshape(b, nj, S5_CHUNK, S5_WIDTH), 2, 3)
    x = lax.optimization_barrier(x.reshape(b, nj, S5_GROUPS, S5_FLAT))
    return jnp.transpose(x, (2, 1, 0, 3)).reshape(S5_GROUPS, nj * b, S5_FLAT)


def _s5_from_groups(y, b):
    g, r, n = y.shape
    nj = r // b
    x = jnp.transpose(y.reshape(g, nj, b, n), (2, 1, 0, 3))
    x = lax.optimization_barrier(x).reshape(b, nj, S5_WIDTH, S5_CHUNK)
    return jnp.swapaxes(x, 2, 3).reshape(b, nj * S5_CHUNK, S5_WIDTH)


def _merge_body(h_ref, mod_ref, xn_ref, ya_ref, yb_ref, yc_ref, u_ref, y5_ref,
                wg_ref, wa_ref, wb_ref, wc_ref, wd_ref, glw_ref, glb_ref, s5d_ref, wo_ref, o_ref):
    xn = xn_ref[...]
    y5 = y5_ref[...].astype(F32) + s5d_ref[...] * u_ref[...].astype(F32)
    gelu = 0.5 * y5 * (1.0 + jnp.tanh(0.7978845608028654 * (y5 + 0.044715 * y5 * y5 * y5)))
    glu = _bdot(gelu, glw_ref[...]) + glb_ref[...]
    yd = glu[:, 0:S5_WIDTH] * _sigmoid(glu[:, S5_WIDTH:2 * S5_WIDTH])
    branches = ((ya_ref[...], wa_ref), (yb_ref[...], wb_ref), (yc_ref[...], wc_ref),
                (yd.astype(BF16), wd_ref))
    merged = None
    for i, (y, w_ref) in enumerate(branches):
        gate = _sigmoid(jnp.dot(xn, wg_ref[i], preferred_element_type=F32))
        term = gate * jnp.dot(y, w_ref[...], preferred_element_type=F32)
        merged = term if merged is None else merged + term
    out = _bdot(merged, wo_ref[...])
    o_ref[...] = h_ref[...] + mod_ref[2:3, :] * out


def _merge_call(h, modv, xn, ya, yb, yc, u, y5, wts, n_ctx_tiles):
    b, t, d = h.shape
    tm = ROW_TILE

    def row(width):
        return pl.BlockSpec((None, tm, width), lambda bi, ti: (bi, ti, 0))

    def const(shape):
        return pl.BlockSpec(shape, lambda bi, ti: (0,) * len(shape), pipeline_mode=pl.Buffered(1))

    w_gate, w_a, w_b, w_c, w_d, glu_w, glu_b, s5_d, w_out = wts
    return pl.pallas_call(
        _merge_body,
        grid=(b, t // tm),
        in_specs=[row(d),
                  pl.BlockSpec((None, None, 6, d),
                               lambda bi, ti: (bi, jnp.where(ti >= n_ctx_tiles, 1, 0), 0, 0)),
                  row(d), row(512), row(512), row(512), row(S5_WIDTH), row(S5_WIDTH),
                  const((4, d, d)), const((512, d)), const((512, d)), const((512, d)),
                  const((S5_WIDTH, d)), const((S5_WIDTH, 2 * S5_WIDTH)), const((1, 2 * S5_WIDTH)),
                  const((1, S5_WIDTH)), const((d, d))],
        out_specs=row(d),
        out_shape=jax.ShapeDtypeStruct((b, t, d), F32),
        compiler_params=_params(2),
        name="branch_merge",
    )(h, modv, xn, ya, yb, yc, u, y5, w_gate, w_a, w_b, w_c, w_d, glu_w, glu_b, s5_d, w_out)


def _ffn_body(h_ref, mod_ref, g_ref, wgu_ref, wd_ref, o_ref):
    h = h_ref[...]
    mod = mod_ref[...]
    xf = (_rms(h, g_ref[...]) * (1.0 + mod[4:5]) + mod[3:4]).astype(BF16)
    gate = jnp.dot(xf, wgu_ref[:, 0:FFN_HIDDEN], preferred_element_type=F32)
    up = jnp.dot(xf, wgu_ref[:, FFN_HIDDEN:2 * FFN_HIDDEN], preferred_element_type=F32)
    act = (_silu(gate) * up).astype(BF16)
    o_ref[...] = h + mod[5:6] * jnp.dot(act, wd_ref[...], preferred_element_type=F32)


def _ffn_call(h, modv, g2, w_gu, w_down, n_ctx_tiles, latent_only):
    b, t, d = h.shape
    tm = ROW_TILE
    skip = n_ctx_tiles if latent_only else 0

    def const(shape):
        return pl.BlockSpec(shape, lambda bi, ti: (0,) * len(shape), pipeline_mode=pl.Buffered(1))

    return pl.pallas_call(
        _ffn_body,
        grid=(b, t // tm - skip),
        in_specs=[pl.BlockSpec((None, tm, d), lambda bi, ti: (bi, ti + skip, 0)),
                  pl.BlockSpec((None, None, 6, d),
                               lambda bi, ti: (bi, jnp.where(ti + skip >= n_ctx_tiles, 1, 0), 0, 0)),
                  const((1, d)), const((d, 2 * FFN_HIDDEN)), const((FFN_HIDDEN, d))],
        out_specs=pl.BlockSpec((None, tm, d), lambda bi, ti: (bi, ti, 0)),
        out_shape=jax.ShapeDtypeStruct((b, t - skip * tm, d), F32),
        compiler_params=_params(2),
        name="swiglu_ffn",
    )(h, modv, g2, w_gu, w_down)


def _rope_tables(ctx_len, seq_len):
    n_rows = seq_len // GRID_W
    rows = jnp.repeat(jnp.arange(n_rows, dtype=F32), GRID_W)
    cols = jnp.tile(jnp.arange(GRID_W, dtype=F32), n_rows)
    quarter = HEAD_DIM // 4
    inv_freq = ROPE_THETA ** (-jnp.arange(quarter, dtype=F32) / quarter)
    ang_r = rows[:, None] * inv_freq
    ang_c = cols[:, None] * inv_freq
    ang = jnp.concatenate([ang_r, ang_r, ang_c, ang_c], axis=-1)
    cos = jnp.concatenate([jnp.ones((ctx_len, HEAD_DIM), F32), jnp.cos(ang)], axis=0)
    sin = jnp.concatenate([jnp.zeros((ctx_len, HEAD_DIM), F32), jnp.sin(ang)], axis=0)
    first = (jnp.arange(HEAD_DIM) % 32) < 16
    sin_a = jnp.where(first, -sin, 0.0)
    sin_b = jnp.where(first, 0.0, sin)
    two = lambda m: jnp.concatenate([m, m], axis=1)
    return jnp.concatenate([two(cos), two(sin_a), two(sin_b)], axis=1)


def _w_in_layout(w_in):
    d = w_in.shape[0]
    a0 = 0
    z = w_in[:, a0:a0 + 512]
    xbc = w_in[:, a0 + 512:a0 + 1280]
    dt = jnp.pad(w_in[:, a0 + 1280:a0 + 1296], ((0, 0), (0, SSD_DT_PAD - 16)))
    rest = w_in[:, 1296:]
    out = jnp.concatenate([z, xbc, dt, rest], axis=1).astype(BF16)
    assert out.shape == (d, _C_END)
    return out


def kernel(x, c, ctx, c_ctx, w_mod, b_mod, norm1_g, norm2_g, w_in, ssd_conv_w, ssd_conv_b, ssd_a_log, ssd_dt_bias, ssd_d, ssd_norm_g, diff_qn_g, diff_kn_g, diff_lam_q1, diff_lam_k1, diff_lam_q2, diff_lam_k2, diff_subln_g, gqa_qn_g, gqa_kn_g, s5_lam_re, s5_lam_im, s5_log_dt, s5_b_re, s5_b_im, s5_c_re, s5_c_im, s5_d, s5_glu_w, s5_glu_b, w_gate, w_br_ssd, w_br_diff, w_br_gqa, w_br_s5, w_out, ffn_w_gate_up, ffn_w_down):
    b, seq_len, d = x.shape
    ctx_len = ctx.shape[1]
    depth = w_mod.shape[0]
    assert b + 1 <= 8
    assert ctx_len % ROW_TILE == 0 and seq_len % ROW_TILE == 0
    n_ctx_tiles = ctx_len // ROW_TILE

    h = jnp.concatenate([ctx, x], axis=1)
    cc = jnp.concatenate([c, c_ctx[None], jnp.zeros((8 - b - 1, d), F32)], axis=0)
    mods = _mod_call(cc, w_mod, b_mod)

    rope_tab = _rope_tables(ctx_len, seq_len)
    blk = jnp.arange(512) // HEAD_DIM
    ones = (blk[:, None] == blk[None, :]).astype(BF16)
    tile8 = lambda g: jnp.tile(g.astype(F32), 512 // HEAD_DIM)

    for layer in range(depth):
        m = mods[layer]
        lat = m[:b].reshape(b, 6, d)
        cmod = jnp.broadcast_to(m[b].reshape(1, 6, d), (b, 6, d))
        modv = jnp.stack([cmod, lat], axis=1)

        gains = jnp.stack([tile8(diff_qn_g[layer]), tile8(diff_kn_g[layer]),
                           tile8(gqa_qn_g[layer]), tile8(gqa_kn_g[layer])]
                          + [jnp.zeros((512,), F32)] * 4, axis=0)
        (xn, z, xbc, dt, qd, kd, vd, qg, kg, vg, u) = _in_proj_call(
            h, modv, norm1_g[layer][None], _w_in_layout(w_in[layer]), ones, gains, rope_tab,
            n_ctx_tiles)

        conv_w = jnp.pad(ssd_conv_w[layer].astype(F32), ((0, 8 - SSD_CONV_W), (0, 0)))
        xbc_act = _conv_call(xbc, conv_w, ssd_conv_b[layer][None].astype(F32), n_ctx_tiles)
        pad16 = lambda v: jnp.pad(v.reshape(1, 16).astype(F32), ((0, 0), (0, LANES - 16)))
        aneg = pad16(-jnp.exp(ssd_a_log[layer].astype(F32)))
        dtb = pad16(ssd_dt_bias[layer])
        dsk = jnp.repeat(ssd_d[layer].astype(F32), SSD_HEAD_DIM)[None]
        ya = _ssd_call(xbc_act, dt, z, aneg, dtb, dsk, ssd_norm_g[layer][None].astype(F32),
                       ctx_len // SSD_CHUNK)

        lam_init = 0.8 - 0.6 * math.exp(-0.3 * layer)
        lamv = jnp.pad(jnp.stack([diff_lam_q1[layer], diff_lam_k1[layer],
                                  diff_lam_q2[layer], diff_lam_k2[layer]]).astype(F32),
                       ((0, 0), (0, LANES - HEAD_DIM)))
        yb = _diff_attn_call(qd, kd, vd, lamv, diff_subln_g[layer][None].astype(F32), lam_init, ctx_len)

        yc = _gqa_attn_call(qg, kg, vg, ctx_len)

        s5w = _s5_weights_call(s5_lam_re[layer], s5_lam_im[layer], s5_log_dt[layer],
                               s5_b_re[layer], s5_b_im[layer], s5_c_re[layer], s5_c_im[layer])
        y5 = _s5_from_groups(_s5_call(_s5_to_groups(u), *s5w, ctx_len // S5_CHUNK,
                                      (ctx_len + seq_len) // S5_CHUNK), b)

        wts = (w_gate[layer].astype(BF16), w_br_ssd[layer].astype(BF16), w_br_diff[layer].astype(BF16),
               w_br_gqa[layer].astype(BF16), w_br_s5[layer].astype(BF16), s5_glu_w[layer].astype(BF16),
               s5_glu_b[layer][None].astype(F32), s5_d[layer][None].astype(F32), w_out[layer].astype(BF16))
        h = _merge_call(h, modv, xn, ya, yb, yc, u, y5, wts, n_ctx_tiles)
        h = _ffn_call(h, modv, norm2_g[layer][None].astype(F32), ffn_w_gate_up[layer].astype(BF16),
                      ffn_w_down[layer].astype(BF16), n_ctx_tiles, latent_only=layer == depth - 1)
    return h
```

```python
import functools
import math

import jax
import jax.numpy as jnp
from jax import lax
from jax.experimental import pallas as pl
from jax.experimental.pallas import tpu as pltpu

F32 = jnp.float32
BF16 = jnp.bfloat16
HIGHEST = lax.Precision.HIGHEST

D_MODEL = 1024
GRID_W = 64
ROPE_THETA = 10000.0
NORM_EPS = 1e-6

SSD_INNER = 512
SSD_HEADS = 8
SSD_HEAD_DIM = 64
SSD_STATE = 64
SSD_CHUNK = 128
SSD_CONV_W = 5
SSD_CONV_CH = 768
SSD_DT_PAD = 128
SSD_BATCH_PER_STEP = 4

DIFF_HEADS = 4
HEAD_DIM = 64
GQA_HEADS = 8
GQA_KV_HEADS = 2

S5_GROUP_CH = 16
S5_STATE = 64
S5_WIDTH = 384
S5_GROUPS = 24
S5_CHUNK = 64
S5_FLAT = S5_CHUNK * S5_GROUP_CH

FFN_HIDDEN = 2816

ROW_TILE = 256
LANES = 128
VMEM_LIMIT = 56 * 1024 * 1024

_C_Z, _C_XBC, _C_DT = 0, 512, 1280
_C_QD, _C_KD, _C_VD = 1408, 1920, 2432
_C_QG, _C_KG, _C_VG = 2944, 3456, 3584
_C_U, _C_END = 3712, 4096


def _params(n_grid):
    return pltpu.CompilerParams(dimension_semantics=("arbitrary",) * n_grid,
                                vmem_limit_bytes=VMEM_LIMIT)


def _bdot(a, b):
    return jnp.dot(a.astype(BF16), b.astype(BF16), preferred_element_type=F32)


def _hdot(a, b):
    return jnp.dot(a, b, precision=HIGHEST, preferred_element_type=F32)


def _rms(x, g):
    return x * lax.rsqrt(jnp.mean(x * x, axis=-1, keepdims=True) + NORM_EPS) * g


def _sigmoid(x):
    return 1.0 / (1.0 + jnp.exp(-x))


def _silu(x):
    return x * _sigmoid(x)


def _mod_body(c_ref, w_ref, b_ref, o_ref):
    o_ref[...] = _hdot(_silu(c_ref[...]), w_ref[...]) + b_ref[...]


def _mod_call(cc, w_mod, b_mod):
    depth = w_mod.shape[0]
    d = D_MODEL
    return pl.pallas_call(
        _mod_body,
        grid=(depth, 6),
        in_specs=[pl.BlockSpec((8, d), lambda l, j: (0, 0)),
                  pl.BlockSpec((None, d, d), lambda l, j: (l, 0, j)),
                  pl.BlockSpec((None, 1, d), lambda l, j: (l, 0, j))],
        out_specs=pl.BlockSpec((None, 8, d), lambda l, j: (l, 0, j)),
        out_shape=jax.ShapeDtypeStruct((depth, 8, 6 * d), F32),
        compiler_params=_params(2),
        name="adaln_mod",
    )(cc, w_mod, b_mod.reshape(depth, 1, 6 * d))


def _head_rms(t, gain, ones):
    ss = jnp.dot((t * t).astype(BF16), ones, preferred_element_type=F32)
    return t * lax.rsqrt(ss * (1.0 / HEAD_DIM) + NORM_EPS) * gain


def _rope(t, cos, sin_a, sin_b):
    w = t.shape[-1]
    return t * cos + pltpu.roll(t, w - 16, 1) * sin_a + pltpu.roll(t, 16, 1) * sin_b


def _in_proj_body(h_ref, mod_ref, g_ref, w_ref, ones_ref, gains_ref, rope_ref,
                  xn_ref, z_ref, xbc_ref, dt_ref, qd_ref, kd_ref, vd_ref,
                  qg_ref, kg_ref, vg_ref, u_ref):
    x = h_ref[...]
    mod = mod_ref[...]
    xn = _rms(x, g_ref[...]) * (1.0 + mod[1:2]) + mod[0:1]
    xb = xn.astype(BF16)
    xn_ref[...] = xb

    def proj(lo, hi):
        return jnp.dot(xb, w_ref[:, lo:hi], preferred_element_type=F32)

    z_ref[...] = proj(_C_Z, _C_XBC)
    xbc_ref[...] = proj(_C_XBC, _C_DT)
    dt_ref[...] = proj(_C_DT, _C_QD)
    u_ref[...] = proj(_C_U, _C_END).astype(u_ref.dtype)

    rope = rope_ref[...]
    cos1, sa1, sb1 = rope[:, 0:128], rope[:, 128:256], rope[:, 256:384]
    cos4 = jnp.concatenate([cos1] * 4, axis=1)
    sa4 = jnp.concatenate([sa1] * 4, axis=1)
    sb4 = jnp.concatenate([sb1] * 4, axis=1)
    ones = ones_ref[...]
    gains = gains_ref[...]
    lane = lax.broadcasted_iota(jnp.int32, (1, LANES), 1)
    lo_half = lane < HEAD_DIM
    scale = HEAD_DIM ** -0.5 * math.log2(math.e)

    qd = _rope(_head_rms(proj(_C_QD, _C_KD), gains[0:1], ones), cos4, sa4, sb4) * scale
    kd = _rope(_head_rms(proj(_C_KD, _C_VD), gains[1:2], ones), cos4, sa4, sb4)
    kd_ref[...] = kd.astype(BF16)
    vd_ref[...] = proj(_C_VD, _C_QG).astype(BF16)
    for h in range(DIFF_HEADS):
        blk = qd[:, h * LANES:(h + 1) * LANES]
        qd_ref[h, 0] = jnp.where(lo_half, blk, 0.0).astype(BF16)
        qd_ref[h, 1] = jnp.where(lo_half, 0.0, blk).astype(BF16)

    qg = _rope(_head_rms(proj(_C_QG, _C_KG), gains[2:3], ones), cos4, sa4, sb4) * scale
    kg = _rope(_head_rms(proj(_C_KG, _C_VG), gains[3:4, :LANES], ones[:LANES, :LANES]),
               cos1, sa1, sb1)
    kg_ref[...] = kg.astype(BF16)
    vg_ref[...] = proj(_C_VG, _C_U).astype(BF16)
    qg_up = pltpu.roll(qg, 4 * LANES - HEAD_DIM, 1)
    qg_dn = pltpu.roll(qg, HEAD_DIM, 1)
    per_kv = GQA_HEADS // GQA_KV_HEADS
    for n in range(GQA_KV_HEADS):
        for i in range(per_kv):
            j = n * per_kv + i
            blk_idx = j // 2
            if (j % 2) == n:
                src = qg
            else:
                src = qg_up if n == 0 else qg_dn
            blk = src[:, blk_idx * LANES:(blk_idx + 1) * LANES]
            keep = lo_half if n == 0 else jnp.logical_not(lo_half)
            qg_ref[n, i] = jnp.where(keep, blk, 0.0).astype(BF16)


def _in_proj_call(h, modv, g1, w_cat, ones, gains, rope_tab, n_ctx_tiles):
    b, t, d = h.shape
    nt = t // ROW_TILE
    tm = ROW_TILE

    def row(width, dtype):
        return (pl.BlockSpec((None, tm, width), lambda bi, ti: (bi, ti, 0)),
                jax.ShapeDtypeStruct((b, t, width), dtype))

    outs = [row(d, BF16), row(512, F32), row(768, F32), row(SSD_DT_PAD, F32)]
    qd = (pl.BlockSpec((None, DIFF_HEADS, 2, tm, LANES), lambda bi, ti: (bi, 0, 0, ti, 0)),
          jax.ShapeDtypeStruct((b, DIFF_HEADS, 2, t, LANES), BF16))
    qg = (pl.BlockSpec((None, GQA_KV_HEADS, 4, tm, LANES), lambda bi, ti: (bi, 0, 0, ti, 0)),
          jax.ShapeDtypeStruct((b, GQA_KV_HEADS, 4, t, LANES), BF16))
    outs += [qd, row(512, BF16), row(512, BF16), qg, row(LANES, BF16), row(LANES, BF16),
             row(S5_WIDTH, BF16)]
    return pl.pallas_call(
        _in_proj_body,
        grid=(b, nt),
        in_specs=[
            pl.BlockSpec((None, tm, d), lambda bi, ti: (bi, ti, 0)),
            pl.BlockSpec((None, None, 6, d),
                         lambda bi, ti: (bi, jnp.where(ti >= n_ctx_tiles, 1, 0), 0, 0)),
            pl.BlockSpec((1, d), lambda bi, ti: (0, 0)),
            pl.BlockSpec((d, _C_END), lambda bi, ti: (0, 0)),
            pl.BlockSpec((512, 512), lambda bi, ti: (0, 0)),
            pl.BlockSpec((8, 512), lambda bi, ti: (0, 0)),
            pl.BlockSpec((tm, 384), lambda bi, ti: (ti, 0)),
        ],
        out_specs=[o[0] for o in outs],
        out_shape=[o[1] for o in outs],
        compiler_params=_params(2),
        name="in_proj",
    )(h, modv, g1, w_cat, ones, gains, rope_tab)


def _conv_body(cur_ref, prev_ref, next_ref, w_ref, b_ref, o_ref, buf_ref, *, n_ctx_tiles, n_tiles):
    t = pl.program_id(1)
    tm = ROW_TILE
    has_prev = jnp.logical_and(t != 0, t != n_ctx_tiles)
    has_next = jnp.logical_and(t != n_ctx_tiles - 1, t != n_tiles - 1)
    buf_ref[0:8, :] = jnp.where(has_prev, prev_ref[...], 0.0)
    buf_ref[8:8 + tm, :] = cur_ref[...]
    buf_ref[8 + tm:16 + tm, :] = jnp.where(has_next, next_ref[...], 0.0)
    w = w_ref[...]
    bias = b_ref[...]
    pad = (SSD_CONV_W - 1) // 2
    full = buf_ref[...]
    n = tm + 16
    acc = bias + w[pad:pad + 1] * full[8:8 + tm]
    for k in range(SSD_CONV_W):
        if k != pad:
            acc = acc + w[k:k + 1] * pltpu.roll(full, (pad - k) % n, 0)[8:8 + tm]
    o_ref[...] = _silu(acc)


def _conv_call(xbc, conv_w, conv_b, n_ctx_tiles):
    b, t, ch = xbc.shape
    tm = ROW_TILE
    nt = t // tm
    per = tm // 8
    last8 = t // 8 - 1
    return pl.pallas_call(
        functools.partial(_conv_body, n_ctx_tiles=n_ctx_tiles, n_tiles=nt),
        grid=(b, nt),
        in_specs=[
            pl.BlockSpec((None, tm, ch), lambda bi, ti: (bi, ti, 0)),
            pl.BlockSpec((None, 8, ch), lambda bi, ti: (bi, jnp.maximum(ti * per - 1, 0), 0)),
            pl.BlockSpec((None, 8, ch), lambda bi, ti: (bi, jnp.minimum((ti + 1) * per, last8), 0)),
            pl.BlockSpec((8, ch), lambda bi, ti: (0, 0)),
            pl.BlockSpec((1, ch), lambda bi, ti: (0, 0)),
        ],
        out_specs=pl.BlockSpec((None, tm, ch), lambda bi, ti: (bi, ti, 0)),
        out_shape=jax.ShapeDtypeStruct((b, t, ch), F32),
        scratch_shapes=[pltpu.VMEM((tm + 16, ch), F32)],
        compiler_params=_params(2),
        name="ssd_conv",
    )(xbc, xbc, xbc, conv_w, conv_b)


def _ssd_chunk_index(d, s, n_ctx_chunks, n_chunks):
    bwd = jnp.where(s < n_ctx_chunks, n_ctx_chunks - 1 - s, n_chunks - 1 - (s - n_ctx_chunks))
    return jnp.where(d == 0, s, bwd)


def _ssd_body(xbc_ref, dt_ref, z_ref, aneg_ref, dtb_ref, dsk_ref, ng_ref, o_ref,
              hst_ref, ysc_ref, *, n_ctx_chunks, n_chunks):
    d = pl.program_id(1)
    s = pl.program_id(2)
    c = _ssd_chunk_index(d, s, n_ctx_chunks, n_chunks)
    q = SSD_CHUNK
    nb = xbc_ref.shape[0]
    r = nb * q

    @pl.when(s == 0)
    def _():
        hst_ref[...] = jnp.zeros_like(hst_ref)

    def run(direction):
        xbc = xbc_ref[...].reshape(r, SSD_CONV_CH)
        x = xbc[:, 0:512]
        bm = xbc[:, 512:640]
        cm = xbc[:, 640:768]
        pre = dt_ref[...].reshape(r, SSD_DT_PAD) + dtb_ref[...]
        dtv = jnp.maximum(pre, 0.0) + jnp.log(1.0 + jnp.exp(-jnp.abs(pre)))
        a = dtv * aneg_ref[...]
        row = lax.broadcasted_iota(jnp.int32, (r, r), 0)
        col = lax.broadcasted_iota(jnp.int32, (r, r), 1)
        order = (row >= col) if direction == 0 else (row <= col)
        tri = jnp.logical_and(row // q == col // q, order)
        a_hi = a.astype(BF16)
        r1 = a - a_hi.astype(F32)
        a_mid = r1.astype(BF16)
        a_lo = (r1 - a_mid.astype(F32)).astype(BF16)
        parts = jnp.dot(tri.astype(BF16), jnp.concatenate([a_hi, a_mid, a_lo], axis=1),
                        preferred_element_type=F32)
        acs = parts[:, 0:LANES] + parts[:, LANES:2 * LANES] + parts[:, 2 * LANES:3 * LANES]
        acs_t = acs.T
        edge = q - 1 if direction == 0 else 0
        lasts = [acs[bi * q + edge:bi * q + edge + 1, :] for bi in range(nb)]
        last = jnp.concatenate([jnp.broadcast_to(v, (q, LANES)) for v in lasts], axis=0)
        dte = jnp.exp(last - acs)
        expa = jnp.exp(acs)
        cdec = jnp.concatenate([jnp.broadcast_to(jnp.exp(v), (8, LANES)) for v in lasts], axis=0)
        erow = lax.broadcasted_iota(jnp.int32, (LANES, SSD_INNER), 0)
        ecol = lax.broadcasted_iota(jnp.int32, (LANES, SSD_INNER), 1)
        expand = (erow == ecol // SSD_HEAD_DIM + SSD_HEADS * direction).astype(BF16)
        spread = jnp.dot(jnp.concatenate([dtv, dtv * dte], axis=0).astype(BF16), expand,
                         preferred_element_type=F32)
        xdt = x * spread[0:r]
        xs = x * spread[r:2 * r]
        fine = jnp.concatenate([expa, cdec], axis=0)
        fine_hi = fine.astype(BF16)
        fine_lo = (fine - fine_hi.astype(F32)).astype(BF16)
        spread2 = (jnp.dot(fine_hi, expand, preferred_element_type=F32)
                   + jnp.dot(fine_lo, expand, preferred_element_type=F32))
        expa_x = spread2[0:r]
        bt = bm.T.astype(BF16)
        lane = lax.broadcasted_iota(jnp.int32, (1, LANES), 1)
        tri_q = tri[0:q, 0:q]
        ys = [[None] * nb for _ in range(4)]
        for g in range(2):
            cg = jnp.where(lane // SSD_STATE == g, cm, 0.0).astype(BF16)
            gmat = jnp.dot(cg, bt, preferred_element_type=F32)
            for hp in range(2):
                p = g * 2 + hp
                sl = slice(p * LANES, (p + 1) * LANES)
                for bi in range(nb):
                    rows = slice(bi * q, (bi + 1) * q)
                    ypair = None
                    for e in range(2):
                        j = SSD_HEADS * direction + 2 * p + e
                        seg = acs[rows, j:j + 1] - acs_t[j:j + 1, rows]
                        dec = jnp.exp(jnp.where(tri_q, seg, -jnp.inf))
                        xh = jnp.where(lane // SSD_HEAD_DIM == e, xdt[rows, sl], 0.0)
                        term = _bdot(gmat[rows, rows] * dec, xh)
                        ypair = term if ypair is None else ypair + term
                    hprev = hst_ref[bi, p]
                    ypair = ypair + _bdot(cg[rows], hprev) * expa_x[rows, sl]
                    hst_ref[bi, p] = hprev * spread2[r + 8 * bi:r + 8 * bi + 1, sl] + jnp.dot(
                        bt[:, rows], xs[rows, sl].astype(BF16), preferred_element_type=F32)
                    ys[p][bi] = ypair
        ys = [jnp.concatenate(v, axis=0) for v in ys]
        return x, ys

    @pl.when(d == 0)
    def _():
        _, ys = run(0)
        for p in range(4):
            ysc_ref[c, :, p * LANES:(p + 1) * LANES] = ys[p].astype(ysc_ref.dtype)

    @pl.when(d == 1)
    def _():
        x, ys = run(1)
        y = jnp.concatenate(ys, axis=1) + ysc_ref[c].astype(F32) + dsk_ref[...] * x
        y = y * _silu(z_ref[...].reshape(r, SSD_INNER))
        o_ref[...] = _rms(y, ng_ref[...]).astype(o_ref.dtype).reshape(nb, q, SSD_INNER)


def _ssd_call(xbc_act, dt, z, aneg, dtb, dsk, ng, n_ctx_chunks):
    b, t, _ = xbc_act.shape
    q = SSD_CHUNK
    nc = t // q
    nb = SSD_BATCH_PER_STEP if b % SSD_BATCH_PER_STEP == 0 else 1
    cidx = functools.partial(_ssd_chunk_index, n_ctx_chunks=n_ctx_chunks, n_chunks=nc)

    def late(bi, di, si):
        return (bi, jnp.where(di == 0, n_ctx_chunks - 1, cidx(di, si)), 0)

    return pl.pallas_call(
        functools.partial(_ssd_body, n_ctx_chunks=n_ctx_chunks, n_chunks=nc),
        grid=(b // nb, 2, nc),
        in_specs=[
            pl.BlockSpec((nb, q, SSD_CONV_CH), lambda bi, di, si: (bi, cidx(di, si), 0)),
            pl.BlockSpec((nb, q, SSD_DT_PAD), lambda bi, di, si: (bi, cidx(di, si), 0)),
            pl.BlockSpec((nb, q, SSD_INNER), late),
            pl.BlockSpec((1, LANES), lambda bi, di, si: (0, 0)),
            pl.BlockSpec((1, LANES), lambda bi, di, si: (0, 0)),
            pl.BlockSpec((1, SSD_INNER), lambda bi, di, si: (0, 0)),
            pl.BlockSpec((1, SSD_INNER), lambda bi, di, si: (0, 0)),
        ],
        out_specs=pl.BlockSpec((nb, q, SSD_INNER), late),
        out_shape=jax.ShapeDtypeStruct((b, t, SSD_INNER), BF16),
        scratch_shapes=[pltpu.VMEM((nb, 4, LANES, LANES), F32),
                        pltpu.VMEM((nc, nb * q, SSD_INNER), BF16)],
        compiler_params=_params(3),
        name="ssd_scan",
    )(xbc_act, dt, z, aneg, dtb, dsk, ng)


def _flash(qs, k_ref, v_ref, m_ref, l_ref, acc_ref, *, tk, n_kv, ctx_len, is_ctx_tile):
    m_ref[...] = jnp.full(m_ref.shape, -jnp.inf, F32)
    l_ref[...] = jnp.zeros(l_ref.shape, F32)
    acc_ref[...] = jnp.zeros(acc_ref.shape, F32)

    def step(start, size, limit):
        for i, q in enumerate(qs):
            lanes = slice(i * LANES, (i + 1) * LANES)
            k = k_ref[start:start + size, lanes]
            v = v_ref[start:start + size, lanes]
            s = lax.dot_general(q, k, (((1,), (1,)), ((), ())), preferred_element_type=F32)
            if limit is not None:
                col = lax.broadcasted_iota(jnp.int32, (1, size), 1) + start
                s = jnp.where(col < limit, s, -jnp.inf)
            m_old = m_ref[i]
            m_new = jnp.maximum(m_old, jnp.max(s, axis=-1, keepdims=True))
            alpha = jnp.exp2(m_old - m_new)
            p = jnp.exp2(s - jnp.concatenate([m_new] * (size // LANES), axis=1))
            psum = p[:, 0:LANES]
            for j in range(1, size // LANES):
                psum = psum + p[:, j * LANES:(j + 1) * LANES]
            l_ref[i] = alpha * l_ref[i] + psum
            acc_ref[i] = alpha * acc_ref[i] + jnp.dot(p.astype(BF16), v, preferred_element_type=F32)
            m_ref[i] = m_new

    @pl.when(is_ctx_tile)
    def _():
        ctx_pad = -(-ctx_len // LANES) * LANES
        for start in range(0, ctx_pad, tk):
            step(start, min(tk, ctx_pad - start), ctx_len if ctx_pad != ctx_len else None)

    @pl.when(jnp.logical_not(is_ctx_tile))
    def _():
        for kc in range(n_kv):
            step(kc * tk, tk, None)

    return [acc_ref[i] * (1.0 / jnp.sum(l_ref[i], axis=-1, keepdims=True)) for i in range(len(qs))]


DIFF_HEADS_PER_STEP = 2


def _diff_attn_body(q_ref, k_ref, v_ref, lam_ref, g_ref, o_ref, m_ref, l_ref, acc_ref,
                    *, tq, lam_init, n_ctx_q, **kw):
    qi = pl.program_id(2)
    qs = [q_ref[i].reshape(2 * tq, LANES) for i in range(DIFF_HEADS_PER_STEP)]
    outs = _flash(qs, k_ref, v_ref, m_ref, l_ref, acc_ref, is_ctx_tile=qi < n_ctx_q, **kw)
    lv = lam_ref[...]
    lam = (jnp.exp(jnp.sum(lv[0:1] * lv[1:2], axis=-1, keepdims=True))
           - jnp.exp(jnp.sum(lv[2:3] * lv[3:4], axis=-1, keepdims=True)) + lam_init)
    for i, o in enumerate(outs):
        out = o[0:tq] - lam * o[tq:2 * tq]
        o_ref[:, i * LANES:(i + 1) * LANES] = (_rms(out, g_ref[...]) * (1.0 - lam_init)).astype(o_ref.dtype)


def _gqa_attn_body(q_ref, k_ref, v_ref, o_ref, m_ref, l_ref, acc_ref, *, tq, n_ctx_q, **kw):
    n = pl.program_id(1)
    qi = pl.program_id(2)
    q = q_ref[...].reshape(4 * tq, LANES)
    (o,) = _flash([q], k_ref, v_ref, m_ref, l_ref, acc_ref, is_ctx_tile=qi < n_ctx_q, **kw)
    lane = lax.broadcasted_iota(jnp.int32, (1, LANES), 1)
    lo_half = lane < HEAD_DIM
    for pair in range(2):
        a = o[(2 * pair) * tq:(2 * pair + 1) * tq]
        b = o[(2 * pair + 1) * tq:(2 * pair + 2) * tq]
        from_lo = jnp.where(lo_half, a, pltpu.roll(b, HEAD_DIM, 1))
        from_hi = jnp.where(lo_half, pltpu.roll(a, HEAD_DIM, 1), b)
        o_ref[:, pair * LANES:(pair + 1) * LANES] = jnp.where(n == 0, from_lo, from_hi).astype(o_ref.dtype)


def _attn_tiles(t, ctx_len):
    tq = ROW_TILE
    tk = next(c for c in (2816, 768, 256) if t % c == 0)
    assert ctx_len % tq == 0 and t % tq == 0 and t % tk == 0
    return dict(tq=tq, tk=tk, n_kv=t // tk, ctx_len=ctx_len, n_ctx_q=ctx_len // tq)


def _diff_attn_call(qd, kd, vd, lamv, subln_g, lam_init, ctx_len):
    b, _, _, t, _ = qd.shape
    cfg = _attn_tiles(t, ctx_len)
    tq = cfg["tq"]
    r = 2 * tq
    hs = DIFF_HEADS_PER_STEP
    return pl.pallas_call(
        functools.partial(_diff_attn_body, lam_init=lam_init, **cfg),
        grid=(b, DIFF_HEADS // hs, t // tq),
        in_specs=[
            pl.BlockSpec((None, hs, 2, tq, LANES), lambda bi, hi, qi: (bi, hi, 0, qi, 0)),
            pl.BlockSpec((None, t, hs * LANES), lambda bi, hi, qi: (bi, 0, hi)),
            pl.BlockSpec((None, t, hs * LANES), lambda bi, hi, qi: (bi, 0, hi)),
            pl.BlockSpec((4, LANES), lambda bi, hi, qi: (0, 0)),
            pl.BlockSpec((1, LANES), lambda bi, hi, qi: (0, 0)),
        ],
        out_specs=pl.BlockSpec((None, tq, hs * LANES), lambda bi, hi, qi: (bi, qi, hi)),
        out_shape=jax.ShapeDtypeStruct((b, t, DIFF_HEADS * LANES), BF16),
        scratch_shapes=[pltpu.VMEM((hs, r, LANES), F32)] * 3,
        compiler_params=_params(3),
        name="diff_attention",
    )(qd, kd, vd, lamv, subln_g)


def _gqa_attn_call(qg, kg, vg, ctx_len):
    b, _, _, t, _ = qg.shape
    cfg = _attn_tiles(t, ctx_len)
    tq = cfg["tq"]
    r = 4 * tq
    return pl.pallas_call(
        functools.partial(_gqa_attn_body, **cfg),
        grid=(b, GQA_KV_HEADS, t // tq),
        in_specs=[
            pl.BlockSpec((None, None, 4, tq, LANES), lambda bi, ni, qi: (bi, ni, 0, qi, 0)),
            pl.BlockSpec((None, t, LANES), lambda bi, ni, qi: (bi, 0, 0)),
            pl.BlockSpec((None, t, LANES), lambda bi, ni, qi: (bi, 0, 0)),
        ],
        out_specs=pl.BlockSpec((None, tq, 2 * LANES), lambda bi, ni, qi: (bi, qi, ni)),
        out_shape=jax.ShapeDtypeStruct((b, t, GQA_HEADS * HEAD_DIM), BF16),
        scratch_shapes=[pltpu.VMEM((1, r, LANES), F32)] * 3,
        compiler_params=_params(3),
        name="gqa_attention",
    )(qg, kg, vg)


def _cmul(ar, ai, br, bi):
    return ar * br - ai * bi, ar * bi + ai * br


def _stack_rows(er, ei, vr, vi):
    re = [er * vr[c:c + 1] - ei * vi[c:c + 1] for c in range(S5_GROUP_CH)]
    im = [er * vi[c:c + 1] + ei * vr[c:c + 1] for c in range(S5_GROUP_CH)]
    return jnp.concatenate(re, axis=0), jnp.concatenate(im, axis=0)


def _split_dot_t(a, b):
    dn = (((1,), (1,)), ((), ()))
    ah = a.astype(BF16)
    al = (a - ah.astype(F32)).astype(BF16)
    bh = b.astype(BF16)
    bl = (b - bh.astype(F32)).astype(BF16)
    return (lax.dot_general(ah, bh, dn, preferred_element_type=F32)
            + lax.dot_general(ah, bl, dn, preferred_element_type=F32)
            + lax.dot_general(al, bh, dn, preferred_element_type=F32))


def _s5_weights_body(lam_ref, bt_ref, c_ref, wi_ref, ws_ref, wo_ref, av_ref):
    tc = S5_CHUNK
    lam = lam_ref[...]
    br_t, bi_t = bt_ref[0:16, :], bt_ref[16:32, :]
    cr, ci = c_ref[0:16, :], c_ref[16:32, :]
    kk = lax.broadcasted_iota(jnp.int32, (tc, S5_STATE), 0).astype(F32)
    k8 = lax.broadcasted_iota(jnp.int32, (8, S5_STATE), 0)
    k8 = jnp.where(k8 == 0, 1.0, jnp.where(k8 == 1, tc - 1.0, float(tc)))
    row = lax.broadcasted_iota(jnp.int32, (S5_FLAT, S5_FLAT), 0) % tc
    col = lax.broadcasted_iota(jnp.int32, (S5_FLAT, S5_FLAT), 1) % tc
    w_intra = None
    state_cols, out_cols, a_rows = [], [], []
    for direction in range(2):
        lr = lam[2 * direction:2 * direction + 1]
        li = lam[2 * direction + 1:2 * direction + 2]
        step = jnp.exp(lam[4 + direction:5 + direction])
        mag = jnp.exp(lr * step)
        ar, ai = mag * jnp.cos(li * step), mag * jnp.sin(li * step)
        den = lr * lr + li * li
        fr = ((ar - 1.0) * lr + ai * li) / den
        fi = (ai * lr - (ar - 1.0) * li) / den
        bbr = fr * br_t - fi * bi_t
        bbi = fr * bi_t + fi * br_t
        cs, sn = jnp.cos(kk * (li * step)), jnp.sin(kk * (li * step))
        grow, decay = jnp.exp(-kk * (lr * step)), jnp.exp(kk * (lr * step))
        pr, pi = decay * cs, decay * sn
        nr, ni = grow * cs, -grow * sn
        m8 = jnp.exp(k8 * (lr * step))
        c8r, c8i = m8 * jnp.cos(k8 * (li * step)), m8 * jnp.sin(k8 * (li * step))
        a_one = (c8r[0:1], c8i[0:1])
        a_last = (c8r[1:2], c8i[1:2])
        a_tc = (c8r[2:3], c8i[2:3])
        if direction == 0:
            x_e, y_e = (nr, ni), (pr, pi)
            s_e = _cmul(nr, ni, *a_last)
            o_e = _cmul(pr, pi, *a_one)
            keep = col >= row
        else:
            x_e, y_e = (pr, pi), (nr, ni)
            s_e = (pr, pi)
            o_e = _cmul(nr, ni, *a_tc)
            keep = row >= col
        xr, xi = _stack_rows(*x_e, bbr, bbi)
        yr, yi = _stack_rows(*y_e, cr, ci)
        full = _split_dot_t(jnp.concatenate([xr, -xi], axis=1), jnp.concatenate([yr, yi], axis=1))
        part = jnp.where(keep, full, 0.0)
        w_intra = part if w_intra is None else w_intra + part
        state_cols.append(_stack_rows(*s_e, bbr, bbi))
        o_r, o_i = _stack_rows(*o_e, cr, ci)
        out_cols += [o_r, -o_i]
        a_rows += [jnp.concatenate([a_tc[0], a_tc[0]], axis=1),
                   jnp.concatenate([-a_tc[1], a_tc[1]], axis=1),
                   jnp.concatenate([a_tc[1], -a_tc[1]], axis=1)]
    wi_ref[...] = w_intra.astype(BF16)
    (fr_, fi_), (br_, bi_) = state_cols
    ws_ref[...] = jnp.concatenate([fr_, fi_, br_, bi_, fi_, fr_, bi_, br_], axis=1).astype(BF16)
    wo_ref[...] = jnp.concatenate(out_cols, axis=1).astype(BF16)
    zero = jnp.zeros((1, LANES), F32)
    av_ref[...] = jnp.concatenate(a_rows + [zero, zero], axis=0)


def _s5_weights_call(lam_re, lam_im, log_dt, b_re, b_im, c_re, c_im):
    g, p = S5_GROUPS, S5_STATE
    bc = lambda v: jnp.broadcast_to(v.astype(F32)[:, None], (g, p))
    zero = jnp.zeros((g, p), F32)
    lam = jnp.stack([lam_re[0], lam_im[0], lam_re[1], lam_im[1], bc(log_dt[0]), bc(log_dt[1]),
                     zero, zero], axis=1).astype(F32)
    bt = jnp.concatenate([jnp.swapaxes(b_re, 1, 2), jnp.swapaxes(b_im, 1, 2)], axis=1).astype(F32)
    cc = jnp.concatenate([c_re, c_im], axis=1).astype(F32)
    n = S5_FLAT
    return pl.pallas_call(
        _s5_weights_body,
        grid=(g,),
        in_specs=[pl.BlockSpec((None, 8, p), lambda gi: (gi, 0, 0)),
                  pl.BlockSpec((None, 32, p), lambda gi: (gi, 0, 0)),
                  pl.BlockSpec((None, 32, p), lambda gi: (gi, 0, 0))],
        out_specs=[pl.BlockSpec((None, n, n), lambda gi: (gi, 0, 0)),
                   pl.BlockSpec((None, n, 4 * LANES), lambda gi: (gi, 0, 0)),
                   pl.BlockSpec((None, n, 2 * LANES), lambda gi: (gi, 0, 0)),
                   pl.BlockSpec((None, 8, LANES), lambda gi: (gi, 0, 0))],
        out_shape=[jax.ShapeDtypeStruct((g, n, n), BF16),
                   jax.ShapeDtypeStruct((g, n, 4 * LANES), BF16),
                   jax.ShapeDtypeStruct((g, n, 2 * LANES), BF16),
                   jax.ShapeDtypeStruct((g, 8, LANES), F32)],
        compiler_params=_params(1),
        name="s5_weights",
    )(lam, bt, cc)


def _s5_body(u_ref, wi_ref, ws_ref, wo_ref, av_ref, y_ref, s_ref, h_ref, *, n_ctx_chunks, n_chunks):
    u = u_ref[...]
    s_ref[...] = jnp.dot(u, ws_ref[...], preferred_element_type=F32)
    av = av_ref[...]
    nj = n_chunks
    rows = u_ref.shape[0] // nj
    a1f, a2f, a2sf, a1b, a2b, a2sb = [jnp.broadcast_to(av[i:i + 1], (rows, LANES)) for i in range(6)]
    hf = hfs = hb = hbs = jnp.zeros((rows, LANES), F32)
    for i in range(nj):
        jf = i * rows
        jb = (n_ctx_chunks - 1 - i if i < n_ctx_chunks else nj - 1 - (i - n_ctx_chunks)) * rows
        h_ref[jf:jf + rows, 0:LANES] = hf
        h_ref[jb:jb + rows, LANES:2 * LANES] = hb
        sf = s_ref[jf:jf + rows, 0:LANES]
        sfs = s_ref[jf:jf + rows, 2 * LANES:3 * LANES]
        sb = s_ref[jb:jb + rows, LANES:2 * LANES]
        sbs = s_ref[jb:jb + rows, 3 * LANES:4 * LANES]
        hf, hfs = a1f * hf + a2f * hfs + sf, a1f * hfs + a2sf * hf + sfs
        hb, hbs = a1b * hb + a2b * hbs + sb, a1b * hbs + a2sb * hb + sbs
    y = (jnp.dot(u, wi_ref[...], preferred_element_type=F32)
         + lax.dot_general(h_ref[...].astype(BF16), wo_ref[...], (((1,), (1,)), ((), ())),
                           preferred_element_type=F32))
    y_ref[...] = y.astype(y_ref.dtype)


def _s5_call(ug, w_intra, w_state, w_out, avec, n_ctx_chunks, n_chunks):
    g, r, w = ug.shape
    return pl.pallas_call(
        functools.partial(_s5_body, n_ctx_chunks=n_ctx_chunks, n_chunks=n_chunks),
        grid=(g,),
        in_specs=[
            pl.BlockSpec((None, r, w), lambda gi: (gi, 0, 0)),
            pl.BlockSpec((None, w, w), lambda gi: (gi, 0, 0)),
            pl.BlockSpec((None, w, 4 * LANES), lambda gi: (gi, 0, 0)),
            pl.BlockSpec((None, w, 2 * LANES), lambda gi: (gi, 0, 0)),
            pl.BlockSpec((None, 8, LANES), lambda gi: (gi, 0, 0)),
        ],
        out_specs=pl.BlockSpec((None, r, w), lambda gi: (gi, 0, 0)),
        out_shape=jax.ShapeDtypeStruct((g, r, w), BF16),
        scratch_shapes=[pltpu.VMEM((r, 4 * LANES), F32), pltpu.VMEM((r, 2 * LANES), F32)],
        compiler_params=_params(1),
        name="s5_scan",
    )(ug, w_intra, w_state, w_out, avec)


def _s5_to_groups(u):
    b, t, _ = u.shape
    nj = t // S5_CHUNK
    x = jnp.swapaxes(u.astype(BF16).reshape(b, nj, S5_CHUNK, S5_WIDTH), 2, 3)
    x = lax.optimization_barrier(x.reshape(b, nj, S5_GROUPS, S5_FLAT))
    return jnp.transpose(x, (2, 1, 0, 3)).reshape(S5_GROUPS, nj * b, S5_FLAT)


def _s5_from_groups(y, b):
    g, r, n = y.shape
    nj = r // b
    x = jnp.transpose(y.reshape(g, nj, b, n), (2, 1, 0, 3))
    x = lax.optimization_barrier(x).reshape(b, nj, S5_WIDTH, S5_CHUNK)
    return jnp.swapaxes(x, 2, 3).reshape(b, nj * S5_CHUNK, S5_WIDTH)


def _merge_body(h_ref, mod_ref, xn_ref, ya_ref, yb_ref, yc_ref, u_ref, y5_ref,
                wg_ref, wa_ref, wb_ref, wc_ref, wd_ref, glw_ref, glb_ref, s5d_ref, wo_ref, o_ref):
    xn = xn_ref[...]
    y5 = y5_ref[...].astype(F32) + s5d_ref[...] * u_ref[...].astype(F32)
    gelu = 0.5 * y5 * (1.0 + jnp.tanh(0.7978845608028654 * (y5 + 0.044715 * y5 * y5 * y5)))
    glu = _bdot(gelu, glw_ref[...]) + glb_ref[...]
    yd = glu[:, 0:S5_WIDTH] * _sigmoid(glu[:, S5_WIDTH:2 * S5_WIDTH])
    branches = ((ya_ref[...], wa_ref), (yb_ref[...], wb_ref), (yc_ref[...], wc_ref),
                (yd.astype(BF16), wd_ref))
    merged = None
    for i, (y, w_ref) in enumerate(branches):
        gate = _sigmoid(jnp.dot(xn, wg_ref[i], preferred_element_type=F32))
        term = gate * jnp.dot(y, w_ref[...], preferred_element_type=F32)
        merged = term if merged is None else merged + term
    out = _bdot(merged, wo_ref[...])
    o_ref[...] = h_ref[...] + mod_ref[2:3, :] * out


def _merge_call(h, modv, xn, ya, yb, yc, u, y5, wts, n_ctx_tiles):
    b, t, d = h.shape
    tm = ROW_TILE

    def row(width):
        return pl.BlockSpec((None, tm, width), lambda bi, ti: (bi, ti, 0))

    def const(shape):
        return pl.BlockSpec(shape, lambda bi, ti: (0,) * len(shape), pipeline_mode=pl.Buffered(1))

    w_gate, w_a, w_b, w_c, w_d, glu_w, glu_b, s5_d, w_out = wts
    return pl.pallas_call(
        _merge_body,
        grid=(b, t // tm),
        in_specs=[row(d),
                  pl.BlockSpec((None, None, 6, d),
                               lambda bi, ti: (bi, jnp.where(ti >= n_ctx_tiles, 1, 0), 0, 0)),
                  row(d), row(512), row(512), row(512), row(S5_WIDTH), row(S5_WIDTH),
                  const((4, d, d)), const((512, d)), const((512, d)), const((512, d)),
                  const((S5_WIDTH, d)), const((S5_WIDTH, 2 * S5_WIDTH)), const((1, 2 * S5_WIDTH)),
                  const((1, S5_WIDTH)), const((d, d))],
        out_specs=row(d),
        out_shape=jax.ShapeDtypeStruct((b, t, d), F32),
        compiler_params=_params(2),
        name="branch_merge",
    )(h, modv, xn, ya, yb, yc, u, y5, w_gate, w_a, w_b, w_c, w_d, glu_w, glu_b, s5_d, w_out)


def _ffn_body(h_ref, mod_ref, g_ref, wgu_ref, wd_ref, o_ref):
    h = h_ref[...]
    mod = mod_ref[...]
    xf = (_rms(h, g_ref[...]) * (1.0 + mod[4:5]) + mod[3:4]).astype(BF16)
    gate = jnp.dot(xf, wgu_ref[:, 0:FFN_HIDDEN], preferred_element_type=F32)
    up = jnp.dot(xf, wgu_ref[:, FFN_HIDDEN:2 * FFN_HIDDEN], preferred_element_type=F32)
    act = (_silu(gate) * up).astype(BF16)
    o_ref[...] = h + mod[5:6] * jnp.dot(act, wd_ref[...], preferred_element_type=F32)


def _ffn_call(h, modv, g2, w_gu, w_down, n_ctx_tiles, latent_only):
    b, t, d = h.shape
    tm = ROW_TILE
    skip = n_ctx_tiles if latent_only else 0

    def const(shape):
        return pl.BlockSpec(shape, lambda bi, ti: (0,) * len(shape), pipeline_mode=pl.Buffered(1))

    return pl.pallas_call(
        _ffn_body,
        grid=(b, t // tm - skip),
        in_specs=[pl.BlockSpec((None, tm, d), lambda bi, ti: (bi, ti + skip, 0)),
                  pl.BlockSpec((None, None, 6, d),
                               lambda bi, ti: (bi, jnp.where(ti + skip >= n_ctx_tiles, 1, 0), 0, 0)),
                  const((1, d)), const((d, 2 * FFN_HIDDEN)), const((FFN_HIDDEN, d))],
        out_specs=pl.BlockSpec((None, tm, d), lambda bi, ti: (bi, ti, 0)),
        out_shape=jax.ShapeDtypeStruct((b, t - skip * tm, d), F32),
        compiler_params=_params(2),
        name="swiglu_ffn",
    )(h, modv, g2, w_gu, w_down)


def _rope_tables(ctx_len, seq_len):
    n_rows = seq_len // GRID_W
    rows = jnp.repeat(jnp.arange(n_rows, dtype=F32), GRID_W)
    cols = jnp.tile(jnp.arange(GRID_W, dtype=F32), n_rows)
    quarter = HEAD_DIM // 4
    inv_freq = ROPE_THETA ** (-jnp.arange(quarter, dtype=F32) / quarter)
    ang_r = rows[:, None] * inv_freq
    ang_c = cols[:, None] * inv_freq
    ang = jnp.concatenate([ang_r, ang_r, ang_c, ang_c], axis=-1)
    cos = jnp.concatenate([jnp.ones((ctx_len, HEAD_DIM), F32), jnp.cos(ang)], axis=0)
    sin = jnp.concatenate([jnp.zeros((ctx_len, HEAD_DIM), F32), jnp.sin(ang)], axis=0)
    first = (jnp.arange(HEAD_DIM) % 32) < 16
    sin_a = jnp.where(first, -sin, 0.0)
    sin_b = jnp.where(first, 0.0, sin)
    two = lambda m: jnp.concatenate([m, m], axis=1)
    return jnp.concatenate([two(cos), two(sin_a), two(sin_b)], axis=1)


def _w_in_layout(w_in):
    d = w_in.shape[0]
    a0 = 0
    z = w_in[:, a0:a0 + 512]
    xbc = w_in[:, a0 + 512:a0 + 1280]
    dt = jnp.pad(w_in[:, a0 + 1280:a0 + 1296], ((0, 0), (0, SSD_DT_PAD - 16)))
    rest = w_in[:, 1296:]
    out = jnp.concatenate([z, xbc, dt, rest], axis=1).astype(BF16)
    assert out.shape == (d, _C_END)
    return out


def kernel(x, c, ctx, c_ctx, w_mod, b_mod, norm1_g, norm2_g, w_in, ssd_conv_w, ssd_conv_b, ssd_a_log, ssd_dt_bias, ssd_d, ssd_norm_g, diff_qn_g, diff_kn_g, diff_lam_q1, diff_lam_k1, diff_lam_q2, diff_lam_k2, diff_subln_g, gqa_qn_g, gqa_kn_g, s5_lam_re, s5_lam_im, s5_log_dt, s5_b_re, s5_b_im, s5_c_re, s5_c_im, s5_d, s5_glu_w, s5_glu_b, w_gate, w_br_ssd, w_br_diff, w_br_gqa, w_br_s5, w_out, ffn_w_gate_up, ffn_w_down):
    b, seq_len, d = x.shape
    ctx_len = ctx.shape[1]
    depth = w_mod.shape[0]
    assert b + 1 <= 8
    assert ctx_len % ROW_TILE == 0 and seq_len % ROW_TILE == 0
    n_ctx_tiles = ctx_len // ROW_TILE

    h = jnp.concatenate([ctx, x], axis=1)
    cc = jnp.concatenate([c, c_ctx[None], jnp.zeros((8 - b - 1, d), F32)], axis=0)
    mods = _mod_call(cc, w_mod, b_mod)

    rope_tab = _rope_tables(ctx_len, seq_len)
    blk = jnp.arange(512) // HEAD_DIM
    ones = (blk[:, None] == blk[None, :]).astype(BF16)
    tile8 = lambda g: jnp.tile(g.astype(F32), 512 // HEAD_DIM)

    for layer in range(depth):
        m = mods[layer]
        lat = m[:b].reshape(b, 6, d)
        cmod = jnp.broadcast_to(m[b].reshape(1, 6, d), (b, 6, d))
        modv = jnp.stack([cmod, lat], axis=1)

        gains = jnp.stack([tile8(diff_qn_g[layer]), tile8(diff_kn_g[layer]),
                           tile8(gqa_qn_g[layer]), tile8(gqa_kn_g[layer])]
                          + [jnp.zeros((512,), F32)] * 4, axis=0)
        (xn, z, xbc, dt, qd, kd, vd, qg, kg, vg, u) = _in_proj_call(
            h, modv, norm1_g[layer][None], _w_in_layout(w_in[layer]), ones, gains, rope_tab,
            n_ctx_tiles)

        conv_w = jnp.pad(ssd_conv_w[layer].astype(F32), ((0, 8 - SSD_CONV_W), (0, 0)))
        xbc_act = _conv_call(xbc, conv_w, ssd_conv_b[layer][None].astype(F32), n_ctx_tiles)
        pad16 = lambda v: jnp.pad(v.reshape(1, 16).astype(F32), ((0, 0), (0, LANES - 16)))
        aneg = pad16(-jnp.exp(ssd_a_log[layer].astype(F32)))
        dtb = pad16(ssd_dt_bias[layer])
        dsk = jnp.repeat(ssd_d[layer].astype(F32), SSD_HEAD_DIM)[None]
        ya = _ssd_call(xbc_act, dt, z, aneg, dtb, dsk, ssd_norm_g[layer][None].astype(F32),
                       ctx_len // SSD_CHUNK)

        lam_init = 0.8 - 0.6 * math.exp(-0.3 * layer)
        lamv = jnp.pad(jnp.stack([diff_lam_q1[layer], diff_lam_k1[layer],
                                  diff_lam_q2[layer], diff_lam_k2[layer]]).astype(F32),
                       ((0, 0), (0, LANES - HEAD_DIM)))
        yb = _diff_attn_call(qd, kd, vd, lamv, diff_subln_g[layer][None].astype(F32), lam_init, ctx_len)

        yc = _gqa_attn_call(qg, kg, vg, ctx_len)

        s5w = _s5_weights_call(s5_lam_re[layer], s5_lam_im[layer], s5_log_dt[layer],
                               s5_b_re[layer], s5_b_im[layer], s5_c_re[layer], s5_c_im[layer])
        y5 = _s5_from_groups(_s5_call(_s5_to_groups(u), *s5w, ctx_len // S5_CHUNK,
                                      (ctx_len + seq_len) // S5_CHUNK), b)

        wts = (w_gate[layer].astype(BF16), w_br_ssd[layer].astype(BF16), w_br_diff[layer].astype(BF16),
               w_br_gqa[layer].astype(BF16), w_br_s5[layer].astype(BF16), s5_glu_w[layer].astype(BF16),
               s5_glu_b[layer][None].astype(F32), s5_d[layer][None].astype(F32), w_out[layer].astype(BF16))
        h = _merge_call(h, modv, xn, ya, yb, yc, u, y5, wts, n_ctx_tiles)
        h = _ffn_call(h, modv, norm2_g[layer][None].astype(F32), ffn_w_gate_up[layer].astype(BF16),
                      ffn_w_down[layer].astype(BF16), n_ctx_tiles, latent_only=layer == depth - 1)
    return h
```

```python
import functools
import math

import jax
import jax.numpy as jnp
from jax import lax
from jax.experimental import pallas as pl
from jax.experimental.pallas import tpu as pltpu

F32 = jnp.float32
BF16 = jnp.bfloat16
HIGHEST = lax.Precision.HIGHEST

D_MODEL = 1024
GRID_W = 64
ROPE_THETA = 10000.0
NORM_EPS = 1e-6

SSD_INNER = 512
SSD_HEADS = 8
SSD_HEAD_DIM = 64
SSD_STATE = 64
SSD_CHUNK = 128
SSD_CONV_W = 5
SSD_CONV_CH = 768
SSD_DT_PAD = 128
SSD_BATCH_PER_STEP = 4

DIFF_HEADS = 4
HEAD_DIM = 64
GQA_HEADS = 8
GQA_KV_HEADS = 2

S5_GROUP_CH = 16
S5_STATE = 64
S5_WIDTH = 384
S5_GROUPS = 24
S5_CHUNK = 64
S5_FLAT = S5_CHUNK * S5_GROUP_CH

FFN_HIDDEN = 2816

ROW_TILE = 256
LANES = 128
VMEM_LIMIT = 56 * 1024 * 1024

_C_Z, _C_XBC, _C_DT = 0, 512, 1280
_C_QD, _C_KD, _C_VD = 1408, 1920, 2432
_C_QG, _C_KG, _C_VG = 2944, 3456, 3584
_C_U, _C_END = 3712, 4096


def _params(n_grid):
    return pltpu.CompilerParams(dimension_semantics=("arbitrary",) * n_grid,
                                vmem_limit_bytes=VMEM_LIMIT)


def _bdot(a, b):
    return jnp.dot(a.astype(BF16), b.astype(BF16), preferred_element_type=F32)


def _hdot(a, b):
    return jnp.dot(a, b, precision=HIGHEST, preferred_element_type=F32)


def _rms(x, g):
    return x * lax.rsqrt(jnp.mean(x * x, axis=-1, keepdims=True) + NORM_EPS) * g


def _sigmoid(x):
    return 1.0 / (1.0 + jnp.exp(-x))


def _silu(x):
    return x * _sigmoid(x)


def _mod_body(c_ref, w_ref, b_ref, o_ref):
    o_ref[...] = _hdot(_silu(c_ref[...]), w_ref[...]) + b_ref[...]


def _mod_call(cc, w_mod, b_mod):
    depth = w_mod.shape[0]
    d = D_MODEL
    return pl.pallas_call(
        _mod_body,
        grid=(depth, 6),
        in_specs=[pl.BlockSpec((8, d), lambda l, j: (0, 0)),
                  pl.BlockSpec((None, d, d), lambda l, j: (l, 0, j)),
                  pl.BlockSpec((None, 1, d), lambda l, j: (l, 0, j))],
        out_specs=pl.BlockSpec((None, 8, d), lambda l, j: (l, 0, j)),
        out_shape=jax.ShapeDtypeStruct((depth, 8, 6 * d), F32),
        compiler_params=_params(2),
        name="adaln_mod",
    )(cc, w_mod, b_mod.reshape(depth, 1, 6 * d))


def _head_rms(t, gain, ones):
    ss = jnp.dot((t * t).astype(BF16), ones, preferred_element_type=F32)
    return t * lax.rsqrt(ss * (1.0 / HEAD_DIM) + NORM_EPS) * gain


def _rope(t, cos, sin_a, sin_b):
    w = t.shape[-1]
    return t * cos + pltpu.roll(t, w - 16, 1) * sin_a + pltpu.roll(t, 16, 1) * sin_b


def _in_proj_body(h_ref, mod_ref, g_ref, w_ref, ones_ref, gains_ref, rope_ref,
                  xn_ref, z_ref, xbc_ref, dt_ref, qd_ref, kd_ref, vd_ref,
                  qg_ref, kg_ref, vg_ref, u_ref):
    x = h_ref[...]
    mod = mod_ref[...]
    xn = _rms(x, g_ref[...]) * (1.0 + mod[1:2]) + mod[0:1]
    xb = xn.astype(BF16)
    xn_ref[...] = xb

    def proj(lo, hi):
        return jnp.dot(xb, w_ref[:, lo:hi], preferred_element_type=F32)

    z_ref[...] = proj(_C_Z, _C_XBC)
    xbc_ref[...] = proj(_C_XBC, _C_DT)
    dt_ref[...] = proj(_C_DT, _C_QD)
    u_ref[...] = proj(_C_U, _C_END).astype(u_ref.dtype)

    rope = rope_ref[...]
    cos1, sa1, sb1 = rope[:, 0:128], rope[:, 128:256], rope[:, 256:384]
    cos4 = jnp.concatenate([cos1] * 4, axis=1)
    sa4 = jnp.concatenate([sa1] * 4, axis=1)
    sb4 = jnp.concatenate([sb1] * 4, axis=1)
    ones = ones_ref[...]
    gains = gains_ref[...]
    lane = lax.broadcasted_iota(jnp.int32, (1, LANES), 1)
    lo_half = lane < HEAD_DIM
    scale = HEAD_DIM ** -0.5 * math.log2(math.e)

    qd = _rope(_head_rms(proj(_C_QD, _C_KD), gains[0:1], ones), cos4, sa4, sb4) * scale
    kd = _rope(_head_rms(proj(_C_KD, _C_VD), gains[1:2], ones), cos4, sa4, sb4)
    kd_ref[...] = kd.astype(BF16)
    vd_ref[...] = proj(_C_VD, _C_QG).astype(BF16)
    for h in range(DIFF_HEADS):
        blk = qd[:, h * LANES:(h + 1) * LANES]
        qd_ref[h, 0] = jnp.where(lo_half, blk, 0.0).astype(BF16)
        qd_ref[h, 1] = jnp.where(lo_half, 0.0, blk).astype(BF16)

    qg = _rope(_head_rms(proj(_C_QG, _C_KG), gains[2:3], ones), cos4, sa4, sb4) * scale
    kg = _rope(_head_rms(proj(_C_KG, _C_VG), gains[3:4, :LANES], ones[:LANES, :LANES]),
               cos1, sa1, sb1)
    kg_ref[...] = kg.astype(BF16)
    vg_ref[...] = proj(_C_VG, _C_U).astype(BF16)
    qg_up = pltpu.roll(qg, 4 * LANES - HEAD_DIM, 1)
    qg_dn = pltpu.roll(qg, HEAD_DIM, 1)
    per_kv = GQA_HEADS // GQA_KV_HEADS
    for n in range(GQA_KV_HEADS):
        for i in range(per_kv):
            j = n * per_kv + i
            blk_idx = j // 2
            if (j % 2) == n:
                src = qg
            else:
                src = qg_up if n == 0 else qg_dn
            blk = src[:, blk_idx * LANES:(blk_idx + 1) * LANES]
            keep = lo_half if n == 0 else jnp.logical_not(lo_half)
            qg_ref[n, i] = jnp.where(keep, blk, 0.0).astype(BF16)


def _in_proj_call(h, modv, g1, w_cat, ones, gains, rope_tab, n_ctx_tiles):
    b, t, d = h.shape
    nt = t // ROW_TILE
    tm = ROW_TILE

    def row(width, dtype):
        return (pl.BlockSpec((None, tm, width), lambda bi, ti: (bi, ti, 0)),
                jax.ShapeDtypeStruct((b, t, width), dtype))

    outs = [row(d, BF16), row(512, F32), row(768, F32), row(SSD_DT_PAD, F32)]
    qd = (pl.BlockSpec((None, DIFF_HEADS, 2, tm, LANES), lambda bi, ti: (bi, 0, 0, ti, 0)),
          jax.ShapeDtypeStruct((b, DIFF_HEADS, 2, t, LANES), BF16))
    qg = (pl.BlockSpec((None, GQA_KV_HEADS, 4, tm, LANES), lambda bi, ti: (bi, 0, 0, ti, 0)),
          jax.ShapeDtypeStruct((b, GQA_KV_HEADS, 4, t, LANES), BF16))
    outs += [qd, row(512, BF16), row(512, BF16), qg, row(LANES, BF16), row(LANES, BF16),
             row(S5_WIDTH, BF16)]
    return pl.pallas_call(
        _in_proj_body,
        grid=(b, nt),
        in_specs=[
            pl.BlockSpec((None, tm, d), lambda bi, ti: (bi, ti, 0)),
            pl.BlockSpec((None, None, 6, d),
                         lambda bi, ti: (bi, jnp.where(ti >= n_ctx_tiles, 1, 0), 0, 0)),
            pl.BlockSpec((1, d), lambda bi, ti: (0, 0)),
            pl.BlockSpec((d, _C_END), lambda bi, ti: (0, 0)),
            pl.BlockSpec((512, 512), lambda bi, ti: (0, 0)),
            pl.BlockSpec((8, 512), lambda bi, ti: (0, 0)),
            pl.BlockSpec((tm, 384), lambda bi, ti: (ti, 0)),
        ],
        out_specs=[o[0] for o in outs],
        out_shape=[o[1] for o in outs],
        compiler_params=_params(2),
        name="in_proj",
    )(h, modv, g1, w_cat, ones, gains, rope_tab)


def _conv_body(cur_ref, prev_ref, next_ref, w_ref, b_ref, o_ref, buf_ref, *, n_ctx_tiles, n_tiles):
    t = pl.program_id(1)
    tm = ROW_TILE
    has_prev = jnp.logical_and(t != 0, t != n_ctx_tiles)
    has_next = jnp.logical_and(t != n_ctx_tiles - 1, t != n_tiles - 1)
    buf_ref[0:8, :] = jnp.where(has_prev, prev_ref[...], 0.0)
    buf_ref[8:8 + tm, :] = cur_ref[...]
    buf_ref[8 + tm:16 + tm, :] = jnp.where(has_next, next_ref[...], 0.0)
    w = w_ref[...]
    bias = b_ref[...]
    pad = (SSD_CONV_W - 1) // 2
    full = buf_ref[...]
    n = tm + 16
    acc = bias + w[pad:pad + 1] * full[8:8 + tm]
    for k in range(SSD_CONV_W):
        if k != pad:
            acc = acc + w[k:k + 1] * pltpu.roll(full, (pad - k) % n, 0)[8:8 + tm]
    o_ref[...] = _silu(acc)


def _conv_call(xbc, conv_w, conv_b, n_ctx_tiles):
    b, t, ch = xbc.shape
    tm = ROW_TILE
    nt = t // tm
    per = tm // 8
    last8 = t // 8 - 1
    return pl.pallas_call(
        functools.partial(_conv_body, n_ctx_tiles=n_ctx_tiles, n_tiles=nt),
        grid=(b, nt),
        in_specs=[
            pl.BlockSpec((None, tm, ch), lambda bi, ti: (bi, ti, 0)),
            pl.BlockSpec((None, 8, ch), lambda bi, ti: (bi, jnp.maximum(ti * per - 1, 0), 0)),
            pl.BlockSpec((None, 8, ch), lambda bi, ti: (bi, jnp.minimum((ti + 1) * per, last8), 0)),
            pl.BlockSpec((8, ch), lambda bi, ti: (0, 0)),
            pl.BlockSpec((1, ch), lambda bi, ti: (0, 0)),
        ],
        out_specs=pl.BlockSpec((None, tm, ch), lambda bi, ti: (bi, ti, 0)),
        out_shape=jax.ShapeDtypeStruct((b, t, ch), F32),
        scratch_shapes=[pltpu.VMEM((tm + 16, ch), F32)],
        compiler_params=_params(2),
        name="ssd_conv",
    )(xbc, xbc, xbc, conv_w, conv_b)


def _ssd_chunk_index(d, s, n_ctx_chunks, n_chunks):
    bwd = jnp.where(s < n_ctx_chunks, n_ctx_chunks - 1 - s, n_chunks - 1 - (s - n_ctx_chunks))
    return jnp.where(d == 0, s, bwd)


def _ssd_body(xbc_ref, dt_ref, z_ref, aneg_ref, dtb_ref, dsk_ref, ng_ref, o_ref,
              hst_ref, ysc_ref, *, n_ctx_chunks, n_chunks):
    d = pl.program_id(1)
    s = pl.program_id(2)
    c = _ssd_chunk_index(d, s, n_ctx_chunks, n_chunks)
    q = SSD_CHUNK
    nb = xbc_ref.shape[0]
    r = nb * q

    @pl.when(s == 0)
    def _():
        hst_ref[...] = jnp.zeros_like(hst_ref)

    def run(direction):
        xbc = xbc_ref[...].reshape(r, SSD_CONV_CH)
        x = xbc[:, 0:512]
        bm = xbc[:, 512:640]
        cm = xbc[:, 640:768]
        pre = dt_ref[...].reshape(r, SSD_DT_PAD) + dtb_ref[...]
        dtv = jnp.maximum(pre, 0.0) + jnp.log(1.0 + jnp.exp(-jnp.abs(pre)))
        a = dtv * aneg_ref[...]
        row = lax.broadcasted_iota(jnp.int32, (r, r), 0)
        col = lax.broadcasted_iota(jnp.int32, (r, r), 1)
        order = (row >= col) if direction == 0 else (row <= col)
        tri = jnp.logical_and(row // q == col // q, order)
        a_hi = a.astype(BF16)
        r1 = a - a_hi.astype(F32)
        a_mid = r1.astype(BF16)
        a_lo = (r1 - a_mid.astype(F32)).astype(BF16)
        terms = [t[bi * q:(bi + 1) * q] for bi in range(nb) for t in (a_hi, a_mid, a_lo)]
        parts = jnp.dot(tri[0:q, 0:q].astype(BF16), jnp.concatenate(terms, axis=1),
                        preferred_element_type=F32)
        acs = jnp.concatenate(
            [parts[:, (3 * bi) * LANES:(3 * bi + 1) * LANES] + parts[:, (3 * bi + 1) * LANES:(3 * bi + 2) * LANES]
             + parts[:, (3 * bi + 2) * LANES:(3 * bi + 3) * LANES] for bi in range(nb)], axis=0)
        acs_t = acs.T
        edge = q - 1 if direction == 0 else 0
        lasts = [acs[bi * q + edge:bi * q + edge + 1, :] for bi in range(nb)]
        last = jnp.concatenate([jnp.broadcast_to(v, (q, LANES)) for v in lasts], axis=0)
        dte = jnp.exp(last - acs)
        expa = jnp.exp(acs)
        cdec = jnp.concatenate([jnp.broadcast_to(jnp.exp(v), (8, LANES)) for v in lasts], axis=0)
        erow = lax.broadcasted_iota(jnp.int32, (LANES, SSD_INNER), 0)
        ecol = lax.broadcasted_iota(jnp.int32, (LANES, SSD_INNER), 1)
        expand = (erow == ecol // SSD_HEAD_DIM + SSD_HEADS * direction).astype(BF16)
        spread = jnp.dot(jnp.concatenate([dtv, dtv * dte], axis=0).astype(BF16), expand,
                         preferred_element_type=F32)
        xdt = x * spread[0:r]
        xs = x * spread[r:2 * r]
        fine = jnp.concatenate([expa, cdec], axis=0)
        fine_hi = fine.astype(BF16)
        fine_lo = (fine - fine_hi.astype(F32)).astype(BF16)
        spread2 = (jnp.dot(fine_hi, expand, preferred_element_type=F32)
                   + jnp.dot(fine_lo, expand, preferred_element_type=F32))
        expa_x = spread2[0:r]
        bt = bm.T.astype(BF16)
        lane = lax.broadcasted_iota(jnp.int32, (1, LANES), 1)
        tri_q = tri[0:q, 0:q]
        ys = [[None] * nb for _ in range(4)]
        for g in range(2):
            cg = jnp.where(lane // SSD_STATE == g, cm, 0.0).astype(BF16)
            gmats = [jnp.dot(cg[bi * q:(bi + 1) * q], bt[:, bi * q:(bi + 1) * q], preferred_element_type=F32)
                     for bi in range(nb)]
            for hp in range(2):
                p = g * 2 + hp
                sl = slice(p * LANES, (p + 1) * LANES)
                for bi in range(nb):
                    rows = slice(bi * q, (bi + 1) * q)
                    ypair = None
                    for e in range(2):
                        j = SSD_HEADS * direction + 2 * p + e
                        seg = acs[rows, j:j + 1] - acs_t[j:j + 1, rows]
                        dec = jnp.exp(jnp.where(tri_q, seg, -jnp.inf))
                        xh = jnp.where(lane // SSD_HEAD_DIM == e, xdt[rows, sl], 0.0)
                        term = _bdot(gmats[bi] * dec, xh)
                        ypair = term if ypair is None else ypair + term
                    hprev = hst_ref[bi, p]
                    ypair = ypair + _bdot(cg[rows], hprev) * expa_x[rows, sl]
                    hst_ref[bi, p] = hprev * spread2[r + 8 * bi:r + 8 * bi + 1, sl] + jnp.dot(
                        bt[:, rows], xs[rows, sl].astype(BF16), preferred_element_type=F32)
                    ys[p][bi] = ypair
        ys = [jnp.concatenate(v, axis=0) for v in ys]
        return x, ys

    @pl.when(d == 0)
    def _():
        _, ys = run(0)
        for p in range(4):
            ysc_ref[c, :, p * LANES:(p + 1) * LANES] = ys[p].astype(ysc_ref.dtype)

    @pl.when(d == 1)
    def _():
        x, ys = run(1)
        y = jnp.concatenate(ys, axis=1) + ysc_ref[c].astype(F32) + dsk_ref[...] * x
        y = y * _silu(z_ref[...].reshape(r, SSD_INNER))
        o_ref[...] = _rms(y, ng_ref[...]).astype(o_ref.dtype).reshape(nb, q, SSD_INNER)


def _ssd_call(xbc_act, dt, z, aneg, dtb, dsk, ng, n_ctx_chunks):
    b, t, _ = xbc_act.shape
    q = SSD_CHUNK
    nc = t // q
    nb = SSD_BATCH_PER_STEP if b % SSD_BATCH_PER_STEP == 0 else 1
    cidx = functools.partial(_ssd_chunk_index, n_ctx_chunks=n_ctx_chunks, n_chunks=nc)

    def late(bi, di, si):
        return (bi, jnp.where(di == 0, n_ctx_chunks - 1, cidx(di, si)), 0)

    return pl.pallas_call(
        functools.partial(_ssd_body, n_ctx_chunks=n_ctx_chunks, n_chunks=nc),
        grid=(b // nb, 2, nc),
        in_specs=[
            pl.BlockSpec((nb, q, SSD_CONV_CH), lambda bi, di, si: (bi, cidx(di, si), 0)),
            pl.BlockSpec((nb, q, SSD_DT_PAD), lambda bi, di, si: (bi, cidx(di, si), 0)),
            pl.BlockSpec((nb, q, SSD_INNER), late),
            pl.BlockSpec((1, LANES), lambda bi, di, si: (0, 0)),
            pl.BlockSpec((1, LANES), lambda bi, di, si: (0, 0)),
            pl.BlockSpec((1, SSD_INNER), lambda bi, di, si: (0, 0)),
            pl.BlockSpec((1, SSD_INNER), lambda bi, di, si: (0, 0)),
        ],
        out_specs=pl.BlockSpec((nb, q, SSD_INNER), late),
        out_shape=jax.ShapeDtypeStruct((b, t, SSD_INNER), BF16),
        scratch_shapes=[pltpu.VMEM((nb, 4, LANES, LANES), F32),
                        pltpu.VMEM((nc, nb * q, SSD_INNER), BF16)],
        compiler_params=_params(3),
        name="ssd_scan",
    )(xbc_act, dt, z, aneg, dtb, dsk, ng)


def _flash(qs, k_ref, v_ref, m_ref, l_ref, acc_ref, *, tk, n_kv, ctx_len, is_ctx_tile):
    m_ref[...] = jnp.full(m_ref.shape, -jnp.inf, F32)
    l_ref[...] = jnp.zeros(l_ref.shape, F32)
    acc_ref[...] = jnp.zeros(acc_ref.shape, F32)

    def step(start, size, limit):
        for i, q in enumerate(qs):
            lanes = slice(i * LANES, (i + 1) * LANES)
            k = k_ref[start:start + size, lanes]
            v = v_ref[start:start + size, lanes]
            s = lax.dot_general(q, k, (((1,), (1,)), ((), ())), preferred_element_type=F32)
            if limit is not None:
                col = lax.broadcasted_iota(jnp.int32, (1, size), 1) + start
                s = jnp.where(col < limit, s, -jnp.inf)
            m_old = m_ref[i]
            m_new = jnp.maximum(m_old, jnp.max(s, axis=-1, keepdims=True))
            alpha = jnp.exp2(m_old - m_new)
            p = jnp.exp2(s - jnp.concatenate([m_new] * (size // LANES), axis=1))
            psum = p[:, 0:LANES]
            for j in range(1, size // LANES):
                psum = psum + p[:, j * LANES:(j + 1) * LANES]
            l_ref[i] = alpha * l_ref[i] + psum
            acc_ref[i] = alpha * acc_ref[i] + jnp.dot(p.astype(BF16), v, preferred_element_type=F32)
            m_ref[i] = m_new

    @pl.when(is_ctx_tile)
    def _():
        ctx_pad = -(-ctx_len // LANES) * LANES
        for start in range(0, ctx_pad, tk):
            step(start, min(tk, ctx_pad - start), ctx_len if ctx_pad != ctx_len else None)

    @pl.when(jnp.logical_not(is_ctx_tile))
    def _():
        for kc in range(n_kv):
            step(kc * tk, tk, None)

    return [acc_ref[i] * (1.0 / jnp.sum(l_ref[i], axis=-1, keepdims=True)) for i in range(len(qs))]


DIFF_HEADS_PER_STEP = 2


def _diff_attn_body(q_ref, k_ref, v_ref, lam_ref, g_ref, o_ref, m_ref, l_ref, acc_ref,
                    *, tq, lam_init, n_ctx_q, **kw):
    qi = pl.program_id(2)
    qs = [q_ref[i].reshape(2 * tq, LANES) for i in range(DIFF_HEADS_PER_STEP)]
    outs = _flash(qs, k_ref, v_ref, m_ref, l_ref, acc_ref, is_ctx_tile=qi < n_ctx_q, **kw)
    lv = lam_ref[...]
    lam = (jnp.exp(jnp.sum(lv[0:1] * lv[1:2], axis=-1, keepdims=True))
           - jnp.exp(jnp.sum(lv[2:3] * lv[3:4], axis=-1, keepdims=True)) + lam_init)
    for i, o in enumerate(outs):
        out = o[0:tq] - lam * o[tq:2 * tq]
        o_ref[:, i * LANES:(i + 1) * LANES] = (_rms(out, g_ref[...]) * (1.0 - lam_init)).astype(o_ref.dtype)


def _gqa_attn_body(q_ref, k_ref, v_ref, o_ref, m_ref, l_ref, acc_ref, *, tq, n_ctx_q, **kw):
    n = pl.program_id(1)
    qi = pl.program_id(2)
    q = q_ref[...].reshape(4 * tq, LANES)
    (o,) = _flash([q], k_ref, v_ref, m_ref, l_ref, acc_ref, is_ctx_tile=qi < n_ctx_q, **kw)
    lane = lax.broadcasted_iota(jnp.int32, (1, LANES), 1)
    lo_half = lane < HEAD_DIM
    for pair in range(2):
        a = o[(2 * pair) * tq:(2 * pair + 1) * tq]
        b = o[(2 * pair + 1) * tq:(2 * pair + 2) * tq]
        from_lo = jnp.where(lo_half, a, pltpu.roll(b, HEAD_DIM, 1))
        from_hi = jnp.where(lo_half, pltpu.roll(a, HEAD_DIM, 1), b)
        o_ref[:, pair * LANES:(pair + 1) * LANES] = jnp.where(n == 0, from_lo, from_hi).astype(o_ref.dtype)


def _attn_tiles(t, ctx_len):
    tq = ROW_TILE
    tk = next(c for c in (2816, 768, 256) if t % c == 0)
    assert ctx_len % tq == 0 and t % tq == 0 and t % tk == 0
    return dict(tq=tq, tk=tk, n_kv=t // tk, ctx_len=ctx_len, n_ctx_q=ctx_len // tq)


def _diff_attn_call(qd, kd, vd, lamv, subln_g, lam_init, ctx_len):
    b, _, _, t, _ = qd.shape
    cfg = _attn_tiles(t, ctx_len)
    tq = cfg["tq"]
    r = 2 * tq
    hs = DIFF_HEADS_PER_STEP
    return pl.pallas_call(
        functools.partial(_diff_attn_body, lam_init=lam_init, **cfg),
        grid=(b, DIFF_HEADS // hs, t // tq),
        in_specs=[
            pl.BlockSpec((None, hs, 2, tq, LANES), lambda bi, hi, qi: (bi, hi, 0, qi, 0)),
            pl.BlockSpec((None, t, hs * LANES), lambda bi, hi, qi: (bi, 0, hi)),
            pl.BlockSpec((None, t, hs * LANES), lambda bi, hi, qi: (bi, 0, hi)),
            pl.BlockSpec((4, LANES), lambda bi, hi, qi: (0, 0)),
            pl.BlockSpec((1, LANES), lambda bi, hi, qi: (0, 0)),
        ],
        out_specs=pl.BlockSpec((None, tq, hs * LANES), lambda bi, hi, qi: (bi, qi, hi)),
        out_shape=jax.ShapeDtypeStruct((b, t, DIFF_HEADS * LANES), BF16),
        scratch_shapes=[pltpu.VMEM((hs, r, LANES), F32)] * 3,
        compiler_params=_params(3),
        name="diff_attention",
    )(qd, kd, vd, lamv, subln_g)


def _gqa_attn_call(qg, kg, vg, ctx_len):
    b, _, _, t, _ = qg.shape
    cfg = _attn_tiles(t, ctx_len)
    tq = cfg["tq"]
    r = 4 * tq
    return pl.pallas_call(
        functools.partial(_gqa_attn_body, **cfg),
        grid=(b, GQA_KV_HEADS, t // tq),
        in_specs=[
            pl.BlockSpec((None, None, 4, tq, LANES), lambda bi, ni, qi: (bi, ni, 0, qi, 0)),
            pl.BlockSpec((None, t, LANES), lambda bi, ni, qi: (bi, 0, 0)),
            pl.BlockSpec((None, t, LANES), lambda bi, ni, qi: (bi, 0, 0)),
        ],
        out_specs=pl.BlockSpec((None, tq, 2 * LANES), lambda bi, ni, qi: (bi, qi, ni)),
        out_shape=jax.ShapeDtypeStruct((b, t, GQA_HEADS * HEAD_DIM), BF16),
        scratch_shapes=[pltpu.VMEM((1, r, LANES), F32)] * 3,
        compiler_params=_params(3),
        name="gqa_attention",
    )(qg, kg, vg)


def _cmul(ar, ai, br, bi):
    return ar * br - ai * bi, ar * bi + ai * br


def _stack_rows(er, ei, vr, vi):
    re = [er * vr[c:c + 1] - ei * vi[c:c + 1] for c in range(S5_GROUP_CH)]
    im = [er * vi[c:c + 1] + ei * vr[c:c + 1] for c in range(S5_GROUP_CH)]
    return jnp.concatenate(re, axis=0), jnp.concatenate(im, axis=0)


def _split_dot_t(a, b):
    dn = (((1,), (1,)), ((), ()))
    ah = a.astype(BF16)
    al = (a - ah.astype(F32)).astype(BF16)
    bh = b.astype(BF16)
    bl = (b - bh.astype(F32)).astype(BF16)
    return lax.dot_general(jnp.concatenate([ah, ah, al], axis=1), jnp.concatenate([bh, bl, bh], axis=1),
                           dn, preferred_element_type=F32)


def _s5_weights_body(lam_ref, bt_ref, c_ref, wi_ref, ws_ref, wo_ref, av_ref):
    tc = S5_CHUNK
    lam = lam_ref[...]
    br_t, bi_t = bt_ref[0:16, :], bt_ref[16:32, :]
    cr, ci = c_ref[0:16, :], c_ref[16:32, :]
    kk = lax.broadcasted_iota(jnp.int32, (tc, S5_STATE), 0).astype(F32)
    k8 = lax.broadcasted_iota(jnp.int32, (8, S5_STATE), 0)
    k8 = jnp.where(k8 == 0, 1.0, jnp.where(k8 == 1, tc - 1.0, float(tc)))
    row = lax.broadcasted_iota(jnp.int32, (S5_FLAT, S5_FLAT), 0) % tc
    col = lax.broadcasted_iota(jnp.int32, (S5_FLAT, S5_FLAT), 1) % tc
    w_intra = None
    state_cols, out_cols, a_rows = [], [], []
    for direction in range(2):
        lr = lam[2 * direction:2 * direction + 1]
        li = lam[2 * direction + 1:2 * direction + 2]
        step = jnp.exp(lam[4 + direction:5 + direction])
        mag = jnp.exp(lr * step)
        ar, ai = mag * jnp.cos(li * step), mag * jnp.sin(li * step)
        den = lr * lr + li * li
        fr = ((ar - 1.0) * lr + ai * li) / den
        fi = (ai * lr - (ar - 1.0) * li) / den
        bbr = fr * br_t - fi * bi_t
        bbi = fr * bi_t + fi * br_t
        cs, sn = jnp.cos(kk * (li * step)), jnp.sin(kk * (li * step))
        grow, decay = jnp.exp(-kk * (lr * step)), jnp.exp(kk * (lr * step))
        pr, pi = decay * cs, decay * sn
        nr, ni = grow * cs, -grow * sn
        m8 = jnp.exp(k8 * (lr * step))
        c8r, c8i = m8 * jnp.cos(k8 * (li * step)), m8 * jnp.sin(k8 * (li * step))
        a_one = (c8r[0:1], c8i[0:1])
        a_last = (c8r[1:2], c8i[1:2])
        a_tc = (c8r[2:3], c8i[2:3])
        if direction == 0:
            x_e, y_e = (nr, ni), (pr, pi)
            s_e = _cmul(nr, ni, *a_last)
            o_e = _cmul(pr, pi, *a_one)
            keep = col >= row
        else:
            x_e, y_e = (pr, pi), (nr, ni)
            s_e = (pr, pi)
            o_e = _cmul(nr, ni, *a_tc)
            keep = row >= col
        xr, xi = _stack_rows(*x_e, bbr, bbi)
        yr, yi = _stack_rows(*y_e, cr, ci)
        full = _split_dot_t(jnp.concatenate([xr, -xi], axis=1), jnp.concatenate([yr, yi], axis=1))
        part = jnp.where(keep, full, 0.0)
        w_intra = part if w_intra is None else w_intra + part
        state_cols.append(_stack_rows(*s_e, bbr, bbi))
        o_r, o_i = _stack_rows(*o_e, cr, ci)
        out_cols += [o_r, -o_i]
        a_rows += [jnp.concatenate([a_tc[0], a_tc[0]], axis=1),
                   jnp.concatenate([-a_tc[1], a_tc[1]], axis=1),
                   jnp.concatenate([a_tc[1], -a_tc[1]], axis=1)]
    wi_ref[...] = w_intra.astype(BF16)
    (fr_, fi_), (br_, bi_) = state_cols
    ws_ref[...] = jnp.concatenate([fr_, fi_, br_, bi_, fi_, fr_, bi_, br_], axis=1).astype(BF16)
    wo_ref[...] = jnp.concatenate(out_cols, axis=1).astype(BF16)
    zero = jnp.zeros((1, LANES), F32)
    av_ref[...] = jnp.concatenate(a_rows + [zero, zero], axis=0)


def _s5_weights_call(lam_re, lam_im, log_dt, b_re, b_im, c_re, c_im):
    g, p = S5_GROUPS, S5_STATE
    bc = lambda v: jnp.broadcast_to(v.astype(F32)[:, None], (g, p))
    zero = jnp.zeros((g, p), F32)
    lam = jnp.stack([lam_re[0], lam_im[0], lam_re[1], lam_im[1], bc(log_dt[0]), bc(log_dt[1]),
                     zero, zero], axis=1).astype(F32)
    bt = jnp.concatenate([jnp.swapaxes(b_re, 1, 2), jnp.swapaxes(b_im, 1, 2)], axis=1).astype(F32)
    cc = jnp.concatenate([c_re, c_im], axis=1).astype(F32)
    n = S5_FLAT
    return pl.pallas_call(
        _s5_weights_body,
        grid=(g,),
        in_specs=[pl.BlockSpec((None, 8, p), lambda gi: (gi, 0, 0)),
                  pl.BlockSpec((None, 32, p), lambda gi: (gi, 0, 0)),
                  pl.BlockSpec((None, 32, p), lambda gi: (gi, 0, 0))],
        out_specs=[pl.BlockSpec((None, n, n), lambda gi: (gi, 0, 0)),
                   pl.BlockSpec((None, n, 4 * LANES), lambda gi: (gi, 0, 0)),
                   pl.BlockSpec((None, n, 2 * LANES), lambda gi: (gi, 0, 0)),
                   pl.BlockSpec((None, 8, LANES), lambda gi: (gi, 0, 0))],
        out_shape=[jax.ShapeDtypeStruct((g, n, n), BF16),
                   jax.ShapeDtypeStruct((g, n, 4 * LANES), BF16),
                   jax.ShapeDtypeStruct((g, n, 2 * LANES), BF16),
                   jax.ShapeDtypeStruct((g, 8, LANES), F32)],
        compiler_params=_params(1),
        name="s5_weights",
    )(lam, bt, cc)


def _s5_body(u_ref, wi_ref, ws_ref, wo_ref, av_ref, y_ref, s_ref, h_ref, *, n_ctx_chunks, n_chunks):
    u = u_ref[...]
    s_ref[...] = jnp.dot(u, ws_ref[...], preferred_element_type=F32)
    av = av_ref[...]
    nj = n_chunks
    rows = u_ref.shape[0] // nj
    a1f, a2f, a2sf, a1b, a2b, a2sb = [jnp.broadcast_to(av[i:i + 1], (rows, LANES)) for i in range(6)]
    hf = hfs = hb = hbs = jnp.zeros((rows, LANES), F32)
    for i in range(nj):
        jf = i * rows
        jb = (n_ctx_chunks - 1 - i if i < n_ctx_chunks else nj - 1 - (i - n_ctx_chunks)) * rows
        h_ref[jf:jf + rows, 0:LANES] = hf
        h_ref[jb:jb + rows, LANES:2 * LANES] = hb
        sf = s_ref[jf:jf + rows, 0:LANES]
        sfs = s_ref[jf:jf + rows, 2 * LANES:3 * LANES]
        sb = s_ref[jb:jb + rows, LANES:2 * LANES]
        sbs = s_ref[jb:jb + rows, 3 * LANES:4 * LANES]
        hf, hfs = a1f * hf + a2f * hfs + sf, a1f * hfs + a2sf * hf + sfs
        hb, hbs = a1b * hb + a2b * hbs + sb, a1b * hbs + a2sb * hb + sbs
    y = (jnp.dot(u, wi_ref[...], preferred_element_type=F32)
         + lax.dot_general(h_ref[...].astype(BF16), wo_ref[...], (((1,), (1,)), ((), ())),
                           preferred_element_type=F32))
    y_ref[...] = y.astype(y_ref.dtype)


def _s5_call(ug, w_intra, w_state, w_out, avec, n_ctx_chunks, n_chunks):
    g, r, w = ug.shape
    return pl.pallas_call(
        functools.partial(_s5_body, n_ctx_chunks=n_ctx_chunks, n_chunks=n_chunks),
        grid=(g,),
        in_specs=[
            pl.BlockSpec((None, r, w), lambda gi: (gi, 0, 0)),
            pl.BlockSpec((None, w, w), lambda gi: (gi, 0, 0)),
            pl.BlockSpec((None, w, 4 * LANES), lambda gi: (gi, 0, 0)),
            pl.BlockSpec((None, w, 2 * LANES), lambda gi: (gi, 0, 0)),
            pl.BlockSpec((None, 8, LANES), lambda gi: (gi, 0, 0)),
        ],
        out_specs=pl.BlockSpec((None, r, w), lambda gi: (gi, 0, 0)),
        out_shape=jax.ShapeDtypeStruct((g, r, w), BF16),
        scratch_shapes=[pltpu.VMEM((r, 4 * LANES), F32), pltpu.VMEM((r, 2 * LANES), F32)],
        compiler_params=_params(1),
        name="s5_scan",
    )(ug, w_intra, w_state, w_out, avec)


def _s5_to_groups(u):
    b, t, _ = u.shape
    nj = t // S5_CHUNK
    x = jnp.swapaxes(u.astype(BF16).reshape(b, nj, S5_CHUNK, S5_WIDTH), 2, 3)
    x = lax.optimization_barrier(x.reshape(b, nj, S5_GROUPS, S5_FLAT))
    return jnp.transpose(x, (2, 1, 0, 3)).reshape(S5_GROUPS, nj * b, S5_FLAT)


def _s5_from_groups(y, b):
    g, r, n = y.shape
    nj = r // b
    x = jnp.transpose(y.reshape(g, nj, b, n), (2, 1, 0, 3))
    x = lax.optimization_barrier(x).reshape(b, nj, S5_WIDTH, S5_CHUNK)
    return jnp.swapaxes(x, 2, 3).reshape(b, nj * S5_CHUNK, S5_WIDTH)


def _merge_body(h_ref, mod_ref, xn_ref, ya_ref, yb_ref, yc_ref, u_ref, y5_ref,
                wg_ref, wa_ref, wb_ref, wc_ref, wd_ref, glw_ref, glb_ref, s5d_ref, wo_ref, o_ref):
    xn = xn_ref[...]
    y5 = y5_ref[...].astype(F32) + s5d_ref[...] * u_ref[...].astype(F32)
    gelu = 0.5 * y5 * (1.0 + jnp.tanh(0.7978845608028654 * (y5 + 0.044715 * y5 * y5 * y5)))
    glu = _bdot(gelu, glw_ref[...]) + glb_ref[...]
    yd = glu[:, 0:S5_WIDTH] * _sigmoid(glu[:, S5_WIDTH:2 * S5_WIDTH])
    branches = ((ya_ref[...], wa_ref), (yb_ref[...], wb_ref), (yc_ref[...], wc_ref),
                (yd.astype(BF16), wd_ref))
    merged = None
    for i, (y, w_ref) in enumerate(branches):
        gate = _sigmoid(jnp.dot(xn, wg_ref[i], preferred_element_type=F32))
        term = gate * jnp.dot(y, w_ref[...], preferred_element_type=F32)
        merged = term if merged is None else merged + term
    out = _bdot(merged, wo_ref[...])
    o_ref[...] = h_ref[...] + mod_ref[2:3, :] * out


def _merge_call(h, modv, xn, ya, yb, yc, u, y5, wts, n_ctx_tiles):
    b, t, d = h.shape
    tm = ROW_TILE

    def row(width):
        return pl.BlockSpec((None, tm, width), lambda bi, ti: (bi, ti, 0))

    def const(shape):
        return pl.BlockSpec(shape, lambda bi, ti: (0,) * len(shape), pipeline_mode=pl.Buffered(1))

    w_gate, w_a, w_b, w_c, w_d, glu_w, glu_b, s5_d, w_out = wts
    return pl.pallas_call(
        _merge_body,
        grid=(b, t // tm),
        in_specs=[row(d),
                  pl.BlockSpec((None, None, 6, d),
                               lambda bi, ti: (bi, jnp.where(ti >= n_ctx_tiles, 1, 0), 0, 0)),
                  row(d), row(512), row(512), row(512), row(S5_WIDTH), row(S5_WIDTH),
                  const((4, d, d)), const((512, d)), const((512, d)), const((512, d)),
                  const((S5_WIDTH, d)), const((S5_WIDTH, 2 * S5_WIDTH)), const((1, 2 * S5_WIDTH)),
                  const((1, S5_WIDTH)), const((d, d))],
        out_specs=row(d),
        out_shape=jax.ShapeDtypeStruct((b, t, d), F32),
        compiler_params=_params(2),
        name="branch_merge",
    )(h, modv, xn, ya, yb, yc, u, y5, w_gate, w_a, w_b, w_c, w_d, glu_w, glu_b, s5_d, w_out)


def _ffn_body(h_ref, mod_ref, g_ref, wgu_ref, wd_ref, o_ref):
    h = h_ref[...]
    mod = mod_ref[...]
    xf = (_rms(h, g_ref[...]) * (1.0 + mod[4:5]) + mod[3:4]).astype(BF16)
    gate = jnp.dot(xf, wgu_ref[:, 0:FFN_HIDDEN], preferred_element_type=F32)
    up = jnp.dot(xf, wgu_ref[:, FFN_HIDDEN:2 * FFN_HIDDEN], preferred_element_type=F32)
    act = (_silu(gate) * up).astype(BF16)
    o_ref[...] = h + mod[5:6] * jnp.dot(act, wd_ref[...], preferred_element_type=F32)


def _ffn_call(h, modv, g2, w_gu, w_down, n_ctx_tiles, latent_only):
    b, t, d = h.shape
    tm = ROW_TILE
    skip = n_ctx_tiles if latent_only else 0

    def const(shape):
        return pl.BlockSpec(shape, lambda bi, ti: (0,) * len(shape), pipeline_mode=pl.Buffered(1))

    return pl.pallas_call(
        _ffn_body,
        grid=(b, t // tm - skip),
        in_specs=[pl.BlockSpec((None, tm, d), lambda bi, ti: (bi, ti + skip, 0)),
                  pl.BlockSpec((None, None, 6, d),
                               lambda bi, ti: (bi, jnp.where(ti + skip >= n_ctx_tiles, 1, 0), 0, 0)),
                  const((1, d)), const((d, 2 * FFN_HIDDEN)), const((FFN_HIDDEN, d))],
        out_specs=pl.BlockSpec((None, tm, d), lambda bi, ti: (bi, ti, 0)),
        out_shape=jax.ShapeDtypeStruct((b, t - skip * tm, d), F32),
        compiler_params=_params(2),
        name="swiglu_ffn",
    )(h, modv, g2, w_gu, w_down)


def _rope_tables(ctx_len, seq_len):
    n_rows = seq_len // GRID_W
    rows = jnp.repeat(jnp.arange(n_rows, dtype=F32), GRID_W)
    cols = jnp.tile(jnp.arange(GRID_W, dtype=F32), n_rows)
    quarter = HEAD_DIM // 4
    inv_freq = ROPE_THETA ** (-jnp.arange(quarter, dtype=F32) / quarter)
    ang_r = rows[:, None] * inv_freq
    ang_c = cols[:, None] * inv_freq
    ang = jnp.concatenate([ang_r, ang_r, ang_c, ang_c], axis=-1)
    cos = jnp.concatenate([jnp.ones((ctx_len, HEAD_DIM), F32), jnp.cos(ang)], axis=0)
    sin = jnp.concatenate([jnp.zeros((ctx_len, HEAD_DIM), F32), jnp.sin(ang)], axis=0)
    first = (jnp.arange(HEAD_DIM) % 32) < 16
    sin_a = jnp.where(first, -sin, 0.0)
    sin_b = jnp.where(first, 0.0, sin)
    two = lambda m: jnp.concatenate([m, m], axis=1)
    return jnp.concatenate([two(cos), two(sin_a), two(sin_b)], axis=1)


def _w_in_layout(w_in):
    d = w_in.shape[0]
    a0 = 0
    z = w_in[:, a0:a0 + 512]
    xbc = w_in[:, a0 + 512:a0 + 1280]
    dt = jnp.pad(w_in[:, a0 + 1280:a0 + 1296], ((0, 0), (0, SSD_DT_PAD - 16)))
    rest = w_in[:, 1296:]
    out = jnp.concatenate([z, xbc, dt, rest], axis=1).astype(BF16)
    assert out.shape == (d, _C_END)
    return out


def kernel(x, c, ctx, c_ctx, w_mod, b_mod, norm1_g, norm2_g, w_in, ssd_conv_w, ssd_conv_b, ssd_a_log, ssd_dt_bias, ssd_d, ssd_norm_g, diff_qn_g, diff_kn_g, diff_lam_q1, diff_lam_k1, diff_lam_q2, diff_lam_k2, diff_subln_g, gqa_qn_g, gqa_kn_g, s5_lam_re, s5_lam_im, s5_log_dt, s5_b_re, s5_b_im, s5_c_re, s5_c_im, s5_d, s5_glu_w, s5_glu_b, w_gate, w_br_ssd, w_br_diff, w_br_gqa, w_br_s5, w_out, ffn_w_gate_up, ffn_w_down):
    b, seq_len, d = x.shape
    ctx_len = ctx.shape[1]
    depth = w_mod.shape[0]
    assert b + 1 <= 8
    assert ctx_len % ROW_TILE == 0 and seq_len % ROW_TILE == 0
    n_ctx_tiles = ctx_len // ROW_TILE

    h = jnp.concatenate([ctx, x], axis=1)
    cc = jnp.concatenate([c, c_ctx[None], jnp.zeros((8 - b - 1, d), F32)], axis=0)
    mods = _mod_call(cc, w_mod, b_mod)

    rope_tab = _rope_tables(ctx_len, seq_len)
    blk = jnp.arange(512) // HEAD_DIM
    ones = (blk[:, None] == blk[None, :]).astype(BF16)
    tile8 = lambda g: jnp.tile(g.astype(F32), 512 // HEAD_DIM)

    for layer in range(depth):
        m = mods[layer]
        lat = m[:b].reshape(b, 6, d)
        cmod = jnp.broadcast_to(m[b].reshape(1, 6, d), (b, 6, d))
        modv = jnp.stack([cmod, lat], axis=1)

        gains = jnp.stack([tile8(diff_qn_g[layer]), tile8(diff_kn_g[layer]),
                           tile8(gqa_qn_g[layer]), tile8(gqa_kn_g[layer])]
                          + [jnp.zeros((512,), F32)] * 4, axis=0)
        (xn, z, xbc, dt, qd, kd, vd, qg, kg, vg, u) = _in_proj_call(
            h, modv, norm1_g[layer][None], _w_in_layout(w_in[layer]), ones, gains, rope_tab,
            n_ctx_tiles)

        conv_w = jnp.pad(ssd_conv_w[layer].astype(F32), ((0, 8 - SSD_CONV_W), (0, 0)))
        xbc_act = _conv_call(xbc, conv_w, ssd_conv_b[layer][None].astype(F32), n_ctx_tiles)
        pad16 = lambda v: jnp.pad(v.reshape(1, 16).astype(F32), ((0, 0), (0, LANES - 16)))
        aneg = pad16(-jnp.exp(ssd_a_log[layer].astype(F32)))
        dtb = pad16(ssd_dt_bias[layer])
        dsk = jnp.repeat(ssd_d[layer].astype(F32), SSD_HEAD_DIM)[None]
        ya = _ssd_call(xbc_act, dt, z, aneg, dtb, dsk, ssd_norm_g[layer][None].astype(F32),
                       ctx_len // SSD_CHUNK)

        lam_init = 0.8 - 0.6 * math.exp(-0.3 * layer)
        lamv = jnp.pad(jnp.stack([diff_lam_q1[layer], diff_lam_k1[layer],
                                  diff_lam_q2[layer], diff_lam_k2[layer]]).astype(F32),
                       ((0, 0), (0, LANES - HEAD_DIM)))
        yb = _diff_attn_call(qd, kd, vd, lamv, diff_subln_g[layer][None].astype(F32), lam_init, ctx_len)

        yc = _gqa_attn_call(qg, kg, vg, ctx_len)

        s5w = _s5_weights_call(s5_lam_re[layer], s5_lam_im[layer], s5_log_dt[layer],
                               s5_b_re[layer], s5_b_im[layer], s5_c_re[layer], s5_c_im[layer])
        y5 = _s5_from_groups(_s5_call(_s5_to_groups(u), *s5w, ctx_len // S5_CHUNK,
                                      (ctx_len + seq_len) // S5_CHUNK), b)

        wts = (w_gate[layer].astype(BF16), w_br_ssd[layer].astype(BF16), w_br_diff[layer].astype(BF16),
               w_br_gqa[layer].astype(BF16), w_br_s5[layer].astype(BF16), s5_glu_w[layer].astype(BF16),
               s5_glu_b[layer][None].astype(F32), s5_d[layer][None].astype(F32), w_out[layer].astype(BF16))
        h = _merge_call(h, modv, xn, ya, yb, yc, u, y5, wts, n_ctx_tiles)
        h = _ffn_call(h, modv, norm2_g[layer][None].astype(F32), ffn_w_gate_up[layer].astype(BF16),
                      ffn_w_down[layer].astype(BF16), n_ctx_tiles, latent_only=layer == depth - 1)
    return h
```

```python
import functools
import math

import jax
import jax.numpy as jnp
from jax import lax
from jax.experimental import pallas as pl
from jax.experimental.pallas import tpu as pltpu

F32 = jnp.float32
BF16 = jnp.bfloat16
HIGHEST = lax.Precision.HIGHEST

D_MODEL = 1024
GRID_W = 64
ROPE_THETA = 10000.0
NORM_EPS = 1e-6

SSD_INNER = 512
SSD_HEADS = 8
SSD_HEAD_DIM = 64
SSD_STATE = 64
SSD_CHUNK = 128
SSD_CONV_W = 5
SSD_CONV_CH = 768
SSD_DT_PAD = 128
SSD_BATCH_PER_STEP = 4

DIFF_HEADS = 4
HEAD_DIM = 64
GQA_HEADS = 8
GQA_KV_HEADS = 2

S5_GROUP_CH = 16
S5_STATE = 64
S5_WIDTH = 384
S5_GROUPS = 24
S5_CHUNK = 64
S5_FLAT = S5_CHUNK * S5_GROUP_CH

FFN_HIDDEN = 2816

ROW_TILE = 256
LANES = 128
VMEM_LIMIT = 56 * 1024 * 1024

_C_Z, _C_XBC, _C_DT = 0, 512, 1280
_C_QD, _C_KD, _C_VD = 1408, 1920, 2432
_C_QG, _C_KG, _C_VG = 2944, 3456, 3584
_C_U, _C_END = 3712, 4096


def _params(n_grid):
    return pltpu.CompilerParams(dimension_semantics=("arbitrary",) * n_grid,
                                vmem_limit_bytes=VMEM_LIMIT)


def _bdot(a, b):
    return jnp.dot(a.astype(BF16), b.astype(BF16), preferred_element_type=F32)


def _hdot(a, b):
    return jnp.dot(a, b, precision=HIGHEST, preferred_element_type=F32)


def _rms(x, g):
    return x * lax.rsqrt(jnp.mean(x * x, axis=-1, keepdims=True) + NORM_EPS) * g


def _sigmoid(x):
    return 1.0 / (1.0 + jnp.exp(-x))


def _silu(x):
    return x * _sigmoid(x)


def _mod_body(c_ref, w_ref, b_ref, o_ref):
    o_ref[...] = _hdot(_silu(c_ref[...]), w_ref[...]) + b_ref[...]


def _mod_call(cc, w_mod, b_mod):
    depth = w_mod.shape[0]
    d = D_MODEL
    return pl.pallas_call(
        _mod_body,
        grid=(depth, 6),
        in_specs=[pl.BlockSpec((8, d), lambda l, j: (0, 0)),
                  pl.BlockSpec((None, d, d), lambda l, j: (l, 0, j)),
                  pl.BlockSpec((None, 1, d), lambda l, j: (l, 0, j))],
        out_specs=pl.BlockSpec((None, 8, d), lambda l, j: (l, 0, j)),
        out_shape=jax.ShapeDtypeStruct((depth, 8, 6 * d), F32),
        compiler_params=_params(2),
        name="adaln_mod",
    )(cc, w_mod, b_mod.reshape(depth, 1, 6 * d))


def _head_rms(t, gain, ones):
    ss = jnp.dot((t * t).astype(BF16), ones, preferred_element_type=F32)
    return t * lax.rsqrt(ss * (1.0 / HEAD_DIM) + NORM_EPS) * gain


def _rope(t, cos, sin_a, sin_b):
    w = t.shape[-1]
    return t * cos + pltpu.roll(t, w - 16, 1) * sin_a + pltpu.roll(t, 16, 1) * sin_b


def _in_proj_body(h_ref, mod_ref, g_ref, w_ref, ones_ref, gains_ref, rope_ref,
                  xn_ref, z_ref, xbc_ref, dt_ref, qd_ref, kd_ref, vd_ref,
                  qg_ref, kg_ref, vg_ref, u_ref):
    x = h_ref[...]
    mod = mod_ref[...]
    xn = _rms(x, g_ref[...]) * (1.0 + mod[1:2]) + mod[0:1]
    xb = xn.astype(BF16)
    xn_ref[...] = xb

    def proj(lo, hi):
        return jnp.dot(xb, w_ref[:, lo:hi], preferred_element_type=F32)

    z_ref[...] = proj(_C_Z, _C_XBC)
    xbc_ref[...] = proj(_C_XBC, _C_DT)
    dt_ref[...] = proj(_C_DT, _C_QD)
    u_ref[...] = proj(_C_U, _C_END).astype(u_ref.dtype)

    rope = rope_ref[...]
    cos1, sa1, sb1 = rope[:, 0:128], rope[:, 128:256], rope[:, 256:384]
    cos4 = jnp.concatenate([cos1] * 4, axis=1)
    sa4 = jnp.concatenate([sa1] * 4, axis=1)
    sb4 = jnp.concatenate([sb1] * 4, axis=1)
    ones = ones_ref[...]
    gains = gains_ref[...]
    lane = lax.broadcasted_iota(jnp.int32, (1, LANES), 1)
    lo_half = lane < HEAD_DIM
    scale = HEAD_DIM ** -0.5 * math.log2(math.e)

    qd = _rope(_head_rms(proj(_C_QD, _C_KD), gains[0:1], ones), cos4, sa4, sb4) * scale
    kd = _rope(_head_rms(proj(_C_KD, _C_VD), gains[1:2], ones), cos4, sa4, sb4)
    kd_ref[...] = kd.astype(BF16)
    vd_ref[...] = proj(_C_VD, _C_QG).astype(BF16)
    for h in range(DIFF_HEADS):
        blk = qd[:, h * LANES:(h + 1) * LANES]
        qd_ref[h, 0] = jnp.where(lo_half, blk, 0.0).astype(BF16)
        qd_ref[h, 1] = jnp.where(lo_half, 0.0, blk).astype(BF16)

    qg = _rope(_head_rms(proj(_C_QG, _C_KG), gains[2:3], ones), cos4, sa4, sb4) * scale
    kg = _rope(_head_rms(proj(_C_KG, _C_VG), gains[3:4, :LANES], ones[:LANES, :LANES]),
               cos1, sa1, sb1)
    kg_ref[...] = kg.astype(BF16)
    vg_ref[...] = proj(_C_VG, _C_U).astype(BF16)
    qg_up = pltpu.roll(qg, 4 * LANES - HEAD_DIM, 1)
    qg_dn = pltpu.roll(qg, HEAD_DIM, 1)
    per_kv = GQA_HEADS // GQA_KV_HEADS
    for n in range(GQA_KV_HEADS):
        for i in range(per_kv):
            j = n * per_kv + i
            blk_idx = j // 2
            if (j % 2) == n:
                src = qg
            else:
                src = qg_up if n == 0 else qg_dn
            blk = src[:, blk_idx * LANES:(blk_idx + 1) * LANES]
            keep = lo_half if n == 0 else jnp.logical_not(lo_half)
            qg_ref[n, i] = jnp.where(keep, blk, 0.0).astype(BF16)


def _in_proj_call(h, modv, g1, w_cat, ones, gains, rope_tab, n_ctx_tiles):
    b, t, d = h.shape
    nt = t // ROW_TILE
    tm = ROW_TILE

    def row(width, dtype):
        return (pl.BlockSpec((None, tm, width), lambda bi, ti: (bi, ti, 0)),
                jax.ShapeDtypeStruct((b, t, width), dtype))

    outs = [row(d, BF16), row(512, F32), row(768, F32), row(SSD_DT_PAD, F32)]
    qd = (pl.BlockSpec((None, DIFF_HEADS, 2, tm, LANES), lambda bi, ti: (bi, 0, 0, ti, 0)),
          jax.ShapeDtypeStruct((b, DIFF_HEADS, 2, t, LANES), BF16))
    qg = (pl.BlockSpec((None, GQA_KV_HEADS, 4, tm, LANES), lambda bi, ti: (bi, 0, 0, ti, 0)),
          jax.ShapeDtypeStruct((b, GQA_KV_HEADS, 4, t, LANES), BF16))
    outs += [qd, row(512, BF16), row(512, BF16), qg, row(LANES, BF16), row(LANES, BF16),
             row(S5_WIDTH, BF16)]
    return pl.pallas_call(
        _in_proj_body,
        grid=(b, nt),
        in_specs=[
            pl.BlockSpec((None, tm, d), lambda bi, ti: (bi, ti, 0)),
            pl.BlockSpec((None, None, 6, d),
                         lambda bi, ti: (bi, jnp.where(ti >= n_ctx_tiles, 1, 0), 0, 0)),
            pl.BlockSpec((1, d), lambda bi, ti: (0, 0)),
            pl.BlockSpec((d, _C_END), lambda bi, ti: (0, 0)),
            pl.BlockSpec((512, 512), lambda bi, ti: (0, 0)),
            pl.BlockSpec((8, 512), lambda bi, ti: (0, 0)),
            pl.BlockSpec((tm, 384), lambda bi, ti: (ti, 0)),
        ],
        out_specs=[o[0] for o in outs],
        out_shape=[o[1] for o in outs],
        compiler_params=_params(2),
        name="in_proj",
    )(h, modv, g1, w_cat, ones, gains, rope_tab)


def _conv_body(cur_ref, prev_ref, next_ref, w_ref, b_ref, o_ref, buf_ref, *, n_ctx_tiles, n_tiles):
    t = pl.program_id(1)
    tm = ROW_TILE
    has_prev = jnp.logical_and(t != 0, t != n_ctx_tiles)
    has_next = jnp.logical_and(t != n_ctx_tiles - 1, t != n_tiles - 1)
    buf_ref[0:8, :] = jnp.where(has_prev, prev_ref[...], 0.0)
    buf_ref[8:8 + tm, :] = cur_ref[...]
    buf_ref[8 + tm:16 + tm, :] = jnp.where(has_next, next_ref[...], 0.0)
    w = w_ref[...]
    bias = b_ref[...]
    pad = (SSD_CONV_W - 1) // 2
    full = buf_ref[...]
    n = tm + 16
    acc = bias + w[pad:pad + 1] * full[8:8 + tm]
    for k in range(SSD_CONV_W):
        if k != pad:
            acc = acc + w[k:k + 1] * pltpu.roll(full, (pad - k) % n, 0)[8:8 + tm]
    o_ref[...] = _silu(acc)


def _conv_call(xbc, conv_w, conv_b, n_ctx_tiles):
    b, t, ch = xbc.shape
    tm = ROW_TILE
    nt = t // tm
    per = tm // 8
    last8 = t // 8 - 1
    return pl.pallas_call(
        functools.partial(_conv_body, n_ctx_tiles=n_ctx_tiles, n_tiles=nt),
        grid=(b, nt),
        in_specs=[
            pl.BlockSpec((None, tm, ch), lambda bi, ti: (bi, ti, 0)),
            pl.BlockSpec((None, 8, ch), lambda bi, ti: (bi, jnp.maximum(ti * per - 1, 0), 0)),
            pl.BlockSpec((None, 8, ch), lambda bi, ti: (bi, jnp.minimum((ti + 1) * per, last8), 0)),
            pl.BlockSpec((8, ch), lambda bi, ti: (0, 0)),
            pl.BlockSpec((1, ch), lambda bi, ti: (0, 0)),
        ],
        out_specs=pl.BlockSpec((None, tm, ch), lambda bi, ti: (bi, ti, 0)),
        out_shape=jax.ShapeDtypeStruct((b, t, ch), F32),
        scratch_shapes=[pltpu.VMEM((tm + 16, ch), F32)],
        compiler_params=_params(2),
        name="ssd_conv",
    )(xbc, xbc, xbc, conv_w, conv_b)


def _ssd_chunk_index(d, s, n_ctx_chunks, n_chunks):
    bwd = jnp.where(s < n_ctx_chunks, n_ctx_chunks - 1 - s, n_chunks - 1 - (s - n_ctx_chunks))
    return jnp.where(d == 0, s, bwd)


def _ssd_body(xbc_ref, dt_ref, z_ref, aneg_ref, dtb_ref, dsk_ref, ng_ref, o_ref,
              hst_ref, ysc_ref, *, n_ctx_chunks, n_chunks):
    d = pl.program_id(1)
    s = pl.program_id(2)
    c = _ssd_chunk_index(d, s, n_ctx_chunks, n_chunks)
    q = SSD_CHUNK
    nb = xbc_ref.shape[0]
    r = nb * q

    @pl.when(s == 0)
    def _():
        hst_ref[...] = jnp.zeros_like(hst_ref)

    def run(direction):
        xbc = xbc_ref[...].reshape(r, SSD_CONV_CH)
        x = xbc[:, 0:512]
        bm = xbc[:, 512:640]
        cm = xbc[:, 640:768]
        pre = dt_ref[...].reshape(r, SSD_DT_PAD) + dtb_ref[...]
        dtv = jnp.maximum(pre, 0.0) + jnp.log(1.0 + jnp.exp(-jnp.abs(pre)))
        a = dtv * aneg_ref[...]
        row = lax.broadcasted_iota(jnp.int32, (r, r), 0)
        col = lax.broadcasted_iota(jnp.int32, (r, r), 1)
        order = (row >= col) if direction == 0 else (row <= col)
        tri = jnp.logical_and(row // q == col // q, order)
        a_hi = a.astype(BF16)
        r1 = a - a_hi.astype(F32)
        a_mid = r1.astype(BF16)
        a_lo = (r1 - a_mid.astype(F32)).astype(BF16)
        terms = [t[bi * q:(bi + 1) * q] for bi in range(nb) for t in (a_hi, a_mid, a_lo)]
        parts = jnp.dot(tri[0:q, 0:q].astype(BF16), jnp.concatenate(terms, axis=1),
                        preferred_element_type=F32)
        acs = jnp.concatenate(
            [parts[:, (3 * bi) * LANES:(3 * bi + 1) * LANES] + parts[:, (3 * bi + 1) * LANES:(3 * bi + 2) * LANES]
             + parts[:, (3 * bi + 2) * LANES:(3 * bi + 3) * LANES] for bi in range(nb)], axis=0)
        acs_t = acs.T
        edge = q - 1 if direction == 0 else 0
        lasts = [acs[bi * q + edge:bi * q + edge + 1, :] for bi in range(nb)]
        last = jnp.concatenate([jnp.broadcast_to(v, (q, LANES)) for v in lasts], axis=0)
        dte = jnp.exp(last - acs)
        expa = jnp.exp(acs)
        cdec = jnp.concatenate([jnp.broadcast_to(jnp.exp(v), (8, LANES)) for v in lasts], axis=0)
        erow = lax.broadcasted_iota(jnp.int32, (LANES, SSD_INNER), 0)
        ecol = lax.broadcasted_iota(jnp.int32, (LANES, SSD_INNER), 1)
        expand = (erow == ecol // SSD_HEAD_DIM + SSD_HEADS * direction).astype(BF16)
        spread = jnp.dot(jnp.concatenate([dtv, dtv * dte], axis=0).astype(BF16), expand,
                         preferred_element_type=F32)
        xdt = x * spread[0:r]
        xs = x * spread[r:2 * r]
        fine = jnp.concatenate([expa, cdec], axis=0)
        fine_hi = fine.astype(BF16)
        fine_lo = (fine - fine_hi.astype(F32)).astype(BF16)
        spread2 = (jnp.dot(fine_hi, expand, preferred_element_type=F32)
                   + jnp.dot(fine_lo, expand, preferred_element_type=F32))
        expa_x = spread2[0:r]
        bt = bm.T.astype(BF16)
        lane = lax.broadcasted_iota(jnp.int32, (1, LANES), 1)
        tri_q = tri[0:q, 0:q]
        ys = [[None] * nb for _ in range(4)]
        for g in range(2):
            cg = jnp.where(lane // SSD_STATE == g, cm, 0.0).astype(BF16)
            gmats = [jnp.dot(cg[bi * q:(bi + 1) * q], bt[:, bi * q:(bi + 1) * q], preferred_element_type=F32)
                     for bi in range(nb)]
            for hp in range(2):
                p = g * 2 + hp
                sl = slice(p * LANES, (p + 1) * LANES)
                for bi in range(nb):
                    rows = slice(bi * q, (bi + 1) * q)
                    ypair = None
                    for e in range(2):
                        j = SSD_HEADS * direction + 2 * p + e
                        seg = acs[rows, j:j + 1] - acs_t[j:j + 1, rows]
                        dec = jnp.exp(jnp.where(tri_q, seg, -jnp.inf))
                        xh = jnp.where(lane // SSD_HEAD_DIM == e, xdt[rows, sl], 0.0)
                        term = _bdot(gmats[bi] * dec, xh)
                        ypair = term if ypair is None else ypair + term
                    hprev = hst_ref[bi, p]
                    ypair = ypair + _bdot(cg[rows], hprev) * expa_x[rows, sl]
                    hst_ref[bi, p] = hprev * spread2[r + 8 * bi:r + 8 * bi + 1, sl] + jnp.dot(
                        bt[:, rows], xs[rows, sl].astype(BF16), preferred_element_type=F32)
                    ys[p][bi] = ypair
        ys = [jnp.concatenate(v, axis=0) for v in ys]
        return x, ys

    @pl.when(d == 0)
    def _():
        _, ys = run(0)
        for p in range(4):
            ysc_ref[c, :, p * LANES:(p + 1) * LANES] = ys[p].astype(ysc_ref.dtype)

    @pl.when(d == 1)
    def _():
        x, ys = run(1)
        y = jnp.concatenate(ys, axis=1) + ysc_ref[c].astype(F32) + dsk_ref[...] * x
        y = y * _silu(z_ref[...].reshape(r, SSD_INNER))
        o_ref[...] = _rms(y, ng_ref[...]).astype(o_ref.dtype).reshape(nb, q, SSD_INNER)


def _ssd_call(xbc_act, dt, z, aneg, dtb, dsk, ng, n_ctx_chunks):
    b, t, _ = xbc_act.shape
    q = SSD_CHUNK
    nc = t // q
    nb = SSD_BATCH_PER_STEP if b % SSD_BATCH_PER_STEP == 0 else 1
    cidx = functools.partial(_ssd_chunk_index, n_ctx_chunks=n_ctx_chunks, n_chunks=nc)

    def late(bi, di, si):
        return (bi, jnp.where(di == 0, n_ctx_chunks - 1, cidx(di, si)), 0)

    return pl.pallas_call(
        functools.partial(_ssd_body, n_ctx_chunks=n_ctx_chunks, n_chunks=nc),
        grid=(b // nb, 2, nc),
        in_specs=[
            pl.BlockSpec((nb, q, SSD_CONV_CH), lambda bi, di, si: (bi, cidx(di, si), 0)),
            pl.BlockSpec((nb, q, SSD_DT_PAD), lambda bi, di, si: (bi, cidx(di, si), 0)),
            pl.BlockSpec((nb, q, SSD_INNER), late),
            pl.BlockSpec((1, LANES), lambda bi, di, si: (0, 0)),
            pl.BlockSpec((1, LANES), lambda bi, di, si: (0, 0)),
            pl.BlockSpec((1, SSD_INNER), lambda bi, di, si: (0, 0)),
            pl.BlockSpec((1, SSD_INNER), lambda bi, di, si: (0, 0)),
        ],
        out_specs=pl.BlockSpec((nb, q, SSD_INNER), late),
        out_shape=jax.ShapeDtypeStruct((b, t, SSD_INNER), BF16),
        scratch_shapes=[pltpu.VMEM((nb, 4, LANES, LANES), F32),
                        pltpu.VMEM((nc, nb * q, SSD_INNER), BF16)],
        compiler_params=_params(3),
        name="ssd_scan",
    )(xbc_act, dt, z, aneg, dtb, dsk, ng)


def _flash(qs, k_ref, v_ref, m_ref, l_ref, acc_ref, *, tk, n_kv, ctx_len, is_ctx_tile):
    m_ref[...] = jnp.full(m_ref.shape, -jnp.inf, F32)
    l_ref[...] = jnp.zeros(l_ref.shape, F32)
    acc_ref[...] = jnp.zeros(acc_ref.shape, F32)

    def step(start, size, limit):
        for i, q in enumerate(qs):
            lanes = slice(i * LANES, (i + 1) * LANES)
            k = k_ref[start:start + size, lanes]
            v = v_ref[start:start + size, lanes]
            s = lax.dot_general(q, k, (((1,), (1,)), ((), ())), preferred_element_type=F32)
            if limit is not None:
                col = lax.broadcasted_iota(jnp.int32, (1, size), 1) + start
                s = jnp.where(col < limit, s, -jnp.inf)
            m_old = m_ref[i]
            m_new = jnp.maximum(m_old, jnp.max(s, axis=-1, keepdims=True))
            alpha = jnp.exp2(m_old - m_new)
            p = jnp.exp2(s - jnp.concatenate([m_new] * (size // LANES), axis=1))
            psum = p[:, 0:LANES]
            for j in range(1, size // LANES):
                psum = psum + p[:, j * LANES:(j + 1) * LANES]
            l_ref[i] = alpha * l_ref[i] + psum
            acc_ref[i] = alpha * acc_ref[i] + jnp.dot(p.astype(BF16), v, preferred_element_type=F32)
            m_ref[i] = m_new

    @pl.when(is_ctx_tile)
    def _():
        ctx_pad = -(-ctx_len // LANES) * LANES
        for start in range(0, ctx_pad, tk):
            step(start, min(tk, ctx_pad - start), ctx_len if ctx_pad != ctx_len else None)

    @pl.when(jnp.logical_not(is_ctx_tile))
    def _():
        for kc in range(n_kv):
            step(kc * tk, tk, None)

    return [acc_ref[i] * (1.0 / jnp.sum(l_ref[i], axis=-1, keepdims=True)) for i in range(len(qs))]


DIFF_HEADS_PER_STEP = 2


def _diff_attn_body(q_ref, k_ref, v_ref, lam_ref, g_ref, o_ref, m_ref, l_ref, acc_ref,
                    *, tq, lam_init, n_ctx_q, **kw):
    qi = pl.program_id(2)
    qs = [q_ref[i].reshape(2 * tq, LANES) for i in range(DIFF_HEADS_PER_STEP)]
    outs = _flash(qs, k_ref, v_ref, m_ref, l_ref, acc_ref, is_ctx_tile=qi < n_ctx_q, **kw)
    lv = lam_ref[...]
    lam = (jnp.exp(jnp.sum(lv[0:1] * lv[1:2], axis=-1, keepdims=True))
           - jnp.exp(jnp.sum(lv[2:3] * lv[3:4], axis=-1, keepdims=True)) + lam_init)
    for i, o in enumerate(outs):
        out = o[0:tq] - lam * o[tq:2 * tq]
        o_ref[:, i * LANES:(i + 1) * LANES] = (_rms(out, g_ref[...]) * (1.0 - lam_init)).astype(o_ref.dtype)


def _gqa_attn_body(q_ref, k_ref, v_ref, o_ref, m_ref, l_ref, acc_ref, *, tq, n_ctx_q, **kw):
    n = pl.program_id(1)
    qi = pl.program_id(2)
    q = q_ref[...].reshape(4 * tq, LANES)
    (o,) = _flash([q], k_ref, v_ref, m_ref, l_ref, acc_ref, is_ctx_tile=qi < n_ctx_q, **kw)
    lane = lax.broadcasted_iota(jnp.int32, (1, LANES), 1)
    lo_half = lane < HEAD_DIM
    for pair in range(2):
        a = o[(2 * pair) * tq:(2 * pair + 1) * tq]
        b = o[(2 * pair + 1) * tq:(2 * pair + 2) * tq]
        from_lo = jnp.where(lo_half, a, pltpu.roll(b, HEAD_DIM, 1))
        from_hi = jnp.where(lo_half, pltpu.roll(a, HEAD_DIM, 1), b)
        o_ref[:, pair * LANES:(pair + 1) * LANES] = jnp.where(n == 0, from_lo, from_hi).astype(o_ref.dtype)


def _attn_tiles(t, ctx_len):
    tq = ROW_TILE
    tk = next(c for c in (2816, 768, 256) if t % c == 0)
    assert ctx_len % tq == 0 and t % tq == 0 and t % tk == 0
    return dict(tq=tq, tk=tk, n_kv=t // tk, ctx_len=ctx_len, n_ctx_q=ctx_len // tq)


def _diff_attn_call(qd, kd, vd, lamv, subln_g, lam_init, ctx_len):
    b, _, _, t, _ = qd.shape
    cfg = _attn_tiles(t, ctx_len)
    tq = cfg["tq"]
    r = 2 * tq
    hs = DIFF_HEADS_PER_STEP
    return pl.pallas_call(
        functools.partial(_diff_attn_body, lam_init=lam_init, **cfg),
        grid=(b, DIFF_HEADS // hs, t // tq),
        in_specs=[
            pl.BlockSpec((None, hs, 2, tq, LANES), lambda bi, hi, qi: (bi, hi, 0, qi, 0)),
            pl.BlockSpec((None, t, hs * LANES), lambda bi, hi, qi: (bi, 0, hi)),
            pl.BlockSpec((None, t, hs * LANES), lambda bi, hi, qi: (bi, 0, hi)),
            pl.BlockSpec((4, LANES), lambda bi, hi, qi: (0, 0)),
            pl.BlockSpec((1, LANES), lambda bi, hi, qi: (0, 0)),
        ],
        out_specs=pl.BlockSpec((None, tq, hs * LANES), lambda bi, hi, qi: (bi, qi, hi)),
        out_shape=jax.ShapeDtypeStruct((b, t, DIFF_HEADS * LANES), BF16),
        scratch_shapes=[pltpu.VMEM((hs, r, LANES), F32)] * 3,
        compiler_params=_params(3),
        name="diff_attention",
    )(qd, kd, vd, lamv, subln_g)


def _gqa_attn_call(qg, kg, vg, ctx_len):
    b, _, _, t, _ = qg.shape
    cfg = _attn_tiles(t, ctx_len)
    tq = cfg["tq"]
    r = 4 * tq
    return pl.pallas_call(
        functools.partial(_gqa_attn_body, **cfg),
        grid=(b, GQA_KV_HEADS, t // tq),
        in_specs=[
            pl.BlockSpec((None, None, 4, tq, LANES), lambda bi, ni, qi: (bi, ni, 0, qi, 0)),
            pl.BlockSpec((None, t, LANES), lambda bi, ni, qi: (bi, 0, 0)),
            pl.BlockSpec((None, t, LANES), lambda bi, ni, qi: (bi, 0, 0)),
        ],
        out_specs=pl.BlockSpec((None, tq, 2 * LANES), lambda bi, ni, qi: (bi, qi, ni)),
        out_shape=jax.ShapeDtypeStruct((b, t, GQA_HEADS * HEAD_DIM), BF16),
        scratch_shapes=[pltpu.VMEM((1, r, LANES), F32)] * 3,
        compiler_params=_params(3),
        name="gqa_attention",
    )(qg, kg, vg)


def _cmul(ar, ai, br, bi):
    return ar * br - ai * bi, ar * bi + ai * br


def _stack_rows(er, ei, vr, vi):
    re = [er * vr[c:c + 1] - ei * vi[c:c + 1] for c in range(S5_GROUP_CH)]
    im = [er * vi[c:c + 1] + ei * vr[c:c + 1] for c in range(S5_GROUP_CH)]
    return jnp.concatenate(re, axis=0), jnp.concatenate(im, axis=0)


def _split_dot_t(a, b):
    dn = (((1,), (1,)), ((), ()))
    ah = a.astype(BF16)
    al = (a - ah.astype(F32)).astype(BF16)
    bh = b.astype(BF16)
    bl = (b - bh.astype(F32)).astype(BF16)
    return lax.dot_general(jnp.concatenate([ah, ah, al], axis=1), jnp.concatenate([bh, bl, bh], axis=1),
                           dn, preferred_element_type=F32)


def _s5_weights_body(lam_ref, bt_ref, c_ref, wi_ref, ws_ref, wo_ref, av_ref):
    tc = S5_CHUNK
    lam = lam_ref[...]
    br_t, bi_t = bt_ref[0:16, :], bt_ref[16:32, :]
    cr, ci = c_ref[0:16, :], c_ref[16:32, :]
    kk = lax.broadcasted_iota(jnp.int32, (tc, S5_STATE), 0).astype(F32)
    k8 = lax.broadcasted_iota(jnp.int32, (8, S5_STATE), 0)
    k8 = jnp.where(k8 == 0, 1.0, jnp.where(k8 == 1, tc - 1.0, float(tc)))
    row = lax.broadcasted_iota(jnp.int32, (S5_FLAT, S5_FLAT), 0) % tc
    col = lax.broadcasted_iota(jnp.int32, (S5_FLAT, S5_FLAT), 1) % tc
    w_intra = None
    state_cols, out_cols, a_rows = [], [], []
    for direction in range(2):
        lr = lam[2 * direction:2 * direction + 1]
        li = lam[2 * direction + 1:2 * direction + 2]
        step = jnp.exp(lam[4 + direction:5 + direction])
        mag = jnp.exp(lr * step)
        ar, ai = mag * jnp.cos(li * step), mag * jnp.sin(li * step)
        den = lr * lr + li * li
        fr = ((ar - 1.0) * lr + ai * li) / den
        fi = (ai * lr - (ar - 1.0) * li) / den
        bbr = fr * br_t - fi * bi_t
        bbi = fr * bi_t + fi * br_t
        cs, sn = jnp.cos(kk * (li * step)), jnp.sin(kk * (li * step))
        grow, decay = jnp.exp(-kk * (lr * step)), jnp.exp(kk * (lr * step))
        pr, pi = decay * cs, decay * sn
        nr, ni = grow * cs, -grow * sn
        m8 = jnp.exp(k8 * (lr * step))
        c8r, c8i = m8 * jnp.cos(k8 * (li * step)), m8 * jnp.sin(k8 * (li * step))
        a_one = (c8r[0:1], c8i[0:1])
        a_last = (c8r[1:2], c8i[1:2])
        a_tc = (c8r[2:3], c8i[2:3])
        if direction == 0:
            x_e, y_e = (nr, ni), (pr, pi)
            s_e = _cmul(nr, ni, *a_last)
            o_e = _cmul(pr, pi, *a_one)
            keep = col >= row
        else:
            x_e, y_e = (pr, pi), (nr, ni)
            s_e = (pr, pi)
            o_e = _cmul(nr, ni, *a_tc)
            keep = row >= col
        xr, xi = _stack_rows(*x_e, bbr, bbi)
        yr, yi = _stack_rows(*y_e, cr, ci)
        full = _split_dot_t(jnp.concatenate([xr, -xi], axis=1), jnp.concatenate([yr, yi], axis=1))
        part = jnp.where(keep, full, 0.0)
        w_intra = part if w_intra is None else w_intra + part
        state_cols.append(_stack_rows(*s_e, bbr, bbi))
        o_r, o_i = _stack_rows(*o_e, cr, ci)
        out_cols += [o_r, -o_i]
        a_rows += [jnp.concatenate([a_tc[0], a_tc[0]], axis=1),
                   jnp.concatenate([-a_tc[1], a_tc[1]], axis=1),
                   jnp.concatenate([a_tc[1], -a_tc[1]], axis=1)]
    wi_ref[...] = w_intra.astype(BF16)
    (fr_, fi_), (br_, bi_) = state_cols
    ws_ref[...] = jnp.concatenate([fr_, fi_, br_, bi_, fi_, fr_, bi_, br_], axis=1).astype(BF16)
    wo_ref[...] = jnp.concatenate(out_cols, axis=1).astype(BF16)
    zero = jnp.zeros((1, LANES), F32)
    av_ref[...] = jnp.concatenate(a_rows + [zero, zero], axis=0)


def _s5_weights_call(lam_re, lam_im, log_dt, b_re, b_im, c_re, c_im):
    g, p = S5_GROUPS, S5_STATE
    bc = lambda v: jnp.broadcast_to(v.astype(F32)[:, None], (g, p))
    zero = jnp.zeros((g, p), F32)
    lam = jnp.stack([lam_re[0], lam_im[0], lam_re[1], lam_im[1], bc(log_dt[0]), bc(log_dt[1]),
                     zero, zero], axis=1).astype(F32)
    bt = jnp.concatenate([jnp.swapaxes(b_re, 1, 2), jnp.swapaxes(b_im, 1, 2)], axis=1).astype(F32)
    cc = jnp.concatenate([c_re, c_im], axis=1).astype(F32)
    n = S5_FLAT
    return pl.pallas_call(
        _s5_weights_body,
        grid=(g,),
        in_specs=[pl.BlockSpec((None, 8, p), lambda gi: (gi, 0, 0)),
                  pl.BlockSpec((None, 32, p), lambda gi: (gi, 0, 0)),
                  pl.BlockSpec((None, 32, p), lambda gi: (gi, 0, 0))],
        out_specs=[pl.BlockSpec((None, n, n), lambda gi: (gi, 0, 0)),
                   pl.BlockSpec((None, n, 4 * LANES), lambda gi: (gi, 0, 0)),
                   pl.BlockSpec((None, n, 2 * LANES), lambda gi: (gi, 0, 0)),
                   pl.BlockSpec((None, 8, LANES), lambda gi: (gi, 0, 0))],
        out_shape=[jax.ShapeDtypeStruct((g, n, n), BF16),
                   jax.ShapeDtypeStruct((g, n, 4 * LANES), BF16),
                   jax.ShapeDtypeStruct((g, n, 2 * LANES), BF16),
                   jax.ShapeDtypeStruct((g, 8, LANES), F32)],
        compiler_params=_params(1),
        name="s5_weights",
    )(lam, bt, cc)


def _s5_body(u_ref, wi_ref, ws_ref, wo_ref, av_ref, y_ref, s_ref, h_ref, *, n_ctx_chunks, n_chunks):
    u = u_ref[...]
    s_ref[...] = jnp.dot(u, ws_ref[...], preferred_element_type=F32)
    av = av_ref[...]
    nj = n_chunks
    rows = u_ref.shape[0] // nj
    a1f, a2f, a2sf, a1b, a2b, a2sb = [jnp.broadcast_to(av[i:i + 1], (rows, LANES)) for i in range(6)]
    hf = hfs = hb = hbs = jnp.zeros((rows, LANES), F32)
    for i in range(nj):
        jf = i * rows
        jb = (n_ctx_chunks - 1 - i if i < n_ctx_chunks else nj - 1 - (i - n_ctx_chunks)) * rows
        h_ref[jf:jf + rows, 0:LANES] = hf
        h_ref[jb:jb + rows, LANES:2 * LANES] = hb
        sf = s_ref[jf:jf + rows, 0:LANES]
        sfs = s_ref[jf:jf + rows, 2 * LANES:3 * LANES]
        sb = s_ref[jb:jb + rows, LANES:2 * LANES]
        sbs = s_ref[jb:jb + rows, 3 * LANES:4 * LANES]
        hf, hfs = a1f * hf + a2f * hfs + sf, a1f * hfs + a2sf * hf + sfs
        hb, hbs = a1b * hb + a2b * hbs + sb, a1b * hbs + a2sb * hb + sbs
    y = (jnp.dot(u, wi_ref[...], preferred_element_type=F32)
         + lax.dot_general(h_ref[...].astype(BF16), wo_ref[...], (((1,), (1,)), ((), ())),
                           preferred_element_type=F32))
    y_ref[...] = y.astype(y_ref.dtype)


def _s5_call(ug, w_intra, w_state, w_out, avec, n_ctx_chunks, n_chunks):
    g, r, w = ug.shape
    return pl.pallas_call(
        functools.partial(_s5_body, n_ctx_chunks=n_ctx_chunks, n_chunks=n_chunks),
        grid=(g,),
        in_specs=[
            pl.BlockSpec((None, r, w), lambda gi: (gi, 0, 0)),
            pl.BlockSpec((None, w, w), lambda gi: (gi, 0, 0)),
            pl.BlockSpec((None, w, 4 * LANES), lambda gi: (gi, 0, 0)),
            pl.BlockSpec((None, w, 2 * LANES), lambda gi: (gi, 0, 0)),
            pl.BlockSpec((None, 8, LANES), lambda gi: (gi, 0, 0)),
        ],
        out_specs=pl.BlockSpec((None, r, w), lambda gi: (gi, 0, 0)),
        out_shape=jax.ShapeDtypeStruct((g, r, w), BF16),
        scratch_shapes=[pltpu.VMEM((r, 4 * LANES), F32), pltpu.VMEM((r, 2 * LANES), F32)],
        compiler_params=_params(1),
        name="s5_scan",
    )(ug, w_intra, w_state, w_out, avec)


def _s5_to_groups(u):
    b, t, _ = u.shape
    nj = t // S5_CHUNK
    x = jnp.swapaxes(u.astype(BF16).reshape(b, nj, S5_CHUNK, S5_WIDTH), 2, 3)
    x = lax.optimization_barrier(x.reshape(b, nj, S5_GROUPS, S5_FLAT))
    return jnp.transpose(x, (2, 1, 0, 3)).reshape(S5_GROUPS, nj * b, S5_FLAT)


def _s5_from_groups(y, b):
    g, r, n = y.shape
    nj = r // b
    x = jnp.transpose(y.reshape(g, nj, b, n), (2, 1, 0, 3))
    x = lax.optimization_barrier(x).reshape(b, nj, S5_WIDTH, S5_CHUNK)
    return jnp.swapaxes(x, 2, 3).reshape(b, nj * S5_CHUNK, S5_WIDTH)


def _merge_body(h_ref, mod_ref, xn_ref, ya_ref, yb_ref, yc_ref, u_ref, y5_ref,
                wg_ref, wa_ref, wb_ref, wc_ref, wd_ref, glw_ref, glb_ref, s5d_ref, wo_ref,
                g2_ref, wgu_ref, wdn_ref, o_ref):
    xn = xn_ref[...]
    y5 = y5_ref[...].astype(F32) + s5d_ref[...] * u_ref[...].astype(F32)
    gelu = 0.5 * y5 * (1.0 + jnp.tanh(0.7978845608028654 * (y5 + 0.044715 * y5 * y5 * y5)))
    glu = _bdot(gelu, glw_ref[...]) + glb_ref[...]
    yd = glu[:, 0:S5_WIDTH] * _sigmoid(glu[:, S5_WIDTH:2 * S5_WIDTH])
    branches = ((ya_ref[...], wa_ref), (yb_ref[...], wb_ref), (yc_ref[...], wc_ref),
                (yd.astype(BF16), wd_ref))
    merged = None
    for i, (y, w_ref) in enumerate(branches):
        gate = _sigmoid(jnp.dot(xn, wg_ref[i], preferred_element_type=F32))
        term = gate * jnp.dot(y, w_ref[...], preferred_element_type=F32)
        merged = term if merged is None else merged + term
    out = _bdot(merged, wo_ref[...])
    h = h_ref[...] + mod_ref[2:3, :] * out
    mod = mod_ref[...]
    xf = (_rms(h, g2_ref[...]) * (1.0 + mod[4:5]) + mod[3:4]).astype(BF16)
    gate = jnp.dot(xf, wgu_ref[:, 0:FFN_HIDDEN], preferred_element_type=F32)
    up = jnp.dot(xf, wgu_ref[:, FFN_HIDDEN:2 * FFN_HIDDEN], preferred_element_type=F32)
    act = (_silu(gate) * up).astype(BF16)
    o_ref[...] = h + mod[5:6] * jnp.dot(act, wdn_ref[...], preferred_element_type=F32)


def _merge_call(h, modv, xn, ya, yb, yc, u, y5, wts, ffn_wts, n_ctx_tiles, latent_only):
    b, t, d = h.shape
    tm = ROW_TILE
    skip = n_ctx_tiles if latent_only else 0

    def row(width):
        return pl.BlockSpec((None, tm, width), lambda bi, ti: (bi, ti + skip, 0))

    def const(shape):
        return pl.BlockSpec(shape, lambda bi, ti: (0,) * len(shape), pipeline_mode=pl.Buffered(1))

    w_gate, w_a, w_b, w_c, w_d, glu_w, glu_b, s5_d, w_out = wts
    g2, w_gu, w_down = ffn_wts
    return pl.pallas_call(
        _merge_body,
        grid=(b, t // tm - skip),
        in_specs=[row(d),
                  pl.BlockSpec((None, None, 6, d),
                               lambda bi, ti: (bi, jnp.where(ti + skip >= n_ctx_tiles, 1, 0), 0, 0)),
                  row(d), row(512), row(512), row(512), row(S5_WIDTH), row(S5_WIDTH),
                  const((4, d, d)), const((512, d)), const((512, d)), const((512, d)),
                  const((S5_WIDTH, d)), const((S5_WIDTH, 2 * S5_WIDTH)), const((1, 2 * S5_WIDTH)),
                  const((1, S5_WIDTH)), const((d, d)),
                  const((1, d)), const((d, 2 * FFN_HIDDEN)), const((FFN_HIDDEN, d))],
        out_specs=pl.BlockSpec((None, tm, d), lambda bi, ti: (bi, ti, 0)),
        out_shape=jax.ShapeDtypeStruct((b, t - skip * tm, d), F32),
        compiler_params=_params(2),
        name="merge_ffn",
    )(h, modv, xn, ya, yb, yc, u, y5, w_gate, w_a, w_b, w_c, w_d, glu_w, glu_b, s5_d, w_out,
      g2, w_gu, w_down)


def _rope_tables(ctx_len, seq_len):
    n_rows = seq_len // GRID_W
    rows = jnp.repeat(jnp.arange(n_rows, dtype=F32), GRID_W)
    cols = jnp.tile(jnp.arange(GRID_W, dtype=F32), n_rows)
    quarter = HEAD_DIM // 4
    inv_freq = ROPE_THETA ** (-jnp.arange(quarter, dtype=F32) / quarter)
    ang_r = rows[:, None] * inv_freq
    ang_c = cols[:, None] * inv_freq
    ang = jnp.concatenate([ang_r, ang_r, ang_c, ang_c], axis=-1)
    cos = jnp.concatenate([jnp.ones((ctx_len, HEAD_DIM), F32), jnp.cos(ang)], axis=0)
    sin = jnp.concatenate([jnp.zeros((ctx_len, HEAD_DIM), F32), jnp.sin(ang)], axis=0)
    first = (jnp.arange(HEAD_DIM) % 32) < 16
    sin_a = jnp.where(first, -sin, 0.0)
    sin_b = jnp.where(first, 0.0, sin)
    two = lambda m: jnp.concatenate([m, m], axis=1)
    return jnp.concatenate([two(cos), two(sin_a), two(sin_b)], axis=1)


def _w_in_layout(w_in):
    d = w_in.shape[0]
    a0 = 0
    z = w_in[:, a0:a0 + 512]
    xbc = w_in[:, a0 + 512:a0 + 1280]
    dt = jnp.pad(w_in[:, a0 + 1280:a0 + 1296], ((0, 0), (0, SSD_DT_PAD - 16)))
    rest = w_in[:, 1296:]
    out = jnp.concatenate([z, xbc, dt, rest], axis=1).astype(BF16)
    assert out.shape == (d, _C_END)
    return out


def kernel(x, c, ctx, c_ctx, w_mod, b_mod, norm1_g, norm2_g, w_in, ssd_conv_w, ssd_conv_b, ssd_a_log, ssd_dt_bias, ssd_d, ssd_norm_g, diff_qn_g, diff_kn_g, diff_lam_q1, diff_lam_k1, diff_lam_q2, diff_lam_k2, diff_subln_g, gqa_qn_g, gqa_kn_g, s5_lam_re, s5_lam_im, s5_log_dt, s5_b_re, s5_b_im, s5_c_re, s5_c_im, s5_d, s5_glu_w, s5_glu_b, w_gate, w_br_ssd, w_br_diff, w_br_gqa, w_br_s5, w_out, ffn_w_gate_up, ffn_w_down):
    b, seq_len, d = x.shape
    ctx_len = ctx.shape[1]
    depth = w_mod.shape[0]
    assert b + 1 <= 8
    assert ctx_len % ROW_TILE == 0 and seq_len % ROW_TILE == 0
    n_ctx_tiles = ctx_len // ROW_TILE

    h = jnp.concatenate([ctx, x], axis=1)
    cc = jnp.concatenate([c, c_ctx[None], jnp.zeros((8 - b - 1, d), F32)], axis=0)
    mods = _mod_call(cc, w_mod, b_mod)

    rope_tab = _rope_tables(ctx_len, seq_len)
    blk = jnp.arange(512) // HEAD_DIM
    ones = (blk[:, None] == blk[None, :]).astype(BF16)
    tile8 = lambda g: jnp.tile(g.astype(F32), 512 // HEAD_DIM)

    for layer in range(depth):
        m = mods[layer]
        lat = m[:b].reshape(b, 6, d)
        cmod = jnp.broadcast_to(m[b].reshape(1, 6, d), (b, 6, d))
        modv = jnp.stack([cmod, lat], axis=1)

        gains = jnp.stack([tile8(diff_qn_g[layer]), tile8(diff_kn_g[layer]),
                           tile8(gqa_qn_g[layer]), tile8(gqa_kn_g[layer])]
                          + [jnp.zeros((512,), F32)] * 4, axis=0)
        (xn, z, xbc, dt, qd, kd, vd, qg, kg, vg, u) = _in_proj_call(
            h, modv, norm1_g[layer][None], _w_in_layout(w_in[layer]), ones, gains, rope_tab,
            n_ctx_tiles)

        conv_w = jnp.pad(ssd_conv_w[layer].astype(F32), ((0, 8 - SSD_CONV_W), (0, 0)))
        xbc_act = _conv_call(xbc, conv_w, ssd_conv_b[layer][None].astype(F32), n_ctx_tiles)
        pad16 = lambda v: jnp.pad(v.reshape(1, 16).astype(F32), ((0, 0), (0, LANES - 16)))
        aneg = pad16(-jnp.exp(ssd_a_log[layer].astype(F32)))
        dtb = pad16(ssd_dt_bias[layer])
        dsk = jnp.repeat(ssd_d[layer].astype(F32), SSD_HEAD_DIM)[None]
        ya = _ssd_call(xbc_act, dt, z, aneg, dtb, dsk, ssd_norm_g[layer][None].astype(F32),
                       ctx_len // SSD_CHUNK)

        lam_init = 0.8 - 0.6 * math.exp(-0.3 * layer)
        lamv = jnp.pad(jnp.stack([diff_lam_q1[layer], diff_lam_k1[layer],
                                  diff_lam_q2[layer], diff_lam_k2[layer]]).astype(F32),
                       ((0, 0), (0, LANES - HEAD_DIM)))
        yb = _diff_attn_call(qd, kd, vd, lamv, diff_subln_g[layer][None].astype(F32), lam_init, ctx_len)

        yc = _gqa_attn_call(qg, kg, vg, ctx_len)

        s5w = _s5_weights_call(s5_lam_re[layer], s5_lam_im[layer], s5_log_dt[layer],
                               s5_b_re[layer], s5_b_im[layer], s5_c_re[layer], s5_c_im[layer])
        y5 = _s5_from_groups(_s5_call(_s5_to_groups(u), *s5w, ctx_len // S5_CHUNK,
                                      (ctx_len + seq_len) // S5_CHUNK), b)

        wts = (w_gate[layer].astype(BF16), w_br_ssd[layer].astype(BF16), w_br_diff[layer].astype(BF16),
               w_br_gqa[layer].astype(BF16), w_br_s5[layer].astype(BF16), s5_glu_w[layer].astype(BF16),
               s5_glu_b[layer][None].astype(F32), s5_d[layer][None].astype(F32), w_out[layer].astype(BF16))
        ffn_wts = (norm2_g[layer][None].astype(F32), ffn_w_gate_up[layer].astype(BF16),
                   ffn_w_down[layer].astype(BF16))
        h = _merge_call(h, modv, xn, ya, yb, yc, u, y5, wts, ffn_wts, n_ctx_tiles,
                        latent_only=layer == depth - 1)
    return h
```

```python
import functools
import math

import jax
import jax.numpy as jnp
from jax import lax
from jax.experimental import pallas as pl
from jax.experimental.pallas import tpu as pltpu

F32 = jnp.float32
BF16 = jnp.bfloat16
HIGHEST = lax.Precision.HIGHEST

D_MODEL = 1024
GRID_W = 64
ROPE_THETA = 10000.0
NORM_EPS = 1e-6

SSD_INNER = 512
SSD_HEADS = 8
SSD_HEAD_DIM = 64
SSD_STATE = 64
SSD_CHUNK = 128
SSD_CONV_W = 5
SSD_CONV_CH = 768
SSD_DT_PAD = 128
SSD_BATCH_PER_STEP = 4

DIFF_HEADS = 4
HEAD_DIM = 64
GQA_HEADS = 8
GQA_KV_HEADS = 2

S5_GROUP_CH = 16
S5_STATE = 64
S5_WIDTH = 384
S5_GROUPS = 24
S5_CHUNK = 64
S5_FLAT = S5_CHUNK * S5_GROUP_CH

FFN_HIDDEN = 2816

ROW_TILE = 256
LANES = 128
VMEM_LIMIT = 56 * 1024 * 1024

_C_Z, _C_XBC, _C_DT = 0, 512, 1280
_C_QD, _C_KD, _C_VD = 1408, 1920, 2432
_C_QG, _C_KG, _C_VG = 2944, 3456, 3584
_C_U, _C_END = 3712, 4096


def _params(n_grid):
    return pltpu.CompilerParams(dimension_semantics=("arbitrary",) * n_grid,
                                vmem_limit_bytes=VMEM_LIMIT)


def _bdot(a, b):
    return jnp.dot(a.astype(BF16), b.astype(BF16), preferred_element_type=F32)


def _hdot(a, b):
    return jnp.dot(a, b, precision=HIGHEST, preferred_element_type=F32)


def _rms(x, g):
    return x * lax.rsqrt(jnp.mean(x * x, axis=-1, keepdims=True) + NORM_EPS) * g


def _sigmoid(x):
    return 1.0 / (1.0 + jnp.exp(-x))


def _silu(x):
    return x * _sigmoid(x)


def _mod_body(c_ref, w_ref, b_ref, o_ref):
    o_ref[...] = _hdot(_silu(c_ref[...]), w_ref[...]) + b_ref[...]


def _mod_call(cc, w_mod, b_mod):
    depth = w_mod.shape[0]
    d = D_MODEL
    return pl.pallas_call(
        _mod_body,
        grid=(depth, 6),
        in_specs=[pl.BlockSpec((8, d), lambda l, j: (0, 0)),
                  pl.BlockSpec((None, d, d), lambda l, j: (l, 0, j)),
                  pl.BlockSpec((None, 1, d), lambda l, j: (l, 0, j))],
        out_specs=pl.BlockSpec((None, 8, d), lambda l, j: (l, 0, j)),
        out_shape=jax.ShapeDtypeStruct((depth, 8, 6 * d), F32),
        compiler_params=_params(2),
        name="adaln_mod",
    )(cc, w_mod, b_mod.reshape(depth, 1, 6 * d))


def _head_rms(t, gain, ones):
    t2 = (t * t).astype(BF16)
    w = t.shape[-1]
    blk = min(w, 256)
    ss = jnp.concatenate([jnp.dot(t2[:, i:i + blk], ones[i:i + blk, i:i + blk], preferred_element_type=F32)
                          for i in range(0, w, blk)], axis=1)
    return t * lax.rsqrt(ss * (1.0 / HEAD_DIM) + NORM_EPS) * gain


def _rope(t, cos, sin_a, sin_b):
    w = t.shape[-1]
    return t * cos + pltpu.roll(t, w - 16, 1) * sin_a + pltpu.roll(t, 16, 1) * sin_b


def _in_proj_body(h_ref, hp_ref, hn_ref, mod_ref, g_ref, w_ref, ones_ref, gains_ref, rope_ref,
                  cw_ref, cb_ref, xn_ref, z_ref, xbc_ref, dt_ref, qd_ref, kd_ref, vd_ref,
                  qg_ref, kg_ref, vg_ref, u_ref, *, n_ctx_tiles, n_tiles):
    t = pl.program_id(1)
    tm = h_ref.shape[0]
    mod = mod_ref[...]

    def normed(v):
        return (_rms(v, g_ref[...]) * (1.0 + mod[1:2]) + mod[0:1]).astype(BF16)

    xb = normed(h_ref[...])
    xn_ref[...] = xb

    def proj(lo, hi):
        return jnp.dot(xb, w_ref[:, lo:hi], preferred_element_type=F32)

    z_ref[...] = proj(_C_Z, _C_XBC)
    dt_ref[...] = proj(_C_DT, _C_QD)

    has_prev = jnp.logical_and(t != 0, t != n_ctx_tiles)
    has_next = jnp.logical_and(t != n_ctx_tiles - 1, t != n_tiles - 1)
    wx = w_ref[:, _C_XBC:_C_DT]
    prev = jnp.dot(normed(hp_ref[...]), wx, preferred_element_type=F32)
    nxt = jnp.dot(normed(hn_ref[...]), wx, preferred_element_type=F32)
    full = jnp.concatenate([jnp.where(has_prev, prev, 0.0), proj(_C_XBC, _C_DT),
                            jnp.where(has_next, nxt, 0.0)], axis=0)
    cw = cw_ref[...]
    pad = (SSD_CONV_W - 1) // 2
    n = tm + 16
    acc = cb_ref[...] + cw[pad:pad + 1] * full[8:8 + tm]
    for k in range(SSD_CONV_W):
        if k != pad:
            acc = acc + cw[k:k + 1] * pltpu.roll(full, (pad - k) % n, 0)[8:8 + tm]
    xbc_ref[...] = _silu(acc)
    u_ref[...] = proj(_C_U, _C_END).astype(u_ref.dtype)

    rope = rope_ref[...]
    cos1, sa1, sb1 = rope[:, 0:128], rope[:, 128:256], rope[:, 256:384]
    cos4 = jnp.concatenate([cos1] * 4, axis=1)
    sa4 = jnp.concatenate([sa1] * 4, axis=1)
    sb4 = jnp.concatenate([sb1] * 4, axis=1)
    ones = ones_ref[...]
    gains = gains_ref[...]
    lane = lax.broadcasted_iota(jnp.int32, (1, LANES), 1)
    lo_half = lane < HEAD_DIM
    scale = HEAD_DIM ** -0.5 * math.log2(math.e)

    qd = _rope(_head_rms(proj(_C_QD, _C_KD), gains[0:1], ones), cos4, sa4, sb4) * scale
    kd = _rope(_head_rms(proj(_C_KD, _C_VD), gains[1:2], ones), cos4, sa4, sb4)
    kd_ref[...] = kd.astype(BF16)
    vd_ref[...] = proj(_C_VD, _C_QG).astype(BF16)
    for h in range(DIFF_HEADS):
        blk = qd[:, h * LANES:(h + 1) * LANES]
        qd_ref[h, 0] = jnp.where(lo_half, blk, 0.0).astype(BF16)
        qd_ref[h, 1] = jnp.where(lo_half, 0.0, blk).astype(BF16)

    qg = _rope(_head_rms(proj(_C_QG, _C_KG), gains[2:3], ones), cos4, sa4, sb4) * scale
    kg = _rope(_head_rms(proj(_C_KG, _C_VG), gains[3:4, :LANES], ones[:LANES, :LANES]),
               cos1, sa1, sb1)
    kg_ref[...] = kg.astype(BF16)
    vg_ref[...] = proj(_C_VG, _C_U).astype(BF16)
    qg_up = pltpu.roll(qg, 4 * LANES - HEAD_DIM, 1)
    qg_dn = pltpu.roll(qg, HEAD_DIM, 1)
    per_kv = GQA_HEADS // GQA_KV_HEADS
    for n in range(GQA_KV_HEADS):
        for i in range(per_kv):
            j = n * per_kv + i
            blk_idx = j // 2
            if (j % 2) == n:
                src = qg
            else:
                src = qg_up if n == 0 else qg_dn
            blk = src[:, blk_idx * LANES:(blk_idx + 1) * LANES]
            keep = lo_half if n == 0 else jnp.logical_not(lo_half)
            qg_ref[n, i] = jnp.where(keep, blk, 0.0).astype(BF16)


def _in_proj_call(h, modv, g1, w_cat, ones, gains, rope_tab, conv_w, conv_b, n_ctx_tiles):
    b, t, d = h.shape
    nt = t // ROW_TILE
    tm = ROW_TILE

    def row(width, dtype):
        return (pl.BlockSpec((None, tm, width), lambda bi, ti: (bi, ti, 0)),
                jax.ShapeDtypeStruct((b, t, width), dtype))

    outs = [row(d, BF16), row(512, F32), row(768, F32), row(SSD_DT_PAD, F32)]
    qd = (pl.BlockSpec((None, DIFF_HEADS, 2, tm, LANES), lambda bi, ti: (bi, 0, 0, ti, 0)),
          jax.ShapeDtypeStruct((b, DIFF_HEADS, 2, t, LANES), BF16))
    qg = (pl.BlockSpec((None, GQA_KV_HEADS, 4, tm, LANES), lambda bi, ti: (bi, 0, 0, ti, 0)),
          jax.ShapeDtypeStruct((b, GQA_KV_HEADS, 4, t, LANES), BF16))
    outs += [qd, row(512, BF16), row(512, BF16), qg, row(LANES, BF16), row(LANES, BF16),
             row(S5_WIDTH, BF16)]
    per = tm // 8
    last8 = t // 8 - 1
    return pl.pallas_call(
        functools.partial(_in_proj_body, n_ctx_tiles=n_ctx_tiles, n_tiles=nt),
        grid=(b, nt),
        in_specs=[
            pl.BlockSpec((None, tm, d), lambda bi, ti: (bi, ti, 0)),
            pl.BlockSpec((None, 8, d), lambda bi, ti: (bi, jnp.maximum(ti * per - 1, 0), 0)),
            pl.BlockSpec((None, 8, d), lambda bi, ti: (bi, jnp.minimum((ti + 1) * per, last8), 0)),
            pl.BlockSpec((None, None, 6, d),
                         lambda bi, ti: (bi, jnp.where(ti >= n_ctx_tiles, 1, 0), 0, 0)),
            pl.BlockSpec((1, d), lambda bi, ti: (0, 0)),
            pl.BlockSpec((d, _C_END), lambda bi, ti: (0, 0)),
            pl.BlockSpec((512, 512), lambda bi, ti: (0, 0)),
            pl.BlockSpec((8, 512), lambda bi, ti: (0, 0)),
            pl.BlockSpec((tm, 384), lambda bi, ti: (ti, 0)),
            pl.BlockSpec((8, SSD_CONV_CH), lambda bi, ti: (0, 0)),
            pl.BlockSpec((1, SSD_CONV_CH), lambda bi, ti: (0, 0)),
        ],
        out_specs=[o[0] for o in outs],
        out_shape=[o[1] for o in outs],
        compiler_params=_params(2),
        name="in_proj",
    )(h, h, h, modv, g1, w_cat, ones, gains, rope_tab, conv_w, conv_b)


def _ssd_chunk_index(d, s, n_ctx_chunks, n_chunks):
    bwd = jnp.where(s < n_ctx_chunks, n_ctx_chunks - 1 - s, n_chunks - 1 - (s - n_ctx_chunks))
    return jnp.where(d == 0, s, bwd)


def _ssd_body(xbc_ref, dt_ref, z_ref, aneg_ref, dtb_ref, dsk_ref, ng_ref, o_ref,
              hst_ref, ysc_ref, *, n_ctx_chunks, n_chunks):
    d = pl.program_id(1)
    s = pl.program_id(2)
    c = _ssd_chunk_index(d, s, n_ctx_chunks, n_chunks)
    q = SSD_CHUNK
    nb = xbc_ref.shape[0]
    r = nb * q

    @pl.when(s == 0)
    def _():
        hst_ref[...] = jnp.zeros_like(hst_ref)

    def run(direction):
        xbc = xbc_ref[...].reshape(r, SSD_CONV_CH)
        x = xbc[:, 0:512]
        bm = xbc[:, 512:640]
        cm = xbc[:, 640:768]
        pre = dt_ref[...].reshape(r, SSD_DT_PAD) + dtb_ref[...]
        dtv = jnp.maximum(pre, 0.0) + jnp.log(1.0 + jnp.exp(-jnp.abs(pre)))
        a = dtv * aneg_ref[...]
        row = lax.broadcasted_iota(jnp.int32, (r, r), 0)
        col = lax.broadcasted_iota(jnp.int32, (r, r), 1)
        order = (row >= col) if direction == 0 else (row <= col)
        tri = jnp.logical_and(row // q == col // q, order)
        a_hi = a.astype(BF16)
        r1 = a - a_hi.astype(F32)
        a_mid = r1.astype(BF16)
        a_lo = (r1 - a_mid.astype(F32)).astype(BF16)
        terms = [t[bi * q:(bi + 1) * q] for bi in range(nb) for t in (a_hi, a_mid, a_lo)]
        parts = jnp.dot(tri[0:q, 0:q].astype(BF16), jnp.concatenate(terms, axis=1),
                        preferred_element_type=F32)
        acs = jnp.concatenate(
            [parts[:, (3 * bi) * LANES:(3 * bi + 1) * LANES] + parts[:, (3 * bi + 1) * LANES:(3 * bi + 2) * LANES]
             + parts[:, (3 * bi + 2) * LANES:(3 * bi + 3) * LANES] for bi in range(nb)], axis=0)
        acs_t = acs.T
        edge = q - 1 if direction == 0 else 0
        lasts = [acs[bi * q + edge:bi * q + edge + 1, :] for bi in range(nb)]
        last = jnp.concatenate([jnp.broadcast_to(v, (q, LANES)) for v in lasts], axis=0)
        dte = jnp.exp(last - acs)
        expa = jnp.exp(acs)
        cdec = jnp.concatenate([jnp.broadcast_to(jnp.exp(v), (8, LANES)) for v in lasts], axis=0)
        erow = lax.broadcasted_iota(jnp.int32, (LANES, SSD_INNER), 0)
        ecol = lax.broadcasted_iota(jnp.int32, (LANES, SSD_INNER), 1)
        expand = (erow == ecol // SSD_HEAD_DIM + SSD_HEADS * direction).astype(BF16)
        spread = jnp.dot(jnp.concatenate([dtv, dtv * dte], axis=0).astype(BF16), expand,
                         preferred_element_type=F32)
        xdt = x * spread[0:r]
        xs = x * spread[r:2 * r]
        fine = jnp.concatenate([expa, cdec], axis=0)
        fine_hi = fine.astype(BF16)
        fine_lo = (fine - fine_hi.astype(F32)).astype(BF16)
        spread2 = (jnp.dot(fine_hi, expand, preferred_element_type=F32)
                   + jnp.dot(fine_lo, expand, preferred_element_type=F32))
        expa_x = spread2[0:r]
        bt = bm.T.astype(BF16)
        lane = lax.broadcasted_iota(jnp.int32, (1, LANES), 1)
        tri_q = tri[0:q, 0:q]
        ys = [[None] * nb for _ in range(4)]
        for g in range(2):
            cg = jnp.where(lane // SSD_STATE == g, cm, 0.0).astype(BF16)
            gmats = [jnp.dot(cg[bi * q:(bi + 1) * q], bt[:, bi * q:(bi + 1) * q], preferred_element_type=F32)
                     for bi in range(nb)]
            for hp in range(2):
                p = g * 2 + hp
                sl = slice(p * LANES, (p + 1) * LANES)
                for bi in range(nb):
                    rows = slice(bi * q, (bi + 1) * q)
                    ypair = None
                    for e in range(2):
                        j = SSD_HEADS * direction + 2 * p + e
                        seg = acs[rows, j:j + 1] - acs_t[j:j + 1, rows]
                        dec = jnp.exp(jnp.where(tri_q, seg, -jnp.inf))
                        xh = jnp.where(lane // SSD_HEAD_DIM == e, xdt[rows, sl], 0.0)
                        term = _bdot(gmats[bi] * dec, xh)
                        ypair = term if ypair is None else ypair + term
                    hprev = hst_ref[bi, p]
                    ypair = ypair + _bdot(cg[rows], hprev) * expa_x[rows, sl]
                    hst_ref[bi, p] = hprev * spread2[r + 8 * bi:r + 8 * bi + 1, sl] + jnp.dot(
                        bt[:, rows], xs[rows, sl].astype(BF16), preferred_element_type=F32)
                    ys[p][bi] = ypair
        ys = [jnp.concatenate(v, axis=0) for v in ys]
        return x, ys

    @pl.when(d == 0)
    def _():
        _, ys = run(0)
        for p in range(4):
            ysc_ref[c, :, p * LANES:(p + 1) * LANES] = ys[p].astype(ysc_ref.dtype)

    @pl.when(d == 1)
    def _():
        x, ys = run(1)
        y = jnp.concatenate(ys, axis=1) + ysc_ref[c].astype(F32) + dsk_ref[...] * x
        y = y * _silu(z_ref[...].reshape(r, SSD_INNER))
        o_ref[...] = _rms(y, ng_ref[...]).astype(o_ref.dtype).reshape(nb, q, SSD_INNER)


def _ssd_call(xbc_act, dt, z, aneg, dtb, dsk, ng, n_ctx_chunks):
    b, t, _ = xbc_act.shape
    q = SSD_CHUNK
    nc = t // q
    nb = SSD_BATCH_PER_STEP if b % SSD_BATCH_PER_STEP == 0 else 1
    cidx = functools.partial(_ssd_chunk_index, n_ctx_chunks=n_ctx_chunks, n_chunks=nc)

    def late(bi, di, si):
        return (bi, jnp.where(di == 0, n_ctx_chunks - 1, cidx(di, si)), 0)

    return pl.pallas_call(
        functools.partial(_ssd_body, n_ctx_chunks=n_ctx_chunks, n_chunks=nc),
        grid=(b // nb, 2, nc),
        in_specs=[
            pl.BlockSpec((nb, q, SSD_CONV_CH), lambda bi, di, si: (bi, cidx(di, si), 0)),
            pl.BlockSpec((nb, q, SSD_DT_PAD), lambda bi, di, si: (bi, cidx(di, si), 0)),
            pl.BlockSpec((nb, q, SSD_INNER), late),
            pl.BlockSpec((1, LANES), lambda bi, di, si: (0, 0)),
            pl.BlockSpec((1, LANES), lambda bi, di, si: (0, 0)),
            pl.BlockSpec((1, SSD_INNER), lambda bi, di, si: (0, 0)),
            pl.BlockSpec((1, SSD_INNER), lambda bi, di, si: (0, 0)),
        ],
        out_specs=pl.BlockSpec((nb, q, SSD_INNER), late),
        out_shape=jax.ShapeDtypeStruct((b, t, SSD_INNER), BF16),
        scratch_shapes=[pltpu.VMEM((nb, 4, LANES, LANES), F32),
                        pltpu.VMEM((nc, nb * q, SSD_INNER), BF16)],
        compiler_params=_params(3),
        name="ssd_scan",
    )(xbc_act, dt, z, aneg, dtb, dsk, ng)


def _flash(qs, k_ref, v_ref, m_ref, l_ref, acc_ref, *, tk, n_kv, ctx_len, is_ctx_tile):
    m_ref[...] = jnp.full(m_ref.shape, -jnp.inf, F32)
    l_ref[...] = jnp.zeros(l_ref.shape, F32)
    acc_ref[...] = jnp.zeros(acc_ref.shape, F32)

    def step(start, size, limit):
        for i, q in enumerate(qs):
            lanes = slice(i * LANES, (i + 1) * LANES)
            k = k_ref[start:start + size, lanes]
            v = v_ref[start:start + size, lanes]
            s = lax.dot_general(q, k, (((1,), (1,)), ((), ())), preferred_element_type=F32)
            if limit is not None:
                col = lax.broadcasted_iota(jnp.int32, (1, size), 1) + start
                s = jnp.where(col < limit, s, -jnp.inf)
            m_old = m_ref[i]
            m_new = jnp.maximum(m_old, jnp.max(s, axis=-1, keepdims=True))
            alpha = jnp.exp2(m_old - m_new)
            p = jnp.exp2(s - jnp.concatenate([m_new] * (size // LANES), axis=1))
            psum = p[:, 0:LANES]
            for j in range(1, size // LANES):
                psum = psum + p[:, j * LANES:(j + 1) * LANES]
            l_ref[i] = alpha * l_ref[i] + psum
            acc_ref[i] = alpha * acc_ref[i] + jnp.dot(p.astype(BF16), v, preferred_element_type=F32)
            m_ref[i] = m_new

    @pl.when(is_ctx_tile)
    def _():
        ctx_pad = -(-ctx_len // LANES) * LANES
        for start in range(0, ctx_pad, tk):
            step(start, min(tk, ctx_pad - start), ctx_len if ctx_pad != ctx_len else None)

    @pl.when(jnp.logical_not(is_ctx_tile))
    def _():
        for kc in range(n_kv):
            step(kc * tk, tk, None)

    return [acc_ref[i] * (1.0 / jnp.sum(l_ref[i], axis=-1, keepdims=True)) for i in range(len(qs))]


DIFF_HEADS_PER_STEP = 2


def _diff_attn_body(q_ref, k_ref, v_ref, lam_ref, g_ref, o_ref, m_ref, l_ref, acc_ref,
                    *, tq, lam_init, n_ctx_q, **kw):
    qi = pl.program_id(2)
    qs = [q_ref[i].reshape(2 * tq, LANES) for i in range(DIFF_HEADS_PER_STEP)]
    outs = _flash(qs, k_ref, v_ref, m_ref, l_ref, acc_ref, is_ctx_tile=qi < n_ctx_q, **kw)
    lv = lam_ref[...]
    lam = (jnp.exp(jnp.sum(lv[0:1] * lv[1:2], axis=-1, keepdims=True))
           - jnp.exp(jnp.sum(lv[2:3] * lv[3:4], axis=-1, keepdims=True)) + lam_init)
    for i, o in enumerate(outs):
        out = o[0:tq] - lam * o[tq:2 * tq]
        o_ref[:, i * LANES:(i + 1) * LANES] = (_rms(out, g_ref[...]) * (1.0 - lam_init)).astype(o_ref.dtype)


def _gqa_attn_body(q_ref, k_ref, v_ref, o_ref, m_ref, l_ref, acc_ref, *, tq, n_ctx_q, **kw):
    n = pl.program_id(1)
    qi = pl.program_id(2)
    q = q_ref[...].reshape(4 * tq, LANES)
    (o,) = _flash([q], k_ref, v_ref, m_ref, l_ref, acc_ref, is_ctx_tile=qi < n_ctx_q, **kw)
    lane = lax.broadcasted_iota(jnp.int32, (1, LANES), 1)
    lo_half = lane < HEAD_DIM
    for pair in range(2):
        a = o[(2 * pair) * tq:(2 * pair + 1) * tq]
        b = o[(2 * pair + 1) * tq:(2 * pair + 2) * tq]
        from_lo = jnp.where(lo_half, a, pltpu.roll(b, HEAD_DIM, 1))
        from_hi = jnp.where(lo_half, pltpu.roll(a, HEAD_DIM, 1), b)
        o_ref[:, pair * LANES:(pair + 1) * LANES] = jnp.where(n == 0, from_lo, from_hi).astype(o_ref.dtype)


def _attn_tiles(t, ctx_len):
    tq = ROW_TILE
    tk = next(c for c in (2816, 768, 256) if t % c == 0)
    assert ctx_len % tq == 0 and t % tq == 0 and t % tk == 0
    return dict(tq=tq, tk=tk, n_kv=t // tk, ctx_len=ctx_len, n_ctx_q=ctx_len // tq)


def _diff_attn_call(qd, kd, vd, lamv, subln_g, lam_init, ctx_len):
    b, _, _, t, _ = qd.shape
    cfg = _attn_tiles(t, ctx_len)
    tq = cfg["tq"]
    r = 2 * tq
    hs = DIFF_HEADS_PER_STEP
    return pl.pallas_call(
        functools.partial(_diff_attn_body, lam_init=lam_init, **cfg),
        grid=(b, DIFF_HEADS // hs, t // tq),
        in_specs=[
            pl.BlockSpec((None, hs, 2, tq, LANES), lambda bi, hi, qi: (bi, hi, 0, qi, 0)),
            pl.BlockSpec((None, t, hs * LANES), lambda bi, hi, qi: (bi, 0, hi)),
            pl.BlockSpec((None, t, hs * LANES), lambda bi, hi, qi: (bi, 0, hi)),
            pl.BlockSpec((4, LANES), lambda bi, hi, qi: (0, 0)),
            pl.BlockSpec((1, LANES), lambda bi, hi, qi: (0, 0)),
        ],
        out_specs=pl.BlockSpec((None, tq, hs * LANES), lambda bi, hi, qi: (bi, qi, hi)),
        out_shape=jax.ShapeDtypeStruct((b, t, DIFF_HEADS * LANES), BF16),
        scratch_shapes=[pltpu.VMEM((hs, r, LANES), F32)] * 3,
        compiler_params=_params(3),
        name="diff_attention",
    )(qd, kd, vd, lamv, subln_g)


def _gqa_attn_call(qg, kg, vg, ctx_len):
    b, _, _, t, _ = qg.shape
    cfg = _attn_tiles(t, ctx_len)
    tq = cfg["tq"]
    r = 4 * tq
    return pl.pallas_call(
        functools.partial(_gqa_attn_body, **cfg),
        grid=(b, GQA_KV_HEADS, t // tq),
        in_specs=[
            pl.BlockSpec((None, None, 4, tq, LANES), lambda bi, ni, qi: (bi, ni, 0, qi, 0)),
            pl.BlockSpec((None, t, LANES), lambda bi, ni, qi: (bi, 0, 0)),
            pl.BlockSpec((None, t, LANES), lambda bi, ni, qi: (bi, 0, 0)),
        ],
        out_specs=pl.BlockSpec((None, tq, 2 * LANES), lambda bi, ni, qi: (bi, qi, ni)),
        out_shape=jax.ShapeDtypeStruct((b, t, GQA_HEADS * HEAD_DIM), BF16),
        scratch_shapes=[pltpu.VMEM((1, r, LANES), F32)] * 3,
        compiler_params=_params(3),
        name="gqa_attention",
    )(qg, kg, vg)


def _cmul(ar, ai, br, bi):
    return ar * br - ai * bi, ar * bi + ai * br


def _stack_rows(er, ei, vr, vi):
    re = [er * vr[c:c + 1] - ei * vi[c:c + 1] for c in range(S5_GROUP_CH)]
    im = [er * vi[c:c + 1] + ei * vr[c:c + 1] for c in range(S5_GROUP_CH)]
    return jnp.concatenate(re, axis=0), jnp.concatenate(im, axis=0)


def _split_dot_t(a, b):
    dn = (((1,), (1,)), ((), ()))
    ah = a.astype(BF16)
    al = (a - ah.astype(F32)).astype(BF16)
    bh = b.astype(BF16)
    bl = (b - bh.astype(F32)).astype(BF16)
    return lax.dot_general(jnp.concatenate([ah, ah, al], axis=1), jnp.concatenate([bh, bl, bh], axis=1),
                           dn, preferred_element_type=F32)


def _s5_weights_body(lam_ref, bt_ref, c_ref, wi_ref, ws_ref, wo_ref, av_ref):
    tc = S5_CHUNK
    lam = lam_ref[...]
    br_t, bi_t = bt_ref[0:16, :], bt_ref[16:32, :]
    cr, ci = c_ref[0:16, :], c_ref[16:32, :]
    kk = lax.broadcasted_iota(jnp.int32, (tc, S5_STATE), 0).astype(F32)
    k8 = lax.broadcasted_iota(jnp.int32, (8, S5_STATE), 0)
    k8 = jnp.where(k8 == 0, 1.0, jnp.where(k8 == 1, tc - 1.0, float(tc)))
    row = lax.broadcasted_iota(jnp.int32, (S5_FLAT, S5_FLAT), 0) % tc
    col = lax.broadcasted_iota(jnp.int32, (S5_FLAT, S5_FLAT), 1) % tc
    w_intra = None
    state_cols, out_cols, a_rows = [], [], []
    for direction in range(2):
        lr = lam[2 * direction:2 * direction + 1]
        li = lam[2 * direction + 1:2 * direction + 2]
        step = jnp.exp(lam[4 + direction:5 + direction])
        mag = jnp.exp(lr * step)
        ar, ai = mag * jnp.cos(li * step), mag * jnp.sin(li * step)
        den = lr * lr + li * li
        fr = ((ar - 1.0) * lr + ai * li) / den
        fi = (ai * lr - (ar - 1.0) * li) / den
        bbr = fr * br_t - fi * bi_t
        bbi = fr * bi_t + fi * br_t
        cs, sn = jnp.cos(kk * (li * step)), jnp.sin(kk * (li * step))
        grow, decay = jnp.exp(-kk * (lr * step)), jnp.exp(kk * (lr * step))
        pr, pi = decay * cs, decay * sn
        nr, ni = grow * cs, -grow * sn
        m8 = jnp.exp(k8 * (lr * step))
        c8r, c8i = m8 * jnp.cos(k8 * (li * step)), m8 * jnp.sin(k8 * (li * step))
        a_one = (c8r[0:1], c8i[0:1])
        a_last = (c8r[1:2], c8i[1:2])
        a_tc = (c8r[2:3], c8i[2:3])
        if direction == 0:
            x_e, y_e = (nr, ni), (pr, pi)
            s_e = _cmul(nr, ni, *a_last)
            o_e = _cmul(pr, pi, *a_one)
            keep = col >= row
        else:
            x_e, y_e = (pr, pi), (nr, ni)
            s_e = (pr, pi)
            o_e = _cmul(nr, ni, *a_tc)
            keep = row >= col
        xr, xi = _stack_rows(*x_e, bbr, bbi)
        yr, yi = _stack_rows(*y_e, cr, ci)
        full = _split_dot_t(jnp.concatenate([xr, -xi], axis=1), jnp.concatenate([yr, yi], axis=1))
        part = jnp.where(keep, full, 0.0)
        w_intra = part if w_intra is None else w_intra + part
        state_cols.append(_stack_rows(*s_e, bbr, bbi))
        o_r, o_i = _stack_rows(*o_e, cr, ci)
        out_cols += [o_r, -o_i]
        a_rows += [jnp.concatenate([a_tc[0], a_tc[0]], axis=1),
                   jnp.concatenate([-a_tc[1], a_tc[1]], axis=1),
                   jnp.concatenate([a_tc[1], -a_tc[1]], axis=1)]
    wi_ref[...] = w_intra.astype(BF16)
    (fr_, fi_), (br_, bi_) = state_cols
    ws_ref[...] = jnp.concatenate([fr_, fi_, br_, bi_, fi_, fr_, bi_, br_], axis=1).astype(BF16)
    wo_ref[...] = jnp.concatenate(out_cols, axis=1).astype(BF16)
    zero = jnp.zeros((1, LANES), F32)
    av_ref[...] = jnp.concatenate(a_rows + [zero, zero], axis=0)


def _s5_weights_call(lam_re, lam_im, log_dt, b_re, b_im, c_re, c_im):
    g, p = S5_GROUPS, S5_STATE
    bc = lambda v: jnp.broadcast_to(v.astype(F32)[:, None], (g, p))
    zero = jnp.zeros((g, p), F32)
    lam = jnp.stack([lam_re[0], lam_im[0], lam_re[1], lam_im[1], bc(log_dt[0]), bc(log_dt[1]),
                     zero, zero], axis=1).astype(F32)
    bt = jnp.concatenate([jnp.swapaxes(b_re, 1, 2), jnp.swapaxes(b_im, 1, 2)], axis=1).astype(F32)
    cc = jnp.concatenate([c_re, c_im], axis=1).astype(F32)
    n = S5_FLAT
    return pl.pallas_call(
        _s5_weights_body,
        grid=(g,),
        in_specs=[pl.BlockSpec((None, 8, p), lambda gi: (gi, 0, 0)),
                  pl.BlockSpec((None, 32, p), lambda gi: (gi, 0, 0)),
                  pl.BlockSpec((None, 32, p), lambda gi: (gi, 0, 0))],
        out_specs=[pl.BlockSpec((None, n, n), lambda gi: (gi, 0, 0)),
                   pl.BlockSpec((None, n, 4 * LANES), lambda gi: (gi, 0, 0)),
                   pl.BlockSpec((None, n, 2 * LANES), lambda gi: (gi, 0, 0)),
                   pl.BlockSpec((None, 8, LANES), lambda gi: (gi, 0, 0))],
        out_shape=[jax.ShapeDtypeStruct((g, n, n), BF16),
                   jax.ShapeDtypeStruct((g, n, 4 * LANES), BF16),
                   jax.ShapeDtypeStruct((g, n, 2 * LANES), BF16),
                   jax.ShapeDtypeStruct((g, 8, LANES), F32)],
        compiler_params=_params(1),
        name="s5_weights",
    )(lam, bt, cc)


def _s5_body(u_ref, wi_ref, ws_ref, wo_ref, av_ref, y_ref, s_ref, h_ref, *, n_ctx_chunks, n_chunks):
    u = u_ref[...]
    s_ref[...] = jnp.dot(u, ws_ref[...], preferred_element_type=F32)
    av = av_ref[...]
    nj = n_chunks
    rows = u_ref.shape[0] // nj
    a1f, a2f, a2sf, a1b, a2b, a2sb = [jnp.broadcast_to(av[i:i + 1], (rows, LANES)) for i in range(6)]
    hf = hfs = hb = hbs = jnp.zeros((rows, LANES), F32)
    for i in range(nj):
        jf = i * rows
        jb = (n_ctx_chunks - 1 - i if i < n_ctx_chunks else nj - 1 - (i - n_ctx_chunks)) * rows
        h_ref[jf:jf + rows, 0:LANES] = hf
        h_ref[jb:jb + rows, LANES:2 * LANES] = hb
        sf = s_ref[jf:jf + rows, 0:LANES]
        sfs = s_ref[jf:jf + rows, 2 * LANES:3 * LANES]
        sb = s_ref[jb:jb + rows, LANES:2 * LANES]
        sbs = s_ref[jb:jb + rows, 3 * LANES:4 * LANES]
        hf, hfs = a1f * hf + a2f * hfs + sf, a1f * hfs + a2sf * hf + sfs
        hb, hbs = a1b * hb + a2b * hbs + sb, a1b * hbs + a2sb * hb + sbs
    y = (jnp.dot(u, wi_ref[...], preferred_element_type=F32)
         + lax.dot_general(h_ref[...].astype(BF16), wo_ref[...], (((1,), (1,)), ((), ())),
                           preferred_element_type=F32))
    y_ref[...] = y.astype(y_ref.dtype)


def _s5_call(ug, w_intra, w_state, w_out, avec, n_ctx_chunks, n_chunks):
    g, r, w = ug.shape
    return pl.pallas_call(
        functools.partial(_s5_body, n_ctx_chunks=n_ctx_chunks, n_chunks=n_chunks),
        grid=(g,),
        in_specs=[
            pl.BlockSpec((None, r, w), lambda gi: (gi, 0, 0)),
            pl.BlockSpec((None, w, w), lambda gi: (gi, 0, 0)),
            pl.BlockSpec((None, w, 4 * LANES), lambda gi: (gi, 0, 0)),
            pl.BlockSpec((None, w, 2 * LANES), lambda gi: (gi, 0, 0)),
            pl.BlockSpec((None, 8, LANES), lambda gi: (gi, 0, 0)),
        ],
        out_specs=pl.BlockSpec((None, r, w), lambda gi: (gi, 0, 0)),
        out_shape=jax.ShapeDtypeStruct((g, r, w), BF16),
        scratch_shapes=[pltpu.VMEM((r, 4 * LANES), F32), pltpu.VMEM((r, 2 * LANES), F32)],
        compiler_params=_params(1),
        name="s5_scan",
    )(ug, w_intra, w_state, w_out, avec)


def _s5_to_groups(u):
    b, t, _ = u.shape
    nj = t // S5_CHUNK
    x = jnp.swapaxes(u.astype(BF16).reshape(b, nj, S5_CHUNK, S5_WIDTH), 2, 3)
    x = lax.optimization_barrier(x.reshape(b, nj, S5_GROUPS, S5_FLAT))
    return jnp.transpose(x, (2, 1, 0, 3)).reshape(S5_GROUPS, nj * b, S5_FLAT)


def _s5_from_groups(y, b):
    g, r, n = y.shape
    nj = r // b
    x = jnp.transpose(y.reshape(g, nj, b, n), (2, 1, 0, 3))
    x = lax.optimization_barrier(x).reshape(b, nj, S5_WIDTH, S5_CHUNK)
    return jnp.swapaxes(x, 2, 3).reshape(b, nj * S5_CHUNK, S5_WIDTH)


def _merge_body(h_ref, mod_ref, xn_ref, ya_ref, yb_ref, yc_ref, u_ref, y5_ref,
                wg_ref, wa_ref, wb_ref, wc_ref, wd_ref, glw_ref, glb_ref, s5d_ref, wo_ref,
                g2_ref, wgu_ref, wdn_ref, o_ref):
    xn = xn_ref[...]
    y5 = y5_ref[...].astype(F32) + s5d_ref[...] * u_ref[...].astype(F32)
    gelu = 0.5 * y5 * (1.0 + jnp.tanh(0.7978845608028654 * (y5 + 0.044715 * y5 * y5 * y5)))
    glu = _bdot(gelu, glw_ref[...]) + glb_ref[...]
    yd = glu[:, 0:S5_WIDTH] * _sigmoid(glu[:, S5_WIDTH:2 * S5_WIDTH])
    branches = ((ya_ref[...], wa_ref), (yb_ref[...], wb_ref), (yc_ref[...], wc_ref),
                (yd.astype(BF16), wd_ref))
    merged = None
    for i, (y, w_ref) in enumerate(branches):
        gate = _sigmoid(jnp.dot(xn, wg_ref[i], preferred_element_type=F32))
        term = gate * jnp.dot(y, w_ref[...], preferred_element_type=F32)
        merged = term if merged is None else merged + term
    out = _bdot(merged, wo_ref[...])
    h = h_ref[...] + mod_ref[2:3, :] * out
    mod = mod_ref[...]
    xf = (_rms(h, g2_ref[...]) * (1.0 + mod[4:5]) + mod[3:4]).astype(BF16)
    gate = jnp.dot(xf, wgu_ref[:, 0:FFN_HIDDEN], preferred_element_type=F32)
    up = jnp.dot(xf, wgu_ref[:, FFN_HIDDEN:2 * FFN_HIDDEN], preferred_element_type=F32)
    act = (_silu(gate) * up).astype(BF16)
    o_ref[...] = h + mod[5:6] * jnp.dot(act, wdn_ref[...], preferred_element_type=F32)


def _merge_call(h, modv, xn, ya, yb, yc, u, y5, wts, ffn_wts, n_ctx_tiles, latent_only):
    b, t, d = h.shape
    tm = ROW_TILE
    skip = n_ctx_tiles if latent_only else 0

    def row(width):
        return pl.BlockSpec((None, tm, width), lambda bi, ti: (bi, ti + skip, 0))

    def const(shape):
        return pl.BlockSpec(shape, lambda bi, ti: (0,) * len(shape), pipeline_mode=pl.Buffered(1))

    w_gate, w_a, w_b, w_c, w_d, glu_w, glu_b, s5_d, w_out = wts
    g2, w_gu, w_down = ffn_wts
    return pl.pallas_call(
        _merge_body,
        grid=(b, t // tm - skip),
        in_specs=[row(d),
                  pl.BlockSpec((None, None, 6, d),
                               lambda bi, ti: (bi, jnp.where(ti + skip >= n_ctx_tiles, 1, 0), 0, 0)),
                  row(d), row(512), row(512), row(512), row(S5_WIDTH), row(S5_WIDTH),
                  const((4, d, d)), const((512, d)), const((512, d)), const((512, d)),
                  const((S5_WIDTH, d)), const((S5_WIDTH, 2 * S5_WIDTH)), const((1, 2 * S5_WIDTH)),
                  const((1, S5_WIDTH)), const((d, d)),
                  const((1, d)), const((d, 2 * FFN_HIDDEN)), const((FFN_HIDDEN, d))],
        out_specs=pl.BlockSpec((None, tm, d), lambda bi, ti: (bi, ti, 0)),
        out_shape=jax.ShapeDtypeStruct((b, t - skip * tm, d), F32),
        compiler_params=_params(2),
        name="merge_ffn",
    )(h, modv, xn, ya, yb, yc, u, y5, w_gate, w_a, w_b, w_c, w_d, glu_w, glu_b, s5_d, w_out,
      g2, w_gu, w_down)


def _rope_tables(ctx_len, seq_len):
    n_rows = seq_len // GRID_W
    rows = jnp.repeat(jnp.arange(n_rows, dtype=F32), GRID_W)
    cols = jnp.tile(jnp.arange(GRID_W, dtype=F32), n_rows)
    quarter = HEAD_DIM // 4
    inv_freq = ROPE_THETA ** (-jnp.arange(quarter, dtype=F32) / quarter)
    ang_r = rows[:, None] * inv_freq
    ang_c = cols[:, None] * inv_freq
    ang = jnp.concatenate([ang_r, ang_r, ang_c, ang_c], axis=-1)
    cos = jnp.concatenate([jnp.ones((ctx_len, HEAD_DIM), F32), jnp.cos(ang)], axis=0)
    sin = jnp.concatenate([jnp.zeros((ctx_len, HEAD_DIM), F32), jnp.sin(ang)], axis=0)
    first = (jnp.arange(HEAD_DIM) % 32) < 16
    sin_a = jnp.where(first, -sin, 0.0)
    sin_b = jnp.where(first, 0.0, sin)
    two = lambda m: jnp.concatenate([m, m], axis=1)
    return jnp.concatenate([two(cos), two(sin_a), two(sin_b)], axis=1)


def _w_in_layout(w_in):
    d = w_in.shape[0]
    a0 = 0
    z = w_in[:, a0:a0 + 512]
    xbc = w_in[:, a0 + 512:a0 + 1280]
    dt = jnp.pad(w_in[:, a0 + 1280:a0 + 1296], ((0, 0), (0, SSD_DT_PAD - 16)))
    rest = w_in[:, 1296:]
    out = jnp.concatenate([z, xbc, dt, rest], axis=1).astype(BF16)
    assert out.shape == (d, _C_END)
    return out


def kernel(x, c, ctx, c_ctx, w_mod, b_mod, norm1_g, norm2_g, w_in, ssd_conv_w, ssd_conv_b, ssd_a_log, ssd_dt_bias, ssd_d, ssd_norm_g, diff_qn_g, diff_kn_g, diff_lam_q1, diff_lam_k1, diff_lam_q2, diff_lam_k2, diff_subln_g, gqa_qn_g, gqa_kn_g, s5_lam_re, s5_lam_im, s5_log_dt, s5_b_re, s5_b_im, s5_c_re, s5_c_im, s5_d, s5_glu_w, s5_glu_b, w_gate, w_br_ssd, w_br_diff, w_br_gqa, w_br_s5, w_out, ffn_w_gate_up, ffn_w_down):
    b, seq_len, d = x.shape
    ctx_len = ctx.shape[1]
    depth = w_mod.shape[0]
    assert b + 1 <= 8
    assert ctx_len % ROW_TILE == 0 and seq_len % ROW_TILE == 0
    n_ctx_tiles = ctx_len // ROW_TILE

    h = jnp.concatenate([ctx, x], axis=1)
    cc = jnp.concatenate([c, c_ctx[None], jnp.zeros((8 - b - 1, d), F32)], axis=0)
    mods = _mod_call(cc, w_mod, b_mod)

    rope_tab = _rope_tables(ctx_len, seq_len)
    blk = jnp.arange(512) // HEAD_DIM
    ones = (blk[:, None] == blk[None, :]).astype(BF16)
    tile8 = lambda g: jnp.tile(g.astype(F32), 512 // HEAD_DIM)

    for layer in range(depth):
        m = mods[layer]
        lat = m[:b].reshape(b, 6, d)
        cmod = jnp.broadcast_to(m[b].reshape(1, 6, d), (b, 6, d))
        modv = jnp.stack([cmod, lat], axis=1)

        gains = jnp.stack([tile8(diff_qn_g[layer]), tile8(diff_kn_g[layer]),
                           tile8(gqa_qn_g[layer]), tile8(gqa_kn_g[layer])]
                          + [jnp.zeros((512,), F32)] * 4, axis=0)
        conv_w = jnp.pad(ssd_conv_w[layer].astype(F32), ((0, 8 - SSD_CONV_W), (0, 0)))
        (xn, z, xbc_act, dt, qd, kd, vd, qg, kg, vg, u) = _in_proj_call(
            h, modv, norm1_g[layer][None], _w_in_layout(w_in[layer]), ones, gains, rope_tab,
            conv_w, ssd_conv_b[layer][None].astype(F32), n_ctx_tiles)

        pad16 = lambda v: jnp.pad(v.reshape(1, 16).astype(F32), ((0, 0), (0, LANES - 16)))
        aneg = pad16(-jnp.exp(ssd_a_log[layer].astype(F32)))
        dtb = pad16(ssd_dt_bias[layer])
        dsk = jnp.repeat(ssd_d[layer].astype(F32), SSD_HEAD_DIM)[None]
        ya = _ssd_call(xbc_act, dt, z, aneg, dtb, dsk, ssd_norm_g[layer][None].astype(F32),
                       ctx_len // SSD_CHUNK)

        lam_init = 0.8 - 0.6 * math.exp(-0.3 * layer)
        lamv = jnp.pad(jnp.stack([diff_lam_q1[layer], diff_lam_k1[layer],
                                  diff_lam_q2[layer], diff_lam_k2[layer]]).astype(F32),
                       ((0, 0), (0, LANES - HEAD_DIM)))
        yb = _diff_attn_call(qd, kd, vd, lamv, diff_subln_g[layer][None].astype(F32), lam_init, ctx_len)

        yc = _gqa_attn_call(qg, kg, vg, ctx_len)

        s5w = _s5_weights_call(s5_lam_re[layer], s5_lam_im[layer], s5_log_dt[layer],
                               s5_b_re[layer], s5_b_im[layer], s5_c_re[layer], s5_c_im[layer])
        y5 = _s5_from_groups(_s5_call(_s5_to_groups(u), *s5w, ctx_len // S5_CHUNK,
                                      (ctx_len + seq_len) // S5_CHUNK), b)

        wts = (w_gate[layer].astype(BF16), w_br_ssd[layer].astype(BF16), w_br_diff[layer].astype(BF16),
               w_br_gqa[layer].astype(BF16), w_br_s5[layer].astype(BF16), s5_glu_w[layer].astype(BF16),
               s5_glu_b[layer][None].astype(F32), s5_d[layer][None].astype(F32), w_out[layer].astype(BF16))
        ffn_wts = (norm2_g[layer][None].astype(F32), ffn_w_gate_up[layer].astype(BF16),
                   ffn_w_down[layer].astype(BF16))
        h = _merge_call(h, modv, xn, ya, yb, yc, u, y5, wts, ffn_wts, n_ctx_tiles,
                        latent_only=layer == depth - 1)
    return h
```

```python
import functools
import math

import jax
import jax.numpy as jnp
from jax import lax
from jax.experimental import pallas as pl
from jax.experimental.pallas import tpu as pltpu

F32 = jnp.float32
BF16 = jnp.bfloat16
HIGHEST = lax.Precision.HIGHEST

D_MODEL = 1024
GRID_W = 64
ROPE_THETA = 10000.0
NORM_EPS = 1e-6

SSD_INNER = 512
SSD_HEADS = 8
SSD_HEAD_DIM = 64
SSD_STATE = 64
SSD_CHUNK = 128
SSD_CONV_W = 5
SSD_CONV_CH = 768
SSD_DT_PAD = 128
SSD_BATCH_PER_STEP = 4

DIFF_HEADS = 4
HEAD_DIM = 64
GQA_HEADS = 8
GQA_KV_HEADS = 2

S5_GROUP_CH = 16
S5_STATE = 64
S5_WIDTH = 384
S5_GROUPS = 24
S5_CHUNK = 64
S5_FLAT = S5_CHUNK * S5_GROUP_CH

FFN_HIDDEN = 2816

ROW_TILE = 256
LANES = 128
VMEM_LIMIT = 56 * 1024 * 1024

_C_Z, _C_XBC, _C_DT = 0, 512, 1280
_C_QD, _C_KD, _C_VD = 1408, 1920, 2432
_C_QG, _C_KG, _C_VG = 2944, 3456, 3584
_C_U, _C_END = 3712, 4096


def _params(n_grid):
    return pltpu.CompilerParams(dimension_semantics=("arbitrary",) * n_grid,
                                vmem_limit_bytes=VMEM_LIMIT)


def _bdot(a, b):
    return jnp.dot(a.astype(BF16), b.astype(BF16), preferred_element_type=F32)


def _hdot(a, b):
    return jnp.dot(a, b, precision=HIGHEST, preferred_element_type=F32)


def _rms(x, g):
    return x * lax.rsqrt(jnp.mean(x * x, axis=-1, keepdims=True) + NORM_EPS) * g


def _sigmoid(x):
    return 1.0 / (1.0 + jnp.exp(-x))


def _silu(x):
    return x * _sigmoid(x)


def _mod_body(c_ref, w_ref, b_ref, o_ref):
    o_ref[...] = _hdot(_silu(c_ref[...]), w_ref[...]) + b_ref[...]


def _mod_call(cc, w_mod, b_mod):
    depth = w_mod.shape[0]
    d = D_MODEL
    return pl.pallas_call(
        _mod_body,
        grid=(depth, 6),
        in_specs=[pl.BlockSpec((8, d), lambda l, j: (0, 0)),
                  pl.BlockSpec((None, d, d), lambda l, j: (l, 0, j)),
                  pl.BlockSpec((None, 1, d), lambda l, j: (l, 0, j))],
        out_specs=pl.BlockSpec((None, 8, d), lambda l, j: (l, 0, j)),
        out_shape=jax.ShapeDtypeStruct((depth, 8, 6 * d), F32),
        compiler_params=_params(2),
        name="adaln_mod",
    )(cc, w_mod, b_mod.reshape(depth, 1, 6 * d))


def _head_rms(t, gain, ones):
    t2 = (t * t).astype(BF16)
    w = t.shape[-1]
    blk = min(w, 256)
    ss = jnp.concatenate([jnp.dot(t2[:, i:i + blk], ones[i:i + blk, i:i + blk], preferred_element_type=F32)
                          for i in range(0, w, blk)], axis=1)
    return t * lax.rsqrt(ss * (1.0 / HEAD_DIM) + NORM_EPS) * gain


def _rope(t, cos, sin_a, sin_b):
    w = t.shape[-1]
    return t * cos + pltpu.roll(t, w - 16, 1) * sin_a + pltpu.roll(t, 16, 1) * sin_b


def _in_proj_body(h_ref, hp_ref, hn_ref, mod_ref, g_ref, w_ref, ones_ref, gains_ref, rope_ref,
                  cw_ref, cb_ref, xn_ref, z_ref, xbc_ref, dt_ref, qd_ref, kd_ref, vd_ref,
                  qg_ref, kg_ref, vg_ref, u_ref, *, n_ctx_tiles, n_tiles):
    t = pl.program_id(1)
    tm = h_ref.shape[0]
    mod = mod_ref[...]

    def normed(v):
        return _rms(v, g_ref[...]) * (1.0 + mod[1:2]) + mod[0:1]

    xn = normed(h_ref[...])
    xb = xn.astype(BF16)
    xn_ref[...] = xb

    def proj(lo, hi):
        return jnp.dot(xb, w_ref[:, lo:hi], preferred_element_type=F32)

    z_ref[...] = proj(_C_Z, _C_XBC)
    dt_ref[...] = proj(_C_DT, _C_QD)

    has_prev = jnp.logical_and(t != 0, t != n_ctx_tiles)
    has_next = jnp.logical_and(t != n_ctx_tiles - 1, t != n_tiles - 1)
    slab = jnp.concatenate([normed(hp_ref[...]), xn, normed(hn_ref[...])], axis=0).astype(BF16)
    full = jnp.dot(slab, w_ref[:, _C_XBC:_C_DT], preferred_element_type=F32)
    srow = lax.broadcasted_iota(jnp.int32, (tm + 16, 1), 0)
    valid = jnp.logical_and(jnp.logical_or(has_prev, srow >= 8), jnp.logical_or(has_next, srow < tm + 8))
    full = jnp.where(valid, full, 0.0)
    cw = cw_ref[...]
    pad = (SSD_CONV_W - 1) // 2
    n = tm + 16
    acc = cb_ref[...] + cw[pad:pad + 1] * full[8:8 + tm]
    for k in range(SSD_CONV_W):
        if k != pad:
            acc = acc + cw[k:k + 1] * pltpu.roll(full, (pad - k) % n, 0)[8:8 + tm]
    xbc_ref[...] = _silu(acc)
    u_ref[...] = proj(_C_U, _C_END).astype(u_ref.dtype)

    rope = rope_ref[...]
    cos1, sa1, sb1 = rope[:, 0:128], rope[:, 128:256], rope[:, 256:384]
    cos4 = jnp.concatenate([cos1] * 4, axis=1)
    sa4 = jnp.concatenate([sa1] * 4, axis=1)
    sb4 = jnp.concatenate([sb1] * 4, axis=1)
    ones = ones_ref[...]
    gains = gains_ref[...]
    lane = lax.broadcasted_iota(jnp.int32, (1, LANES), 1)
    lo_half = lane < HEAD_DIM
    scale = HEAD_DIM ** -0.5 * math.log2(math.e)

    qd = _rope(_head_rms(proj(_C_QD, _C_KD), gains[0:1], ones), cos4, sa4, sb4) * scale
    kd = _rope(_head_rms(proj(_C_KD, _C_VD), gains[1:2], ones), cos4, sa4, sb4)
    kd_ref[...] = kd.astype(BF16)
    vd_ref[...] = proj(_C_VD, _C_QG).astype(BF16)
    for h in range(DIFF_HEADS):
        blk = qd[:, h * LANES:(h + 1) * LANES]
        qd_ref[h, 0] = jnp.where(lo_half, blk, 0.0).astype(BF16)
        qd_ref[h, 1] = jnp.where(lo_half, 0.0, blk).astype(BF16)

    qg = _rope(_head_rms(proj(_C_QG, _C_KG), gains[2:3], ones), cos4, sa4, sb4) * scale
    kg = _rope(_head_rms(proj(_C_KG, _C_VG), gains[3:4, :LANES], ones[:LANES, :LANES]),
               cos1, sa1, sb1)
    kg_ref[...] = kg.astype(BF16)
    vg_ref[...] = proj(_C_VG, _C_U).astype(BF16)
    qg_up = pltpu.roll(qg, 4 * LANES - HEAD_DIM, 1)
    qg_dn = pltpu.roll(qg, HEAD_DIM, 1)
    per_kv = GQA_HEADS // GQA_KV_HEADS
    for n in range(GQA_KV_HEADS):
        for i in range(per_kv):
            j = n * per_kv + i
            blk_idx = j // 2
            if (j % 2) == n:
                src = qg
            else:
                src = qg_up if n == 0 else qg_dn
            blk = src[:, blk_idx * LANES:(blk_idx + 1) * LANES]
            keep = lo_half if n == 0 else jnp.logical_not(lo_half)
            qg_ref[n, i] = jnp.where(keep, blk, 0.0).astype(BF16)


def _in_proj_call(h, modv, g1, w_cat, ones, gains, rope_tab, conv_w, conv_b, n_ctx_tiles):
    b, t, d = h.shape
    nt = t // ROW_TILE
    tm = ROW_TILE

    def row(width, dtype):
        return (pl.BlockSpec((None, tm, width), lambda bi, ti: (bi, ti, 0)),
                jax.ShapeDtypeStruct((b, t, width), dtype))

    outs = [row(d, BF16), row(512, F32), row(768, F32), row(SSD_DT_PAD, F32)]
    qd = (pl.BlockSpec((None, DIFF_HEADS, 2, tm, LANES), lambda bi, ti: (bi, 0, 0, ti, 0)),
          jax.ShapeDtypeStruct((b, DIFF_HEADS, 2, t, LANES), BF16))
    qg = (pl.BlockSpec((None, GQA_KV_HEADS, 4, tm, LANES), lambda bi, ti: (bi, 0, 0, ti, 0)),
          jax.ShapeDtypeStruct((b, GQA_KV_HEADS, 4, t, LANES), BF16))
    outs += [qd, row(512, BF16), row(512, BF16), qg, row(LANES, BF16), row(LANES, BF16),
             row(S5_WIDTH, BF16)]
    per = tm // 8
    last8 = t // 8 - 1
    return pl.pallas_call(
        functools.partial(_in_proj_body, n_ctx_tiles=n_ctx_tiles, n_tiles=nt),
        grid=(b, nt),
        in_specs=[
            pl.BlockSpec((None, tm, d), lambda bi, ti: (bi, ti, 0)),
            pl.BlockSpec((None, 8, d), lambda bi, ti: (bi, jnp.maximum(ti * per - 1, 0), 0)),
            pl.BlockSpec((None, 8, d), lambda bi, ti: (bi, jnp.minimum((ti + 1) * per, last8), 0)),
            pl.BlockSpec((None, None, 6, d),
                         lambda bi, ti: (bi, jnp.where(ti >= n_ctx_tiles, 1, 0), 0, 0)),
            pl.BlockSpec((1, d), lambda bi, ti: (0, 0)),
            pl.BlockSpec((d, _C_END), lambda bi, ti: (0, 0)),
            pl.BlockSpec((512, 512), lambda bi, ti: (0, 0)),
            pl.BlockSpec((8, 512), lambda bi, ti: (0, 0)),
            pl.BlockSpec((tm, 384), lambda bi, ti: (ti, 0)),
            pl.BlockSpec((8, SSD_CONV_CH), lambda bi, ti: (0, 0)),
            pl.BlockSpec((1, SSD_CONV_CH), lambda bi, ti: (0, 0)),
        ],
        out_specs=[o[0] for o in outs],
        out_shape=[o[1] for o in outs],
        compiler_params=_params(2),
        name="in_proj",
    )(h, h, h, modv, g1, w_cat, ones, gains, rope_tab, conv_w, conv_b)


def _ssd_chunk_index(d, s, n_ctx_chunks, n_chunks):
    bwd = jnp.where(s < n_ctx_chunks, n_ctx_chunks - 1 - s, n_chunks - 1 - (s - n_ctx_chunks))
    return jnp.where(d == 0, s, bwd)


def _ssd_body(xbc_ref, dt_ref, z_ref, aneg_ref, dtb_ref, dsk_ref, ng_ref, o_ref,
              hst_ref, ysc_ref, *, n_ctx_chunks, n_chunks):
    d = pl.program_id(1)
    s = pl.program_id(2)
    c = _ssd_chunk_index(d, s, n_ctx_chunks, n_chunks)
    q = SSD_CHUNK
    nb = xbc_ref.shape[0]
    r = nb * q

    @pl.when(s == 0)
    def _():
        hst_ref[...] = jnp.zeros_like(hst_ref)

    def run(direction):
        xbc = xbc_ref[...].reshape(r, SSD_CONV_CH)
        x = xbc[:, 0:512]
        bm = xbc[:, 512:640]
        cm = xbc[:, 640:768]
        pre = dt_ref[...].reshape(r, SSD_DT_PAD) + dtb_ref[...]
        dtv = jnp.maximum(pre, 0.0) + jnp.log(1.0 + jnp.exp(-jnp.abs(pre)))
        a = dtv * aneg_ref[...]
        row = lax.broadcasted_iota(jnp.int32, (r, r), 0)
        col = lax.broadcasted_iota(jnp.int32, (r, r), 1)
        order = (row >= col) if direction == 0 else (row <= col)
        tri = jnp.logical_and(row // q == col // q, order)
        a_hi = a.astype(BF16)
        r1 = a - a_hi.astype(F32)
        a_mid = r1.astype(BF16)
        a_lo = (r1 - a_mid.astype(F32)).astype(BF16)
        terms = [t[bi * q:(bi + 1) * q] for bi in range(nb) for t in (a_hi, a_mid, a_lo)]
        parts = jnp.dot(tri[0:q, 0:q].astype(BF16), jnp.concatenate(terms, axis=1),
                        preferred_element_type=F32)
        acs = jnp.concatenate(
            [parts[:, (3 * bi) * LANES:(3 * bi + 1) * LANES] + parts[:, (3 * bi + 1) * LANES:(3 * bi + 2) * LANES]
             + parts[:, (3 * bi + 2) * LANES:(3 * bi + 3) * LANES] for bi in range(nb)], axis=0)
        acs_t = acs.T
        edge = q - 1 if direction == 0 else 0
        lasts = [acs[bi * q + edge:bi * q + edge + 1, :] for bi in range(nb)]
        last = jnp.concatenate([jnp.broadcast_to(v, (q, LANES)) for v in lasts], axis=0)
        dte = jnp.exp(last - acs)
        expa = jnp.exp(acs)
        cdec = jnp.concatenate([jnp.broadcast_to(jnp.exp(v), (8, LANES)) for v in lasts], axis=0)
        erow = lax.broadcasted_iota(jnp.int32, (LANES, SSD_INNER), 0)
        ecol = lax.broadcasted_iota(jnp.int32, (LANES, SSD_INNER), 1)
        expand = (erow == ecol // SSD_HEAD_DIM + SSD_HEADS * direction).astype(BF16)
        spread = jnp.dot(jnp.concatenate([dtv, dtv * dte], axis=0).astype(BF16), expand,
                         preferred_element_type=F32)
        xdt = x * spread[0:r]
        xs = x * spread[r:2 * r]
        fine = jnp.concatenate([expa, cdec], axis=0)
        fine_hi = fine.astype(BF16)
        fine_lo = (fine - fine_hi.astype(F32)).astype(BF16)
        spread2 = (jnp.dot(fine_hi, expand, preferred_element_type=F32)
                   + jnp.dot(fine_lo, expand, preferred_element_type=F32))
        expa_x = spread2[0:r]
        bt = bm.T.astype(BF16)
        lane = lax.broadcasted_iota(jnp.int32, (1, LANES), 1)
        tri_q = tri[0:q, 0:q]
        ys = [[None] * nb for _ in range(4)]
        for g in range(2):
            cg = jnp.where(lane // SSD_STATE == g, cm, 0.0).astype(BF16)
            gmats = [jnp.dot(cg[bi * q:(bi + 1) * q], bt[:, bi * q:(bi + 1) * q], preferred_element_type=F32)
                     for bi in range(nb)]
            for hp in range(2):
                p = g * 2 + hp
                sl = slice(p * LANES, (p + 1) * LANES)
                for bi in range(nb):
                    rows = slice(bi * q, (bi + 1) * q)
                    ypair = None
                    for e in range(2):
                        j = SSD_HEADS * direction + 2 * p + e
                        seg = acs[rows, j:j + 1] - acs_t[j:j + 1, rows]
                        dec = jnp.exp(jnp.where(tri_q, seg, -jnp.inf))
                        xh = jnp.where(lane // SSD_HEAD_DIM == e, xdt[rows, sl], 0.0)
                        term = _bdot(gmats[bi] * dec, xh)
                        ypair = term if ypair is None else ypair + term
                    hprev = hst_ref[bi, p]
                    ypair = ypair + _bdot(cg[rows], hprev) * expa_x[rows, sl]
                    hst_ref[bi, p] = hprev * spread2[r + 8 * bi:r + 8 * bi + 1, sl] + jnp.dot(
                        bt[:, rows], xs[rows, sl].astype(BF16), preferred_element_type=F32)
                    ys[p][bi] = ypair
        ys = [jnp.concatenate(v, axis=0) for v in ys]
        return x, ys

    @pl.when(d == 0)
    def _():
        _, ys = run(0)
        for p in range(4):
            ysc_ref[c, :, p * LANES:(p + 1) * LANES] = ys[p].astype(ysc_ref.dtype)

    @pl.when(d == 1)
    def _():
        x, ys = run(1)
        y = jnp.concatenate(ys, axis=1) + ysc_ref[c].astype(F32) + dsk_ref[...] * x
        y = y * _silu(z_ref[...].reshape(r, SSD_INNER))
        o_ref[...] = _rms(y, ng_ref[...]).astype(o_ref.dtype).reshape(nb, q, SSD_INNER)


def _ssd_call(xbc_act, dt, z, aneg, dtb, dsk, ng, n_ctx_chunks):
    b, t, _ = xbc_act.shape
    q = SSD_CHUNK
    nc = t // q
    nb = SSD_BATCH_PER_STEP if b % SSD_BATCH_PER_STEP == 0 else 1
    cidx = functools.partial(_ssd_chunk_index, n_ctx_chunks=n_ctx_chunks, n_chunks=nc)

    def late(bi, di, si):
        return (bi, jnp.where(di == 0, n_ctx_chunks - 1, cidx(di, si)), 0)

    return pl.pallas_call(
        functools.partial(_ssd_body, n_ctx_chunks=n_ctx_chunks, n_chunks=nc),
        grid=(b // nb, 2, nc),
        in_specs=[
            pl.BlockSpec((nb, q, SSD_CONV_CH), lambda bi, di, si: (bi, cidx(di, si), 0)),
            pl.BlockSpec((nb, q, SSD_DT_PAD), lambda bi, di, si: (bi, cidx(di, si), 0)),
            pl.BlockSpec((nb, q, SSD_INNER), late),
            pl.BlockSpec((1, LANES), lambda bi, di, si: (0, 0)),
            pl.BlockSpec((1, LANES), lambda bi, di, si: (0, 0)),
            pl.BlockSpec((1, SSD_INNER), lambda bi, di, si: (0, 0)),
            pl.BlockSpec((1, SSD_INNER), lambda bi, di, si: (0, 0)),
        ],
        out_specs=pl.BlockSpec((nb, q, SSD_INNER), late),
        out_shape=jax.ShapeDtypeStruct((b, t, SSD_INNER), BF16),
        scratch_shapes=[pltpu.VMEM((nb, 4, LANES, LANES), F32),
                        pltpu.VMEM((nc, nb * q, SSD_INNER), BF16)],
        compiler_params=_params(3),
        name="ssd_scan",
    )(xbc_act, dt, z, aneg, dtb, dsk, ng)


def _flash(qs, k_ref, v_ref, m_ref, l_ref, acc_ref, *, tk, n_kv, ctx_len, is_ctx_tile):
    m_ref[...] = jnp.full(m_ref.shape, -jnp.inf, F32)
    l_ref[...] = jnp.zeros(l_ref.shape, F32)
    acc_ref[...] = jnp.zeros(acc_ref.shape, F32)

    def step(start, size, limit):
        for i, q in enumerate(qs):
            lanes = slice(i * LANES, (i + 1) * LANES)
            k = k_ref[start:start + size, lanes]
            v = v_ref[start:start + size, lanes]
            s = lax.dot_general(q, k, (((1,), (1,)), ((), ())), preferred_element_type=F32)
            if limit is not None:
                col = lax.broadcasted_iota(jnp.int32, (1, size), 1) + start
                s = jnp.where(col < limit, s, -jnp.inf)
            m_old = m_ref[i]
            m_new = jnp.maximum(m_old, jnp.max(s, axis=-1, keepdims=True))
            alpha = jnp.exp2(m_old - m_new)
            p = jnp.exp2(s - jnp.concatenate([m_new] * (size // LANES), axis=1))
            psum = p[:, 0:LANES]
            for j in range(1, size // LANES):
                psum = psum + p[:, j * LANES:(j + 1) * LANES]
            l_ref[i] = alpha * l_ref[i] + psum
            acc_ref[i] = alpha * acc_ref[i] + jnp.dot(p.astype(BF16), v, preferred_element_type=F32)
            m_ref[i] = m_new

    @pl.when(is_ctx_tile)
    def _():
        ctx_pad = -(-ctx_len // LANES) * LANES
        for start in range(0, ctx_pad, tk):
            step(start, min(tk, ctx_pad - start), ctx_len if ctx_pad != ctx_len else None)

    @pl.when(jnp.logical_not(is_ctx_tile))
    def _():
        for kc in range(n_kv):
            step(kc * tk, tk, None)

    return [acc_ref[i] * (1.0 / jnp.sum(l_ref[i], axis=-1, keepdims=True)) for i in range(len(qs))]


DIFF_HEADS_PER_STEP = 2


def _diff_attn_body(q_ref, k_ref, v_ref, lam_ref, g_ref, o_ref, m_ref, l_ref, acc_ref,
                    *, tq, lam_init, n_ctx_q, **kw):
    qi = pl.program_id(2)
    qs = [q_ref[i].reshape(2 * tq, LANES) for i in range(DIFF_HEADS_PER_STEP)]
    outs = _flash(qs, k_ref, v_ref, m_ref, l_ref, acc_ref, is_ctx_tile=qi < n_ctx_q, **kw)
    lv = lam_ref[...]
    lam = (jnp.exp(jnp.sum(lv[0:1] * lv[1:2], axis=-1, keepdims=True))
           - jnp.exp(jnp.sum(lv[2:3] * lv[3:4], axis=-1, keepdims=True)) + lam_init)
    for i, o in enumerate(outs):
        out = o[0:tq] - lam * o[tq:2 * tq]
        o_ref[:, i * LANES:(i + 1) * LANES] = (_rms(out, g_ref[...]) * (1.0 - lam_init)).astype(o_ref.dtype)


def _gqa_attn_body(q_ref, k_ref, v_ref, o_ref, m_ref, l_ref, acc_ref, *, tq, n_ctx_q, **kw):
    n = pl.program_id(1)
    qi = pl.program_id(2)
    q = q_ref[...].reshape(4 * tq, LANES)
    (o,) = _flash([q], k_ref, v_ref, m_ref, l_ref, acc_ref, is_ctx_tile=qi < n_ctx_q, **kw)
    lane = lax.broadcasted_iota(jnp.int32, (1, LANES), 1)
    lo_half = lane < HEAD_DIM
    for pair in range(2):
        a = o[(2 * pair) * tq:(2 * pair + 1) * tq]
        b = o[(2 * pair + 1) * tq:(2 * pair + 2) * tq]
        from_lo = jnp.where(lo_half, a, pltpu.roll(b, HEAD_DIM, 1))
        from_hi = jnp.where(lo_half, pltpu.roll(a, HEAD_DIM, 1), b)
        o_ref[:, pair * LANES:(pair + 1) * LANES] = jnp.where(n == 0, from_lo, from_hi).astype(o_ref.dtype)


def _attn_tiles(t, ctx_len):
    tq = ROW_TILE
    tk = next(c for c in (2816, 768, 256) if t % c == 0)
    assert ctx_len % tq == 0 and t % tq == 0 and t % tk == 0
    return dict(tq=tq, tk=tk, n_kv=t // tk, ctx_len=ctx_len, n_ctx_q=ctx_len // tq)


def _diff_attn_call(qd, kd, vd, lamv, subln_g, lam_init, ctx_len):
    b, _, _, t, _ = qd.shape
    cfg = _attn_tiles(t, ctx_len)
    tq = cfg["tq"]
    r = 2 * tq
    hs = DIFF_HEADS_PER_STEP
    return pl.pallas_call(
        functools.partial(_diff_attn_body, lam_init=lam_init, **cfg),
        grid=(b, DIFF_HEADS // hs, t // tq),
        in_specs=[
            pl.BlockSpec((None, hs, 2, tq, LANES), lambda bi, hi, qi: (bi, hi, 0, qi, 0)),
            pl.BlockSpec((None, t, hs * LANES), lambda bi, hi, qi: (bi, 0, hi)),
            pl.BlockSpec((None, t, hs * LANES), lambda bi, hi, qi: (bi, 0, hi)),
            pl.BlockSpec((4, LANES), lambda bi, hi, qi: (0, 0)),
            pl.BlockSpec((1, LANES), lambda bi, hi, qi: (0, 0)),
        ],
        out_specs=pl.BlockSpec((None, tq, hs * LANES), lambda bi, hi, qi: (bi, qi, hi)),
        out_shape=jax.ShapeDtypeStruct((b, t, DIFF_HEADS * LANES), BF16),
        scratch_shapes=[pltpu.VMEM((hs, r, LANES), F32)] * 3,
        compiler_params=_params(3),
        name="diff_attention",
    )(qd, kd, vd, lamv, subln_g)


def _gqa_attn_call(qg, kg, vg, ctx_len):
    b, _, _, t, _ = qg.shape
    cfg = _attn_tiles(t, ctx_len)
    tq = cfg["tq"]
    r = 4 * tq
    return pl.pallas_call(
        functools.partial(_gqa_attn_body, **cfg),
        grid=(b, GQA_KV_HEADS, t // tq),
        in_specs=[
            pl.BlockSpec((None, None, 4, tq, LANES), lambda bi, ni, qi: (bi, ni, 0, qi, 0)),
            pl.BlockSpec((None, t, LANES), lambda bi, ni, qi: (bi, 0, 0)),
            pl.BlockSpec((None, t, LANES), lambda bi, ni, qi: (bi, 0, 0)),
        ],
        out_specs=pl.BlockSpec((None, tq, 2 * LANES), lambda bi, ni, qi: (bi, qi, ni)),
        out_shape=jax.ShapeDtypeStruct((b, t, GQA_HEADS * HEAD_DIM), BF16),
        scratch_shapes=[pltpu.VMEM((1, r, LANES), F32)] * 3,
        compiler_params=_params(3),
        name="gqa_attention",
    )(qg, kg, vg)


def _cmul(ar, ai, br, bi):
    return ar * br - ai * bi, ar * bi + ai * br


def _stack_rows(er, ei, vr, vi):
    re = [er * vr[c:c + 1] - ei * vi[c:c + 1] for c in range(S5_GROUP_CH)]
    im = [er * vi[c:c + 1] + ei * vr[c:c + 1] for c in range(S5_GROUP_CH)]
    return jnp.concatenate(re, axis=0), jnp.concatenate(im, axis=0)


def _split_dot_t(a, b):
    dn = (((1,), (1,)), ((), ()))
    ah = a.astype(BF16)
    al = (a - ah.astype(F32)).astype(BF16)
    bh = b.astype(BF16)
    bl = (b - bh.astype(F32)).astype(BF16)
    return lax.dot_general(jnp.concatenate([ah, ah, al], axis=1), jnp.concatenate([bh, bl, bh], axis=1),
                           dn, preferred_element_type=F32)


def _s5_weights_body(lam_ref, bt_ref, c_ref, wi_ref, ws_ref, wo_ref, av_ref):
    tc = S5_CHUNK
    lam = lam_ref[...]
    br_t, bi_t = bt_ref[0:16, :], bt_ref[16:32, :]
    cr, ci = c_ref[0:16, :], c_ref[16:32, :]
    kk = lax.broadcasted_iota(jnp.int32, (tc, S5_STATE), 0).astype(F32)
    k8 = lax.broadcasted_iota(jnp.int32, (8, S5_STATE), 0)
    k8 = jnp.where(k8 == 0, 1.0, jnp.where(k8 == 1, tc - 1.0, float(tc)))
    row = lax.broadcasted_iota(jnp.int32, (S5_FLAT, S5_FLAT), 0) % tc
    col = lax.broadcasted_iota(jnp.int32, (S5_FLAT, S5_FLAT), 1) % tc
    w_intra = None
    state_cols, out_cols, a_rows = [], [], []
    for direction in range(2):
        lr = lam[2 * direction:2 * direction + 1]
        li = lam[2 * direction + 1:2 * direction + 2]
        step = jnp.exp(lam[4 + direction:5 + direction])
        mag = jnp.exp(lr * step)
        ar, ai = mag * jnp.cos(li * step), mag * jnp.sin(li * step)
        den = lr * lr + li * li
        fr = ((ar - 1.0) * lr + ai * li) / den
        fi = (ai * lr - (ar - 1.0) * li) / den
        bbr = fr * br_t - fi * bi_t
        bbi = fr * bi_t + fi * br_t
        cs, sn = jnp.cos(kk * (li * step)), jnp.sin(kk * (li * step))
        grow, decay = jnp.exp(-kk * (lr * step)), jnp.exp(kk * (lr * step))
        pr, pi = decay * cs, decay * sn
        nr, ni = grow * cs, -grow * sn
        m8 = jnp.exp(k8 * (lr * step))
        c8r, c8i = m8 * jnp.cos(k8 * (li * step)), m8 * jnp.sin(k8 * (li * step))
        a_one = (c8r[0:1], c8i[0:1])
        a_last = (c8r[1:2], c8i[1:2])
        a_tc = (c8r[2:3], c8i[2:3])
        if direction == 0:
            x_e, y_e = (nr, ni), (pr, pi)
            s_e = _cmul(nr, ni, *a_last)
            o_e = _cmul(pr, pi, *a_one)
            keep = col >= row
        else:
            x_e, y_e = (pr, pi), (nr, ni)
            s_e = (pr, pi)
            o_e = _cmul(nr, ni, *a_tc)
            keep = row >= col
        xr, xi = _stack_rows(*x_e, bbr, bbi)
        yr, yi = _stack_rows(*y_e, cr, ci)
        full = _split_dot_t(jnp.concatenate([xr, -xi], axis=1), jnp.concatenate([yr, yi], axis=1))
        part = jnp.where(keep, full, 0.0)
        w_intra = part if w_intra is None else w_intra + part
        state_cols.append(_stack_rows(*s_e, bbr, bbi))
        o_r, o_i = _stack_rows(*o_e, cr, ci)
        out_cols += [o_r, -o_i]
        a_rows += [jnp.concatenate([a_tc[0], a_tc[0]], axis=1),
                   jnp.concatenate([-a_tc[1], a_tc[1]], axis=1),
                   jnp.concatenate([a_tc[1], -a_tc[1]], axis=1)]
    wi_ref[...] = w_intra.astype(BF16)
    (fr_, fi_), (br_, bi_) = state_cols
    ws_ref[...] = jnp.concatenate([fr_, fi_, br_, bi_, fi_, fr_, bi_, br_], axis=1).astype(BF16)
    wo_ref[...] = jnp.concatenate(out_cols, axis=1).astype(BF16)
    zero = jnp.zeros((1, LANES), F32)
    av_ref[...] = jnp.concatenate(a_rows + [zero, zero], axis=0)


def _s5_weights_call(lam_re, lam_im, log_dt, b_re, b_im, c_re, c_im):
    g, p = S5_GROUPS, S5_STATE
    bc = lambda v: jnp.broadcast_to(v.astype(F32)[:, None], (g, p))
    zero = jnp.zeros((g, p), F32)
    lam = jnp.stack([lam_re[0], lam_im[0], lam_re[1], lam_im[1], bc(log_dt[0]), bc(log_dt[1]),
                     zero, zero], axis=1).astype(F32)
    bt = jnp.concatenate([jnp.swapaxes(b_re, 1, 2), jnp.swapaxes(b_im, 1, 2)], axis=1).astype(F32)
    cc = jnp.concatenate([c_re, c_im], axis=1).astype(F32)
    n = S5_FLAT
    return pl.pallas_call(
        _s5_weights_body,
        grid=(g,),
        in_specs=[pl.BlockSpec((None, 8, p), lambda gi: (gi, 0, 0)),
                  pl.BlockSpec((None, 32, p), lambda gi: (gi, 0, 0)),
                  pl.BlockSpec((None, 32, p), lambda gi: (gi, 0, 0))],
        out_specs=[pl.BlockSpec((None, n, n), lambda gi: (gi, 0, 0)),
                   pl.BlockSpec((None, n, 4 * LANES), lambda gi: (gi, 0, 0)),
                   pl.BlockSpec((None, n, 2 * LANES), lambda gi: (gi, 0, 0)),
                   pl.BlockSpec((None, 8, LANES), lambda gi: (gi, 0, 0))],
        out_shape=[jax.ShapeDtypeStruct((g, n, n), BF16),
                   jax.ShapeDtypeStruct((g, n, 4 * LANES), BF16),
                   jax.ShapeDtypeStruct((g, n, 2 * LANES), BF16),
                   jax.ShapeDtypeStruct((g, 8, LANES), F32)],
        compiler_params=_params(1),
        name="s5_weights",
    )(lam, bt, cc)


def _s5_body(u_ref, wi_ref, ws_ref, wo_ref, av_ref, y_ref, s_ref, h_ref, *, n_ctx_chunks, n_chunks):
    u = u_ref[...]
    s_ref[...] = jnp.dot(u, ws_ref[...], preferred_element_type=F32)
    av = av_ref[...]
    nj = n_chunks
    rows = u_ref.shape[0] // nj
    a1f, a2f, a2sf, a1b, a2b, a2sb = [jnp.broadcast_to(av[i:i + 1], (rows, LANES)) for i in range(6)]
    hf = hfs = hb = hbs = jnp.zeros((rows, LANES), F32)
    for i in range(nj):
        jf = i * rows
        jb = (n_ctx_chunks - 1 - i if i < n_ctx_chunks else nj - 1 - (i - n_ctx_chunks)) * rows
        h_ref[jf:jf + rows, 0:LANES] = hf
        h_ref[jb:jb + rows, LANES:2 * LANES] = hb
        sf = s_ref[jf:jf + rows, 0:LANES]
        sfs = s_ref[jf:jf + rows, 2 * LANES:3 * LANES]
        sb = s_ref[jb:jb + rows, LANES:2 * LANES]
        sbs = s_ref[jb:jb + rows, 3 * LANES:4 * LANES]
        hf, hfs = a1f * hf + a2f * hfs + sf, a1f * hfs + a2sf * hf + sfs
        hb, hbs = a1b * hb + a2b * hbs + sb, a1b * hbs + a2sb * hb + sbs
    y = (jnp.dot(u, wi_ref[...], preferred_element_type=F32)
         + lax.dot_general(h_ref[...].astype(BF16), wo_ref[...], (((1,), (1,)), ((), ())),
                           preferred_element_type=F32))
    y_ref[...] = y.astype(y_ref.dtype)


def _s5_call(ug, w_intra, w_state, w_out, avec, n_ctx_chunks, n_chunks):
    g, r, w = ug.shape
    return pl.pallas_call(
        functools.partial(_s5_body, n_ctx_chunks=n_ctx_chunks, n_chunks=n_chunks),
        grid=(g,),
        in_specs=[
            pl.BlockSpec((None, r, w), lambda gi: (gi, 0, 0)),
            pl.BlockSpec((None, w, w), lambda gi: (gi, 0, 0)),
            pl.BlockSpec((None, w, 4 * LANES), lambda gi: (gi, 0, 0)),
            pl.BlockSpec((None, w, 2 * LANES), lambda gi: (gi, 0, 0)),
            pl.BlockSpec((None, 8, LANES), lambda gi: (gi, 0, 0)),
        ],
        out_specs=pl.BlockSpec((None, r, w), lambda gi: (gi, 0, 0)),
        out_shape=jax.ShapeDtypeStruct((g, r, w), BF16),
        scratch_shapes=[pltpu.VMEM((r, 4 * LANES), F32), pltpu.VMEM((r, 2 * LANES), F32)],
        compiler_params=_params(1),
        name="s5_scan",
    )(ug, w_intra, w_state, w_out, avec)


def _s5_to_groups(u):
    b, t, _ = u.shape
    nj = t // S5_CHUNK
    x = jnp.swapaxes(u.astype(BF16).reshape(b, nj, S5_CHUNK, S5_WIDTH), 2, 3)
    x = lax.optimization_barrier(x.reshape(b, nj, S5_GROUPS, S5_FLAT))
    return jnp.transpose(x, (2, 1, 0, 3)).reshape(S5_GROUPS, nj * b, S5_FLAT)


def _s5_from_groups(y, b):
    g, r, n = y.shape
    nj = r // b
    x = jnp.transpose(y.reshape(g, nj, b, n), (2, 1, 0, 3))
    x = lax.optimization_barrier(x).reshape(b, nj, S5_WIDTH, S5_CHUNK)
    return jnp.swapaxes(x, 2, 3).reshape(b, nj * S5_CHUNK, S5_WIDTH)


def _merge_body(h_ref, mod_ref, xn_ref, ya_ref, yb_ref, yc_ref, u_ref, y5_ref,
                wg_ref, wa_ref, wb_ref, wc_ref, wd_ref, glw_ref, glb_ref, s5d_ref, wo_ref,
                g2_ref, wgu_ref, wdn_ref, o_ref):
    xn = xn_ref[...]
    y5 = y5_ref[...].astype(F32) + s5d_ref[...] * u_ref[...].astype(F32)
    gelu = 0.5 * y5 * (1.0 + jnp.tanh(0.7978845608028654 * (y5 + 0.044715 * y5 * y5 * y5)))
    glu = _bdot(gelu, glw_ref[...]) + glb_ref[...]
    yd = glu[:, 0:S5_WIDTH] * _sigmoid(glu[:, S5_WIDTH:2 * S5_WIDTH])
    branches = ((ya_ref[...], wa_ref), (yb_ref[...], wb_ref), (yc_ref[...], wc_ref),
                (yd.astype(BF16), wd_ref))
    merged = None
    for i, (y, w_ref) in enumerate(branches):
        gate = _sigmoid(jnp.dot(xn, wg_ref[i], preferred_element_type=F32))
        term = gate * jnp.dot(y, w_ref[...], preferred_element_type=F32)
        merged = term if merged is None else merged + term
    out = _bdot(merged, wo_ref[...])
    h = h_ref[...] + mod_ref[2:3, :] * out
    mod = mod_ref[...]
    xf = (_rms(h, g2_ref[...]) * (1.0 + mod[4:5]) + mod[3:4]).astype(BF16)
    gate = jnp.dot(xf, wgu_ref[:, 0:FFN_HIDDEN], preferred_element_type=F32)
    up = jnp.dot(xf, wgu_ref[:, FFN_HIDDEN:2 * FFN_HIDDEN], preferred_element_type=F32)
    act = (_silu(gate) * up).astype(BF16)
    o_ref[...] = h + mod[5:6] * jnp.dot(act, wdn_ref[...], preferred_element_type=F32)


def _merge_call(h, modv, xn, ya, yb, yc, u, y5, wts, ffn_wts, n_ctx_tiles, latent_only):
    b, t, d = h.shape
    tm = ROW_TILE
    skip = n_ctx_tiles if latent_only else 0

    def row(width):
        return pl.BlockSpec((None, tm, width), lambda bi, ti: (bi, ti + skip, 0))

    def const(shape):
        return pl.BlockSpec(shape, lambda bi, ti: (0,) * len(shape), pipeline_mode=pl.Buffered(1))

    w_gate, w_a, w_b, w_c, w_d, glu_w, glu_b, s5_d, w_out = wts
    g2, w_gu, w_down = ffn_wts
    return pl.pallas_call(
        _merge_body,
        grid=(b, t // tm - skip),
        in_specs=[row(d),
                  pl.BlockSpec((None, None, 6, d),
                               lambda bi, ti: (bi, jnp.where(ti + skip >= n_ctx_tiles, 1, 0), 0, 0)),
                  row(d), row(512), row(512), row(512), row(S5_WIDTH), row(S5_WIDTH),
                  const((4, d, d)), const((512, d)), const((512, d)), const((512, d)),
                  const((S5_WIDTH, d)), const((S5_WIDTH, 2 * S5_WIDTH)), const((1, 2 * S5_WIDTH)),
                  const((1, S5_WIDTH)), const((d, d)),
                  const((1, d)), const((d, 2 * FFN_HIDDEN)), const((FFN_HIDDEN, d))],
        out_specs=pl.BlockSpec((None, tm, d), lambda bi, ti: (bi, ti, 0)),
        out_shape=jax.ShapeDtypeStruct((b, t - skip * tm, d), F32),
        compiler_params=_params(2),
        name="merge_ffn",
    )(h, modv, xn, ya, yb, yc, u, y5, w_gate, w_a, w_b, w_c, w_d, glu_w, glu_b, s5_d, w_out,
      g2, w_gu, w_down)


def _rope_tables(ctx_len, seq_len):
    n_rows = seq_len // GRID_W
    rows = jnp.repeat(jnp.arange(n_rows, dtype=F32), GRID_W)
    cols = jnp.tile(jnp.arange(GRID_W, dtype=F32), n_rows)
    quarter = HEAD_DIM // 4
    inv_freq = ROPE_THETA ** (-jnp.arange(quarter, dtype=F32) / quarter)
    ang_r = rows[:, None] * inv_freq
    ang_c = cols[:, None] * inv_freq
    ang = jnp.concatenate([ang_r, ang_r, ang_c, ang_c], axis=-1)
    cos = jnp.concatenate([jnp.ones((ctx_len, HEAD_DIM), F32), jnp.cos(ang)], axis=0)
    sin = jnp.concatenate([jnp.zeros((ctx_len, HEAD_DIM), F32), jnp.sin(ang)], axis=0)
    first = (jnp.arange(HEAD_DIM) % 32) < 16
    sin_a = jnp.where(first, -sin, 0.0)
    sin_b = jnp.where(first, 0.0, sin)
    two = lambda m: jnp.concatenate([m, m], axis=1)
    return jnp.concatenate([two(cos), two(sin_a), two(sin_b)], axis=1)


def _w_in_layout(w_in):
    d = w_in.shape[0]
    a0 = 0
    z = w_in[:, a0:a0 + 512]
    xbc = w_in[:, a0 + 512:a0 + 1280]
    dt = jnp.pad(w_in[:, a0 + 1280:a0 + 1296], ((0, 0), (0, SSD_DT_PAD - 16)))
    rest = w_in[:, 1296:]
    out = jnp.concatenate([z, xbc, dt, rest], axis=1).astype(BF16)
    assert out.shape == (d, _C_END)
    return out


def kernel(x, c, ctx, c_ctx, w_mod, b_mod, norm1_g, norm2_g, w_in, ssd_conv_w, ssd_conv_b, ssd_a_log, ssd_dt_bias, ssd_d, ssd_norm_g, diff_qn_g, diff_kn_g, diff_lam_q1, diff_lam_k1, diff_lam_q2, diff_lam_k2, diff_subln_g, gqa_qn_g, gqa_kn_g, s5_lam_re, s5_lam_im, s5_log_dt, s5_b_re, s5_b_im, s5_c_re, s5_c_im, s5_d, s5_glu_w, s5_glu_b, w_gate, w_br_ssd, w_br_diff, w_br_gqa, w_br_s5, w_out, ffn_w_gate_up, ffn_w_down):
    b, seq_len, d = x.shape
    ctx_len = ctx.shape[1]
    depth = w_mod.shape[0]
    assert b + 1 <= 8
    assert ctx_len % ROW_TILE == 0 and seq_len % ROW_TILE == 0
    n_ctx_tiles = ctx_len // ROW_TILE

    h = jnp.concatenate([ctx, x], axis=1)
    cc = jnp.concatenate([c, c_ctx[None], jnp.zeros((8 - b - 1, d), F32)], axis=0)
    mods = _mod_call(cc, w_mod, b_mod)

    rope_tab = _rope_tables(ctx_len, seq_len)
    blk = jnp.arange(512) // HEAD_DIM
    ones = (blk[:, None] == blk[None, :]).astype(BF16)
    tile8 = lambda g: jnp.tile(g.astype(F32), 512 // HEAD_DIM)

    for layer in range(depth):
        m = mods[layer]
        lat = m[:b].reshape(b, 6, d)
        cmod = jnp.broadcast_to(m[b].reshape(1, 6, d), (b, 6, d))
        modv = jnp.stack([cmod, lat], axis=1)

        gains = jnp.stack([tile8(diff_qn_g[layer]), tile8(diff_kn_g[layer]),
                           tile8(gqa_qn_g[layer]), tile8(gqa_kn_g[layer])]
                          + [jnp.zeros((512,), F32)] * 4, axis=0)
        conv_w = jnp.pad(ssd_conv_w[layer].astype(F32), ((0, 8 - SSD_CONV_W), (0, 0)))
        (xn, z, xbc_act, dt, qd, kd, vd, qg, kg, vg, u) = _in_proj_call(
            h, modv, norm1_g[layer][None], _w_in_layout(w_in[layer]), ones, gains, rope_tab,
            conv_w, ssd_conv_b[layer][None].astype(F32), n_ctx_tiles)

        pad16 = lambda v: jnp.pad(v.reshape(1, 16).astype(F32), ((0, 0), (0, LANES - 16)))
        aneg = pad16(-jnp.exp(ssd_a_log[layer].astype(F32)))
        dtb = pad16(ssd_dt_bias[layer])
        dsk = jnp.repeat(ssd_d[layer].astype(F32), SSD_HEAD_DIM)[None]
        ya = _ssd_call(xbc_act, dt, z, aneg, dtb, dsk, ssd_norm_g[layer][None].astype(F32),
                       ctx_len // SSD_CHUNK)

        lam_init = 0.8 - 0.6 * math.exp(-0.3 * layer)
        lamv = jnp.pad(jnp.stack([diff_lam_q1[layer], diff_lam_k1[layer],
                                  diff_lam_q2[layer], diff_lam_k2[layer]]).astype(F32),
                       ((0, 0), (0, LANES - HEAD_DIM)))
        yb = _diff_attn_call(qd, kd, vd, lamv, diff_subln_g[layer][None].astype(F32), lam_init, ctx_len)

        yc = _gqa_attn_call(qg, kg, vg, ctx_len)

        s5w = _s5_weights_call(s5_lam_re[layer], s5_lam_im[layer], s5_log_dt[layer],
                               s5_b_re[layer], s5_b_im[layer], s5_c_re[layer], s5_c_im[layer])
        y5 = _s5_from_groups(_s5_call(_s5_to_groups(u), *s5w, ctx_len // S5_CHUNK,
                                      (ctx_len + seq_len) // S5_CHUNK), b)

        wts = (w_gate[layer].astype(BF16), w_br_ssd[layer].astype(BF16), w_br_diff[layer].astype(BF16),
               w_br_gqa[layer].astype(BF16), w_br_s5[layer].astype(BF16), s5_glu_w[layer].astype(BF16),
               s5_glu_b[layer][None].astype(F32), s5_d[layer][None].astype(F32), w_out[layer].astype(BF16))
        ffn_wts = (norm2_g[layer][None].astype(F32), ffn_w_gate_up[layer].astype(BF16),
                   ffn_w_down[layer].astype(BF16))
        h = _merge_call(h, modv, xn, ya, yb, yc, u, y5, wts, ffn_wts, n_ctx_tiles,
                        latent_only=layer == depth - 1)
    return h
```

```python
import functools
import math

import jax
import jax.numpy as jnp
from jax import lax
from jax.experimental import pallas as pl
from jax.experimental.pallas import tpu as pltpu

F32 = jnp.float32
BF16 = jnp.bfloat16
HIGHEST = lax.Precision.HIGHEST

D_MODEL = 1024
GRID_W = 64
ROPE_THETA = 10000.0
NORM_EPS = 1e-6

SSD_INNER = 512
SSD_HEADS = 8
SSD_HEAD_DIM = 64
SSD_STATE = 64
SSD_CHUNK = 128
SSD_CONV_W = 5
SSD_CONV_CH = 768
SSD_DT_PAD = 128
SSD_BATCH_PER_STEP = 4

DIFF_HEADS = 4
HEAD_DIM = 64
GQA_HEADS = 8
GQA_KV_HEADS = 2

S5_GROUP_CH = 16
S5_STATE = 64
S5_WIDTH = 384
S5_GROUPS = 24
S5_CHUNK = 64
S5_FLAT = S5_CHUNK * S5_GROUP_CH

FFN_HIDDEN = 2816

ROW_TILE = 256
LANES = 128
VMEM_LIMIT = 56 * 1024 * 1024

_C_Z, _C_XBC, _C_DT = 0, 512, 1280
_C_QD, _C_KD, _C_VD = 1408, 1920, 2432
_C_QG, _C_KG, _C_VG = 2944, 3456, 3584
_C_U, _C_END = 3712, 4096


def _params(n_grid):
    return pltpu.CompilerParams(dimension_semantics=("arbitrary",) * n_grid,
                                vmem_limit_bytes=VMEM_LIMIT)


def _bdot(a, b):
    return jnp.dot(a.astype(BF16), b.astype(BF16), preferred_element_type=F32)


def _hdot(a, b):
    return jnp.dot(a, b, precision=HIGHEST, preferred_element_type=F32)


def _rms(x, g):
    return x * lax.rsqrt(jnp.mean(x * x, axis=-1, keepdims=True) + NORM_EPS) * g


def _sigmoid(x):
    return 1.0 / (1.0 + jnp.exp(-x))


def _silu(x):
    return x * _sigmoid(x)


def _mod_body(c_ref, w_ref, b_ref, o_ref):
    o_ref[...] = _hdot(_silu(c_ref[...]), w_ref[...]) + b_ref[...]


def _mod_call(cc, w_mod, b_mod):
    depth = w_mod.shape[0]
    d = D_MODEL
    return pl.pallas_call(
        _mod_body,
        grid=(depth, 6),
        in_specs=[pl.BlockSpec((8, d), lambda l, j: (0, 0)),
                  pl.BlockSpec((None, d, d), lambda l, j: (l, 0, j)),
                  pl.BlockSpec((None, 1, d), lambda l, j: (l, 0, j))],
        out_specs=pl.BlockSpec((None, 8, d), lambda l, j: (l, 0, j)),
        out_shape=jax.ShapeDtypeStruct((depth, 8, 6 * d), F32),
        compiler_params=_params(2),
        name="adaln_mod",
    )(cc, w_mod, b_mod.reshape(depth, 1, 6 * d))


def _head_rms(t, gain, ones):
    t2 = (t * t).astype(BF16)
    w = t.shape[-1]
    blk = min(w, 256)
    ss = jnp.concatenate([jnp.dot(t2[:, i:i + blk], ones[i:i + blk, i:i + blk], preferred_element_type=F32)
                          for i in range(0, w, blk)], axis=1)
    return t * lax.rsqrt(ss * (1.0 / HEAD_DIM) + NORM_EPS) * gain


def _rope(t, cos, sin_a, sin_b):
    w = t.shape[-1]
    return t * cos + pltpu.roll(t, w - 16, 1) * sin_a + pltpu.roll(t, 16, 1) * sin_b


def _in_proj_body(h_ref, hp_ref, hn_ref, mod_ref, g_ref, w_ref, ones_ref, gains_ref, rope_ref,
                  cw_ref, cb_ref, xn_ref, z_ref, xbc_ref, dt_ref, qd_ref, kd_ref, vd_ref,
                  qg_ref, kg_ref, vg_ref, u_ref, *, n_ctx_tiles, n_tiles):
    t = pl.program_id(1)
    tm = h_ref.shape[0]
    mod = mod_ref[...]

    def normed(v):
        return _rms(v, g_ref[...]) * (1.0 + mod[1:2]) + mod[0:1]

    xn = normed(h_ref[...])
    xb = xn.astype(BF16)
    xn_ref[...] = xb

    def proj(lo, hi):
        return jnp.dot(xb, w_ref[:, lo:hi], preferred_element_type=F32)

    z_ref[...] = proj(_C_Z, _C_XBC)
    dt_ref[...] = proj(_C_DT, _C_QD)

    has_prev = jnp.logical_and(t != 0, t != n_ctx_tiles)
    has_next = jnp.logical_and(t != n_ctx_tiles - 1, t != n_tiles - 1)
    slab = jnp.concatenate([normed(hp_ref[...]), xn, normed(hn_ref[...])], axis=0).astype(BF16)
    full = jnp.dot(slab, w_ref[:, _C_XBC:_C_DT], preferred_element_type=F32)
    srow = lax.broadcasted_iota(jnp.int32, (tm + 16, 1), 0)
    valid = jnp.logical_and(jnp.logical_or(has_prev, srow >= 8), jnp.logical_or(has_next, srow < tm + 8))
    full = jnp.where(valid, full, 0.0)
    cw = cw_ref[...]
    pad = (SSD_CONV_W - 1) // 2
    n = tm + 16
    acc = cb_ref[...] + cw[pad:pad + 1] * full[8:8 + tm]
    for k in range(SSD_CONV_W):
        if k != pad:
            acc = acc + cw[k:k + 1] * pltpu.roll(full, (pad - k) % n, 0)[8:8 + tm]
    xbc_ref[...] = _silu(acc)
    u_ref[...] = proj(_C_U, _C_END).astype(u_ref.dtype)

    rope = rope_ref[...]
    cos1, sa1, sb1 = rope[:, 0:128], rope[:, 128:256], rope[:, 256:384]
    cos4 = jnp.concatenate([cos1] * 4, axis=1)
    sa4 = jnp.concatenate([sa1] * 4, axis=1)
    sb4 = jnp.concatenate([sb1] * 4, axis=1)
    ones = ones_ref[...]
    gains = gains_ref[...]
    lane = lax.broadcasted_iota(jnp.int32, (1, LANES), 1)
    lo_half = lane < HEAD_DIM
    scale = HEAD_DIM ** -0.5 * math.log2(math.e)

    qd = _rope(_head_rms(proj(_C_QD, _C_KD), gains[0:1], ones), cos4, sa4, sb4) * scale
    kd = _rope(_head_rms(proj(_C_KD, _C_VD), gains[1:2], ones), cos4, sa4, sb4)
    kd_ref[...] = kd.astype(BF16)
    vd_ref[...] = proj(_C_VD, _C_QG).astype(BF16)
    for h in range(DIFF_HEADS):
        blk = qd[:, h * LANES:(h + 1) * LANES]
        qd_ref[h, 0] = jnp.where(lo_half, blk, 0.0).astype(BF16)
        qd_ref[h, 1] = jnp.where(lo_half, 0.0, blk).astype(BF16)

    qg = _rope(_head_rms(proj(_C_QG, _C_KG), gains[2:3], ones), cos4, sa4, sb4) * scale
    kg = _rope(_head_rms(proj(_C_KG, _C_VG), gains[3:4, :LANES], ones[:LANES, :LANES]),
               cos1, sa1, sb1)
    kg_ref[...] = kg.astype(BF16)
    vg_ref[...] = proj(_C_VG, _C_U).astype(BF16)
    qg_up = pltpu.roll(qg, 4 * LANES - HEAD_DIM, 1)
    qg_dn = pltpu.roll(qg, HEAD_DIM, 1)
    per_kv = GQA_HEADS // GQA_KV_HEADS
    for n in range(GQA_KV_HEADS):
        for i in range(per_kv):
            j = n * per_kv + i
            blk_idx = j // 2
            if (j % 2) == n:
                src = qg
            else:
                src = qg_up if n == 0 else qg_dn
            blk = src[:, blk_idx * LANES:(blk_idx + 1) * LANES]
            keep = lo_half if n == 0 else jnp.logical_not(lo_half)
            qg_ref[n, i] = jnp.where(keep, blk, 0.0).astype(BF16)


def _in_proj_call(h, modv, g1, w_cat, ones, gains, rope_tab, conv_w, conv_b, n_ctx_tiles):
    b, t, d = h.shape
    nt = t // ROW_TILE
    tm = ROW_TILE

    def row(width, dtype):
        return (pl.BlockSpec((None, tm, width), lambda bi, ti: (bi, ti, 0)),
                jax.ShapeDtypeStruct((b, t, width), dtype))

    outs = [row(d, BF16), row(512, F32), row(768, F32), row(SSD_DT_PAD, F32)]
    qd = (pl.BlockSpec((None, DIFF_HEADS, 2, tm, LANES), lambda bi, ti: (bi, 0, 0, ti, 0)),
          jax.ShapeDtypeStruct((b, DIFF_HEADS, 2, t, LANES), BF16))
    qg = (pl.BlockSpec((None, GQA_KV_HEADS, 4, tm, LANES), lambda bi, ti: (bi, 0, 0, ti, 0)),
          jax.ShapeDtypeStruct((b, GQA_KV_HEADS, 4, t, LANES), BF16))
    outs += [qd, row(512, BF16), row(512, BF16), qg, row(LANES, BF16), row(LANES, BF16),
             row(S5_WIDTH, BF16)]
    per = tm // 8
    last8 = t // 8 - 1
    return pl.pallas_call(
        functools.partial(_in_proj_body, n_ctx_tiles=n_ctx_tiles, n_tiles=nt),
        grid=(b, nt),
        in_specs=[
            pl.BlockSpec((None, tm, d), lambda bi, ti: (bi, ti, 0)),
            pl.BlockSpec((None, 8, d), lambda bi, ti: (bi, jnp.maximum(ti * per - 1, 0), 0)),
            pl.BlockSpec((None, 8, d), lambda bi, ti: (bi, jnp.minimum((ti + 1) * per, last8), 0)),
            pl.BlockSpec((None, None, 6, d),
                         lambda bi, ti: (bi, jnp.where(ti >= n_ctx_tiles, 1, 0), 0, 0)),
            pl.BlockSpec((1, d), lambda bi, ti: (0, 0)),
            pl.BlockSpec((d, _C_END), lambda bi, ti: (0, 0)),
            pl.BlockSpec((512, 512), lambda bi, ti: (0, 0)),
            pl.BlockSpec((8, 512), lambda bi, ti: (0, 0)),
            pl.BlockSpec((tm, 384), lambda bi, ti: (ti, 0)),
            pl.BlockSpec((8, SSD_CONV_CH), lambda bi, ti: (0, 0)),
            pl.BlockSpec((1, SSD_CONV_CH), lambda bi, ti: (0, 0)),
        ],
        out_specs=[o[0] for o in outs],
        out_shape=[o[1] for o in outs],
        compiler_params=_params(2),
        name="in_proj",
    )(h, h, h, modv, g1, w_cat, ones, gains, rope_tab, conv_w, conv_b)


def _ssd_chunk_index(d, s, n_ctx_chunks, n_chunks):
    bwd = jnp.where(s < n_ctx_chunks, n_ctx_chunks - 1 - s, n_chunks - 1 - (s - n_ctx_chunks))
    return jnp.where(d == 0, s, bwd)


def _ssd_body(xbc_ref, dt_ref, z_ref, aneg_ref, dtb_ref, dsk_ref, ng_ref, o_ref,
              hst_ref, ysc_ref, *, n_ctx_chunks, n_chunks):
    d = pl.program_id(1)
    s = pl.program_id(2)
    c = _ssd_chunk_index(d, s, n_ctx_chunks, n_chunks)
    q = SSD_CHUNK
    nb = xbc_ref.shape[0]
    r = nb * q

    @pl.when(s == 0)
    def _():
        hst_ref[...] = jnp.zeros_like(hst_ref)

    def run(direction):
        xbc = xbc_ref[...].reshape(r, SSD_CONV_CH)
        x = xbc[:, 0:512]
        bm = xbc[:, 512:640]
        cm = xbc[:, 640:768]
        pre = dt_ref[...].reshape(r, SSD_DT_PAD) + dtb_ref[...]
        dtv = jnp.maximum(pre, 0.0) + jnp.log(1.0 + jnp.exp(-jnp.abs(pre)))
        a = dtv * aneg_ref[...]
        row = lax.broadcasted_iota(jnp.int32, (r, r), 0)
        col = lax.broadcasted_iota(jnp.int32, (r, r), 1)
        order = (row >= col) if direction == 0 else (row <= col)
        tri = jnp.logical_and(row // q == col // q, order)
        a_hi = a.astype(BF16)
        r1 = a - a_hi.astype(F32)
        a_mid = r1.astype(BF16)
        a_lo = (r1 - a_mid.astype(F32)).astype(BF16)
        terms = [t[bi * q:(bi + 1) * q] for bi in range(nb) for t in (a_hi, a_mid, a_lo)]
        parts = jnp.dot(tri[0:q, 0:q].astype(BF16), jnp.concatenate(terms, axis=1),
                        preferred_element_type=F32)
        acs = jnp.concatenate(
            [parts[:, (3 * bi) * LANES:(3 * bi + 1) * LANES] + parts[:, (3 * bi + 1) * LANES:(3 * bi + 2) * LANES]
             + parts[:, (3 * bi + 2) * LANES:(3 * bi + 3) * LANES] for bi in range(nb)], axis=0)
        acs_t = acs.T
        edge = q - 1 if direction == 0 else 0
        lasts = [acs[bi * q + edge:bi * q + edge + 1, :] for bi in range(nb)]
        last = jnp.concatenate([jnp.broadcast_to(v, (q, LANES)) for v in lasts], axis=0)
        dte = jnp.exp(last - acs)
        expa = jnp.exp(acs)
        cdec = jnp.concatenate([jnp.broadcast_to(jnp.exp(v), (8, LANES)) for v in lasts], axis=0)
        erow = lax.broadcasted_iota(jnp.int32, (LANES, SSD_INNER), 0)
        ecol = lax.broadcasted_iota(jnp.int32, (LANES, SSD_INNER), 1)
        expand = (erow == ecol // SSD_HEAD_DIM + SSD_HEADS * direction).astype(BF16)
        spread = jnp.dot(jnp.concatenate([dtv, dtv * dte], axis=0).astype(BF16), expand,
                         preferred_element_type=F32)
        xdt = x * spread[0:r]
        xs = x * spread[r:2 * r]
        fine = jnp.concatenate([expa, cdec], axis=0)
        fine_hi = fine.astype(BF16)
        fine_lo = (fine - fine_hi.astype(F32)).astype(BF16)
        spread2 = (jnp.dot(fine_hi, expand, preferred_element_type=F32)
                   + jnp.dot(fine_lo, expand, preferred_element_type=F32))
        expa_x = spread2[0:r]
        bt = bm.T.astype(BF16)
        lane = lax.broadcasted_iota(jnp.int32, (1, LANES), 1)
        tri_q = tri[0:q, 0:q]
        ys = [[None] * nb for _ in range(4)]
        for g in range(2):
            cg = jnp.where(lane // SSD_STATE == g, cm, 0.0).astype(BF16)
            gmats = [jnp.dot(cg[bi * q:(bi + 1) * q], bt[:, bi * q:(bi + 1) * q], preferred_element_type=F32)
                     for bi in range(nb)]
            for hp in range(2):
                p = g * 2 + hp
                sl = slice(p * LANES, (p + 1) * LANES)
                for bi in range(nb):
                    rows = slice(bi * q, (bi + 1) * q)
                    ypair = None
                    for e in range(2):
                        j = SSD_HEADS * direction + 2 * p + e
                        seg = acs[rows, j:j + 1] - acs_t[j:j + 1, rows]
                        dec = jnp.exp(jnp.where(tri_q, seg, -jnp.inf))
                        xh = jnp.where(lane // SSD_HEAD_DIM == e, xdt[rows, sl], 0.0)
                        term = _bdot(gmats[bi] * dec, xh)
                        ypair = term if ypair is None else ypair + term
                    hprev = hst_ref[bi, p]
                    ypair = ypair + _bdot(cg[rows], hprev) * expa_x[rows, sl]
                    hst_ref[bi, p] = hprev * spread2[r + 8 * bi:r + 8 * bi + 1, sl] + jnp.dot(
                        bt[:, rows], xs[rows, sl].astype(BF16), preferred_element_type=F32)
                    ys[p][bi] = ypair
        ys = [jnp.concatenate(v, axis=0) for v in ys]
        return x, ys

    @pl.when(d == 0)
    def _():
        _, ys = run(0)
        for p in range(4):
            ysc_ref[c, :, p * LANES:(p + 1) * LANES] = ys[p].astype(ysc_ref.dtype)

    @pl.when(d == 1)
    def _():
        x, ys = run(1)
        y = jnp.concatenate(ys, axis=1) + ysc_ref[c].astype(F32) + dsk_ref[...] * x
        y = y * _silu(z_ref[...].reshape(r, SSD_INNER))
        o_ref[...] = _rms(y, ng_ref[...]).astype(o_ref.dtype).reshape(nb, q, SSD_INNER)


def _ssd_call(xbc_act, dt, z, aneg, dtb, dsk, ng, n_ctx_chunks):
    b, t, _ = xbc_act.shape
    q = SSD_CHUNK
    nc = t // q
    nb = SSD_BATCH_PER_STEP if b % SSD_BATCH_PER_STEP == 0 else 1
    cidx = functools.partial(_ssd_chunk_index, n_ctx_chunks=n_ctx_chunks, n_chunks=nc)

    def late(bi, di, si):
        return (bi, jnp.where(di == 0, n_ctx_chunks - 1, cidx(di, si)), 0)

    return pl.pallas_call(
        functools.partial(_ssd_body, n_ctx_chunks=n_ctx_chunks, n_chunks=nc),
        grid=(b // nb, 2, nc),
        in_specs=[
            pl.BlockSpec((nb, q, SSD_CONV_CH), lambda bi, di, si: (bi, cidx(di, si), 0)),
            pl.BlockSpec((nb, q, SSD_DT_PAD), lambda bi, di, si: (bi, cidx(di, si), 0)),
            pl.BlockSpec((nb, q, SSD_INNER), late),
            pl.BlockSpec((1, LANES), lambda bi, di, si: (0, 0)),
            pl.BlockSpec((1, LANES), lambda bi, di, si: (0, 0)),
            pl.BlockSpec((1, SSD_INNER), lambda bi, di, si: (0, 0)),
            pl.BlockSpec((1, SSD_INNER), lambda bi, di, si: (0, 0)),
        ],
        out_specs=pl.BlockSpec((nb, q, SSD_INNER), late),
        out_shape=jax.ShapeDtypeStruct((b, t, SSD_INNER), BF16),
        scratch_shapes=[pltpu.VMEM((nb, 4, LANES, LANES), F32),
                        pltpu.VMEM((nc, nb * q, SSD_INNER), BF16)],
        compiler_params=_params(3),
        name="ssd_scan",
    )(xbc_act, dt, z, aneg, dtb, dsk, ng)


def _flash(qs, k_ref, v_ref, m_ref, l_ref, acc_ref, *, tk, n_kv, ctx_len, is_ctx_tile):
    m_ref[...] = jnp.full(m_ref.shape, -jnp.inf, F32)
    l_ref[...] = jnp.zeros(l_ref.shape, F32)
    acc_ref[...] = jnp.zeros(acc_ref.shape, F32)

    def step(start, size, limit):
        for i, q in enumerate(qs):
            lanes = slice(i * LANES, (i + 1) * LANES)
            k = k_ref[start:start + size, lanes]
            v = v_ref[start:start + size, lanes]
            s = lax.dot_general(q, k, (((1,), (1,)), ((), ())), preferred_element_type=F32)
            if limit is not None:
                col = lax.broadcasted_iota(jnp.int32, (1, size), 1) + start
                s = jnp.where(col < limit, s, -jnp.inf)
            m_old = m_ref[i]
            m_new = jnp.maximum(m_old, jnp.max(s, axis=-1, keepdims=True))
            alpha = jnp.exp2(m_old - m_new)
            p = jnp.exp2(s - jnp.concatenate([m_new] * (size // LANES), axis=1))
            psum = p[:, 0:LANES]
            for j in range(1, size // LANES):
                psum = psum + p[:, j * LANES:(j + 1) * LANES]
            l_ref[i] = alpha * l_ref[i] + psum
            acc_ref[i] = alpha * acc_ref[i] + jnp.dot(p.astype(BF16), v, preferred_element_type=F32)
            m_ref[i] = m_new

    @pl.when(is_ctx_tile)
    def _():
        ctx_pad = -(-ctx_len // LANES) * LANES
        for start in range(0, ctx_pad, tk):
            step(start, min(tk, ctx_pad - start), ctx_len if ctx_pad != ctx_len else None)

    @pl.when(jnp.logical_not(is_ctx_tile))
    def _():
        for kc in range(n_kv):
            step(kc * tk, tk, None)

    return [acc_ref[i] * (1.0 / jnp.sum(l_ref[i], axis=-1, keepdims=True)) for i in range(len(qs))]


DIFF_HEADS_PER_STEP = 2


def _diff_attn_body(q_ref, k_ref, v_ref, lam_ref, g_ref, o_ref, m_ref, l_ref, acc_ref,
                    *, tq, lam_init, n_ctx_q, **kw):
    qi = pl.program_id(2)
    qs = [q_ref[i].reshape(2 * tq, LANES) for i in range(DIFF_HEADS_PER_STEP)]
    outs = _flash(qs, k_ref, v_ref, m_ref, l_ref, acc_ref, is_ctx_tile=qi < n_ctx_q, **kw)
    lv = lam_ref[...]
    lam = (jnp.exp(jnp.sum(lv[0:1] * lv[1:2], axis=-1, keepdims=True))
           - jnp.exp(jnp.sum(lv[2:3] * lv[3:4], axis=-1, keepdims=True)) + lam_init)
    for i, o in enumerate(outs):
        out = o[0:tq] - lam * o[tq:2 * tq]
        o_ref[:, i * LANES:(i + 1) * LANES] = (_rms(out, g_ref[...]) * (1.0 - lam_init)).astype(o_ref.dtype)


def _gqa_attn_body(q_ref, k_ref, v_ref, o_ref, m_ref, l_ref, acc_ref, *, tq, n_ctx_q, **kw):
    n = pl.program_id(1)
    qi = pl.program_id(2)
    q = q_ref[...].reshape(4 * tq, LANES)
    (o,) = _flash([q], k_ref, v_ref, m_ref, l_ref, acc_ref, is_ctx_tile=qi < n_ctx_q, **kw)
    lane = lax.broadcasted_iota(jnp.int32, (1, LANES), 1)
    lo_half = lane < HEAD_DIM
    for pair in range(2):
        a = o[(2 * pair) * tq:(2 * pair + 1) * tq]
        b = o[(2 * pair + 1) * tq:(2 * pair + 2) * tq]
        from_lo = jnp.where(lo_half, a, pltpu.roll(b, HEAD_DIM, 1))
        from_hi = jnp.where(lo_half, pltpu.roll(a, HEAD_DIM, 1), b)
        o_ref[:, pair * LANES:(pair + 1) * LANES] = jnp.where(n == 0, from_lo, from_hi).astype(o_ref.dtype)


def _attn_tiles(t, ctx_len):
    tq = ROW_TILE
    tk = next(c for c in (2816, 768, 256) if t % c == 0)
    assert ctx_len % tq == 0 and t % tq == 0 and t % tk == 0
    return dict(tq=tq, tk=tk, n_kv=t // tk, ctx_len=ctx_len, n_ctx_q=ctx_len // tq)


def _diff_attn_call(qd, kd, vd, lamv, subln_g, lam_init, ctx_len):
    b, _, _, t, _ = qd.shape
    cfg = _attn_tiles(t, ctx_len)
    tq = cfg["tq"]
    r = 2 * tq
    hs = DIFF_HEADS_PER_STEP
    return pl.pallas_call(
        functools.partial(_diff_attn_body, lam_init=lam_init, **cfg),
        grid=(b, DIFF_HEADS // hs, t // tq),
        in_specs=[
            pl.BlockSpec((None, hs, 2, tq, LANES), lambda bi, hi, qi: (bi, hi, 0, qi, 0)),
            pl.BlockSpec((None, t, hs * LANES), lambda bi, hi, qi: (bi, 0, hi)),
            pl.BlockSpec((None, t, hs * LANES), lambda bi, hi, qi: (bi, 0, hi)),
            pl.BlockSpec((4, LANES), lambda bi, hi, qi: (0, 0)),
            pl.BlockSpec((1, LANES), lambda bi, hi, qi: (0, 0)),
        ],
        out_specs=pl.BlockSpec((None, tq, hs * LANES), lambda bi, hi, qi: (bi, qi, hi)),
        out_shape=jax.ShapeDtypeStruct((b, t, DIFF_HEADS * LANES), BF16),
        scratch_shapes=[pltpu.VMEM((hs, r, LANES), F32)] * 3,
        compiler_params=_params(3),
        name="diff_attention",
    )(qd, kd, vd, lamv, subln_g)


def _gqa_attn_call(qg, kg, vg, ctx_len):
    b, _, _, t, _ = qg.shape
    cfg = _attn_tiles(t, ctx_len)
    tq = cfg["tq"]
    r = 4 * tq
    return pl.pallas_call(
        functools.partial(_gqa_attn_body, **cfg),
        grid=(b, GQA_KV_HEADS, t // tq),
        in_specs=[
            pl.BlockSpec((None, None, 4, tq, LANES), lambda bi, ni, qi: (bi, ni, 0, qi, 0)),
            pl.BlockSpec((None, t, LANES), lambda bi, ni, qi: (bi, 0, 0)),
            pl.BlockSpec((None, t, LANES), lambda bi, ni, qi: (bi, 0, 0)),
        ],
        out_specs=pl.BlockSpec((None, tq, 2 * LANES), lambda bi, ni, qi: (bi, qi, ni)),
        out_shape=jax.ShapeDtypeStruct((b, t, GQA_HEADS * HEAD_DIM), BF16),
        scratch_shapes=[pltpu.VMEM((1, r, LANES), F32)] * 3,
        compiler_params=_params(3),
        name="gqa_attention",
    )(qg, kg, vg)


def _cmul(ar, ai, br, bi):
    return ar * br - ai * bi, ar * bi + ai * br


def _stack_rows(er, ei, vr, vi):
    re = [er * vr[c:c + 1] - ei * vi[c:c + 1] for c in range(S5_GROUP_CH)]
    im = [er * vi[c:c + 1] + ei * vr[c:c + 1] for c in range(S5_GROUP_CH)]
    return jnp.concatenate(re, axis=0), jnp.concatenate(im, axis=0)


def _split_dot_t(a, b):
    dn = (((1,), (1,)), ((), ()))
    ah = a.astype(BF16)
    al = (a - ah.astype(F32)).astype(BF16)
    bh = b.astype(BF16)
    bl = (b - bh.astype(F32)).astype(BF16)
    return lax.dot_general(jnp.concatenate([ah, ah, al], axis=1), jnp.concatenate([bh, bl, bh], axis=1),
                           dn, preferred_element_type=F32)


def _s5_weights_body(lam_ref, bt_ref, c_ref, wi_ref, ws_ref, wo_ref, av_ref):
    tc = S5_CHUNK
    lam = lam_ref[...]
    br_t, bi_t = bt_ref[0:16, :], bt_ref[16:32, :]
    cr, ci = c_ref[0:16, :], c_ref[16:32, :]
    kk = lax.broadcasted_iota(jnp.int32, (tc, S5_STATE), 0).astype(F32)
    k8 = lax.broadcasted_iota(jnp.int32, (8, S5_STATE), 0)
    k8 = jnp.where(k8 == 0, 1.0, jnp.where(k8 == 1, tc - 1.0, float(tc)))
    row = lax.broadcasted_iota(jnp.int32, (S5_FLAT, S5_FLAT), 0) % tc
    col = lax.broadcasted_iota(jnp.int32, (S5_FLAT, S5_FLAT), 1) % tc
    w_intra = None
    state_cols, out_cols, a_rows = [], [], []
    for direction in range(2):
        lr = lam[2 * direction:2 * direction + 1]
        li = lam[2 * direction + 1:2 * direction + 2]
        step = jnp.exp(lam[4 + direction:5 + direction])
        mag = jnp.exp(lr * step)
        ar, ai = mag * jnp.cos(li * step), mag * jnp.sin(li * step)
        den = lr * lr + li * li
        fr = ((ar - 1.0) * lr + ai * li) / den
        fi = (ai * lr - (ar - 1.0) * li) / den
        bbr = fr * br_t - fi * bi_t
        bbi = fr * bi_t + fi * br_t
        cs, sn = jnp.cos(kk * (li * step)), jnp.sin(kk * (li * step))
        grow, decay = jnp.exp(-kk * (lr * step)), jnp.exp(kk * (lr * step))
        pr, pi = decay * cs, decay * sn
        nr, ni = grow * cs, -grow * sn
        m8 = jnp.exp(k8 * (lr * step))
        c8r, c8i = m8 * jnp.cos(k8 * (li * step)), m8 * jnp.sin(k8 * (li * step))
        a_one = (c8r[0:1], c8i[0:1])
        a_last = (c8r[1:2], c8i[1:2])
        a_tc = (c8r[2:3], c8i[2:3])
        if direction == 0:
            x_e, y_e = (nr, ni), (pr, pi)
            s_e = _cmul(nr, ni, *a_last)
            o_e = _cmul(pr, pi, *a_one)
            keep = col >= row
        else:
            x_e, y_e = (pr, pi), (nr, ni)
            s_e = (pr, pi)
            o_e = _cmul(nr, ni, *a_tc)
            keep = row >= col
        xr, xi = _stack_rows(*x_e, bbr, bbi)
        yr, yi = _stack_rows(*y_e, cr, ci)
        full = _split_dot_t(jnp.concatenate([xr, -xi], axis=1), jnp.concatenate([yr, yi], axis=1))
        part = jnp.where(keep, full, 0.0)
        w_intra = part if w_intra is None else w_intra + part
        state_cols.append(_stack_rows(*s_e, bbr, bbi))
        o_r, o_i = _stack_rows(*o_e, cr, ci)
        out_cols += [o_r, -o_i]
        a_rows += [jnp.concatenate([a_tc[0], a_tc[0]], axis=1),
                   jnp.concatenate([-a_tc[1], a_tc[1]], axis=1),
                   jnp.concatenate([a_tc[1], -a_tc[1]], axis=1)]
    wi_ref[...] = w_intra.astype(BF16)
    (fr_, fi_), (br_, bi_) = state_cols
    ws_ref[...] = jnp.concatenate([fr_, fi_, br_, bi_, fi_, fr_, bi_, br_], axis=1).astype(BF16)
    wo_ref[...] = jnp.concatenate(out_cols, axis=1).astype(BF16)
    zero = jnp.zeros((1, LANES), F32)
    av_ref[...] = jnp.concatenate(a_rows + [zero, zero], axis=0)


def _s5_body(u_ref, wi_ref, ws_ref, wo_ref, av_ref, y_ref, s_ref, h_ref, *, n_ctx_chunks, n_chunks):
    u = u_ref[...]
    s_ref[...] = jnp.dot(u, ws_ref[...], preferred_element_type=F32)
    av = av_ref[...]
    nj = n_chunks
    rows = u_ref.shape[0] // nj
    a1f, a2f, a2sf, a1b, a2b, a2sb = [jnp.broadcast_to(av[i:i + 1], (rows, LANES)) for i in range(6)]
    hf = hfs = hb = hbs = jnp.zeros((rows, LANES), F32)
    for i in range(nj):
        jf = i * rows
        jb = (n_ctx_chunks - 1 - i if i < n_ctx_chunks else nj - 1 - (i - n_ctx_chunks)) * rows
        h_ref[jf:jf + rows, 0:LANES] = hf
        h_ref[jb:jb + rows, LANES:2 * LANES] = hb
        sf = s_ref[jf:jf + rows, 0:LANES]
        sfs = s_ref[jf:jf + rows, 2 * LANES:3 * LANES]
        sb = s_ref[jb:jb + rows, LANES:2 * LANES]
        sbs = s_ref[jb:jb + rows, 3 * LANES:4 * LANES]
        hf, hfs = a1f * hf + a2f * hfs + sf, a1f * hfs + a2sf * hf + sfs
        hb, hbs = a1b * hb + a2b * hbs + sb, a1b * hbs + a2sb * hb + sbs
    y = (jnp.dot(u, wi_ref[...], preferred_element_type=F32)
         + lax.dot_general(h_ref[...].astype(BF16), wo_ref[...], (((1,), (1,)), ((), ())),
                           preferred_element_type=F32))
    y_ref[...] = y.astype(y_ref.dtype)


def _s5_fused_body(lam_ref, bt_ref, c_ref, u_ref, y_ref, wi_ref, ws_ref, wo_ref, av_ref, s_ref, h_ref, **kw):
    _s5_weights_body(lam_ref, bt_ref, c_ref, wi_ref, ws_ref, wo_ref, av_ref)
    _s5_body(u_ref, wi_ref, ws_ref, wo_ref, av_ref, y_ref, s_ref, h_ref, **kw)


def _s5_call(ug, lam_re, lam_im, log_dt, b_re, b_im, c_re, c_im, n_ctx_chunks, n_chunks):
    g, r, w = ug.shape
    p = S5_STATE
    bc = lambda v: jnp.broadcast_to(v.astype(F32)[:, None], (g, p))
    zero = jnp.zeros((g, p), F32)
    lam = jnp.stack([lam_re[0], lam_im[0], lam_re[1], lam_im[1], bc(log_dt[0]), bc(log_dt[1]),
                     zero, zero], axis=1).astype(F32)
    bt = jnp.concatenate([jnp.swapaxes(b_re, 1, 2), jnp.swapaxes(b_im, 1, 2)], axis=1).astype(F32)
    cc = jnp.concatenate([c_re, c_im], axis=1).astype(F32)
    return pl.pallas_call(
        functools.partial(_s5_fused_body, n_ctx_chunks=n_ctx_chunks, n_chunks=n_chunks),
        grid=(g,),
        in_specs=[
            pl.BlockSpec((None, 8, p), lambda gi: (gi, 0, 0)),
            pl.BlockSpec((None, 32, p), lambda gi: (gi, 0, 0)),
            pl.BlockSpec((None, 32, p), lambda gi: (gi, 0, 0)),
            pl.BlockSpec((None, r, w), lambda gi: (gi, 0, 0)),
        ],
        out_specs=pl.BlockSpec((None, r, w), lambda gi: (gi, 0, 0)),
        out_shape=jax.ShapeDtypeStruct((g, r, w), BF16),
        scratch_shapes=[pltpu.VMEM((w, w), BF16), pltpu.VMEM((w, 4 * LANES), BF16),
                        pltpu.VMEM((w, 2 * LANES), BF16), pltpu.VMEM((8, LANES), F32),
                        pltpu.VMEM((r, 4 * LANES), F32), pltpu.VMEM((r, 2 * LANES), F32)],
        compiler_params=_params(1),
        name="s5_scan",
    )(lam, bt, cc, ug)


def _s5_to_groups(u):
    b, t, _ = u.shape
    nj = t // S5_CHUNK
    x = jnp.swapaxes(u.astype(BF16).reshape(b, nj, S5_CHUNK, S5_WIDTH), 2, 3)
    x = lax.optimization_barrier(x.reshape(b, nj, S5_GROUPS, S5_FLAT))
    return jnp.transpose(x, (2, 1, 0, 3)).reshape(S5_GROUPS, nj * b, S5_FLAT)


def _s5_from_groups(y, b):
    g, r, n = y.shape
    nj = r // b
    x = jnp.transpose(y.reshape(g, nj, b, n), (2, 1, 0, 3))
    x = lax.optimization_barrier(x).reshape(b, nj, S5_WIDTH, S5_CHUNK)
    return jnp.swapaxes(x, 2, 3).reshape(b, nj * S5_CHUNK, S5_WIDTH)


def _merge_body(h_ref, mod_ref, xn_ref, ya_ref, yb_ref, yc_ref, u_ref, y5_ref,
                wg_ref, wa_ref, wb_ref, wc_ref, wd_ref, glw_ref, glb_ref, s5d_ref, wo_ref,
                g2_ref, wgu_ref, wdn_ref, o_ref):
    xn = xn_ref[...]
    y5 = y5_ref[...].astype(F32) + s5d_ref[...] * u_ref[...].astype(F32)
    gelu = 0.5 * y5 * (1.0 + jnp.tanh(0.7978845608028654 * (y5 + 0.044715 * y5 * y5 * y5)))
    glu = _bdot(gelu, glw_ref[...]) + glb_ref[...]
    yd = glu[:, 0:S5_WIDTH] * _sigmoid(glu[:, S5_WIDTH:2 * S5_WIDTH])
    branches = ((ya_ref[...], wa_ref), (yb_ref[...], wb_ref), (yc_ref[...], wc_ref),
                (yd.astype(BF16), wd_ref))
    merged = None
    for i, (y, w_ref) in enumerate(branches):
        gate = _sigmoid(jnp.dot(xn, wg_ref[i], preferred_element_type=F32))
        term = gate * jnp.dot(y, w_ref[...], preferred_element_type=F32)
        merged = term if merged is None else merged + term
    out = _bdot(merged, wo_ref[...])
    h = h_ref[...] + mod_ref[2:3, :] * out
    mod = mod_ref[...]
    xf = (_rms(h, g2_ref[...]) * (1.0 + mod[4:5]) + mod[3:4]).astype(BF16)
    gate = jnp.dot(xf, wgu_ref[:, 0:FFN_HIDDEN], preferred_element_type=F32)
    up = jnp.dot(xf, wgu_ref[:, FFN_HIDDEN:2 * FFN_HIDDEN], preferred_element_type=F32)
    act = (_silu(gate) * up).astype(BF16)
    o_ref[...] = h + mod[5:6] * jnp.dot(act, wdn_ref[...], preferred_element_type=F32)


def _merge_call(h, modv, xn, ya, yb, yc, u, y5, wts, ffn_wts, n_ctx_tiles, latent_only):
    b, t, d = h.shape
    tm = ROW_TILE
    skip = n_ctx_tiles if latent_only else 0

    def row(width):
        return pl.BlockSpec((None, tm, width), lambda bi, ti: (bi, ti + skip, 0))

    def const(shape):
        return pl.BlockSpec(shape, lambda bi, ti: (0,) * len(shape), pipeline_mode=pl.Buffered(1))

    w_gate, w_a, w_b, w_c, w_d, glu_w, glu_b, s5_d, w_out = wts
    g2, w_gu, w_down = ffn_wts
    return pl.pallas_call(
        _merge_body,
        grid=(b, t // tm - skip),
        in_specs=[row(d),
                  pl.BlockSpec((None, None, 6, d),
                               lambda bi, ti: (bi, jnp.where(ti + skip >= n_ctx_tiles, 1, 0), 0, 0)),
                  row(d), row(512), row(512), row(512), row(S5_WIDTH), row(S5_WIDTH),
                  const((4, d, d)), const((512, d)), const((512, d)), const((512, d)),
                  const((S5_WIDTH, d)), const((S5_WIDTH, 2 * S5_WIDTH)), const((1, 2 * S5_WIDTH)),
                  const((1, S5_WIDTH)), const((d, d)),
                  const((1, d)), const((d, 2 * FFN_HIDDEN)), const((FFN_HIDDEN, d))],
        out_specs=pl.BlockSpec((None, tm, d), lambda bi, ti: (bi, ti, 0)),
        out_shape=jax.ShapeDtypeStruct((b, t - skip * tm, d), F32),
        compiler_params=_params(2),
        name="merge_ffn",
    )(h, modv, xn, ya, yb, yc, u, y5, w_gate, w_a, w_b, w_c, w_d, glu_w, glu_b, s5_d, w_out,
      g2, w_gu, w_down)


def _rope_tables(ctx_len, seq_len):
    n_rows = seq_len // GRID_W
    rows = jnp.repeat(jnp.arange(n_rows, dtype=F32), GRID_W)
    cols = jnp.tile(jnp.arange(GRID_W, dtype=F32), n_rows)
    quarter = HEAD_DIM // 4
    inv_freq = ROPE_THETA ** (-jnp.arange(quarter, dtype=F32) / quarter)
    ang_r = rows[:, None] * inv_freq
    ang_c = cols[:, None] * inv_freq
    ang = jnp.concatenate([ang_r, ang_r, ang_c, ang_c], axis=-1)
    cos = jnp.concatenate([jnp.ones((ctx_len, HEAD_DIM), F32), jnp.cos(ang)], axis=0)
    sin = jnp.concatenate([jnp.zeros((ctx_len, HEAD_DIM), F32), jnp.sin(ang)], axis=0)
    first = (jnp.arange(HEAD_DIM) % 32) < 16
    sin_a = jnp.where(first, -sin, 0.0)
    sin_b = jnp.where(first, 0.0, sin)
    two = lambda m: jnp.concatenate([m, m], axis=1)
    return jnp.concatenate([two(cos), two(sin_a), two(sin_b)], axis=1)


def _w_in_layout(w_in):
    d = w_in.shape[0]
    a0 = 0
    z = w_in[:, a0:a0 + 512]
    xbc = w_in[:, a0 + 512:a0 + 1280]
    dt = jnp.pad(w_in[:, a0 + 1280:a0 + 1296], ((0, 0), (0, SSD_DT_PAD - 16)))
    rest = w_in[:, 1296:]
    out = jnp.concatenate([z, xbc, dt, rest], axis=1).astype(BF16)
    assert out.shape == (d, _C_END)
    return out


def kernel(x, c, ctx, c_ctx, w_mod, b_mod, norm1_g, norm2_g, w_in, ssd_conv_w, ssd_conv_b, ssd_a_log, ssd_dt_bias, ssd_d, ssd_norm_g, diff_qn_g, diff_kn_g, diff_lam_q1, diff_lam_k1, diff_lam_q2, diff_lam_k2, diff_subln_g, gqa_qn_g, gqa_kn_g, s5_lam_re, s5_lam_im, s5_log_dt, s5_b_re, s5_b_im, s5_c_re, s5_c_im, s5_d, s5_glu_w, s5_glu_b, w_gate, w_br_ssd, w_br_diff, w_br_gqa, w_br_s5, w_out, ffn_w_gate_up, ffn_w_down):
    b, seq_len, d = x.shape
    ctx_len = ctx.shape[1]
    depth = w_mod.shape[0]
    assert b + 1 <= 8
    assert ctx_len % ROW_TILE == 0 and seq_len % ROW_TILE == 0
    n_ctx_tiles = ctx_len // ROW_TILE

    h = jnp.concatenate([ctx, x], axis=1)
    cc = jnp.concatenate([c, c_ctx[None], jnp.zeros((8 - b - 1, d), F32)], axis=0)
    mods = _mod_call(cc, w_mod, b_mod)

    rope_tab = _rope_tables(ctx_len, seq_len)
    blk = jnp.arange(512) // HEAD_DIM
    ones = (blk[:, None] == blk[None, :]).astype(BF16)
    tile8 = lambda g: jnp.tile(g.astype(F32), 512 // HEAD_DIM)

    for layer in range(depth):
        m = mods[layer]
        lat = m[:b].reshape(b, 6, d)
        cmod = jnp.broadcast_to(m[b].reshape(1, 6, d), (b, 6, d))
        modv = jnp.stack([cmod, lat], axis=1)

        gains = jnp.stack([tile8(diff_qn_g[layer]), tile8(diff_kn_g[layer]),
                           tile8(gqa_qn_g[layer]), tile8(gqa_kn_g[layer])]
                          + [jnp.zeros((512,), F32)] * 4, axis=0)
        conv_w = jnp.pad(ssd_conv_w[layer].astype(F32), ((0, 8 - SSD_CONV_W), (0, 0)))
        (xn, z, xbc_act, dt, qd, kd, vd, qg, kg, vg, u) = _in_proj_call(
            h, modv, norm1_g[layer][None], _w_in_layout(w_in[layer]), ones, gains, rope_tab,
            conv_w, ssd_conv_b[layer][None].astype(F32), n_ctx_tiles)

        pad16 = lambda v: jnp.pad(v.reshape(1, 16).astype(F32), ((0, 0), (0, LANES - 16)))
        aneg = pad16(-jnp.exp(ssd_a_log[layer].astype(F32)))
        dtb = pad16(ssd_dt_bias[layer])
        dsk = jnp.repeat(ssd_d[layer].astype(F32), SSD_HEAD_DIM)[None]
        ya = _ssd_call(xbc_act, dt, z, aneg, dtb, dsk, ssd_norm_g[layer][None].astype(F32),
                       ctx_len // SSD_CHUNK)

        lam_init = 0.8 - 0.6 * math.exp(-0.3 * layer)
        lamv = jnp.pad(jnp.stack([diff_lam_q1[layer], diff_lam_k1[layer],
                                  diff_lam_q2[layer], diff_lam_k2[layer]]).astype(F32),
                       ((0, 0), (0, LANES - HEAD_DIM)))
        yb = _diff_attn_call(qd, kd, vd, lamv, diff_subln_g[layer][None].astype(F32), lam_init, ctx_len)

        yc = _gqa_attn_call(qg, kg, vg, ctx_len)

        y5 = _s5_from_groups(
            _s5_call(_s5_to_groups(u), s5_lam_re[layer], s5_lam_im[layer], s5_log_dt[layer],
                     s5_b_re[layer], s5_b_im[layer], s5_c_re[layer], s5_c_im[layer],
                     ctx_len // S5_CHUNK, (ctx_len + seq_len) // S5_CHUNK), b)

        wts = (w_gate[layer].astype(BF16), w_br_ssd[layer].astype(BF16), w_br_diff[layer].astype(BF16),
               w_br_gqa[layer].astype(BF16), w_br_s5[layer].astype(BF16), s5_glu_w[layer].astype(BF16),
               s5_glu_b[layer][None].astype(F32), s5_d[layer][None].astype(F32), w_out[layer].astype(BF16))
        ffn_wts = (norm2_g[layer][None].astype(F32), ffn_w_gate_up[layer].astype(BF16),
                   ffn_w_down[layer].astype(BF16))
        h = _merge_call(h, modv, xn, ya, yb, yc, u, y5, wts, ffn_wts, n_ctx_tiles,
                        latent_only=layer == depth - 1)
    return h
```

```python
import functools
import math

import jax
import jax.numpy as jnp
from jax import lax
from jax.experimental import pallas as pl
from jax.experimental.pallas import tpu as pltpu

F32 = jnp.float32
BF16 = jnp.bfloat16
HIGHEST = lax.Precision.HIGHEST

D_MODEL = 1024
GRID_W = 64
ROPE_THETA = 10000.0
NORM_EPS = 1e-6

SSD_INNER = 512
SSD_HEADS = 8
SSD_HEAD_DIM = 64
SSD_STATE = 64
SSD_CHUNK = 128
SSD_CONV_W = 5
SSD_CONV_CH = 768
SSD_DT_PAD = 128
SSD_BATCH_PER_STEP = 4

DIFF_HEADS = 4
HEAD_DIM = 64
GQA_HEADS = 8
GQA_KV_HEADS = 2

S5_GROUP_CH = 16
S5_STATE = 64
S5_WIDTH = 384
S5_GROUPS = 24
S5_CHUNK = 64
S5_FLAT = S5_CHUNK * S5_GROUP_CH

FFN_HIDDEN = 2816

ROW_TILE = 256
LANES = 128
VMEM_LIMIT = 56 * 1024 * 1024

_C_Z, _C_XBC, _C_DT = 0, 512, 1280
_C_QD, _C_KD, _C_VD = 1408, 1920, 2432
_C_QG, _C_KG, _C_VG = 2944, 3456, 3584
_C_U, _C_END = 3712, 4096


def _params(n_grid):
    return pltpu.CompilerParams(dimension_semantics=("arbitrary",) * n_grid,
                                vmem_limit_bytes=VMEM_LIMIT)


def _bdot(a, b):
    return jnp.dot(a.astype(BF16), b.astype(BF16), preferred_element_type=F32)


def _hdot(a, b):
    return jnp.dot(a, b, precision=HIGHEST, preferred_element_type=F32)


def _rms(x, g):
    return x * lax.rsqrt(jnp.mean(x * x, axis=-1, keepdims=True) + NORM_EPS) * g


def _sigmoid(x):
    return 1.0 / (1.0 + jnp.exp(-x))


def _silu(x):
    return x * _sigmoid(x)


def _mod_body(c_ref, w_ref, b_ref, o_ref):
    o_ref[...] = _hdot(_silu(c_ref[...]), w_ref[...]) + b_ref[...]


def _mod_call(cc, w_mod, b_mod):
    depth = w_mod.shape[0]
    d = D_MODEL
    return pl.pallas_call(
        _mod_body,
        grid=(depth, 6),
        in_specs=[pl.BlockSpec((8, d), lambda l, j: (0, 0)),
                  pl.BlockSpec((None, d, d), lambda l, j: (l, 0, j)),
                  pl.BlockSpec((None, 1, d), lambda l, j: (l, 0, j))],
        out_specs=pl.BlockSpec((None, 8, d), lambda l, j: (l, 0, j)),
        out_shape=jax.ShapeDtypeStruct((depth, 8, 6 * d), F32),
        compiler_params=_params(2),
        name="adaln_mod",
    )(cc, w_mod, b_mod.reshape(depth, 1, 6 * d))


def _head_rms(t, gain, ones):
    t2 = (t * t).astype(BF16)
    w = t.shape[-1]
    blk = min(w, 256)
    ss = jnp.concatenate([jnp.dot(t2[:, i:i + blk], ones[i:i + blk, i:i + blk], preferred_element_type=F32)
                          for i in range(0, w, blk)], axis=1)
    return t * lax.rsqrt(ss * (1.0 / HEAD_DIM) + NORM_EPS) * gain


def _rope(t, cos, sin_a, sin_b):
    w = t.shape[-1]
    return t * cos + pltpu.roll(t, w - 16, 1) * sin_a + pltpu.roll(t, 16, 1) * sin_b


def _in_proj_body(h_ref, hp_ref, hn_ref, mod_ref, g_ref, w_ref, ones_ref, gains_ref, rope_ref,
                  cw_ref, cb_ref, xn_ref, z_ref, xbc_ref, dt_ref, qd_ref, kd_ref, vd_ref,
                  qg_ref, kg_ref, vg_ref, u_ref, *, n_ctx_tiles, n_tiles):
    t = pl.program_id(1)
    tm = h_ref.shape[0]
    mod = mod_ref[...]

    def normed(v):
        return _rms(v, g_ref[...]) * (1.0 + mod[1:2]) + mod[0:1]

    xn = normed(h_ref[...])
    xb = xn.astype(BF16)
    xn_ref[...] = xb

    def proj(lo, hi):
        return jnp.dot(xb, w_ref[:, lo:hi], preferred_element_type=F32)

    z_ref[...] = proj(_C_Z, _C_XBC)
    dt_ref[...] = proj(_C_DT, _C_QD)

    has_prev = jnp.logical_and(t != 0, t != n_ctx_tiles)
    has_next = jnp.logical_and(t != n_ctx_tiles - 1, t != n_tiles - 1)
    slab = jnp.concatenate([normed(hp_ref[...]), xn, normed(hn_ref[...])], axis=0).astype(BF16)
    full = jnp.dot(slab, w_ref[:, _C_XBC:_C_DT], preferred_element_type=F32)
    srow = lax.broadcasted_iota(jnp.int32, (tm + 16, 1), 0)
    valid = jnp.logical_and(jnp.logical_or(has_prev, srow >= 8), jnp.logical_or(has_next, srow < tm + 8))
    full = jnp.where(valid, full, 0.0)
    cw = cw_ref[...]
    pad = (SSD_CONV_W - 1) // 2
    n = tm + 16
    acc = cb_ref[...] + cw[pad:pad + 1] * full[8:8 + tm]
    for k in range(SSD_CONV_W):
        if k != pad:
            acc = acc + cw[k:k + 1] * pltpu.roll(full, (pad - k) % n, 0)[8:8 + tm]
    xbc_ref[...] = _silu(acc).astype(xbc_ref.dtype)
    u_ref[...] = proj(_C_U, _C_END).astype(u_ref.dtype)

    rope = rope_ref[...]
    cos1, sa1, sb1 = rope[:, 0:128], rope[:, 128:256], rope[:, 256:384]
    cos4 = jnp.concatenate([cos1] * 4, axis=1)
    sa4 = jnp.concatenate([sa1] * 4, axis=1)
    sb4 = jnp.concatenate([sb1] * 4, axis=1)
    ones = ones_ref[...]
    gains = gains_ref[...]
    lane = lax.broadcasted_iota(jnp.int32, (1, LANES), 1)
    lo_half = lane < HEAD_DIM
    scale = HEAD_DIM ** -0.5 * math.log2(math.e)

    qd = _rope(_head_rms(proj(_C_QD, _C_KD), gains[0:1], ones), cos4, sa4, sb4) * scale
    kd = _rope(_head_rms(proj(_C_KD, _C_VD), gains[1:2], ones), cos4, sa4, sb4)
    kd_ref[...] = kd.astype(BF16)
    vd_ref[...] = proj(_C_VD, _C_QG).astype(BF16)
    for h in range(DIFF_HEADS):
        blk = qd[:, h * LANES:(h + 1) * LANES]
        qd_ref[h, 0] = jnp.where(lo_half, blk, 0.0).astype(BF16)
        qd_ref[h, 1] = jnp.where(lo_half, 0.0, blk).astype(BF16)

    qg = _rope(_head_rms(proj(_C_QG, _C_KG), gains[2:3], ones), cos4, sa4, sb4) * scale
    kg = _rope(_head_rms(proj(_C_KG, _C_VG), gains[3:4, :LANES], ones[:LANES, :LANES]),
               cos1, sa1, sb1)
    kg_ref[...] = kg.astype(BF16)
    vg_ref[...] = proj(_C_VG, _C_U).astype(BF16)
    qg_up = pltpu.roll(qg, 4 * LANES - HEAD_DIM, 1)
    qg_dn = pltpu.roll(qg, HEAD_DIM, 1)
    per_kv = GQA_HEADS // GQA_KV_HEADS
    for n in range(GQA_KV_HEADS):
        for i in range(per_kv):
            j = n * per_kv + i
            blk_idx = j // 2
            if (j % 2) == n:
                src = qg
            else:
                src = qg_up if n == 0 else qg_dn
            blk = src[:, blk_idx * LANES:(blk_idx + 1) * LANES]
            keep = lo_half if n == 0 else jnp.logical_not(lo_half)
            qg_ref[n, i] = jnp.where(keep, blk, 0.0).astype(BF16)


def _in_proj_call(h, modv, g1, w_cat, ones, gains, rope_tab, conv_w, conv_b, n_ctx_tiles):
    b, t, d = h.shape
    nt = t // ROW_TILE
    tm = ROW_TILE

    def row(width, dtype):
        return (pl.BlockSpec((None, tm, width), lambda bi, ti: (bi, ti, 0)),
                jax.ShapeDtypeStruct((b, t, width), dtype))

    outs = [row(d, BF16), row(512, F32), row(768, BF16), row(SSD_DT_PAD, F32)]
    qd = (pl.BlockSpec((None, DIFF_HEADS, 2, tm, LANES), lambda bi, ti: (bi, 0, 0, ti, 0)),
          jax.ShapeDtypeStruct((b, DIFF_HEADS, 2, t, LANES), BF16))
    qg = (pl.BlockSpec((None, GQA_KV_HEADS, 4, tm, LANES), lambda bi, ti: (bi, 0, 0, ti, 0)),
          jax.ShapeDtypeStruct((b, GQA_KV_HEADS, 4, t, LANES), BF16))
    outs += [qd, row(512, BF16), row(512, BF16), qg, row(LANES, BF16), row(LANES, BF16),
             row(S5_WIDTH, BF16)]
    per = tm // 8
    last8 = t // 8 - 1
    return pl.pallas_call(
        functools.partial(_in_proj_body, n_ctx_tiles=n_ctx_tiles, n_tiles=nt),
        grid=(b, nt),
        in_specs=[
            pl.BlockSpec((None, tm, d), lambda bi, ti: (bi, ti, 0)),
            pl.BlockSpec((None, 8, d), lambda bi, ti: (bi, jnp.maximum(ti * per - 1, 0), 0)),
            pl.BlockSpec((None, 8, d), lambda bi, ti: (bi, jnp.minimum((ti + 1) * per, last8), 0)),
            pl.BlockSpec((None, None, 6, d),
                         lambda bi, ti: (bi, jnp.where(ti >= n_ctx_tiles, 1, 0), 0, 0)),
            pl.BlockSpec((1, d), lambda bi, ti: (0, 0)),
            pl.BlockSpec((d, _C_END), lambda bi, ti: (0, 0)),
            pl.BlockSpec((512, 512), lambda bi, ti: (0, 0)),
            pl.BlockSpec((8, 512), lambda bi, ti: (0, 0)),
            pl.BlockSpec((tm, 384), lambda bi, ti: (ti, 0)),
            pl.BlockSpec((8, SSD_CONV_CH), lambda bi, ti: (0, 0)),
            pl.BlockSpec((1, SSD_CONV_CH), lambda bi, ti: (0, 0)),
        ],
        out_specs=[o[0] for o in outs],
        out_shape=[o[1] for o in outs],
        compiler_params=_params(2),
        name="in_proj",
    )(h, h, h, modv, g1, w_cat, ones, gains, rope_tab, conv_w, conv_b)


def _ssd_chunk_index(d, s, n_ctx_chunks, n_chunks):
    bwd = jnp.where(s < n_ctx_chunks, n_ctx_chunks - 1 - s, n_chunks - 1 - (s - n_ctx_chunks))
    return jnp.where(d == 0, s, bwd)


def _ssd_body(xbc_ref, dt_ref, z_ref, aneg_ref, dtb_ref, dsk_ref, ng_ref, o_ref,
              hst_ref, ysc_ref, *, n_ctx_chunks, n_chunks):
    d = pl.program_id(1)
    s = pl.program_id(2)
    c = _ssd_chunk_index(d, s, n_ctx_chunks, n_chunks)
    q = SSD_CHUNK
    nb = xbc_ref.shape[0]
    r = nb * q

    @pl.when(s == 0)
    def _():
        hst_ref[...] = jnp.zeros_like(hst_ref)

    def run(direction):
        xbc = xbc_ref[...].reshape(r, SSD_CONV_CH).astype(F32)
        x = xbc[:, 0:512]
        bm = xbc[:, 512:640]
        cm = xbc[:, 640:768]
        pre = dt_ref[...].reshape(r, SSD_DT_PAD) + dtb_ref[...]
        dtv = jnp.maximum(pre, 0.0) + jnp.log(1.0 + jnp.exp(-jnp.abs(pre)))
        a = dtv * aneg_ref[...]
        row = lax.broadcasted_iota(jnp.int32, (r, r), 0)
        col = lax.broadcasted_iota(jnp.int32, (r, r), 1)
        order = (row >= col) if direction == 0 else (row <= col)
        tri = jnp.logical_and(row // q == col // q, order)
        a_hi = a.astype(BF16)
        r1 = a - a_hi.astype(F32)
        a_mid = r1.astype(BF16)
        a_lo = (r1 - a_mid.astype(F32)).astype(BF16)
        terms = [t[bi * q:(bi + 1) * q] for bi in range(nb) for t in (a_hi, a_mid, a_lo)]
        parts = jnp.dot(tri[0:q, 0:q].astype(BF16), jnp.concatenate(terms, axis=1),
                        preferred_element_type=F32)
        acs = jnp.concatenate(
            [parts[:, (3 * bi) * LANES:(3 * bi + 1) * LANES] + parts[:, (3 * bi + 1) * LANES:(3 * bi + 2) * LANES]
             + parts[:, (3 * bi + 2) * LANES:(3 * bi + 3) * LANES] for bi in range(nb)], axis=0)
        acs_t = acs.T
        edge = q - 1 if direction == 0 else 0
        lasts = [acs[bi * q + edge:bi * q + edge + 1, :] for bi in range(nb)]
        last = jnp.concatenate([jnp.broadcast_to(v, (q, LANES)) for v in lasts], axis=0)
        dte = jnp.exp(last - acs)
        expa = jnp.exp(acs)
        cdec = jnp.concatenate([jnp.broadcast_to(jnp.exp(v), (8, LANES)) for v in lasts], axis=0)
        erow = lax.broadcasted_iota(jnp.int32, (LANES, SSD_INNER), 0)
        ecol = lax.broadcasted_iota(jnp.int32, (LANES, SSD_INNER), 1)
        expand = (erow == ecol // SSD_HEAD_DIM + SSD_HEADS * direction).astype(BF16)
        spread = jnp.dot(jnp.concatenate([dtv, dtv * dte], axis=0).astype(BF16), expand,
                         preferred_element_type=F32)
        xdt = x * spread[0:r]
        xs = x * spread[r:2 * r]
        fine = jnp.concatenate([expa, cdec], axis=0)
        fine_hi = fine.astype(BF16)
        fine_lo = (fine - fine_hi.astype(F32)).astype(BF16)
        spread2 = (jnp.dot(fine_hi, expand, preferred_element_type=F32)
                   + jnp.dot(fine_lo, expand, preferred_element_type=F32))
        expa_x = spread2[0:r]
        bt = bm.T.astype(BF16)
        lane = lax.broadcasted_iota(jnp.int32, (1, LANES), 1)
        tri_q = tri[0:q, 0:q]
        ys = [[None] * nb for _ in range(4)]
        for g in range(2):
            cg = jnp.where(lane // SSD_STATE == g, cm, 0.0).astype(BF16)
            gmats = [jnp.dot(cg[bi * q:(bi + 1) * q], bt[:, bi * q:(bi + 1) * q], preferred_element_type=F32)
                     for bi in range(nb)]
            for hp in range(2):
                p = g * 2 + hp
                sl = slice(p * LANES, (p + 1) * LANES)
                for bi in range(nb):
                    rows = slice(bi * q, (bi + 1) * q)
                    ypair = None
                    for e in range(2):
                        j = SSD_HEADS * direction + 2 * p + e
                        seg = acs[rows, j:j + 1] - acs_t[j:j + 1, rows]
                        dec = jnp.exp(jnp.where(tri_q, seg, -jnp.inf))
                        xh = jnp.where(lane // SSD_HEAD_DIM == e, xdt[rows, sl], 0.0)
                        term = _bdot(gmats[bi] * dec, xh)
                        ypair = term if ypair is None else ypair + term
                    hprev = hst_ref[bi, p]
                    ypair = ypair + _bdot(cg[rows], hprev) * expa_x[rows, sl]
                    hst_ref[bi, p] = hprev * spread2[r + 8 * bi:r + 8 * bi + 1, sl] + jnp.dot(
                        bt[:, rows], xs[rows, sl].astype(BF16), preferred_element_type=F32)
                    ys[p][bi] = ypair
        ys = [jnp.concatenate(v, axis=0) for v in ys]
        return x, ys

    @pl.when(d == 0)
    def _():
        _, ys = run(0)
        for p in range(4):
            ysc_ref[c, :, p * LANES:(p + 1) * LANES] = ys[p].astype(ysc_ref.dtype)

    @pl.when(d == 1)
    def _():
        x, ys = run(1)
        y = jnp.concatenate(ys, axis=1) + ysc_ref[c].astype(F32) + dsk_ref[...] * x
        y = y * _silu(z_ref[...].reshape(r, SSD_INNER))
        o_ref[...] = _rms(y, ng_ref[...]).astype(o_ref.dtype).reshape(nb, q, SSD_INNER)


def _ssd_call(xbc_act, dt, z, aneg, dtb, dsk, ng, n_ctx_chunks):
    b, t, _ = xbc_act.shape
    q = SSD_CHUNK
    nc = t // q
    nb = SSD_BATCH_PER_STEP if b % SSD_BATCH_PER_STEP == 0 else 1
    cidx = functools.partial(_ssd_chunk_index, n_ctx_chunks=n_ctx_chunks, n_chunks=nc)

    def late(bi, di, si):
        return (bi, jnp.where(di == 0, n_ctx_chunks - 1, cidx(di, si)), 0)

    return pl.pallas_call(
        functools.partial(_ssd_body, n_ctx_chunks=n_ctx_chunks, n_chunks=nc),
        grid=(b // nb, 2, nc),
        in_specs=[
            pl.BlockSpec((nb, q, SSD_CONV_CH), lambda bi, di, si: (bi, cidx(di, si), 0)),
            pl.BlockSpec((nb, q, SSD_DT_PAD), lambda bi, di, si: (bi, cidx(di, si), 0)),
            pl.BlockSpec((nb, q, SSD_INNER), late),
            pl.BlockSpec((1, LANES), lambda bi, di, si: (0, 0)),
            pl.BlockSpec((1, LANES), lambda bi, di, si: (0, 0)),
            pl.BlockSpec((1, SSD_INNER), lambda bi, di, si: (0, 0)),
            pl.BlockSpec((1, SSD_INNER), lambda bi, di, si: (0, 0)),
        ],
        out_specs=pl.BlockSpec((nb, q, SSD_INNER), late),
        out_shape=jax.ShapeDtypeStruct((b, t, SSD_INNER), BF16),
        scratch_shapes=[pltpu.VMEM((nb, 4, LANES, LANES), F32),
                        pltpu.VMEM((nc, nb * q, SSD_INNER), BF16)],
        compiler_params=_params(3),
        name="ssd_scan",
    )(xbc_act, dt, z, aneg, dtb, dsk, ng)


def _flash(qs, k_ref, v_ref, m_ref, l_ref, acc_ref, *, tk, n_kv, ctx_len, is_ctx_tile):
    m_ref[...] = jnp.full(m_ref.shape, -jnp.inf, F32)
    l_ref[...] = jnp.zeros(l_ref.shape, F32)
    acc_ref[...] = jnp.zeros(acc_ref.shape, F32)

    def step(start, size, limit):
        for i, q in enumerate(qs):
            lanes = slice(i * LANES, (i + 1) * LANES)
            k = k_ref[start:start + size, lanes]
            v = v_ref[start:start + size, lanes]
            s = lax.dot_general(q, k, (((1,), (1,)), ((), ())), preferred_element_type=F32)
            if limit is not None:
                col = lax.broadcasted_iota(jnp.int32, (1, size), 1) + start
                s = jnp.where(col < limit, s, -jnp.inf)
            m_old = m_ref[i]
            m_new = jnp.maximum(m_old, jnp.max(s, axis=-1, keepdims=True))
            alpha = jnp.exp2(m_old - m_new)
            p = jnp.exp2(s - jnp.concatenate([m_new] * (size // LANES), axis=1))
            psum = p[:, 0:LANES]
            for j in range(1, size // LANES):
                psum = psum + p[:, j * LANES:(j + 1) * LANES]
            l_ref[i] = alpha * l_ref[i] + psum
            acc_ref[i] = alpha * acc_ref[i] + jnp.dot(p.astype(BF16), v, preferred_element_type=F32)
            m_ref[i] = m_new

    @pl.when(is_ctx_tile)
    def _():
        ctx_pad = -(-ctx_len // LANES) * LANES
        for start in range(0, ctx_pad, tk):
            step(start, min(tk, ctx_pad - start), ctx_len if ctx_pad != ctx_len else None)

    @pl.when(jnp.logical_not(is_ctx_tile))
    def _():
        for kc in range(n_kv):
            step(kc * tk, tk, None)

    return [acc_ref[i] * (1.0 / jnp.sum(l_ref[i], axis=-1, keepdims=True)) for i in range(len(qs))]


DIFF_HEADS_PER_STEP = 2


def _diff_attn_body(q_ref, k_ref, v_ref, lam_ref, g_ref, o_ref, m_ref, l_ref, acc_ref,
                    *, tq, lam_init, n_ctx_q, **kw):
    qi = pl.program_id(2)
    qs = [q_ref[i].reshape(2 * tq, LANES) for i in range(DIFF_HEADS_PER_STEP)]
    outs = _flash(qs, k_ref, v_ref, m_ref, l_ref, acc_ref, is_ctx_tile=qi < n_ctx_q, **kw)
    lv = lam_ref[...]
    lam = (jnp.exp(jnp.sum(lv[0:1] * lv[1:2], axis=-1, keepdims=True))
           - jnp.exp(jnp.sum(lv[2:3] * lv[3:4], axis=-1, keepdims=True)) + lam_init)
    for i, o in enumerate(outs):
        out = o[0:tq] - lam * o[tq:2 * tq]
        o_ref[:, i * LANES:(i + 1) * LANES] = (_rms(out, g_ref[...]) * (1.0 - lam_init)).astype(o_ref.dtype)


def _gqa_attn_body(q_ref, k_ref, v_ref, o_ref, m_ref, l_ref, acc_ref, *, tq, n_ctx_q, **kw):
    n = pl.program_id(1)
    qi = pl.program_id(2)
    q = q_ref[...].reshape(4 * tq, LANES)
    (o,) = _flash([q], k_ref, v_ref, m_ref, l_ref, acc_ref, is_ctx_tile=qi < n_ctx_q, **kw)
    lane = lax.broadcasted_iota(jnp.int32, (1, LANES), 1)
    lo_half = lane < HEAD_DIM
    for pair in range(2):
        a = o[(2 * pair) * tq:(2 * pair + 1) * tq]
        b = o[(2 * pair + 1) * tq:(2 * pair + 2) * tq]
        from_lo = jnp.where(lo_half, a, pltpu.roll(b, HEAD_DIM, 1))
        from_hi = jnp.where(lo_half, pltpu.roll(a, HEAD_DIM, 1), b)
        o_ref[:, pair * LANES:(pair + 1) * LANES] = jnp.where(n == 0, from_lo, from_hi).astype(o_ref.dtype)


def _attn_tiles(t, ctx_len):
    tq = ROW_TILE
    tk = next(c for c in (2816, 768, 256) if t % c == 0)
    assert ctx_len % tq == 0 and t % tq == 0 and t % tk == 0
    return dict(tq=tq, tk=tk, n_kv=t // tk, ctx_len=ctx_len, n_ctx_q=ctx_len // tq)


def _diff_attn_call(qd, kd, vd, lamv, subln_g, lam_init, ctx_len):
    b, _, _, t, _ = qd.shape
    cfg = _attn_tiles(t, ctx_len)
    tq = cfg["tq"]
    r = 2 * tq
    hs = DIFF_HEADS_PER_STEP
    return pl.pallas_call(
        functools.partial(_diff_attn_body, lam_init=lam_init, **cfg),
        grid=(b, DIFF_HEADS // hs, t // tq),
        in_specs=[
            pl.BlockSpec((None, hs, 2, tq, LANES), lambda bi, hi, qi: (bi, hi, 0, qi, 0)),
            pl.BlockSpec((None, t, hs * LANES), lambda bi, hi, qi: (bi, 0, hi)),
            pl.BlockSpec((None, t, hs * LANES), lambda bi, hi, qi: (bi, 0, hi)),
            pl.BlockSpec((4, LANES), lambda bi, hi, qi: (0, 0)),
            pl.BlockSpec((1, LANES), lambda bi, hi, qi: (0, 0)),
        ],
        out_specs=pl.BlockSpec((None, tq, hs * LANES), lambda bi, hi, qi: (bi, qi, hi)),
        out_shape=jax.ShapeDtypeStruct((b, t, DIFF_HEADS * LANES), BF16),
        scratch_shapes=[pltpu.VMEM((hs, r, LANES), F32)] * 3,
        compiler_params=_params(3),
        name="diff_attention",
    )(qd, kd, vd, lamv, subln_g)


def _gqa_attn_call(qg, kg, vg, ctx_len):
    b, _, _, t, _ = qg.shape
    cfg = _attn_tiles(t, ctx_len)
    tq = cfg["tq"]
    r = 4 * tq
    return pl.pallas_call(
        functools.partial(_gqa_attn_body, **cfg),
        grid=(b, GQA_KV_HEADS, t // tq),
        in_specs=[
            pl.BlockSpec((None, None, 4, tq, LANES), lambda bi, ni, qi: (bi, ni, 0, qi, 0)),
            pl.BlockSpec((None, t, LANES), lambda bi, ni, qi: (bi, 0, 0)),
            pl.BlockSpec((None, t, LANES), lambda bi, ni, qi: (bi, 0, 0)),
        ],
        out_specs=pl.BlockSpec((None, tq, 2 * LANES), lambda bi, ni, qi: (bi, qi, ni)),
        out_shape=jax.ShapeDtypeStruct((b, t, GQA_HEADS * HEAD_DIM), BF16),
        scratch_shapes=[pltpu.VMEM((1, r, LANES), F32)] * 3,
        compiler_params=_params(3),
        name="gqa_attention",
    )(qg, kg, vg)


def _cmul(ar, ai, br, bi):
    return ar * br - ai * bi, ar * bi + ai * br


def _stack_rows(er, ei, vr, vi):
    re = [er * vr[c:c + 1] - ei * vi[c:c + 1] for c in range(S5_GROUP_CH)]
    im = [er * vi[c:c + 1] + ei * vr[c:c + 1] for c in range(S5_GROUP_CH)]
    return jnp.concatenate(re, axis=0), jnp.concatenate(im, axis=0)


def _split_dot_t(a, b):
    dn = (((1,), (1,)), ((), ()))
    ah = a.astype(BF16)
    al = (a - ah.astype(F32)).astype(BF16)
    bh = b.astype(BF16)
    bl = (b - bh.astype(F32)).astype(BF16)
    return lax.dot_general(jnp.concatenate([ah, ah, al], axis=1), jnp.concatenate([bh, bl, bh], axis=1),
                           dn, preferred_element_type=F32)


def _s5_weights_body(lam_ref, bt_ref, c_ref, wi_ref, ws_ref, wo_ref, av_ref):
    tc = S5_CHUNK
    lam = lam_ref[...]
    br_t, bi_t = bt_ref[0:16, :], bt_ref[16:32, :]
    cr, ci = c_ref[0:16, :], c_ref[16:32, :]
    kk = lax.broadcasted_iota(jnp.int32, (tc, S5_STATE), 0).astype(F32)
    k8 = lax.broadcasted_iota(jnp.int32, (8, S5_STATE), 0)
    k8 = jnp.where(k8 == 0, 1.0, jnp.where(k8 == 1, tc - 1.0, float(tc)))
    row = lax.broadcasted_iota(jnp.int32, (S5_FLAT, S5_FLAT), 0) % tc
    col = lax.broadcasted_iota(jnp.int32, (S5_FLAT, S5_FLAT), 1) % tc
    w_intra = None
    state_cols, out_cols, a_rows = [], [], []
    for direction in range(2):
        lr = lam[2 * direction:2 * direction + 1]
        li = lam[2 * direction + 1:2 * direction + 2]
        step = jnp.exp(lam[4 + direction:5 + direction])
        mag = jnp.exp(lr * step)
        ar, ai = mag * jnp.cos(li * step), mag * jnp.sin(li * step)
        den = lr * lr + li * li
        fr = ((ar - 1.0) * lr + ai * li) / den
        fi = (ai * lr - (ar - 1.0) * li) / den
        bbr = fr * br_t - fi * bi_t
        bbi = fr * bi_t + fi * br_t
        cs, sn = jnp.cos(kk * (li * step)), jnp.sin(kk * (li * step))
        grow, decay = jnp.exp(-kk * (lr * step)), jnp.exp(kk * (lr * step))
        pr, pi = decay * cs, decay * sn
        nr, ni = grow * cs, -grow * sn
        m8 = jnp.exp(k8 * (lr * step))
        c8r, c8i = m8 * jnp.cos(k8 * (li * step)), m8 * jnp.sin(k8 * (li * step))
        a_one = (c8r[0:1], c8i[0:1])
        a_last = (c8r[1:2], c8i[1:2])
        a_tc = (c8r[2:3], c8i[2:3])
        if direction == 0:
            x_e, y_e = (nr, ni), (pr, pi)
            s_e = _cmul(nr, ni, *a_last)
            o_e = _cmul(pr, pi, *a_one)
            keep = col >= row
        else:
            x_e, y_e = (pr, pi), (nr, ni)
            s_e = (pr, pi)
            o_e = _cmul(nr, ni, *a_tc)
            keep = row >= col
        xr, xi = _stack_rows(*x_e, bbr, bbi)
        yr, yi = _stack_rows(*y_e, cr, ci)
        full = _split_dot_t(jnp.concatenate([xr, -xi], axis=1), jnp.concatenate([yr, yi], axis=1))
        part = jnp.where(keep, full, 0.0)
        w_intra = part if w_intra is None else w_intra + part
        state_cols.append(_stack_rows(*s_e, bbr, bbi))
        o_r, o_i = _stack_rows(*o_e, cr, ci)
        out_cols += [o_r, -o_i]
        a_rows += [jnp.concatenate([a_tc[0], a_tc[0]], axis=1),
                   jnp.concatenate([-a_tc[1], a_tc[1]], axis=1),
                   jnp.concatenate([a_tc[1], -a_tc[1]], axis=1)]
    wi_ref[...] = w_intra.astype(BF16)
    (fr_, fi_), (br_, bi_) = state_cols
    ws_ref[...] = jnp.concatenate([fr_, fi_, br_, bi_, fi_, fr_, bi_, br_], axis=1).astype(BF16)
    wo_ref[...] = jnp.concatenate(out_cols, axis=1).astype(BF16)
    zero = jnp.zeros((1, LANES), F32)
    av_ref[...] = jnp.concatenate(a_rows + [zero, zero], axis=0)


def _s5_weights_call(lam_re, lam_im, log_dt, b_re, b_im, c_re, c_im):
    g, p = S5_GROUPS, S5_STATE
    bc = lambda v: jnp.broadcast_to(v.astype(F32)[:, None], (g, p))
    zero = jnp.zeros((g, p), F32)
    lam = jnp.stack([lam_re[0], lam_im[0], lam_re[1], lam_im[1], bc(log_dt[0]), bc(log_dt[1]),
                     zero, zero], axis=1).astype(F32)
    bt = jnp.concatenate([jnp.swapaxes(b_re, 1, 2), jnp.swapaxes(b_im, 1, 2)], axis=1).astype(F32)
    cc = jnp.concatenate([c_re, c_im], axis=1).astype(F32)
    n = S5_FLAT
    return pl.pallas_call(
        _s5_weights_body,
        grid=(g,),
        in_specs=[pl.BlockSpec((None, 8, p), lambda gi: (gi, 0, 0)),
                  pl.BlockSpec((None, 32, p), lambda gi: (gi, 0, 0)),
                  pl.BlockSpec((None, 32, p), lambda gi: (gi, 0, 0))],
        out_specs=[pl.BlockSpec((None, n, n), lambda gi: (gi, 0, 0)),
                   pl.BlockSpec((None, n, 4 * LANES), lambda gi: (gi, 0, 0)),
                   pl.BlockSpec((None, n, 2 * LANES), lambda gi: (gi, 0, 0)),
                   pl.BlockSpec((None, 8, LANES), lambda gi: (gi, 0, 0))],
        out_shape=[jax.ShapeDtypeStruct((g, n, n), BF16),
                   jax.ShapeDtypeStruct((g, n, 4 * LANES), BF16),
                   jax.ShapeDtypeStruct((g, n, 2 * LANES), BF16),
                   jax.ShapeDtypeStruct((g, 8, LANES), F32)],
        compiler_params=_params(1),
        name="s5_weights",
    )(lam, bt, cc)


def _s5_body(u_ref, wi_ref, ws_ref, wo_ref, av_ref, y_ref, s_ref, h_ref, *, n_ctx_chunks, n_chunks):
    u = u_ref[...]
    s_ref[...] = jnp.dot(u, ws_ref[...], preferred_element_type=F32)
    av = av_ref[...]
    nj = n_chunks
    rows = u_ref.shape[0] // nj
    a1f, a2f, a2sf, a1b, a2b, a2sb = [jnp.broadcast_to(av[i:i + 1], (rows, LANES)) for i in range(6)]
    hf = hfs = hb = hbs = jnp.zeros((rows, LANES), F32)
    for i in range(nj):
        jf = i * rows
        jb = (n_ctx_chunks - 1 - i if i < n_ctx_chunks else nj - 1 - (i - n_ctx_chunks)) * rows
        h_ref[jf:jf + rows, 0:LANES] = hf
        h_ref[jb:jb + rows, LANES:2 * LANES] = hb
        sf = s_ref[jf:jf + rows, 0:LANES]
        sfs = s_ref[jf:jf + rows, 2 * LANES:3 * LANES]
        sb = s_ref[jb:jb + rows, LANES:2 * LANES]
        sbs = s_ref[jb:jb + rows, 3 * LANES:4 * LANES]
        hf, hfs = a1f * hf + a2f * hfs + sf, a1f * hfs + a2sf * hf + sfs
        hb, hbs = a1b * hb + a2b * hbs + sb, a1b * hbs + a2sb * hb + sbs
    y = (jnp.dot(u, wi_ref[...], preferred_element_type=F32)
         + lax.dot_general(h_ref[...].astype(BF16), wo_ref[...], (((1,), (1,)), ((), ())),
                           preferred_element_type=F32))
    y_ref[...] = y.astype(y_ref.dtype)


def _s5_call(ug, w_intra, w_state, w_out, avec, n_ctx_chunks, n_chunks):
    g, r, w = ug.shape
    return pl.pallas_call(
        functools.partial(_s5_body, n_ctx_chunks=n_ctx_chunks, n_chunks=n_chunks),
        grid=(g,),
        in_specs=[
            pl.BlockSpec((None, r, w), lambda gi: (gi, 0, 0)),
            pl.BlockSpec((None, w, w), lambda gi: (gi, 0, 0)),
            pl.BlockSpec((None, w, 4 * LANES), lambda gi: (gi, 0, 0)),
            pl.BlockSpec((None, w, 2 * LANES), lambda gi: (gi, 0, 0)),
            pl.BlockSpec((None, 8, LANES), lambda gi: (gi, 0, 0)),
        ],
        out_specs=pl.BlockSpec((None, r, w), lambda gi: (gi, 0, 0)),
        out_shape=jax.ShapeDtypeStruct((g, r, w), BF16),
        scratch_shapes=[pltpu.VMEM((r, 4 * LANES), F32), pltpu.VMEM((r, 2 * LANES), F32)],
        compiler_params=_params(1),
        name="s5_scan",
    )(ug, w_intra, w_state, w_out, avec)


def _s5_to_groups(u):
    b, t, _ = u.shape
    nj = t // S5_CHUNK
    x = jnp.swapaxes(u.astype(BF16).reshape(b, nj, S5_CHUNK, S5_WIDTH), 2, 3)
    x = lax.optimization_barrier(x.reshape(b, nj, S5_GROUPS, S5_FLAT))
    return jnp.transpose(x, (2, 1, 0, 3)).reshape(S5_GROUPS, nj * b, S5_FLAT)


def _s5_from_groups(y, b):
    g, r, n = y.shape
    nj = r // b
    x = jnp.transpose(y.reshape(g, nj, b, n), (2, 1, 0, 3))
    x = lax.optimization_barrier(x).reshape(b, nj, S5_WIDTH, S5_CHUNK)
    return jnp.swapaxes(x, 2, 3).reshape(b, nj * S5_CHUNK, S5_WIDTH)


def _merge_body(h_ref, mod_ref, xn_ref, ya_ref, yb_ref, yc_ref, u_ref, y5_ref,
                wg_ref, wa_ref, wb_ref, wc_ref, wd_ref, glw_ref, glb_ref, s5d_ref, wo_ref,
                g2_ref, wgu_ref, wdn_ref, o_ref):
    xn = xn_ref[...]
    y5 = y5_ref[...].astype(F32) + s5d_ref[...] * u_ref[...].astype(F32)
    gelu = 0.5 * y5 * (1.0 + jnp.tanh(0.7978845608028654 * (y5 + 0.044715 * y5 * y5 * y5)))
    glu = _bdot(gelu, glw_ref[...]) + glb_ref[...]
    yd = glu[:, 0:S5_WIDTH] * _sigmoid(glu[:, S5_WIDTH:2 * S5_WIDTH])
    branches = ((ya_ref[...], wa_ref), (yb_ref[...], wb_ref), (yc_ref[...], wc_ref),
                (yd.astype(BF16), wd_ref))
    merged = None
    for i, (y, w_ref) in enumerate(branches):
        gate = _sigmoid(jnp.dot(xn, wg_ref[i], preferred_element_type=F32))
        term = gate * jnp.dot(y, w_ref[...], preferred_element_type=F32)
        merged = term if merged is None else merged + term
    out = _bdot(merged, wo_ref[...])
    h = h_ref[...] + mod_ref[2:3, :] * out
    mod = mod_ref[...]
    xf = (_rms(h, g2_ref[...]) * (1.0 + mod[4:5]) + mod[3:4]).astype(BF16)
    gate = jnp.dot(xf, wgu_ref[:, 0:FFN_HIDDEN], preferred_element_type=F32)
    up = jnp.dot(xf, wgu_ref[:, FFN_HIDDEN:2 * FFN_HIDDEN], preferred_element_type=F32)
    act = (_silu(gate) * up).astype(BF16)
    o_ref[...] = h + mod[5:6] * jnp.dot(act, wdn_ref[...], preferred_element_type=F32)


def _merge_call(h, modv, xn, ya, yb, yc, u, y5, wts, ffn_wts, n_ctx_tiles, latent_only):
    b, t, d = h.shape
    tm = ROW_TILE
    skip = n_ctx_tiles if latent_only else 0

    def row(width):
        return pl.BlockSpec((None, tm, width), lambda bi, ti: (bi, ti + skip, 0))

    def const(shape):
        return pl.BlockSpec(shape, lambda bi, ti: (0,) * len(shape), pipeline_mode=pl.Buffered(1))

    w_gate, w_a, w_b, w_c, w_d, glu_w, glu_b, s5_d, w_out = wts
    g2, w_gu, w_down = ffn_wts
    return pl.pallas_call(
        _merge_body,
        grid=(b, t // tm - skip),
        in_specs=[row(d),
                  pl.BlockSpec((None, None, 6, d),
                               lambda bi, ti: (bi, jnp.where(ti + skip >= n_ctx_tiles, 1, 0), 0, 0)),
                  row(d), row(512), row(512), row(512), row(S5_WIDTH), row(S5_WIDTH),
                  const((4, d, d)), const((512, d)), const((512, d)), const((512, d)),
                  const((S5_WIDTH, d)), const((S5_WIDTH, 2 * S5_WIDTH)), const((1, 2 * S5_WIDTH)),
                  const((1, S5_WIDTH)), const((d, d)),
                  const((1, d)), const((d, 2 * FFN_HIDDEN)), const((FFN_HIDDEN, d))],
        out_specs=pl.BlockSpec((None, tm, d), lambda bi, ti: (bi, ti, 0)),
        out_shape=jax.ShapeDtypeStruct((b, t - skip * tm, d), F32),
        compiler_params=_params(2),
        name="merge_ffn",
    )(h, modv, xn, ya, yb, yc, u, y5, w_gate, w_a, w_b, w_c, w_d, glu_w, glu_b, s5_d, w_out,
      g2, w_gu, w_down)


def _rope_tables(ctx_len, seq_len):
    n_rows = seq_len // GRID_W
    rows = jnp.repeat(jnp.arange(n_rows, dtype=F32), GRID_W)
    cols = jnp.tile(jnp.arange(GRID_W, dtype=F32), n_rows)
    quarter = HEAD_DIM // 4
    inv_freq = ROPE_THETA ** (-jnp.arange(quarter, dtype=F32) / quarter)
    ang_r = rows[:, None] * inv_freq
    ang_c = cols[:, None] * inv_freq
    ang = jnp.concatenate([ang_r, ang_r, ang_c, ang_c], axis=-1)
    cos = jnp.concatenate([jnp.ones((ctx_len, HEAD_DIM), F32), jnp.cos(ang)], axis=0)
    sin = jnp.concatenate([jnp.zeros((ctx_len, HEAD_DIM), F32), jnp.sin(ang)], axis=0)
    first = (jnp.arange(HEAD_DIM) % 32) < 16
    sin_a = jnp.where(first, -sin, 0.0)
    sin_b = jnp.where(first, 0.0, sin)
    two = lambda m: jnp.concatenate([m, m], axis=1)
    return jnp.concatenate([two(cos), two(sin_a), two(sin_b)], axis=1)


def _w_in_layout(w_in):
    d = w_in.shape[0]
    a0 = 0
    z = w_in[:, a0:a0 + 512]
    xbc = w_in[:, a0 + 512:a0 + 1280]
    dt = jnp.pad(w_in[:, a0 + 1280:a0 + 1296], ((0, 0), (0, SSD_DT_PAD - 16)))
    rest = w_in[:, 1296:]
    out = jnp.concatenate([z, xbc, dt, rest], axis=1).astype(BF16)
    assert out.shape == (d, _C_END)
    return out


def kernel(x, c, ctx, c_ctx, w_mod, b_mod, norm1_g, norm2_g, w_in, ssd_conv_w, ssd_conv_b, ssd_a_log, ssd_dt_bias, ssd_d, ssd_norm_g, diff_qn_g, diff_kn_g, diff_lam_q1, diff_lam_k1, diff_lam_q2, diff_lam_k2, diff_subln_g, gqa_qn_g, gqa_kn_g, s5_lam_re, s5_lam_im, s5_log_dt, s5_b_re, s5_b_im, s5_c_re, s5_c_im, s5_d, s5_glu_w, s5_glu_b, w_gate, w_br_ssd, w_br_diff, w_br_gqa, w_br_s5, w_out, ffn_w_gate_up, ffn_w_down):
    b, seq_len, d = x.shape
    ctx_len = ctx.shape[1]
    depth = w_mod.shape[0]
    assert b + 1 <= 8
    assert ctx_len % ROW_TILE == 0 and seq_len % ROW_TILE == 0
    n_ctx_tiles = ctx_len // ROW_TILE

    h = jnp.concatenate([ctx, x], axis=1)
    cc = jnp.concatenate([c, c_ctx[None], jnp.zeros((8 - b - 1, d), F32)], axis=0)
    mods = _mod_call(cc, w_mod, b_mod)

    rope_tab = _rope_tables(ctx_len, seq_len)
    blk = jnp.arange(512) // HEAD_DIM
    ones = (blk[:, None] == blk[None, :]).astype(BF16)
    tile8 = lambda g: jnp.tile(g.astype(F32), 512 // HEAD_DIM)

    for layer in range(depth):
        m = mods[layer]
        lat = m[:b].reshape(b, 6, d)
        cmod = jnp.broadcast_to(m[b].reshape(1, 6, d), (b, 6, d))
        modv = jnp.stack([cmod, lat], axis=1)

        gains = jnp.stack([tile8(diff_qn_g[layer]), tile8(diff_kn_g[layer]),
                           tile8(gqa_qn_g[layer]), tile8(gqa_kn_g[layer])]
                          + [jnp.zeros((512,), F32)] * 4, axis=0)
        conv_w = jnp.pad(ssd_conv_w[layer].astype(F32), ((0, 8 - SSD_CONV_W), (0, 0)))
        (xn, z, xbc_act, dt, qd, kd, vd, qg, kg, vg, u) = _in_proj_call(
            h, modv, norm1_g[layer][None], _w_in_layout(w_in[layer]), ones, gains, rope_tab,
            conv_w, ssd_conv_b[layer][None].astype(F32), n_ctx_tiles)

        pad16 = lambda v: jnp.pad(v.reshape(1, 16).astype(F32), ((0, 0), (0, LANES - 16)))
        aneg = pad16(-jnp.exp(ssd_a_log[layer].astype(F32)))
        dtb = pad16(ssd_dt_bias[layer])
        dsk = jnp.repeat(ssd_d[layer].astype(F32), SSD_HEAD_DIM)[None]
        ya = _ssd_call(xbc_act, dt, z, aneg, dtb, dsk, ssd_norm_g[layer][None].astype(F32),
                       ctx_len // SSD_CHUNK)

        lam_init = 0.8 - 0.6 * math.exp(-0.3 * layer)
        lamv = jnp.pad(jnp.stack([diff_lam_q1[layer], diff_lam_k1[layer],
                                  diff_lam_q2[layer], diff_lam_k2[layer]]).astype(F32),
                       ((0, 0), (0, LANES - HEAD_DIM)))
        yb = _diff_attn_call(qd, kd, vd, lamv, diff_subln_g[layer][None].astype(F32), lam_init, ctx_len)

        yc = _gqa_attn_call(qg, kg, vg, ctx_len)

        s5w = _s5_weights_call(s5_lam_re[layer], s5_lam_im[layer], s5_log_dt[layer],
                               s5_b_re[layer], s5_b_im[layer], s5_c_re[layer], s5_c_im[layer])
        y5 = _s5_from_groups(_s5_call(_s5_to_groups(u), *s5w, ctx_len // S5_CHUNK,
                                      (ctx_len + seq_len) // S5_CHUNK), b)

        wts = (w_gate[layer].astype(BF16), w_br_ssd[layer].astype(BF16), w_br_diff[layer].astype(BF16),
               w_br_gqa[layer].astype(BF16), w_br_s5[layer].astype(BF16), s5_glu_w[layer].astype(BF16),
               s5_glu_b[layer][None].astype(F32), s5_d[layer][None].astype(F32), w_out[layer].astype(BF16))
        ffn_wts = (norm2_g[layer][None].astype(F32), ffn_w_gate_up[layer].astype(BF16),
                   ffn_w_down[layer].astype(BF16))
        h = _merge_call(h, modv, xn, ya, yb, yc, u, y5, wts, ffn_wts, n_ctx_tiles,
                        latent_only=layer == depth - 1)
    return h
```

```python
import functools
import math

import jax
import jax.numpy as jnp
from jax import lax
from jax.experimental import pallas as pl
from jax.experimental.pallas import tpu as pltpu

F32 = jnp.float32
BF16 = jnp.bfloat16
HIGHEST = lax.Precision.HIGHEST

D_MODEL = 1024
GRID_W = 64
ROPE_THETA = 10000.0
NORM_EPS = 1e-6

SSD_INNER = 512
SSD_HEADS = 8
SSD_HEAD_DIM = 64
SSD_STATE = 64
SSD_CHUNK = 128
SSD_CONV_W = 5
SSD_CONV_CH = 768
SSD_DT_PAD = 128
SSD_BATCH_PER_STEP = 4

DIFF_HEADS = 4
HEAD_DIM = 64
GQA_HEADS = 8
GQA_KV_HEADS = 2

S5_GROUP_CH = 16
S5_STATE = 64
S5_WIDTH = 384
S5_GROUPS = 24
S5_CHUNK = 64
S5_FLAT = S5_CHUNK * S5_GROUP_CH

FFN_HIDDEN = 2816

ROW_TILE = 256
LANES = 128
VMEM_LIMIT = 56 * 1024 * 1024

_C_Z, _C_XBC, _C_DT = 0, 512, 1280
_C_QD, _C_KD, _C_VD = 1408, 1920, 2432
_C_QG, _C_KG, _C_VG = 2944, 3456, 3584
_C_U, _C_END = 3712, 4096


def _params(n_grid):
    return pltpu.CompilerParams(dimension_semantics=("arbitrary",) * n_grid,
                                vmem_limit_bytes=VMEM_LIMIT)


def _bdot(a, b):
    return jnp.dot(a.astype(BF16), b.astype(BF16), preferred_element_type=F32)


def _hdot(a, b):
    return jnp.dot(a, b, precision=HIGHEST, preferred_element_type=F32)


def _rms(x, g):
    return x * lax.rsqrt(jnp.mean(x * x, axis=-1, keepdims=True) + NORM_EPS) * g


def _sigmoid(x):
    return 1.0 / (1.0 + jnp.exp(-x))


def _silu(x):
    return x * _sigmoid(x)


def _mod_body(c_ref, w_ref, b_ref, o_ref):
    o_ref[...] = _hdot(_silu(c_ref[...]), w_ref[...]) + b_ref[...]


def _mod_call(cc, w_mod, b_mod):
    depth = w_mod.shape[0]
    d = D_MODEL
    return pl.pallas_call(
        _mod_body,
        grid=(depth, 6),
        in_specs=[pl.BlockSpec((8, d), lambda l, j: (0, 0)),
                  pl.BlockSpec((None, d, d), lambda l, j: (l, 0, j)),
                  pl.BlockSpec((None, 1, d), lambda l, j: (l, 0, j))],
        out_specs=pl.BlockSpec((None, 8, d), lambda l, j: (l, 0, j)),
        out_shape=jax.ShapeDtypeStruct((depth, 8, 6 * d), F32),
        compiler_params=_params(2),
        name="adaln_mod",
    )(cc, w_mod, b_mod.reshape(depth, 1, 6 * d))


def _head_rms(t, gain, ones):
    t2 = (t * t).astype(BF16)
    w = t.shape[-1]
    blk = min(w, 256)
    ss = jnp.concatenate([jnp.dot(t2[:, i:i + blk], ones[i:i + blk, i:i + blk], preferred_element_type=F32)
                          for i in range(0, w, blk)], axis=1)
    return t * lax.rsqrt(ss * (1.0 / HEAD_DIM) + NORM_EPS) * gain


def _rope(t, cos, sin_a, sin_b):
    w = t.shape[-1]
    return t * cos + pltpu.roll(t, w - 16, 1) * sin_a + pltpu.roll(t, 16, 1) * sin_b


def _in_proj_body(h_ref, hp_ref, hn_ref, mod_ref, g_ref, w_ref, ones_ref, gains_ref, rope_ref,
                  cw_ref, cb_ref, xn_ref, z_ref, xbc_ref, dt_ref, qd_ref, kd_ref, vd_ref,
                  qg_ref, kg_ref, vg_ref, u_ref, *, n_ctx_tiles, n_tiles):
    t = pl.program_id(1)
    tm = h_ref.shape[0]
    mod = mod_ref[...]

    def normed(v):
        return _rms(v, g_ref[...]) * (1.0 + mod[1:2]) + mod[0:1]

    xn = normed(h_ref[...])
    xb = xn.astype(BF16)
    xn_ref[...] = xb

    def proj(lo, hi):
        return jnp.dot(xb, w_ref[:, lo:hi], preferred_element_type=F32)

    z_ref[...] = proj(_C_Z, _C_XBC)
    dt_ref[...] = proj(_C_DT, _C_QD)

    has_prev = jnp.logical_and(t != 0, t != n_ctx_tiles)
    has_next = jnp.logical_and(t != n_ctx_tiles - 1, t != n_tiles - 1)
    slab = jnp.concatenate([normed(hp_ref[...]), xn, normed(hn_ref[...])], axis=0).astype(BF16)
    full = jnp.dot(slab, w_ref[:, _C_XBC:_C_DT], preferred_element_type=F32)
    srow = lax.broadcasted_iota(jnp.int32, (tm + 16, 1), 0)
    valid = jnp.logical_and(jnp.logical_or(has_prev, srow >= 8), jnp.logical_or(has_next, srow < tm + 8))
    full = jnp.where(valid, full, 0.0)
    cw = cw_ref[...]
    pad = (SSD_CONV_W - 1) // 2
    n = tm + 16
    acc = cb_ref[...] + cw[pad:pad + 1] * full[8:8 + tm]
    for k in range(SSD_CONV_W):
        if k != pad:
            acc = acc + cw[k:k + 1] * pltpu.roll(full, (pad - k) % n, 0)[8:8 + tm]
    xbc_ref[...] = _silu(acc)
    u_ref[...] = proj(_C_U, _C_END).astype(u_ref.dtype)

    rope = rope_ref[...]
    cos1, sa1, sb1 = rope[:, 0:128], rope[:, 128:256], rope[:, 256:384]
    cos4 = jnp.concatenate([cos1] * 4, axis=1)
    sa4 = jnp.concatenate([sa1] * 4, axis=1)
    sb4 = jnp.concatenate([sb1] * 4, axis=1)
    ones = ones_ref[...]
    gains = gains_ref[...]
    lane = lax.broadcasted_iota(jnp.int32, (1, LANES), 1)
    lo_half = lane < HEAD_DIM
    scale = HEAD_DIM ** -0.5 * math.log2(math.e)

    qd = _rope(_head_rms(proj(_C_QD, _C_KD), gains[0:1], ones), cos4, sa4, sb4) * scale
    kd = _rope(_head_rms(proj(_C_KD, _C_VD), gains[1:2], ones), cos4, sa4, sb4)
    kd_ref[...] = kd.astype(BF16)
    vd_ref[...] = proj(_C_VD, _C_QG).astype(BF16)
    for h in range(DIFF_HEADS):
        blk = qd[:, h * LANES:(h + 1) * LANES]
        qd_ref[h, 0] = jnp.where(lo_half, blk, 0.0).astype(BF16)
        qd_ref[h, 1] = jnp.where(lo_half, 0.0, blk).astype(BF16)

    qg = _rope(_head_rms(proj(_C_QG, _C_KG), gains[2:3], ones), cos4, sa4, sb4) * scale
    kg = _rope(_head_rms(proj(_C_KG, _C_VG), gains[3:4, :LANES], ones[:LANES, :LANES]),
               cos1, sa1, sb1)
    kg_ref[...] = kg.astype(BF16)
    vg_ref[...] = proj(_C_VG, _C_U).astype(BF16)
    qg_up = pltpu.roll(qg, 4 * LANES - HEAD_DIM, 1)
    qg_dn = pltpu.roll(qg, HEAD_DIM, 1)
    per_kv = GQA_HEADS // GQA_KV_HEADS
    for n in range(GQA_KV_HEADS):
        for i in range(per_kv):
            j = n * per_kv + i
            blk_idx = j // 2
            if (j % 2) == n:
                src = qg
            else:
                src = qg_up if n == 0 else qg_dn
            blk = src[:, blk_idx * LANES:(blk_idx + 1) * LANES]
            keep = lo_half if n == 0 else jnp.logical_not(lo_half)
            qg_ref[n, i] = jnp.where(keep, blk, 0.0).astype(BF16)


def _in_proj_call(h, modv, g1, w_cat, ones, gains, rope_tab, conv_w, conv_b, n_ctx_tiles):
    b, t, d = h.shape
    nt = t // ROW_TILE
    tm = ROW_TILE

    def row(width, dtype):
        return (pl.BlockSpec((None, tm, width), lambda bi, ti: (bi, ti, 0)),
                jax.ShapeDtypeStruct((b, t, width), dtype))

    outs = [row(d, BF16), row(512, F32), row(768, F32), row(SSD_DT_PAD, F32)]
    qd = (pl.BlockSpec((None, DIFF_HEADS, 2, tm, LANES), lambda bi, ti: (bi, 0, 0, ti, 0)),
          jax.ShapeDtypeStruct((b, DIFF_HEADS, 2, t, LANES), BF16))
    qg = (pl.BlockSpec((None, GQA_KV_HEADS, 4, tm, LANES), lambda bi, ti: (bi, 0, 0, ti, 0)),
          jax.ShapeDtypeStruct((b, GQA_KV_HEADS, 4, t, LANES), BF16))
    outs += [qd, row(512, BF16), row(512, BF16), qg, row(LANES, BF16), row(LANES, BF16),
             row(S5_WIDTH, BF16)]
    per = tm // 8
    last8 = t // 8 - 1
    return pl.pallas_call(
        functools.partial(_in_proj_body, n_ctx_tiles=n_ctx_tiles, n_tiles=nt),
        grid=(b, nt),
        in_specs=[
            pl.BlockSpec((None, tm, d), lambda bi, ti: (bi, ti, 0)),
            pl.BlockSpec((None, 8, d), lambda bi, ti: (bi, jnp.maximum(ti * per - 1, 0), 0)),
            pl.BlockSpec((None, 8, d), lambda bi, ti: (bi, jnp.minimum((ti + 1) * per, last8), 0)),
            pl.BlockSpec((None, None, 6, d),
                         lambda bi, ti: (bi, jnp.where(ti >= n_ctx_tiles, 1, 0), 0, 0)),
            pl.BlockSpec((1, d), lambda bi, ti: (0, 0)),
            pl.BlockSpec((d, _C_END), lambda bi, ti: (0, 0)),
            pl.BlockSpec((512, 512), lambda bi, ti: (0, 0)),
            pl.BlockSpec((8, 512), lambda bi, ti: (0, 0)),
            pl.BlockSpec((tm, 384), lambda bi, ti: (ti, 0)),
            pl.BlockSpec((8, SSD_CONV_CH), lambda bi, ti: (0, 0)),
            pl.BlockSpec((1, SSD_CONV_CH), lambda bi, ti: (0, 0)),
        ],
        out_specs=[o[0] for o in outs],
        out_shape=[o[1] for o in outs],
        compiler_params=_params(2),
        name="in_proj",
    )(h, h, h, modv, g1, w_cat, ones, gains, rope_tab, conv_w, conv_b)


def _ssd_chunk_index(d, s, n_ctx_chunks, n_chunks):
    bwd = jnp.where(s < n_ctx_chunks, n_ctx_chunks - 1 - s, n_chunks - 1 - (s - n_ctx_chunks))
    return jnp.where(d == 0, s, bwd)


def _ssd_body(xbc_ref, dt_ref, z_ref, aneg_ref, dtb_ref, dsk_ref, ng_ref, o_ref,
              hst_ref, ysc_ref, *, n_ctx_chunks, n_chunks):
    d = pl.program_id(1)
    s = pl.program_id(2)
    c = _ssd_chunk_index(d, s, n_ctx_chunks, n_chunks)
    q = SSD_CHUNK
    nb = xbc_ref.shape[0]
    r = nb * q

    @pl.when(s == 0)
    def _():
        hst_ref[...] = jnp.zeros_like(hst_ref)

    def run(direction):
        xbc = xbc_ref[...].reshape(r, SSD_CONV_CH)
        x = xbc[:, 0:512]
        bm = xbc[:, 512:640]
        cm = xbc[:, 640:768]
        pre = dt_ref[...].reshape(r, SSD_DT_PAD) + dtb_ref[...]
        dtv = jnp.maximum(pre, 0.0) + jnp.log(1.0 + jnp.exp(-jnp.abs(pre)))
        a = dtv * aneg_ref[...]
        row = lax.broadcasted_iota(jnp.int32, (r, r), 0)
        col = lax.broadcasted_iota(jnp.int32, (r, r), 1)
        order = (row >= col) if direction == 0 else (row <= col)
        tri = jnp.logical_and(row // q == col // q, order)
        a_hi = a.astype(BF16)
        r1 = a - a_hi.astype(F32)
        a_mid = r1.astype(BF16)
        a_lo = (r1 - a_mid.astype(F32)).astype(BF16)
        terms = [t[bi * q:(bi + 1) * q] for bi in range(nb) for t in (a_hi, a_mid, a_lo)]
        parts = jnp.dot(tri[0:q, 0:q].astype(BF16), jnp.concatenate(terms, axis=1),
                        preferred_element_type=F32)
        acs = jnp.concatenate(
            [parts[:, (3 * bi) * LANES:(3 * bi + 1) * LANES] + parts[:, (3 * bi + 1) * LANES:(3 * bi + 2) * LANES]
             + parts[:, (3 * bi + 2) * LANES:(3 * bi + 3) * LANES] for bi in range(nb)], axis=0)
        acs_t = acs.T
        edge = q - 1 if direction == 0 else 0
        lasts = [acs[bi * q + edge:bi * q + edge + 1, :] for bi in range(nb)]
        last = jnp.concatenate([jnp.broadcast_to(v, (q, LANES)) for v in lasts], axis=0)
        dte = jnp.exp(last - acs)
        expa = jnp.exp(acs)
        cdec = jnp.concatenate([jnp.broadcast_to(jnp.exp(v), (8, LANES)) for v in lasts], axis=0)
        erow = lax.broadcasted_iota(jnp.int32, (LANES, SSD_INNER), 0)
        ecol = lax.broadcasted_iota(jnp.int32, (LANES, SSD_INNER), 1)
        expand = (erow == ecol // SSD_HEAD_DIM + SSD_HEADS * direction).astype(BF16)
        spread = jnp.dot(jnp.concatenate([dtv, dtv * dte], axis=0).astype(BF16), expand,
                         preferred_element_type=F32)
        xdt = x * spread[0:r]
        xs = x * spread[r:2 * r]
        fine = jnp.concatenate([expa, cdec], axis=0)
        fine_hi = fine.astype(BF16)
        fine_lo = (fine - fine_hi.astype(F32)).astype(BF16)
        spread2 = jnp.dot(jnp.concatenate([fine_hi, fine_lo], axis=1), jnp.concatenate([expand, expand], axis=0),
                          preferred_element_type=F32)
        expa_x = spread2[0:r]
        bt = bm.T.astype(BF16)
        lane = lax.broadcasted_iota(jnp.int32, (1, LANES), 1)
        tri_q = tri[0:q, 0:q]
        ys = [[None] * nb for _ in range(4)]
        for g in range(2):
            cg = jnp.where(lane // SSD_STATE == g, cm, 0.0).astype(BF16)
            gmats = [jnp.dot(cg[bi * q:(bi + 1) * q], bt[:, bi * q:(bi + 1) * q], preferred_element_type=F32)
                     for bi in range(nb)]
            for hp in range(2):
                p = g * 2 + hp
                sl = slice(p * LANES, (p + 1) * LANES)
                for bi in range(nb):
                    rows = slice(bi * q, (bi + 1) * q)
                    ypair = None
                    for e in range(2):
                        j = SSD_HEADS * direction + 2 * p + e
                        seg = acs[rows, j:j + 1] - acs_t[j:j + 1, rows]
                        dec = jnp.exp(jnp.where(tri_q, seg, -jnp.inf))
                        xh = jnp.where(lane // SSD_HEAD_DIM == e, xdt[rows, sl], 0.0)
                        term = _bdot(gmats[bi] * dec, xh)
                        ypair = term if ypair is None else ypair + term
                    hprev = hst_ref[bi, p]
                    ypair = ypair + _bdot(cg[rows], hprev) * expa_x[rows, sl]
                    hst_ref[bi, p] = hprev * spread2[r + 8 * bi:r + 8 * bi + 1, sl] + jnp.dot(
                        bt[:, rows], xs[rows, sl].astype(BF16), preferred_element_type=F32)
                    ys[p][bi] = ypair
        ys = [jnp.concatenate(v, axis=0) for v in ys]
        return x, ys

    @pl.when(d == 0)
    def _():
        _, ys = run(0)
        for p in range(4):
            ysc_ref[c, :, p * LANES:(p + 1) * LANES] = ys[p].astype(ysc_ref.dtype)

    @pl.when(d == 1)
    def _():
        x, ys = run(1)
        y = jnp.concatenate(ys, axis=1) + ysc_ref[c].astype(F32) + dsk_ref[...] * x
        y = y * _silu(z_ref[...].reshape(r, SSD_INNER))
        o_ref[...] = _rms(y, ng_ref[...]).astype(o_ref.dtype).reshape(nb, q, SSD_INNER)


def _ssd_call(xbc_act, dt, z, aneg, dtb, dsk, ng, n_ctx_chunks):
    b, t, _ = xbc_act.shape
    q = SSD_CHUNK
    nc = t // q
    nb = SSD_BATCH_PER_STEP if b % SSD_BATCH_PER_STEP == 0 else 1
    cidx = functools.partial(_ssd_chunk_index, n_ctx_chunks=n_ctx_chunks, n_chunks=nc)

    def late(bi, di, si):
        return (bi, jnp.where(di == 0, n_ctx_chunks - 1, cidx(di, si)), 0)

    return pl.pallas_call(
        functools.partial(_ssd_body, n_ctx_chunks=n_ctx_chunks, n_chunks=nc),
        grid=(b // nb, 2, nc),
        in_specs=[
            pl.BlockSpec((nb, q, SSD_CONV_CH), lambda bi, di, si: (bi, cidx(di, si), 0)),
            pl.BlockSpec((nb, q, SSD_DT_PAD), lambda bi, di, si: (bi, cidx(di, si), 0)),
            pl.BlockSpec((nb, q, SSD_INNER), late),
            pl.BlockSpec((1, LANES), lambda bi, di, si: (0, 0)),
            pl.BlockSpec((1, LANES), lambda bi, di, si: (0, 0)),
            pl.BlockSpec((1, SSD_INNER), lambda bi, di, si: (0, 0)),
            pl.BlockSpec((1, SSD_INNER), lambda bi, di, si: (0, 0)),
        ],
        out_specs=pl.BlockSpec((nb, q, SSD_INNER), late),
        out_shape=jax.ShapeDtypeStruct((b, t, SSD_INNER), BF16),
        scratch_shapes=[pltpu.VMEM((nb, 4, LANES, LANES), F32),
                        pltpu.VMEM((nc, nb * q, SSD_INNER), BF16)],
        compiler_params=_params(3),
        name="ssd_scan",
    )(xbc_act, dt, z, aneg, dtb, dsk, ng)


def _flash(qs, k_ref, v_ref, m_ref, l_ref, acc_ref, *, tk, n_kv, ctx_len, is_ctx_tile):
    m_ref[...] = jnp.full(m_ref.shape, -jnp.inf, F32)
    l_ref[...] = jnp.zeros(l_ref.shape, F32)
    acc_ref[...] = jnp.zeros(acc_ref.shape, F32)

    def step(start, size, limit):
        for i, q in enumerate(qs):
            lanes = slice(i * LANES, (i + 1) * LANES)
            k = k_ref[start:start + size, lanes]
            v = v_ref[start:start + size, lanes]
            s = lax.dot_general(q, k, (((1,), (1,)), ((), ())), preferred_element_type=F32)
            if limit is not None:
                col = lax.broadcasted_iota(jnp.int32, (1, size), 1) + start
                s = jnp.where(col < limit, s, -jnp.inf)
            m_old = m_ref[i]
            m_new = jnp.maximum(m_old, jnp.max(s, axis=-1, keepdims=True))
            alpha = jnp.exp2(m_old - m_new)
            p = jnp.exp2(s - jnp.concatenate([m_new] * (size // LANES), axis=1))
            psum = p[:, 0:LANES]
            for j in range(1, size // LANES):
                psum = psum + p[:, j * LANES:(j + 1) * LANES]
            l_ref[i] = alpha * l_ref[i] + psum
            acc_ref[i] = alpha * acc_ref[i] + jnp.dot(p.astype(BF16), v, preferred_element_type=F32)
            m_ref[i] = m_new

    @pl.when(is_ctx_tile)
    def _():
        ctx_pad = -(-ctx_len // LANES) * LANES
        for start in range(0, ctx_pad, tk):
            step(start, min(tk, ctx_pad - start), ctx_len if ctx_pad != ctx_len else None)

    @pl.when(jnp.logical_not(is_ctx_tile))
    def _():
        for kc in range(n_kv):
            step(kc * tk, tk, None)

    return [acc_ref[i] * (1.0 / jnp.sum(l_ref[i], axis=-1, keepdims=True)) for i in range(len(qs))]


DIFF_HEADS_PER_STEP = 2


def _diff_attn_body(q_ref, k_ref, v_ref, lam_ref, g_ref, o_ref, m_ref, l_ref, acc_ref,
                    *, tq, lam_init, n_ctx_q, **kw):
    qi = pl.program_id(2)
    qs = [q_ref[i].reshape(2 * tq, LANES) for i in range(DIFF_HEADS_PER_STEP)]
    outs = _flash(qs, k_ref, v_ref, m_ref, l_ref, acc_ref, is_ctx_tile=qi < n_ctx_q, **kw)
    lv = lam_ref[...]
    lam = (jnp.exp(jnp.sum(lv[0:1] * lv[1:2], axis=-1, keepdims=True))
           - jnp.exp(jnp.sum(lv[2:3] * lv[3:4], axis=-1, keepdims=True)) + lam_init)
    for i, o in enumerate(outs):
        out = o[0:tq] - lam * o[tq:2 * tq]
        o_ref[:, i * LANES:(i + 1) * LANES] = (_rms(out, g_ref[...]) * (1.0 - lam_init)).astype(o_ref.dtype)


def _gqa_attn_body(q_ref, k_ref, v_ref, o_ref, m_ref, l_ref, acc_ref, *, tq, n_ctx_q, **kw):
    n = pl.program_id(1)
    qi = pl.program_id(2)
    q = q_ref[...].reshape(4 * tq, LANES)
    (o,) = _flash([q], k_ref, v_ref, m_ref, l_ref, acc_ref, is_ctx_tile=qi < n_ctx_q, **kw)
    lane = lax.broadcasted_iota(jnp.int32, (1, LANES), 1)
    lo_half = lane < HEAD_DIM
    for pair in range(2):
        a = o[(2 * pair) * tq:(2 * pair + 1) * tq]
        b = o[(2 * pair + 1) * tq:(2 * pair + 2) * tq]
        from_lo = jnp.where(lo_half, a, pltpu.roll(b, HEAD_DIM, 1))
        from_hi = jnp.where(lo_half, pltpu.roll(a, HEAD_DIM, 1), b)
        o_ref[:, pair * LANES:(pair + 1) * LANES] = jnp.where(n == 0, from_lo, from_hi).astype(o_ref.dtype)


def _attn_tiles(t, ctx_len):
    tq = ROW_TILE
    tk = next(c for c in (2816, 768, 256) if t % c == 0)
    assert ctx_len % tq == 0 and t % tq == 0 and t % tk == 0
    return dict(tq=tq, tk=tk, n_kv=t // tk, ctx_len=ctx_len, n_ctx_q=ctx_len // tq)


def _diff_attn_call(qd, kd, vd, lamv, subln_g, lam_init, ctx_len):
    b, _, _, t, _ = qd.shape
    cfg = _attn_tiles(t, ctx_len)
    tq = cfg["tq"]
    r = 2 * tq
    hs = DIFF_HEADS_PER_STEP
    return pl.pallas_call(
        functools.partial(_diff_attn_body, lam_init=lam_init, **cfg),
        grid=(b, DIFF_HEADS // hs, t // tq),
        in_specs=[
            pl.BlockSpec((None, hs, 2, tq, LANES), lambda bi, hi, qi: (bi, hi, 0, qi, 0)),
            pl.BlockSpec((None, t, hs * LANES), lambda bi, hi, qi: (bi, 0, hi)),
            pl.BlockSpec((None, t, hs * LANES), lambda bi, hi, qi: (bi, 0, hi)),
            pl.BlockSpec((4, LANES), lambda bi, hi, qi: (0, 0)),
            pl.BlockSpec((1, LANES), lambda bi, hi, qi: (0, 0)),
        ],
        out_specs=pl.BlockSpec((None, tq, hs * LANES), lambda bi, hi, qi: (bi, qi, hi)),
        out_shape=jax.ShapeDtypeStruct((b, t, DIFF_HEADS * LANES), BF16),
        scratch_shapes=[pltpu.VMEM((hs, r, LANES), F32)] * 3,
        compiler_params=_params(3),
        name="diff_attention",
    )(qd, kd, vd, lamv, subln_g)


def _gqa_attn_call(qg, kg, vg, ctx_len):
    b, _, _, t, _ = qg.shape
    cfg = _attn_tiles(t, ctx_len)
    tq = cfg["tq"]
    r = 4 * tq
    return pl.pallas_call(
        functools.partial(_gqa_attn_body, **cfg),
        grid=(b, GQA_KV_HEADS, t // tq),
        in_specs=[
            pl.BlockSpec((None, None, 4, tq, LANES), lambda bi, ni, qi: (bi, ni, 0, qi, 0)),
            pl.BlockSpec((None, t, LANES), lambda bi, ni, qi: (bi, 0, 0)),
            pl.BlockSpec((None, t, LANES), lambda bi, ni, qi: (bi, 0, 0)),
        ],
        out_specs=pl.BlockSpec((None, tq, 2 * LANES), lambda bi, ni, qi: (bi, qi, ni)),
        out_shape=jax.ShapeDtypeStruct((b, t, GQA_HEADS * HEAD_DIM), BF16),
        scratch_shapes=[pltpu.VMEM((1, r, LANES), F32)] * 3,
        compiler_params=_params(3),
        name="gqa_attention",
    )(qg, kg, vg)


def _cmul(ar, ai, br, bi):
    return ar * br - ai * bi, ar * bi + ai * br


def _stack_rows(er, ei, vr, vi):
    re = [er * vr[c:c + 1] - ei * vi[c:c + 1] for c in range(S5_GROUP_CH)]
    im = [er * vi[c:c + 1] + ei * vr[c:c + 1] for c in range(S5_GROUP_CH)]
    return jnp.concatenate(re, axis=0), jnp.concatenate(im, axis=0)


def _split_dot_t(a, b):
    dn = (((1,), (1,)), ((), ()))
    ah = a.astype(BF16)
    al = (a - ah.astype(F32)).astype(BF16)
    bh = b.astype(BF16)
    bl = (b - bh.astype(F32)).astype(BF16)
    return lax.dot_general(jnp.concatenate([ah, ah, al], axis=1), jnp.concatenate([bh, bl, bh], axis=1),
                           dn, preferred_element_type=F32)


def _s5_weights_body(lam_ref, bt_ref, c_ref, wi_ref, ws_ref, wo_ref, av_ref):
    tc = S5_CHUNK
    lam = lam_ref[...]
    br_t, bi_t = bt_ref[0:16, :], bt_ref[16:32, :]
    cr, ci = c_ref[0:16, :], c_ref[16:32, :]
    kk = lax.broadcasted_iota(jnp.int32, (tc, S5_STATE), 0).astype(F32)
    k8 = lax.broadcasted_iota(jnp.int32, (8, S5_STATE), 0)
    k8 = jnp.where(k8 == 0, 1.0, jnp.where(k8 == 1, tc - 1.0, float(tc)))
    row = lax.broadcasted_iota(jnp.int32, (S5_FLAT, S5_FLAT), 0) % tc
    col = lax.broadcasted_iota(jnp.int32, (S5_FLAT, S5_FLAT), 1) % tc
    w_intra = None
    state_cols, out_cols, a_rows = [], [], []
    for direction in range(2):
        lr = lam[2 * direction:2 * direction + 1]
        li = lam[2 * direction + 1:2 * direction + 2]
        step = jnp.exp(lam[4 + direction:5 + direction])
        mag = jnp.exp(lr * step)
        ar, ai = mag * jnp.cos(li * step), mag * jnp.sin(li * step)
        den = lr * lr + li * li
        fr = ((ar - 1.0) * lr + ai * li) / den
        fi = (ai * lr - (ar - 1.0) * li) / den
        bbr = fr * br_t - fi * bi_t
        bbi = fr * bi_t + fi * br_t
        cs, sn = jnp.cos(kk * (li * step)), jnp.sin(kk * (li * step))
        grow, decay = jnp.exp(-kk * (lr * step)), jnp.exp(kk * (lr * step))
        pr, pi = decay * cs, decay * sn
        nr, ni = grow * cs, -grow * sn
        m8 = jnp.exp(k8 * (lr * step))
        c8r, c8i = m8 * jnp.cos(k8 * (li * step)), m8 * jnp.sin(k8 * (li * step))
        a_one = (c8r[0:1], c8i[0:1])
        a_last = (c8r[1:2], c8i[1:2])
        a_tc = (c8r[2:3], c8i[2:3])
        if direction == 0:
            x_e, y_e = (nr, ni), (pr, pi)
            s_e = _cmul(nr, ni, *a_last)
            o_e = _cmul(pr, pi, *a_one)
            keep = col >= row
        else:
            x_e, y_e = (pr, pi), (nr, ni)
            s_e = (pr, pi)
            o_e = _cmul(nr, ni, *a_tc)
            keep = row >= col
        xr, xi = _stack_rows(*x_e, bbr, bbi)
        yr, yi = _stack_rows(*y_e, cr, ci)
        full = _split_dot_t(jnp.concatenate([xr, -xi], axis=1), jnp.concatenate([yr, yi], axis=1))
        part = jnp.where(keep, full, 0.0)
        w_intra = part if w_intra is None else w_intra + part
        state_cols.append(_stack_rows(*s_e, bbr, bbi))
        o_r, o_i = _stack_rows(*o_e, cr, ci)
        out_cols += [o_r, -o_i]
        a_rows += [jnp.concatenate([a_tc[0], a_tc[0]], axis=1),
                   jnp.concatenate([-a_tc[1], a_tc[1]], axis=1),
                   jnp.concatenate([a_tc[1], -a_tc[1]], axis=1)]
    wi_ref[...] = w_intra.astype(BF16)
    (fr_, fi_), (br_, bi_) = state_cols
    ws_ref[...] = jnp.concatenate([fr_, fi_, br_, bi_, fi_, fr_, bi_, br_], axis=1).astype(BF16)
    wo_ref[...] = jnp.concatenate(out_cols, axis=1).astype(BF16)
    zero = jnp.zeros((1, LANES), F32)
    av_ref[...] = jnp.concatenate(a_rows + [zero, zero], axis=0)


def _s5_weights_call(lam_re, lam_im, log_dt, b_re, b_im, c_re, c_im):
    g, p = S5_GROUPS, S5_STATE
    bc = lambda v: jnp.broadcast_to(v.astype(F32)[:, None], (g, p))
    zero = jnp.zeros((g, p), F32)
    lam = jnp.stack([lam_re[0], lam_im[0], lam_re[1], lam_im[1], bc(log_dt[0]), bc(log_dt[1]),
                     zero, zero], axis=1).astype(F32)
    bt = jnp.concatenate([jnp.swapaxes(b_re, 1, 2), jnp.swapaxes(b_im, 1, 2)], axis=1).astype(F32)
    cc = jnp.concatenate([c_re, c_im], axis=1).astype(F32)
    n = S5_FLAT
    return pl.pallas_call(
        _s5_weights_body,
        grid=(g,),
        in_specs=[pl.BlockSpec((None, 8, p), lambda gi: (gi, 0, 0)),
                  pl.BlockSpec((None, 32, p), lambda gi: (gi, 0, 0)),
                  pl.BlockSpec((None, 32, p), lambda gi: (gi, 0, 0))],
        out_specs=[pl.BlockSpec((None, n, n), lambda gi: (gi, 0, 0)),
                   pl.BlockSpec((None, n, 4 * LANES), lambda gi: (gi, 0, 0)),
                   pl.BlockSpec((None, n, 2 * LANES), lambda gi: (gi, 0, 0)),
                   pl.BlockSpec((None, 8, LANES), lambda gi: (gi, 0, 0))],
        out_shape=[jax.ShapeDtypeStruct((g, n, n), BF16),
                   jax.ShapeDtypeStruct((g, n, 4 * LANES), BF16),
                   jax.ShapeDtypeStruct((g, n, 2 * LANES), BF16),
                   jax.ShapeDtypeStruct((g, 8, LANES), F32)],
        compiler_params=_params(1),
        name="s5_weights",
    )(lam, bt, cc)


def _s5_body(u_ref, wi_ref, ws_ref, wo_ref, av_ref, y_ref, s_ref, h_ref, *, n_ctx_chunks, n_chunks):
    u = u_ref[...]
    s_ref[...] = jnp.dot(u, ws_ref[...], preferred_element_type=F32)
    av = av_ref[...]
    nj = n_chunks
    rows = u_ref.shape[0] // nj
    a1f, a2f, a2sf, a1b, a2b, a2sb = [jnp.broadcast_to(av[i:i + 1], (rows, LANES)) for i in range(6)]
    hf = hfs = hb = hbs = jnp.zeros((rows, LANES), F32)
    for i in range(nj):
        jf = i * rows
        jb = (n_ctx_chunks - 1 - i if i < n_ctx_chunks else nj - 1 - (i - n_ctx_chunks)) * rows
        h_ref[jf:jf + rows, 0:LANES] = hf
        h_ref[jb:jb + rows, LANES:2 * LANES] = hb
        sf = s_ref[jf:jf + rows, 0:LANES]
        sfs = s_ref[jf:jf + rows, 2 * LANES:3 * LANES]
        sb = s_ref[jb:jb + rows, LANES:2 * LANES]
        sbs = s_ref[jb:jb + rows, 3 * LANES:4 * LANES]
        hf, hfs = a1f * hf + a2f * hfs + sf, a1f * hfs + a2sf * hf + sfs
        hb, hbs = a1b * hb + a2b * hbs + sb, a1b * hbs + a2sb * hb + sbs
    y = (jnp.dot(u, wi_ref[...], preferred_element_type=F32)
         + lax.dot_general(h_ref[...].astype(BF16), wo_ref[...], (((1,), (1,)), ((), ())),
                           preferred_element_type=F32))
    y_ref[...] = y.astype(y_ref.dtype)


def _s5_call(ug, w_intra, w_state, w_out, avec, n_ctx_chunks, n_chunks):
    g, r, w = ug.shape
    return pl.pallas_call(
        functools.partial(_s5_body, n_ctx_chunks=n_ctx_chunks, n_chunks=n_chunks),
        grid=(g,),
        in_specs=[
            pl.BlockSpec((None, r, w), lambda gi: (gi, 0, 0)),
            pl.BlockSpec((None, w, w), lambda gi: (gi, 0, 0)),
            pl.BlockSpec((None, w, 4 * LANES), lambda gi: (gi, 0, 0)),
            pl.BlockSpec((None, w, 2 * LANES), lambda gi: (gi, 0, 0)),
            pl.BlockSpec((None, 8, LANES), lambda gi: (gi, 0, 0)),
        ],
        out_specs=pl.BlockSpec((None, r, w), lambda gi: (gi, 0, 0)),
        out_shape=jax.ShapeDtypeStruct((g, r, w), BF16),
        scratch_shapes=[pltpu.VMEM((r, 4 * LANES), F32), pltpu.VMEM((r, 2 * LANES), F32)],
        compiler_params=_params(1),
        name="s5_scan",
    )(ug, w_intra, w_state, w_out, avec)


def _s5_to_groups(u):
    b, t, _ = u.shape
    nj = t // S5_CHUNK
    x = jnp.swapaxes(u.astype(BF16).reshape(b, nj, S5_CHUNK, S5_WIDTH), 2, 3)
    x = lax.optimization_barrier(x.reshape(b, nj, S5_GROUPS, S5_FLAT))
    return jnp.transpose(x, (2, 1, 0, 3)).reshape(S5_GROUPS, nj * b, S5_FLAT)


def _s5_from_groups(y, b):
    g, r, n = y.shape
    nj = r // b
    x = jnp.transpose(y.reshape(g, nj, b, n), (2, 1, 0, 3))
    x = lax.optimization_barrier(x).reshape(b, nj, S5_WIDTH, S5_CHUNK)
    return jnp.swapaxes(x, 2, 3).reshape(b, nj * S5_CHUNK, S5_WIDTH)


def _merge_body(h_ref, mod_ref, xn_ref, ya_ref, yb_ref, yc_ref, u_ref, y5_ref,
                wg_ref, wa_ref, wb_ref, wc_ref, wd_ref, glw_ref, glb_ref, s5d_ref, wo_ref,
                g2_ref, wgu_ref, wdn_ref, o_ref):
    xn = xn_ref[...]
    y5 = y5_ref[...].astype(F32) + s5d_ref[...] * u_ref[...].astype(F32)
    gelu = 0.5 * y5 * (1.0 + jnp.tanh(0.7978845608028654 * (y5 + 0.044715 * y5 * y5 * y5)))
    glu = _bdot(gelu, glw_ref[...]) + glb_ref[...]
    yd = glu[:, 0:S5_WIDTH] * _sigmoid(glu[:, S5_WIDTH:2 * S5_WIDTH])
    branches = ((ya_ref[...], wa_ref), (yb_ref[...], wb_ref), (yc_ref[...], wc_ref),
                (yd.astype(BF16), wd_ref))
    merged = None
    for i, (y, w_ref) in enumerate(branches):
        gate = _sigmoid(jnp.dot(xn, wg_ref[i], preferred_element_type=F32))
        term = gate * jnp.dot(y, w_ref[...], preferred_element_type=F32)
        merged = term if merged is None else merged + term
    out = _bdot(merged, wo_ref[...])
    h = h_ref[...] + mod_ref[2:3, :] * out
    mod = mod_ref[...]
    xf = (_rms(h, g2_ref[...]) * (1.0 + mod[4:5]) + mod[3:4]).astype(BF16)
    gate = jnp.dot(xf, wgu_ref[:, 0:FFN_HIDDEN], preferred_element_type=F32)
    up = jnp.dot(xf, wgu_ref[:, FFN_HIDDEN:2 * FFN_HIDDEN], preferred_element_type=F32)
    act = (_silu(gate) * up).astype(BF16)
    o_ref[...] = h + mod[5:6] * jnp.dot(act, wdn_ref[...], preferred_element_type=F32)


def _merge_call(h, modv, xn, ya, yb, yc, u, y5, wts, ffn_wts, n_ctx_tiles, latent_only):
    b, t, d = h.shape
    tm = ROW_TILE
    skip = n_ctx_tiles if latent_only else 0

    def row(width):
        return pl.BlockSpec((None, tm, width), lambda bi, ti: (bi, ti + skip, 0))

    def const(shape):
        return pl.BlockSpec(shape, lambda bi, ti: (0,) * len(shape), pipeline_mode=pl.Buffered(1))

    w_gate, w_a, w_b, w_c, w_d, glu_w, glu_b, s5_d, w_out = wts
    g2, w_gu, w_down = ffn_wts
    return pl.pallas_call(
        _merge_body,
        grid=(b, t // tm - skip),
        in_specs=[row(d),
                  pl.BlockSpec((None, None, 6, d),
                               lambda bi, ti: (bi, jnp.where(ti + skip >= n_ctx_tiles, 1, 0), 0, 0)),
                  row(d), row(512), row(512), row(512), row(S5_WIDTH), row(S5_WIDTH),
                  const((4, d, d)), const((512, d)), const((512, d)), const((512, d)),
                  const((S5_WIDTH, d)), const((S5_WIDTH, 2 * S5_WIDTH)), const((1, 2 * S5_WIDTH)),
                  const((1, S5_WIDTH)), const((d, d)),
                  const((1, d)), const((d, 2 * FFN_HIDDEN)), const((FFN_HIDDEN, d))],
        out_specs=pl.BlockSpec((None, tm, d), lambda bi, ti: (bi, ti, 0)),
        out_shape=jax.ShapeDtypeStruct((b, t - skip * tm, d), F32),
        compiler_params=_params(2),
        name="merge_ffn",
    )(h, modv, xn, ya, yb, yc, u, y5, w_gate, w_a, w_b, w_c, w_d, glu_w, glu_b, s5_d, w_out,
      g2, w_gu, w_down)


def _rope_tables(ctx_len, seq_len):
    n_rows = seq_len // GRID_W
    rows = jnp.repeat(jnp.arange(n_rows, dtype=F32), GRID_W)
    cols = jnp.tile(jnp.arange(GRID_W, dtype=F32), n_rows)
    quarter = HEAD_DIM // 4
    inv_freq = ROPE_THETA ** (-jnp.arange(quarter, dtype=F32) / quarter)
    ang_r = rows[:, None] * inv_freq
    ang_c = cols[:, None] * inv_freq
    ang = jnp.concatenate([ang_r, ang_r, ang_c, ang_c], axis=-1)
    cos = jnp.concatenate([jnp.ones((ctx_len, HEAD_DIM), F32), jnp.cos(ang)], axis=0)
    sin = jnp.concatenate([jnp.zeros((ctx_len, HEAD_DIM), F32), jnp.sin(ang)], axis=0)
    first = (jnp.arange(HEAD_DIM) % 32) < 16
    sin_a = jnp.where(first, -sin, 0.0)
    sin_b = jnp.where(first, 0.0, sin)
    two = lambda m: jnp.concatenate([m, m], axis=1)
    return jnp.concatenate([two(cos), two(sin_a), two(sin_b)], axis=1)


def _w_in_layout(w_in):
    d = w_in.shape[0]
    a0 = 0
    z = w_in[:, a0:a0 + 512]
    xbc = w_in[:, a0 + 512:a0 + 1280]
    dt = jnp.pad(w_in[:, a0 + 1280:a0 + 1296], ((0, 0), (0, SSD_DT_PAD - 16)))
    rest = w_in[:, 1296:]
    out = jnp.concatenate([z, xbc, dt, rest], axis=1).astype(BF16)
    assert out.shape == (d, _C_END)
    return out


def kernel(x, c, ctx, c_ctx, w_mod, b_mod, norm1_g, norm2_g, w_in, ssd_conv_w, ssd_conv_b, ssd_a_log, ssd_dt_bias, ssd_d, ssd_norm_g, diff_qn_g, diff_kn_g, diff_lam_q1, diff_lam_k1, diff_lam_q2, diff_lam_k2, diff_subln_g, gqa_qn_g, gqa_kn_g, s5_lam_re, s5_lam_im, s5_log_dt, s5_b_re, s5_b_im, s5_c_re, s5_c_im, s5_d, s5_glu_w, s5_glu_b, w_gate, w_br_ssd, w_br_diff, w_br_gqa, w_br_s5, w_out, ffn_w_gate_up, ffn_w_down):
    b, seq_len, d = x.shape
    ctx_len = ctx.shape[1]
    depth = w_mod.shape[0]
    assert b + 1 <= 8
    assert ctx_len % ROW_TILE == 0 and seq_len % ROW_TILE == 0
    n_ctx_tiles = ctx_len // ROW_TILE

    h = jnp.concatenate([ctx, x], axis=1)
    cc = jnp.concatenate([c, c_ctx[None], jnp.zeros((8 - b - 1, d), F32)], axis=0)
    mods = _mod_call(cc, w_mod, b_mod)

    rope_tab = _rope_tables(ctx_len, seq_len)
    blk = jnp.arange(512) // HEAD_DIM
    ones = (blk[:, None] == blk[None, :]).astype(BF16)
    tile8 = lambda g: jnp.tile(g.astype(F32), 512 // HEAD_DIM)

    for layer in range(depth):
        m = mods[layer]
        lat = m[:b].reshape(b, 6, d)
        cmod = jnp.broadcast_to(m[b].reshape(1, 6, d), (b, 6, d))
        modv = jnp.stack([cmod, lat], axis=1)

        gains = jnp.stack([tile8(diff_qn_g[layer]), tile8(diff_kn_g[layer]),
                           tile8(gqa_qn_g[layer]), tile8(gqa_kn_g[layer])]
                          + [jnp.zeros((512,), F32)] * 4, axis=0)
        conv_w = jnp.pad(ssd_conv_w[layer].astype(F32), ((0, 8 - SSD_CONV_W), (0, 0)))
        (xn, z, xbc_act, dt, qd, kd, vd, qg, kg, vg, u) = _in_proj_call(
            h, modv, norm1_g[layer][None], _w_in_layout(w_in[layer]), ones, gains, rope_tab,
            conv_w, ssd_conv_b[layer][None].astype(F32), n_ctx_tiles)

        pad16 = lambda v: jnp.pad(v.reshape(1, 16).astype(F32), ((0, 0), (0, LANES - 16)))
        aneg = pad16(-jnp.exp(ssd_a_log[layer].astype(F32)))
        dtb = pad16(ssd_dt_bias[layer])
        dsk = jnp.repeat(ssd_d[layer].astype(F32), SSD_HEAD_DIM)[None]
        ya = _ssd_call(xbc_act, dt, z, aneg, dtb, dsk, ssd_norm_g[layer][None].astype(F32),
                       ctx_len // SSD_CHUNK)

        lam_init = 0.8 - 0.6 * math.exp(-0.3 * layer)
        lamv = jnp.pad(jnp.stack([diff_lam_q1[layer], diff_lam_k1[layer],
                                  diff_lam_q2[layer], diff_lam_k2[layer]]).astype(F32),
                       ((0, 0), (0, LANES - HEAD_DIM)))
        yb = _diff_attn_call(qd, kd, vd, lamv, diff_subln_g[layer][None].astype(F32), lam_init, ctx_len)

        yc = _gqa_attn_call(qg, kg, vg, ctx_len)

        s5w = _s5_weights_call(s5_lam_re[layer], s5_lam_im[layer], s5_log_dt[layer],
                               s5_b_re[layer], s5_b_im[layer], s5_c_re[layer], s5_c_im[layer])
        y5 = _s5_from_groups(_s5_call(_s5_to_groups(u), *s5w, ctx_len // S5_CHUNK,
                                      (ctx_len + seq_len) // S5_CHUNK), b)

        wts = (w_gate[layer].astype(BF16), w_br_ssd[layer].astype(BF16), w_br_diff[layer].astype(BF16),
               w_br_gqa[layer].astype(BF16), w_br_s5[layer].astype(BF16), s5_glu_w[layer].astype(BF16),
               s5_glu_b[layer][None].astype(F32), s5_d[layer][None].astype(F32), w_out[layer].astype(BF16))
        ffn_wts = (norm2_g[layer][None].astype(F32), ffn_w_gate_up[layer].astype(BF16),
                   ffn_w_down[layer].astype(BF16))
        h = _merge_call(h, modv, xn, ya, yb, yc, u, y5, wts, ffn_wts, n_ctx_tiles,
                        latent_only=layer == depth - 1)
    return h
```
